```python
import math
import jax
import jax.numpy as jnp
from jax import lax
import numpy as np

D_MODEL = 2048
BATCH = 4
SEQ = 4096
DEPTH = 2

GRID_W = 64
CTX_LEN = 256
N_MOD = 9
D_FF = 5632
RET_WIDTH = D_MODEL // 2
RET_HEADS = 8
RET_HEAD_DIM = RET_WIDTH // RET_HEADS
HY_WIDTH = D_MODEL - RET_WIDTH
HY_ORDER = 2
HY_SHORT = 3
HY_EMB_DIM = 33
HY_HIDDEN = 64
HY_DECAY_SHORT_PCT = 0.3
HY_DECAY_LONG_PCT = 1.5
HY_DECAY_TARGET = 1e-2
CHUNK = 128
ROPE_BASE = 10000.0
POOL_WINDOWS = (2, 4, 8, 16)
POOL_GROUP = D_MODEL // len(POOL_WINDOWS)
PROJ_WIDTH = 4 * RET_WIDTH + (HY_ORDER + 1) * HY_WIDTH
EPS = 1e-6
F32 = jnp.float32

kernel_name = "hybrid_retnet_hyena_pool_macaron"


def rms_norm(x):
    xf = x.astype(F32)
    return (xf * lax.rsqrt(jnp.mean(xf * xf, axis=-1, keepdims=True) + EPS)).astype(x.dtype)


def modulate(x, mod, base):
    return rms_norm(x) * (1.0 + mod[:, base + 1]) + mod[:, base]


def swiglu(h, w1, w3, w2):
    return (jax.nn.silu(h @ w1) * (h @ w3)) @ w2


def ffn_half(x, mod, base, w1, w3, w2):
    h = modulate(x, mod, base)
    return x + 0.5 * mod[:, base + 2] * swiglu(h, w1, w3, w2)


def axial_rotary(n, dtype):
    pos = jnp.arange(n)
    row = (pos // GRID_W).astype(F32)
    col = (pos % GRID_W).astype(F32)
    n_freq = RET_HEAD_DIM // 4
    inv = ROPE_BASE ** (-jnp.arange(n_freq, dtype=F32) / n_freq)
    ang = jnp.concatenate([row[:, None] * inv, col[:, None] * inv], axis=-1)
    return jnp.cos(ang).astype(dtype), jnp.sin(ang).astype(dtype)


def apply_rotary(x, cos, sin):
    half = x.shape[-1] // 2
    x1, x2 = x[..., :half], x[..., half:]
    c, s = cos[:, None, :], sin[:, None, :]
    return jnp.concatenate([x1 * c - x2 * s, x2 * c + x1 * s], axis=-1)


def retention_chunkwise(q, k, v, log_gamma, s0):
    b, n, h, d = q.shape
    nc = n // CHUNK
    dt = q.dtype
    lg = log_gamma.astype(F32)
    pos = jnp.arange(CHUNK, dtype=F32)
    diff = pos[:, None] - pos[None, :]
    decay_in = jnp.where(diff >= 0, jnp.exp(jnp.maximum(diff, 0.0)[None] * lg[:, None, None]), 0.0).astype(dt)
    q_dec = jnp.exp((pos + 1.0)[None, :] * lg[:, None]).astype(dt)
    k_dec = jnp.exp((CHUNK - 1.0 - pos)[None, :] * lg[:, None]).astype(dt)
    chunk_dec = jnp.exp(CHUNK * lg).astype(dt)
    qc = q.reshape(b, nc, CHUNK, h, d)
    kc = k.reshape(b, nc, CHUNK, h, d)
    vc = v.reshape(b, nc, CHUNK, h, d)
    scores = jnp.einsum('bnihd,bnjhd->bnhij', qc, kc) * decay_in
    inner = jnp.einsum('bnhij,bnjhe->bnihe', scores, vc)
    upd = jnp.einsum('bnjhd,hj,bnjhe->nbhde', kc, k_dec, vc)

    def step(s, u):
        return chunk_dec[None, :, None, None] * s + u, s

    s_fin, s_prev = lax.scan(step, s0, upd)
    cross = jnp.einsum('bnihd,hi,nbhde->bnihe', qc, q_dec, s_prev)
    return (inner + cross).reshape(b, n, h, d), s_fin


def bi_retention(q, k, v, log_gamma, s0_f, s0_b):
    out_f, s_f = retention_chunkwise(q, k, v, log_gamma[0], s0_f)
    out_b, s_b = retention_chunkwise(q[:, ::-1], k[:, ::-1], v[:, ::-1], log_gamma[1], s0_b)
    return out_f + out_b[:, ::-1], s_f, s_b


def retention_final_states(k, v, log_gamma):
    n = k.shape[1]
    pos = jnp.arange(n, dtype=F32)
    lg = log_gamma.astype(F32)
    w_f = jnp.exp((n - 1.0 - pos)[None, :] * lg[0][:, None]).astype(k.dtype)
    w_b = jnp.exp(pos[None, :] * lg[1][:, None]).astype(k.dtype)
    s_f = jnp.einsum('blhd,hl,blhe->bhde', k, w_f, v)
    s_b = jnp.einsum('blhd,hl,blhe->bhde', k, w_b, v)
    return s_f, s_b


def short_conv(u, w, bias):
    n = u.shape[1]
    pad = HY_SHORT // 2
    up = jnp.pad(u, ((0, 0), (pad, HY_SHORT - 1 - pad), (0, 0)))
    out = bias + up[:, 0:n] * w[0]
    for j in range(1, HY_SHORT):
        out = out + up[:, j:j + n] * w[j]
    return out


def hyena_filters(n, fw1, fb1, fw2, fb2, fw3, fb3, freq, fw4):
    t = jnp.linspace(0.0, 1.0, n, dtype=F32)[:, None]
    bands = (HY_EMB_DIM - 1) // 2
    f = jnp.linspace(1e-4, bands - 1, bands, dtype=F32)[None, :]
    w = 2.0 * math.pi * jnp.arange(n, dtype=F32)[:, None] / n
    z = jnp.concatenate([t, jnp.cos(f * w), -jnp.sin(f * w)], axis=-1).astype(fw1.dtype)
    a = jnp.sin(freq[0] * (z @ fw1 + fb1))
    a = jnp.sin(freq[1] * (a @ fw2 + fb2))
    a = jnp.sin(freq[2] * (a @ fw3 + fb3))
    hf = (a @ fw4).astype(F32).reshape(n, HY_ORDER, 2, HY_WIDTH)
    max_decay = math.log(HY_DECAY_TARGET) / HY_DECAY_SHORT_PCT
    min_decay = math.log(HY_DECAY_TARGET) / HY_DECAY_LONG_PCT
    deltas = jnp.abs(jnp.linspace(min_decay, max_decay, HY_WIDTH, dtype=F32))
    window = jnp.exp(-t * deltas[None, :])
    return hf * window[:, None, None, :]


def long_conv_bidir(u, h_fwd, h_bwd, bias):
    n = u.shape[1]
    k_full = jnp.concatenate([h_fwd, jnp.zeros_like(h_fwd[:1]), h_bwd[:0:-1]], axis=0)
    uf = jnp.fft.rfft(u.astype(F32), n=2 * n, axis=1)
    kf = jnp.fft.rfft(k_full, n=2 * n, axis=0)
    y = jnp.fft.irfft(uf * kf[None], n=2 * n, axis=1)[:, :n]
    return (y + u.astype(F32) * bias.astype(F32)).astype(u.dtype)


def hyena(p, conv_w, conv_b, filt, bias):
    u = short_conv(p, conv_w, conv_b)
    v, x1, x2 = jnp.split(u, HY_ORDER + 1, axis=-1)
    z = x1 * long_conv_bidir(v, filt[:, 0, 0], filt[:, 0, 1], bias[0])
    return x2 * long_conv_bidir(z, filt[:, 1, 0], filt[:, 1, 1], bias[1])


def ab_mixer(h, w_in, w_out, log_decay, conv_w, conv_b, filt, hy_bias, rot, s0_f, s0_b):
    b, n, _ = h.shape
    p = h @ w_in
    q, k, v, g = jnp.split(p[..., :4 * RET_WIDTH], 4, axis=-1)
    heads = (b, n, RET_HEADS, RET_HEAD_DIM)
    q = q.reshape(heads)
    k = k.reshape(heads) * RET_HEAD_DIM ** -0.5
    v = v.reshape(heads)
    if rot is not None:
        q = apply_rotary(q, rot[0], rot[1])
        k = apply_rotary(k, rot[0], rot[1])
    r, s_f, s_b = bi_retention(q, k, v, log_decay, s0_f, s0_b)
    r = rms_norm(r).reshape(b, n, RET_WIDTH) * jax.nn.silu(g)
    y_h = hyena(p[..., 4 * RET_WIDTH:], conv_w, conv_b, filt, hy_bias)
    y = jnp.concatenate([r, y_h], axis=-1) @ w_out
    return y, s_f, s_b


def box_mean(x, axis, win):
    n = x.shape[axis]
    lo, hi = -(win // 2), win - 1 - win // 2
    xf = x.astype(F32)
    zero = jnp.zeros_like(lax.slice_in_dim(xf, 0, 1, axis=axis))
    cs = jnp.concatenate([zero, jnp.cumsum(xf, axis=axis)], axis=axis)
    idx = jnp.arange(n)
    top = jnp.minimum(idx + hi, n - 1) + 1
    bot = jnp.maximum(idx + lo, 0)
    total = jnp.take(cs, top, axis=axis) - jnp.take(cs, bot, axis=axis)
    shape = [1] * x.ndim
    shape[axis] = n
    cnt = (top - bot).astype(F32).reshape(shape)
    return (total / cnt).astype(x.dtype)


def pool_mixer(h, w, scale, on_grid):
    b, n, d = h.shape
    if on_grid:
        rows = n // GRID_W
        hh = h.reshape(b, rows, GRID_W, d)
    else:
        hh = h
    outs = []
    for gi, win in enumerate(POOL_WINDOWS):
        xg = hh[..., gi * POOL_GROUP:(gi + 1) * POOL_GROUP]
        if on_grid:
            m = box_mean(box_mean(xg, 2, win), 1, win)
        else:
            m = box_mean(xg, 1, win)
        outs.append((m - xg) @ w[gi])
    return jnp.concatenate(outs, axis=-1).reshape(b, n, d) * scale


def setup_inputs(seed: int = 0) -> dict:
    key = jax.random.key(seed)
    ks = jax.random.split(key, 29)
    n_even = (DEPTH + 1) // 2
    n_odd = DEPTH // 2

    def nrm(k, shape, std):
        return jax.random.normal(k, shape, F32) * std

    base_decay = jnp.log(1.0 - 2.0 ** (-5.0 - jnp.arange(RET_HEADS, dtype=F32)))
    return {
        'x': nrm(ks[0], (BATCH, SEQ, D_MODEL), 1.0),
        'c': nrm(ks[1], (BATCH, D_MODEL), 1.0),
        'ctx': nrm(ks[2], (BATCH, CTX_LEN, D_MODEL), 1.0),
        'c_ctx': nrm(ks[3], (D_MODEL,), 1.0),
        'w_mod': nrm(ks[4], (DEPTH, D_MODEL, N_MOD * D_MODEL), 0.5 * D_MODEL ** -0.5),
        'b_mod': nrm(ks[5], (DEPTH, N_MOD * D_MODEL), 0.02),
        'ffn1_w1': nrm(ks[6], (DEPTH, D_MODEL, D_FF), D_MODEL ** -0.5),
        'ffn1_w3': nrm(ks[7], (DEPTH, D_MODEL, D_FF), D_MODEL ** -0.5),
        'ffn1_w2': nrm(ks[8], (DEPTH, D_FF, D_MODEL), D_FF ** -0.5),
        'ffn2_w1': nrm(ks[9], (DEPTH, D_MODEL, D_FF), D_MODEL ** -0.5),
        'ffn2_w3': nrm(ks[10], (DEPTH, D_MODEL, D_FF), D_MODEL ** -0.5),
        'ffn2_w2': nrm(ks[11], (DEPTH, D_FF, D_MODEL), D_FF ** -0.5),
        'ab_w_in': nrm(ks[12], (n_even, D_MODEL, PROJ_WIDTH), D_MODEL ** -0.5),
        'ab_w_out': nrm(ks[13], (n_even, D_MODEL, D_MODEL), D_MODEL ** -0.5),
        'ret_log_decay': base_decay * (1.0 + 0.05 * jax.random.normal(ks[14], (n_even, 2, RET_HEADS), F32)),
        'hy_conv_w': nrm(ks[15], (n_even, HY_SHORT, (HY_ORDER + 1) * HY_WIDTH), HY_SHORT ** -0.5),
        'hy_conv_b': nrm(ks[16], (n_even, (HY_ORDER + 1) * HY_WIDTH), 0.02),
        'hy_f_w1': nrm(ks[17], (n_even, HY_EMB_DIM, HY_HIDDEN), HY_EMB_DIM ** -0.5),
        'hy_f_b1': nrm(ks[18], (n_even, HY_HIDDEN), 0.1),
        'hy_f_w2': nrm(ks[19], (n_even, HY_HIDDEN, HY_HIDDEN), HY_HIDDEN ** -0.5),
        'hy_f_b2': nrm(ks[20], (n_even, HY_HIDDEN), 0.1),
        'hy_f_w3': nrm(ks[21], (n_even, HY_HIDDEN, HY_HIDDEN), HY_HIDDEN ** -0.5),
        'hy_f_b3': nrm(ks[22], (n_even, HY_HIDDEN), 0.1),
        'hy_f_freq': 1.0 + nrm(ks[23], (n_even, 3, HY_HIDDEN), 0.1),
        'hy_f_w4': nrm(ks[24], (n_even, HY_HIDDEN, HY_ORDER * 2 * HY_WIDTH), 0.005),
        'hy_bias': nrm(ks[25], (n_even, HY_ORDER, HY_WIDTH), 0.5),
        'pool_w': nrm(ks[26], (n_odd, len(POOL_WINDOWS), POOL_GROUP, POOL_GROUP), POOL_GROUP ** -0.5),
        'pool_scale': 1.0 + nrm(ks[27], (n_odd, D_MODEL), 0.1),
        'final_gain': 1.0 + nrm(ks[28], (D_MODEL,), 0.1),
    }


def reference(x, c, ctx, c_ctx, w_mod, b_mod, ffn1_w1, ffn1_w3, ffn1_w2, ffn2_w1, ffn2_w3, ffn2_w2,
              ab_w_in, ab_w_out, ret_log_decay, hy_conv_w, hy_conv_b, hy_f_w1, hy_f_b1, hy_f_w2, hy_f_b2,
              hy_f_w3, hy_f_b3, hy_f_freq, hy_f_w4, hy_bias, pool_w, pool_scale, final_gain):
    b, n, d = x.shape
    n_ctx = ctx.shape[1]
    rot = axial_rotary(n, x.dtype)
    sc = jax.nn.silu(c)
    sc_ctx = jax.nn.silu(c_ctx)[None, :]
    for i in range(DEPTH):
        ctx_out = any(j % 2 == 0 for j in range(i + 1, DEPTH))
        ctx_in = (i % 2 == 0) or ctx_out
        mod = (sc @ w_mod[i] + b_mod[i]).reshape(b, N_MOD, 1, d)
        x = ffn_half(x, mod, 0, ffn1_w1[i], ffn1_w3[i], ffn1_w2[i])
        h = modulate(x, mod, 3)
        if ctx_in:
            mod_c = (sc_ctx @ w_mod[i] + b_mod[i]).reshape(1, N_MOD, 1, d)
            ctx = ffn_half(ctx, mod_c, 0, ffn1_w1[i], ffn1_w3[i], ffn1_w2[i])
            h_c = modulate(ctx, mod_c, 3)
        if i % 2 == 0:
            e = i // 2
            fparams = (hy_f_w1[e], hy_f_b1[e], hy_f_w2[e], hy_f_b2[e], hy_f_w3[e], hy_f_b3[e], hy_f_freq[e], hy_f_w4[e])
            if ctx_out:
                zero = jnp.zeros((ctx.shape[0], RET_HEADS, RET_HEAD_DIM, RET_HEAD_DIM), ctx.dtype)
                y_c, s_f, s_b = ab_mixer(h_c, ab_w_in[e], ab_w_out[e], ret_log_decay[e], hy_conv_w[e], hy_conv_b[e],
                                         hyena_filters(n_ctx, *fparams), hy_bias[e], None, zero, zero)
            else:
                heads_c = (ctx.shape[0], n_ctx, RET_HEADS, RET_HEAD_DIM)
                k_c = (h_c @ ab_w_in[e][:, RET_WIDTH:2 * RET_WIDTH]).reshape(heads_c) * RET_HEAD_DIM ** -0.5
                v_c = (h_c @ ab_w_in[e][:, 2 * RET_WIDTH:3 * RET_WIDTH]).reshape(heads_c)
                s_f, s_b = retention_final_states(k_c, v_c, ret_log_decay[e])
            y, _, _ = ab_mixer(h, ab_w_in[e], ab_w_out[e], ret_log_decay[e], hy_conv_w[e], hy_conv_b[e],
                               hyena_filters(n, *fparams), hy_bias[e], rot, s_f, s_b)
        else:
            o = i // 2
            y = pool_mixer(h, pool_w[o], pool_scale[o], True)
            if ctx_out:
                y_c = pool_mixer(h_c, pool_w[o], pool_scale[o], False)
        x = x + mod[:, 5] * y
        x = ffn_half(x, mod, 6, ffn2_w1[i], ffn2_w3[i], ffn2_w2[i])
        if ctx_out:
            ctx = ctx + mod_c[:, 5] * y_c
            ctx = ffn_half(ctx, mod_c, 6, ffn2_w1[i], ffn2_w3[i], ffn2_w2[i])
    return rms_norm(x) * final_gain
```

```python
import functools
import math

import numpy as np
import jax
import jax.numpy as jnp
from jax import lax
from jax.experimental import pallas as pl
from jax.experimental.pallas import tpu as pltpu

F32 = jnp.float32
BF16 = jnp.bfloat16
HIGHEST = lax.Precision.HIGHEST

EPS = 1e-6
GRID_W = 64
N_MOD = 9
RET_HEADS = 8
HEAD_DIM = 128
HY_ORDER = 2
HY_SHORT = 3
HY_EMB_DIM = 33
HY_DECAY_SHORT_PCT = 0.3
HY_DECAY_LONG_PCT = 1.5
HY_DECAY_TARGET = 1e-2
ROPE_BASE = 10000.0
POOL_WINDOWS = (2, 4, 8, 16)
RET_CHUNK = 128
LANE = 128
VMEM_LIMIT = 56 * 1024 * 1024


def _tile(dim, pref):
    if dim <= pref:
        return dim
    for t in range(pref - pref % LANE, 0, -LANE):
        if dim % t == 0:
            return t
    raise ValueError((dim, pref))


def _params(sem):
    return pltpu.CompilerParams(dimension_semantics=sem, vmem_limit_bytes=VMEM_LIMIT)


def _rms(x):
    return x * lax.rsqrt(jnp.mean(x * x, axis=-1, keepdims=True) + EPS)


def _silu(x):
    return x * jax.nn.sigmoid(x)


def _mod_kernel(c_ref, w_ref, b_ref, o_ref):
    sc = _silu(c_ref[...])
    o_ref[0] = jnp.dot(sc, w_ref[0], preferred_element_type=F32, precision=HIGHEST) + b_ref[0]


def _mod_call(c_all, w_mod, b_mod):
    depth, d, nd = w_mod.shape
    rows = c_all.shape[0]
    tn = _tile(nd, 1024)
    return pl.pallas_call(
        _mod_kernel,
        grid=(depth, nd // tn),
        in_specs=[
            pl.BlockSpec((rows, d), lambda l, j: (0, 0)),
            pl.BlockSpec((1, d, tn), lambda l, j: (l, 0, j)),
            pl.BlockSpec((1, 1, tn), lambda l, j: (l, 0, j)),
        ],
        out_specs=pl.BlockSpec((1, rows, tn), lambda l, j: (l, 0, j)),
        out_shape=jax.ShapeDtypeStruct((depth, rows, nd), F32),
        compiler_params=_params(("arbitrary", "arbitrary")),
        name="mod",
    )(c_all, w_mod, b_mod.reshape(depth, 1, nd))


def _ffn_kernel(*refs, n_f, emit_h, final):
    x_ref, mod_ref, w1_ref, w3_ref, w2_ref = refs[:5]
    pos = 5
    modn_ref = gain_ref = hn_ref = None
    if emit_h:
        modn_ref = refs[pos]
        pos += 1
    if final:
        gain_ref = refs[pos]
        pos += 1
    o_ref = refs[pos]
    pos += 1
    if emit_h:
        hn_ref = refs[pos]
        pos += 1
    h_scr, acc_scr = refs[pos], refs[pos + 1]
    f = pl.program_id(2)

    @pl.when(f == 0)
    def _():
        m = mod_ref[0]
        h = _rms(x_ref[0]) * (1.0 + m[1:2]) + m[0:1]
        h_scr[...] = h.astype(BF16)
        acc_scr[...] = jnp.zeros_like(acc_scr)

    h = h_scr[...]
    a = jnp.dot(h, w1_ref[...], preferred_element_type=F32)
    b = jnp.dot(h, w3_ref[...], preferred_element_type=F32)
    g = (_silu(a) * b).astype(BF16)
    acc_scr[...] += jnp.dot(g, w2_ref[...], preferred_element_type=F32)

    @pl.when(f == n_f - 1)
    def _():
        m = mod_ref[0]
        xo = x_ref[0] + 0.5 * m[2:3] * acc_scr[...]
        if emit_h:
            mn = modn_ref[0]
            hn_ref[0] = (_rms(xo) * (1.0 + mn[1:2]) + mn[0:1]).astype(BF16)
        if final:
            xo = _rms(xo) * gain_ref[...]
        o_ref[0] = xo


def _ffn_call(x, mod3, w1, w3, w2, modn=None, gain=None):
    bsz, s, d = x.shape
    dff = w1.shape[1]
    tm = _tile(s, 512)
    tf = _tile(dff, 512)
    n_f = dff // tf
    emit_h = modn is not None
    final = gain is not None
    in_specs = [
        pl.BlockSpec((1, tm, d), lambda b, i, f: (b, i, 0)),
        pl.BlockSpec((1, 3, d), lambda b, i, f: (b, 0, 0)),
        pl.BlockSpec((d, tf), lambda b, i, f: (0, f)),
        pl.BlockSpec((d, tf), lambda b, i, f: (0, f)),
        pl.BlockSpec((tf, d), lambda b, i, f: (f, 0)),
    ]
    args = [x, mod3, w1, w3, w2]
    if emit_h:
        in_specs.append(pl.BlockSpec((1, 3, d), lambda b, i, f: (b, 0, 0)))
        args.append(modn)
    if final:
        in_specs.append(pl.BlockSpec((1, d), lambda b, i, f: (0, 0)))
        args.append(gain.reshape(1, d))
    out_specs = [pl.BlockSpec((1, tm, d), lambda b, i, f: (b, i, 0))]
    out_shape = [jax.ShapeDtypeStruct((bsz, s, d), F32)]
    if emit_h:
        out_specs.append(pl.BlockSpec((1, tm, d), lambda b, i, f: (b, i, 0)))
        out_shape.append(jax.ShapeDtypeStruct((bsz, s, d), BF16))
    outs = pl.pallas_call(
        functools.partial(_ffn_kernel, n_f=n_f, emit_h=emit_h, final=final),
        grid=(bsz, s // tm, n_f),
        in_specs=in_specs,
        out_specs=out_specs,
        out_shape=out_shape,
        scratch_shapes=[pltpu.VMEM((tm, d), BF16), pltpu.VMEM((tm, d), F32)],
        compiler_params=_params(("arbitrary", "arbitrary", "arbitrary")),
        name="ffn",
    )(*args)
    return outs if emit_h else outs[0]


def _matmul_kernel(a_ref, b_ref, o_ref):
    o_ref[...] = jnp.dot(a_ref[...], b_ref[...], preferred_element_type=F32).astype(o_ref.dtype)


def _matmul_call(a, b, out_dtype, tm_pref=1024, tn_pref=1024):
    m, k = a.shape
    n = b.shape[1]
    tm = _tile(m, tm_pref)
    tn = _tile(n, tn_pref)
    return pl.pallas_call(
        _matmul_kernel,
        grid=(m // tm, n // tn),
        in_specs=[
            pl.BlockSpec((tm, k), lambda i, j: (i, 0)),
            pl.BlockSpec((k, tn), lambda i, j: (0, j)),
        ],
        out_specs=pl.BlockSpec((tm, tn), lambda i, j: (i, j)),
        out_shape=jax.ShapeDtypeStruct((m, n), out_dtype),
        compiler_params=_params(("arbitrary", "arbitrary")),
        name="matmul",
    )(a, b)


def _ret_kernel(lg_ref, q_ref, k_ref, v_ref, g_ref, cos_ref, sin_ref, kc_ref, vc_ref, o_ref,
                qr_scr, kr_scr, sb_scr, *, n, n_ctx, chunk):
    hd = HEAD_DIM
    head = pl.program_id(1)
    lgf = lg_ref[0, head]
    lgb = lg_ref[1, head]
    nc = n // chunk
    k_scale = hd ** -0.5

    posc = lax.broadcasted_iota(jnp.int32, (n_ctx, hd), 0).astype(F32)
    kc = kc_ref[0].astype(F32) * k_scale
    vc = vc_ref[0]
    kcf = (kc * jnp.exp((n_ctx - 1.0 - posc) * lgf)).T.astype(BF16)
    kcb = (kc * jnp.exp(posc * lgb)).T.astype(BF16)
    s0_f = jnp.dot(kcf, vc, preferred_element_type=F32)
    s0_b = jnp.dot(kcb, vc, preferred_element_type=F32)

    pos = lax.broadcasted_iota(jnp.int32, (chunk, hd), 0).astype(F32)
    qdec_f = jnp.exp((pos + 1.0) * lgf)
    kdec_f = jnp.exp((chunk - 1.0 - pos) * lgf)
    qdec_b = jnp.exp((chunk - pos) * lgb)
    kdec_b = jnp.exp(pos * lgb)
    cdec_f = jnp.exp(jnp.full((1, hd), chunk, F32) * lgf)
    cdec_b = jnp.exp(jnp.full((1, hd), chunk, F32) * lgb)
    ri = lax.broadcasted_iota(jnp.int32, (chunk, chunk), 0)
    ci = lax.broadcasted_iota(jnp.int32, (chunk, chunk), 1)
    diff = (ri - ci).astype(F32)
    dmat = (jnp.where(diff >= 0, jnp.exp(jnp.maximum(diff, 0.0) * lgf), 0.0)
            + jnp.where(diff <= 0, jnp.exp(jnp.maximum(-diff, 0.0) * lgb), 0.0))

    def rotary(x, rows):
        return x * cos_ref[rows, :] + pltpu.roll(x, hd // 2, 1) * sin_ref[rows, :]

    def back_body(j, sb):
        c = nc - 1 - j
        rows = pl.ds(pl.multiple_of(c * chunk, chunk), chunk)
        qr_scr[rows, :] = rotary(q_ref[0, rows, :].astype(F32), rows)
        k = rotary(k_ref[0, rows, :].astype(F32), rows) * k_scale
        kr_scr[rows, :] = k
        sb_scr[c] = sb
        upd = jnp.dot((k * kdec_b).T.astype(BF16), v_ref[0, rows, :], preferred_element_type=F32)
        return cdec_b * sb + upd

    lax.fori_loop(0, nc, back_body, s0_b)

    def fwd_body(c, sf):
        rows = pl.ds(pl.multiple_of(c * chunk, chunk), chunk)
        q = qr_scr[rows, :]
        k = kr_scr[rows, :]
        v = v_ref[0, rows, :]
        s = lax.dot_general(q.astype(BF16), k.astype(BF16), (((1,), (1,)), ((), ())),
                            preferred_element_type=F32) * dmat
        qq = jnp.concatenate([q * qdec_f, q * qdec_b], axis=1).astype(BF16)
        st = jnp.concatenate([sf, sb_scr[c]], axis=0).astype(BF16)
        o = (jnp.dot(s.astype(BF16), v, preferred_element_type=F32)
             + jnp.dot(qq, st, preferred_element_type=F32))
        g = g_ref[0, rows, :].astype(F32)
        o_ref[0, rows, :] = (_rms(o) * _silu(g)).astype(o_ref.dtype)
        upd = jnp.dot((k * kdec_f).T.astype(BF16), v, preferred_element_type=F32)
        return cdec_f * sf + upd

    lax.fori_loop(0, nc, fwd_body, s0_f)


def _ret_call(p, kvc, log_decay, cos2, sin2, ret_w):
    bsz, n, _ = p.shape
    n_ctx = kvc.shape[1]
    heads = ret_w // HEAD_DIM
    hd = HEAD_DIM
    chunk = _tile(n, RET_CHUNK)
    kern = functools.partial(_ret_kernel, n=n, n_ctx=n_ctx, chunk=chunk)
    col = lambda off: (lambda b, h, lg: (b, 0, off * heads + h))
    return pl.pallas_call(
        kern,
        grid_spec=pltpu.PrefetchScalarGridSpec(
            num_scalar_prefetch=1,
            grid=(bsz, heads),
            in_specs=[
                pl.BlockSpec((1, n, hd), col(0)),
                pl.BlockSpec((1, n, hd), col(1)),
                pl.BlockSpec((1, n, hd), col(2)),
                pl.BlockSpec((1, n, hd), col(3)),
                pl.BlockSpec((n, hd), lambda b, h, lg: (0, 0)),
                pl.BlockSpec((n, hd), lambda b, h, lg: (0, 0)),
                pl.BlockSpec((1, n_ctx, hd), col(0)),
                pl.BlockSpec((1, n_ctx, hd), col(1)),
            ],
            out_specs=pl.BlockSpec((1, n, hd), lambda b, h, lg: (b, 0, h)),
            scratch_shapes=[
                pltpu.VMEM((n, hd), F32),
                pltpu.VMEM((n, hd), F32),
                pltpu.VMEM((n // chunk, hd, hd), F32),
            ],
        ),
        out_shape=jax.ShapeDtypeStruct((bsz, n, ret_w), BF16),
        compiler_params=_params(("arbitrary", "arbitrary")),
        name="retention",
    )(log_decay, p, p, p, p, cos2, sin2, kvc, kvc)


def _filt_kernel(z_ref, w1_ref, b1_ref, w2_ref, b2_ref, w3_ref, b3_ref, fr_ref, w4_ref, dl_ref,
                 hs_ref, hd_ref, nyq_ref, *, n, tt, hw):
    i = pl.program_id(0)
    dot = functools.partial(jnp.dot, preferred_element_type=F32, precision=HIGHEST)
    fr = fr_ref[...]
    a = jnp.sin(fr[0:1] * (dot(z_ref[...], w1_ref[...]) + b1_ref[...]))
    a = jnp.sin(fr[1:2] * (dot(a, w2_ref[...]) + b2_ref[...]))
    a = jnp.sin(fr[2:3] * (dot(a, w3_ref[...]) + b3_ref[...]))
    hf = dot(a, w4_ref[...])
    ti = lax.broadcasted_iota(jnp.int32, (tt, hw), 0) + i * tt
    t = ti.astype(F32) * (1.0 / (n - 1.0))
    window = jnp.exp(-t * dl_ref[...])
    sign = (1 - 2 * (ti & 1)).astype(F32)

    @pl.when(i == 0)
    def _():
        nyq_ref[...] = jnp.zeros_like(nyq_ref)

    for o in range(HY_ORDER):
        h_f = hf[:, (2 * o) * hw:(2 * o + 1) * hw] * window
        h_b = jnp.where(ti == 0, 0.0, hf[:, (2 * o + 1) * hw:(2 * o + 2) * hw] * window)
        hsum = h_f + h_b
        hs_ref[:, o * hw:(o + 1) * hw] = hsum.astype(BF16)
        hd_ref[:, o * hw:(o + 1) * hw] = (h_f - h_b).astype(BF16)
        nyq_ref[:, o * hw:(o + 1) * hw] += jnp.sum(hsum * sign, axis=0, keepdims=True)


def _filt_call(n, fw1, fb1, fw2, fb2, fw3, fb3, freq, fw4, hw):
    hid = fw1.shape[1]
    pk = LANE
    bands = (HY_EMB_DIM - 1) // 2
    tpos = np.linspace(0.0, 1.0, n, dtype=np.float32)[:, None].astype(np.float64)
    fb = np.linspace(1e-4, bands - 1, bands, dtype=np.float32)[None, :].astype(np.float64)
    wv = (2.0 * math.pi * np.arange(n, dtype=np.float64)[:, None] / n)
    z = np.concatenate([tpos, np.cos(fb * wv), -np.sin(fb * wv)], axis=-1)
    z = np.pad(z, ((0, 0), (0, pk - HY_EMB_DIM))).astype(np.float32)
    max_decay = math.log(HY_DECAY_TARGET) / HY_DECAY_SHORT_PCT
    min_decay = math.log(HY_DECAY_TARGET) / HY_DECAY_LONG_PCT
    deltas = np.abs(np.linspace(min_decay, max_decay, hw, dtype=np.float32)).reshape(1, hw)

    padc = lambda w: jnp.pad(w, ((0, 0), (0, pk - hid)))
    w1 = jnp.pad(fw1, ((0, pk - HY_EMB_DIM), (0, pk - hid)))
    w2 = jnp.pad(fw2, ((0, pk - hid), (0, pk - hid)))
    w3 = jnp.pad(fw3, ((0, pk - hid), (0, pk - hid)))
    w4 = jnp.pad(fw4, ((0, pk - hid), (0, 0)))
    b1, b2, b3 = (padc(b.reshape(1, hid)) for b in (fb1, fb2, fb3))
    fr = jnp.pad(freq, ((0, 8 - freq.shape[0]), (0, pk - hid)))
    tt = _tile(n, 256)
    nw = HY_ORDER * hw
    full = lambda shape: pl.BlockSpec(shape, lambda i: (0, 0))
    return pl.pallas_call(
        functools.partial(_filt_kernel, n=n, tt=tt, hw=hw),
        grid=(n // tt,),
        in_specs=[
            pl.BlockSpec((tt, pk), lambda i: (i, 0)),
            full((pk, pk)), full((1, pk)), full((pk, pk)), full((1, pk)),
            full((pk, pk)), full((1, pk)), full((8, pk)), full((pk, 2 * nw)), full((1, hw)),
        ],
        out_specs=[
            pl.BlockSpec((tt, nw), lambda i: (i, 0)),
            pl.BlockSpec((tt, nw), lambda i: (i, 0)),
            pl.BlockSpec((1, nw), lambda i: (0, 0)),
        ],
        out_shape=[
            jax.ShapeDtypeStruct((n, nw), BF16),
            jax.ShapeDtypeStruct((n, nw), BF16),
            jax.ShapeDtypeStruct((1, nw), F32),
        ],
        compiler_params=_params(("arbitrary",)),
        name="hyena_filters",
    )(jnp.asarray(z), w1, b1, w2, b2, w3, b3, fr, w4, jnp.asarray(deltas))


def _dftgen_kernel(ac_ref, as_ref, bc_ref, bs_ref, c_ref, sf_ref, si_ref, *, rows, n):
    i = pl.program_id(0)
    ac, as_ = ac_ref[0], as_ref[0]
    bc, bs = bc_ref[...], bs_ref[...]
    c = ac * bc - as_ * bs
    s = as_ * bc + ac * bs
    k = lax.broadcasted_iota(jnp.int32, (rows, n), 0) + i * rows
    t = lax.broadcasted_iota(jnp.int32, (rows, n), 1)
    sign = (1 - 2 * ((k + t) & 1)).astype(F32)
    c_ref[...] = c.astype(BF16)
    sf_ref[...] = jnp.where(k == 0, sign, s).astype(BF16)
    si_ref[...] = jnp.where(t == 0, sign, s).astype(BF16)


def _dftgen_call(n):
    rows = 64
    assert n % rows == 0
    big = 2 * n
    t = np.arange(n, dtype=np.int64)[None, :]
    khi = (np.arange(n // rows, dtype=np.int64) * rows)[:, None]
    klo = np.arange(rows, dtype=np.int64)[:, None]
    ang_a = 2.0 * math.pi * ((khi * t) % big) / big
    ang_b = 2.0 * math.pi * ((klo * t) % big) / big
    a_c = jnp.asarray(np.cos(ang_a).astype(np.float32).reshape(n // rows, 1, n))
    a_s = jnp.asarray(np.sin(ang_a).astype(np.float32).reshape(n // rows, 1, n))
    b_c = jnp.asarray(np.cos(ang_b).astype(np.float32))
    b_s = jnp.asarray(np.sin(ang_b).astype(np.float32))
    mat = jax.ShapeDtypeStruct((n, n), BF16)
    return pl.pallas_call(
        functools.partial(_dftgen_kernel, rows=rows, n=n),
        grid=(n // rows,),
        in_specs=[
            pl.BlockSpec((1, 1, n), lambda i: (i, 0, 0)),
            pl.BlockSpec((1, 1, n), lambda i: (i, 0, 0)),
            pl.BlockSpec((rows, n), lambda i: (0, 0)),
            pl.BlockSpec((rows, n), lambda i: (0, 0)),
        ],
        out_specs=[pl.BlockSpec((rows, n), lambda i: (i, 0))] * 3,
        out_shape=[mat, mat, mat],
        compiler_params=_params(("arbitrary",)),
        name="dft_matrices",
    )(a_c, a_s, b_c, b_s)


def _sconv_kernel(p_ref, w_ref, b_ref, o_ref, *, n, rc):
    w = w_ref[...]
    bias = b_ref[...]
    row = lax.broadcasted_iota(jnp.int32, (rc, p_ref.shape[2]), 0)
    for c in range(n // rc):
        cur = p_ref[0, c * rc:(c + 1) * rc, :].astype(F32)
        up = pltpu.roll(cur, 1, 0)
        dn = pltpu.roll(cur, rc - 1, 0)
        if c > 0:
            prev = p_ref[0, c * rc - 16:c * rc, :].astype(F32)[15:16]
        else:
            prev = jnp.zeros((1, cur.shape[1]), F32)
        if c < n // rc - 1:
            nxt = p_ref[0, (c + 1) * rc:(c + 1) * rc + 16, :].astype(F32)[0:1]
        else:
            nxt = jnp.zeros((1, cur.shape[1]), F32)
        up = jnp.where(row == 0, prev, up)
        dn = jnp.where(row == rc - 1, nxt, dn)
        out = bias + up * w[0:1] + cur * w[1:2] + dn * w[2:3]
        o_ref[0, c * rc:(c + 1) * rc, :] = out.astype(o_ref.dtype)


def _sconv_call(p, conv_w, conv_b, col0):
    bsz, n, _ = p.shape
    width = conv_w.shape[1]
    tc = _tile(width, 512)
    rc = _tile(n, 512)
    off = col0 // tc
    return pl.pallas_call(
        functools.partial(_sconv_kernel, n=n, rc=rc),
        grid=(bsz, width // tc),
        in_specs=[
            pl.BlockSpec((1, n, tc), lambda b, j: (b, 0, off + j)),
            pl.BlockSpec((HY_SHORT, tc), lambda b, j: (0, j)),
            pl.BlockSpec((1, tc), lambda b, j: (0, j)),
        ],
        out_specs=pl.BlockSpec((1, n, tc), lambda b, j: (b, 0, j)),
        out_shape=jax.ShapeDtypeStruct((bsz, n, width), BF16),
        compiler_params=_params(("arbitrary", "arbitrary")),
        name="short_conv",
    )(p, conv_w, conv_b.reshape(1, width))


def _dft_fwd_kernel(c_ref, s_ref, u_ref, ck_ref, sk_ref, nyq_ref, p_ref, q_ref, *, tm):
    i = pl.program_id(0)
    u = u_ref[0]
    cu = jnp.dot(c_ref[...], u, preferred_element_type=F32)
    su = jnp.dot(s_ref[...], u, preferred_element_type=F32)
    ck = ck_ref[...]
    sk = sk_ref[...]
    k = lax.broadcasted_iota(jnp.int32, cu.shape, 0) + i * tm
    p = jnp.where(k == 0, 0.5 * cu * ck, cu * ck - su * sk)
    q = jnp.where(k == 0, 0.5 * su * nyq_ref[...], cu * sk + su * ck)
    p_ref[0] = p.astype(p_ref.dtype)
    q_ref[0] = q.astype(q_ref.dtype)


def _dft_fwd_call(cmat, sfmat, u, u_col0, ck, sk, nyq, k_col0, hw):
    bsz, n, _ = u.shape
    tm = _tile(n, 512)
    tn = _tile(hw, 512)
    uo, ko = u_col0 // tn, k_col0 // tn
    return pl.pallas_call(
        functools.partial(_dft_fwd_kernel, tm=tm),
        grid=(n // tm, bsz, hw // tn),
        in_specs=[
            pl.BlockSpec((tm, n), lambda i, b, j: (i, 0)),
            pl.BlockSpec((tm, n), lambda i, b, j: (i, 0)),
            pl.BlockSpec((1, n, tn), lambda i, b, j: (b, 0, uo + j)),
            pl.BlockSpec((tm, tn), lambda i, b, j: (i, ko + j)),
            pl.BlockSpec((tm, tn), lambda i, b, j: (i, ko + j)),
            pl.BlockSpec((1, tn), lambda i, b, j: (0, ko + j)),
        ],
        out_specs=[pl.BlockSpec((1, tm, tn), lambda i, b, j: (b, i, j))] * 2,
        out_shape=[jax.ShapeDtypeStruct((bsz, n, hw), BF16)] * 2,
        compiler_params=_params(("arbitrary", "arbitrary", "arbitrary")),
        name="dft_forward",
    )(cmat, sfmat, u, ck, sk, nyq)


def _dft_inv_kernel(c_ref, s_ref, p_ref, q_ref, u_ref, gate_ref, bias_ref, o_ref, *, scale):
    y = (jnp.dot(c_ref[...], p_ref[0], preferred_element_type=F32)
         + jnp.dot(s_ref[...], q_ref[0], preferred_element_type=F32)) * scale
    y = y + u_ref[0].astype(F32) * bias_ref[...]
    o_ref[0] = (gate_ref[0].astype(F32) * y).astype(o_ref.dtype)


def _dft_inv_call(cmat, simat, p, q, u, u_col0, gate, gate_col0, bias):
    bsz, n, hw = p.shape
    tm = _tile(n, 512)
    tn = _tile(hw, 512)
    uo, go = u_col0 // tn, gate_col0 // tn
    return pl.pallas_call(
        functools.partial(_dft_inv_kernel, scale=1.0 / n),
        grid=(n // tm, bsz, hw // tn),
        in_specs=[
            pl.BlockSpec((tm, n), lambda i, b, j: (i, 0)),
            pl.BlockSpec((tm, n), lambda i, b, j: (i, 0)),
            pl.BlockSpec((1, n, tn), lambda i, b, j: (b, 0, j)),
            pl.BlockSpec((1, n, tn), lambda i, b, j: (b, 0, j)),
            pl.BlockSpec((1, tm, tn), lambda i, b, j: (b, i, uo + j)),
            pl.BlockSpec((1, tm, tn), lambda i, b, j: (b, i, go + j)),
            pl.BlockSpec((1, tn), lambda i, b, j: (0, j)),
        ],
        out_specs=pl.BlockSpec((1, tm, tn), lambda i, b, j: (b, i, j)),
        out_shape=jax.ShapeDtypeStruct((bsz, n, hw), BF16),
        compiler_params=_params(("arbitrary", "arbitrary", "arbitrary")),
        name="dft_inverse",
    )(cmat, simat, p, q, u, gate, bias.reshape(1, hw))


def _wout_kernel(r_ref, y_ref, wa_ref, wb_ref, x_ref, gate_ref, o_ref):
    y = (jnp.dot(r_ref[0], wa_ref[...], preferred_element_type=F32)
         + jnp.dot(y_ref[0], wb_ref[...], preferred_element_type=F32))
    o_ref[0] = x_ref[0] + gate_ref[0] * y


def _wout_call(r, yh, w_out, x, gate):
    bsz, n, d = x.shape
    rw = r.shape[2]
    tm = _tile(n, 512)
    return pl.pallas_call(
        _wout_kernel,
        grid=(bsz, n // tm),
        in_specs=[
            pl.BlockSpec((1, tm, rw), lambda b, i: (b, i, 0)),
            pl.BlockSpec((1, tm, d - rw), lambda b, i: (b, i, 0)),
            pl.BlockSpec((rw, d), lambda b, i: (0, 0)),
            pl.BlockSpec((d - rw, d), lambda b, i: (0, 0)),
            pl.BlockSpec((1, tm, d), lambda b, i: (b, i, 0)),
            pl.BlockSpec((1, 1, d), lambda b, i: (b, 0, 0)),
        ],
        out_specs=pl.BlockSpec((1, tm, d), lambda b, i: (b, i, 0)),
        out_shape=jax.ShapeDtypeStruct((bsz, n, d), F32),
        compiler_params=_params(("arbitrary", "arbitrary")),
        name="mixer_out",
    )(r, yh, w_out[:rw], w_out[rw:], x, gate)


def _pool_kernel(h_ref, m_ref, ic_ref, w_ref, sc_ref, x_ref, gate_ref, o_ref, *, tm, nt):
    i = pl.program_id(2)
    rows_c = pl.ds(pl.multiple_of(i * tm, tm), tm)
    rows_l = pl.ds(pl.multiple_of(jnp.maximum(i - 1, 0) * tm, tm), tm)
    rows_r = pl.ds(pl.multiple_of(jnp.minimum(i + 1, nt - 1) * tm, tm), tm)
    hc = h_ref[0, rows_c, :]
    acc = jnp.dot(m_ref[0, :, tm:2 * tm], hc, preferred_element_type=F32)
    left = jnp.dot(m_ref[0, :, 0:tm], h_ref[0, rows_l, :], preferred_element_type=F32)
    right = jnp.dot(m_ref[0, :, 2 * tm:3 * tm], h_ref[0, rows_r, :], preferred_element_type=F32)
    acc = acc + jnp.where(i > 0, left, 0.0) + jnp.where(i < nt - 1, right, 0.0)
    pooled = acc * ic_ref[0] - hc.astype(F32)
    y = jnp.dot(pooled.astype(BF16), w_ref[0], preferred_element_type=F32) * sc_ref[...]
    o_ref[0] = x_ref[0] + gate_ref[0] * y


def _pool_tables(n, tm):
    rows_per_tile = tm // GRID_W
    n_rows = n // GRID_W
    tok = np.arange(tm)
    r_out, c_out = tok // GRID_W, tok % GRID_W
    tin = np.arange(3 * tm)
    r_in, c_in = tin // GRID_W - rows_per_tile, tin % GRID_W
    mats, inv = [], []
    for win in POOL_WINDOWS:
        lo, hi = -(win // 2), win - 1 - win // 2
        dr = r_in[None, :] - r_out[:, None]
        dc = c_in[None, :] - c_out[:, None]
        mats.append(((dr >= lo) & (dr <= hi) & (dc >= lo) & (dc <= hi)).astype(np.float32))
        idx_r, idx_c = np.arange(n_rows), np.arange(GRID_W)
        cnt_r = np.minimum(idx_r + hi, n_rows - 1) + 1 - np.maximum(idx_r + lo, 0)
        cnt_c = np.minimum(idx_c + hi, GRID_W - 1) + 1 - np.maximum(idx_c + lo, 0)
        inv.append((1.0 / (cnt_r[:, None] * cnt_c[None, :]).astype(np.float64)).reshape(n, 1))
    return (jnp.asarray(np.stack(mats), dtype=BF16),
            jnp.asarray(np.stack(inv).astype(np.float32)))


def _pool_call(h, pool_w, pool_scale, x, gate):
    bsz, n, d = x.shape
    ng = len(POOL_WINDOWS)
    pg = d // ng
    tm = _tile(n, 512)
    nt = n // tm
    assert tm % GRID_W == 0 and tm // GRID_W >= max(POOL_WINDOWS) // 2
    mats, inv = _pool_tables(n, tm)
    return pl.pallas_call(
        functools.partial(_pool_kernel, tm=tm, nt=nt),
        grid=(bsz, ng, nt),
        in_specs=[
            pl.BlockSpec((1, n, pg), lambda b, g, i: (b, 0, g)),
            pl.BlockSpec((1, tm, 3 * tm), lambda b, g, i: (g, 0, 0)),
            pl.BlockSpec((1, tm, 1), lambda b, g, i: (g, i, 0)),
            pl.BlockSpec((1, pg, pg), lambda b, g, i: (g, 0, 0)),
            pl.BlockSpec((1, pg), lambda b, g, i: (0, g)),
            pl.BlockSpec((1, tm, pg), lambda b, g, i: (b, i, g)),
            pl.BlockSpec((1, 1, pg), lambda b, g, i: (b, 0, g)),
        ],
        out_specs=pl.BlockSpec((1, tm, pg), lambda b, g, i: (b, i, g)),
        out_shape=jax.ShapeDtypeStruct((bsz, n, d), F32),
        compiler_params=_params(("arbitrary", "arbitrary", "arbitrary")),
        name="pool_mixer",
    )(h, mats, inv, pool_w, pool_scale.reshape(1, d), x, gate)


def _rotary_tables(n):
    pos = np.arange(n)
    row = (pos // GRID_W).astype(np.float32)
    col = (pos % GRID_W).astype(np.float32)
    n_freq = HEAD_DIM // 4
    inv = (np.float32(ROPE_BASE) ** (-np.arange(n_freq, dtype=np.float32) / np.float32(n_freq))).astype(np.float32)
    ang = np.concatenate([row[:, None] * inv, col[:, None] * inv], axis=-1).astype(np.float32)
    cos, sin = np.cos(ang.astype(np.float64)), np.sin(ang.astype(np.float64))
    cos2 = np.concatenate([cos, cos], axis=-1).astype(np.float32)
    sin2 = np.concatenate([-sin, sin], axis=-1).astype(np.float32)
    return jnp.asarray(cos2), jnp.asarray(sin2)


def kernel(x, c, ctx, c_ctx, w_mod, b_mod, ffn1_w1, ffn1_w3, ffn1_w2, ffn2_w1, ffn2_w3, ffn2_w2, ab_w_in, ab_w_out, ret_log_decay, hy_conv_w, hy_conv_b, hy_f_w1, hy_f_b1, hy_f_w2, hy_f_b2, hy_f_w3, hy_f_b3, hy_f_freq, hy_f_w4, hy_bias, pool_w, pool_scale, final_gain):
    bsz, n, d = x.shape
    n_ctx = ctx.shape[1]
    depth = w_mod.shape[0]
    assert depth == 2, "layer 0 = retention/Hyena mixer, layer 1 = pooling mixer"
    ret_w = RET_HEADS * HEAD_DIM
    hw = d - ret_w
    assert hy_f_w4.shape[2] == HY_ORDER * 2 * hw

    c_all = jnp.concatenate([c, c_ctx[None, :], jnp.zeros((8 - bsz - 1, d), F32)], axis=0)
    mod = _mod_call(c_all, w_mod, b_mod).reshape(depth, 8, N_MOD, d)

    def mod_rows(layer, base, count=3):
        return mod[layer, :bsz, base:base + count]

    def mod_ctx(base, count=3):
        return jnp.broadcast_to(mod[0, bsz:bsz + 1, base:base + count], (bsz, count, d))

    bf = lambda w: w.astype(BF16)

    w1, w3, w2 = bf(ffn1_w1[0]), bf(ffn1_w3[0]), bf(ffn1_w2[0])
    x, h = _ffn_call(x, mod_rows(0, 0), w1, w3, w2, modn=mod_rows(0, 3))
    _, h_c = _ffn_call(ctx, mod_ctx(0), w1, w3, w2, modn=mod_ctx(3))

    w_in = bf(ab_w_in[0])
    p = _matmul_call(h.reshape(bsz * n, d), w_in, BF16).reshape(bsz, n, -1)
    kvc = _matmul_call(h_c.reshape(bsz * n_ctx, d), w_in[:, ret_w:3 * ret_w], BF16)
    kvc = kvc.reshape(bsz, n_ctx, 2 * ret_w)

    cos2, sin2 = _rotary_tables(n)
    r = _ret_call(p, kvc, ret_log_decay[0], cos2, sin2, ret_w)

    hsum, hdiff, nyq = _filt_call(n, hy_f_w1[0], hy_f_b1[0], hy_f_w2[0], hy_f_b2[0], hy_f_w3[0],
                                  hy_f_b3[0], hy_f_freq[0], hy_f_w4[0], hw)
    cmat, sfmat, simat = _dftgen_call(n)
    ck = _matmul_call(cmat, hsum, F32, 512, 512)
    sk = _matmul_call(sfmat, hdiff, F32, 512, 512)
    u = _sconv_call(p, hy_conv_w[0], hy_conv_b[0], 4 * ret_w)
    pq = _dft_fwd_call(cmat, sfmat, u, 0, ck, sk, nyq, 0, hw)
    z = _dft_inv_call(cmat, simat, pq[0], pq[1], u, 0, u, hw, hy_bias[0, 0])
    pq = _dft_fwd_call(cmat, sfmat, z, 0, ck, sk, nyq, hw, hw)
    yh = _dft_inv_call(cmat, simat, pq[0], pq[1], z, 0, u, 2 * hw, hy_bias[0, 1])

    x = _wout_call(r, yh, bf(ab_w_out[0]), x, mod_rows(0, 5, 1))
    x = _ffn_call(x, mod_rows(0, 6), bf(ffn2_w1[0]), bf(ffn2_w3[0]), bf(ffn2_w2[0]))

    x, h = _ffn_call(x, mod_rows(1, 0), bf(ffn1_w1[1]), bf(ffn1_w3[1]), bf(ffn1_w2[1]),
                     modn=mod_rows(1, 3))
    x = _pool_call(h, bf(pool_w[0]), pool_scale[0], x, mod_rows(1, 5, 1))
    return _ffn_call(x, mod_rows(1, 6), bf(ffn2_w1[1]), bf(ffn2_w3[1]), bf(ffn2_w2[1]),
                     gain=final_gain)
```

```python
import functools
import math

import numpy as np
import jax
import jax.numpy as jnp
from jax import lax
from jax.experimental import pallas as pl
from jax.experimental.pallas import tpu as pltpu

F32 = jnp.float32
BF16 = jnp.bfloat16
HIGHEST = lax.Precision.HIGHEST

EPS = 1e-6
GRID_W = 64
N_MOD = 9
RET_HEADS = 8
HEAD_DIM = 128
HY_ORDER = 2
HY_SHORT = 3
HY_EMB_DIM = 33
HY_DECAY_SHORT_PCT = 0.3
HY_DECAY_LONG_PCT = 1.5
HY_DECAY_TARGET = 1e-2
ROPE_BASE = 10000.0
POOL_WINDOWS = (2, 4, 8, 16)
RET_CHUNK = 256
RET_UNROLL = 4
LANE = 128
VMEM_LIMIT = 56 * 1024 * 1024


def _tile(dim, pref):
    if dim <= pref:
        return dim
    for t in range(pref - pref % LANE, 0, -LANE):
        if dim % t == 0:
            return t
    raise ValueError((dim, pref))


def _params(sem):
    return pltpu.CompilerParams(dimension_semantics=sem, vmem_limit_bytes=VMEM_LIMIT)


def _rms(x):
    return x * lax.rsqrt(jnp.mean(x * x, axis=-1, keepdims=True) + EPS)


def _silu(x):
    return x * jax.nn.sigmoid(x)


def _mod_kernel(c_ref, w_ref, b_ref, o_ref):
    sc = _silu(c_ref[...])
    o_ref[0] = jnp.dot(sc, w_ref[0], preferred_element_type=F32, precision=HIGHEST) + b_ref[0]


def _mod_call(c_all, w_mod, b_mod):
    depth, d, nd = w_mod.shape
    rows = c_all.shape[0]
    tn = _tile(nd, 1024)
    return pl.pallas_call(
        _mod_kernel,
        grid=(depth, nd // tn),
        in_specs=[
            pl.BlockSpec((rows, d), lambda l, j: (0, 0)),
            pl.BlockSpec((1, d, tn), lambda l, j: (l, 0, j)),
            pl.BlockSpec((1, 1, tn), lambda l, j: (l, 0, j)),
        ],
        out_specs=pl.BlockSpec((1, rows, tn), lambda l, j: (l, 0, j)),
        out_shape=jax.ShapeDtypeStruct((depth, rows, nd), F32),
        compiler_params=_params(("arbitrary", "arbitrary")),
        name="mod",
    )(c_all, w_mod, b_mod.reshape(depth, 1, nd))


def _ffn_kernel(*refs, n_f, emit_h, final):
    x_ref, mod_ref, w1_ref, w3_ref, w2_ref = refs[:5]
    pos = 5
    modn_ref = gain_ref = hn_ref = None
    if emit_h:
        modn_ref = refs[pos]
        pos += 1
    if final:
        gain_ref = refs[pos]
        pos += 1
    o_ref = refs[pos]
    pos += 1
    if emit_h:
        hn_ref = refs[pos]
        pos += 1
    h_scr, acc_scr = refs[pos], refs[pos + 1]
    f = pl.program_id(2)

    @pl.when(f == 0)
    def _():
        m = mod_ref[0]
        h = _rms(x_ref[0]) * (1.0 + m[1:2]) + m[0:1]
        h_scr[...] = h.astype(BF16)
        acc_scr[...] = jnp.zeros_like(acc_scr)

    h = h_scr[...]
    a = jnp.dot(h, w1_ref[...], preferred_element_type=F32)
    b = jnp.dot(h, w3_ref[...], preferred_element_type=F32)
    g = (_silu(a) * b).astype(BF16)
    acc_scr[...] += jnp.dot(g, w2_ref[...], preferred_element_type=F32)

    @pl.when(f == n_f - 1)
    def _():
        m = mod_ref[0]
        xo = x_ref[0] + 0.5 * m[2:3] * acc_scr[...]
        if emit_h:
            mn = modn_ref[0]
            hn_ref[0] = (_rms(xo) * (1.0 + mn[1:2]) + mn[0:1]).astype(BF16)
        if final:
            xo = _rms(xo) * gain_ref[...]
        o_ref[0] = xo


def _ffn_call(x, mod3, w1, w3, w2, modn=None, gain=None):
    bsz, s, d = x.shape
    dff = w1.shape[1]
    tm = _tile(s, 512)
    tf = _tile(dff, 512)
    n_f = dff // tf
    emit_h = modn is not None
    final = gain is not None
    in_specs = [
        pl.BlockSpec((1, tm, d), lambda b, i, f: (b, i, 0)),
        pl.BlockSpec((1, 3, d), lambda b, i, f: (b, 0, 0)),
        pl.BlockSpec((d, tf), lambda b, i, f: (0, f)),
        pl.BlockSpec((d, tf), lambda b, i, f: (0, f)),
        pl.BlockSpec((tf, d), lambda b, i, f: (f, 0)),
    ]
    args = [x, mod3, w1, w3, w2]
    if emit_h:
        in_specs.append(pl.BlockSpec((1, 3, d), lambda b, i, f: (b, 0, 0)))
        args.append(modn)
    if final:
        in_specs.append(pl.BlockSpec((1, d), lambda b, i, f: (0, 0)))
        args.append(gain.reshape(1, d))
    out_specs = [pl.BlockSpec((1, tm, d), lambda b, i, f: (b, i, 0))]
    out_shape = [jax.ShapeDtypeStruct((bsz, s, d), F32)]
    if emit_h:
        out_specs.append(pl.BlockSpec((1, tm, d), lambda b, i, f: (b, i, 0)))
        out_shape.append(jax.ShapeDtypeStruct((bsz, s, d), BF16))
    outs = pl.pallas_call(
        functools.partial(_ffn_kernel, n_f=n_f, emit_h=emit_h, final=final),
        grid=(bsz, s // tm, n_f),
        in_specs=in_specs,
        out_specs=out_specs,
        out_shape=out_shape,
        scratch_shapes=[pltpu.VMEM((tm, d), BF16), pltpu.VMEM((tm, d), F32)],
        compiler_params=_params(("arbitrary", "arbitrary", "arbitrary")),
        name="ffn",
    )(*args)
    return outs if emit_h else outs[0]


def _matmul_kernel(a_ref, b_ref, o_ref):
    o_ref[...] = jnp.dot(a_ref[...], b_ref[...], preferred_element_type=F32).astype(o_ref.dtype)


def _matmul_call(a, b, out_dtype, tm_pref=1024, tn_pref=1024):
    m, k = a.shape
    n = b.shape[1]
    tm = _tile(m, tm_pref)
    tn = _tile(n, tn_pref)
    return pl.pallas_call(
        _matmul_kernel,
        grid=(m // tm, n // tn),
        in_specs=[
            pl.BlockSpec((tm, k), lambda i, j: (i, 0)),
            pl.BlockSpec((k, tn), lambda i, j: (0, j)),
        ],
        out_specs=pl.BlockSpec((tm, tn), lambda i, j: (i, j)),
        out_shape=jax.ShapeDtypeStruct((m, n), out_dtype),
        compiler_params=_params(("arbitrary", "arbitrary")),
        name="matmul",
    )(a, b)


def _ret_kernel(lg_ref, q_ref, k_ref, v_ref, g_ref, cos_ref, sin_ref, kc_ref, vc_ref, o_ref,
                qr_scr, kr_scr, upd_scr, st_scr, *, n, n_ctx, chunk):
    hd = HEAD_DIM
    head = pl.program_id(1)
    lgf = lg_ref[0, head]
    lgb = lg_ref[1, head]
    nc = n // chunk
    k_scale = hd ** -0.5

    posc = lax.broadcasted_iota(jnp.int32, (n_ctx, hd), 0).astype(F32)
    kc = kc_ref[0].astype(F32) * k_scale
    vc = vc_ref[0]
    kcf = (kc * jnp.exp((n_ctx - 1.0 - posc) * lgf)).T.astype(BF16)
    kcb = (kc * jnp.exp(posc * lgb)).T.astype(BF16)
    s0_f = jnp.dot(kcf, vc, preferred_element_type=F32)
    s0_b = jnp.dot(kcb, vc, preferred_element_type=F32)

    pos = lax.broadcasted_iota(jnp.int32, (chunk, hd), 0).astype(F32)
    qdec_f = jnp.exp((pos + 1.0) * lgf)
    kdec_f = jnp.exp((chunk - 1.0 - pos) * lgf)
    qdec_b = jnp.exp((chunk - pos) * lgb)
    kdec_b = jnp.exp(pos * lgb)
    cdec_f = jnp.exp(jnp.full((1, hd), chunk, F32) * lgf)
    cdec_b = jnp.exp(jnp.full((1, hd), chunk, F32) * lgb)
    ri = lax.broadcasted_iota(jnp.int32, (chunk, chunk), 0)
    ci = lax.broadcasted_iota(jnp.int32, (chunk, chunk), 1)
    diff = (ri - ci).astype(F32)
    dmat = (jnp.where(diff >= 0, jnp.exp(jnp.maximum(diff, 0.0) * lgf), 0.0)
            + jnp.where(diff <= 0, jnp.exp(jnp.maximum(-diff, 0.0) * lgb), 0.0))

    def rotary(x, rows):
        return x * cos_ref[rows, :] + pltpu.roll(x, hd // 2, 1) * sin_ref[rows, :]

    def chunk_rows(c):
        return pl.ds(pl.multiple_of(c * chunk, chunk), chunk)

    def prep_body(c, carry):
        rows = chunk_rows(c)
        qr_scr[rows, :] = rotary(q_ref[0, rows, :].astype(F32), rows)
        k = rotary(k_ref[0, rows, :].astype(F32), rows) * k_scale
        kr_scr[rows, :] = k
        kk = jnp.concatenate([k * kdec_f, k * kdec_b], axis=1).T.astype(BF16)
        upd_scr[c] = jnp.dot(kk, v_ref[0, rows, :], preferred_element_type=F32)
        return carry

    lax.fori_loop(0, nc, prep_body, 0, unroll=math.gcd(nc, RET_UNROLL))

    def fwd_scan(c, sf):
        st_scr[c, 0:hd, :] = sf.astype(BF16)
        return cdec_f * sf + upd_scr[c, 0:hd, :]

    def bwd_scan(j, sb):
        c = nc - 1 - j
        st_scr[c, hd:2 * hd, :] = sb.astype(BF16)
        return cdec_b * sb + upd_scr[c, hd:2 * hd, :]

    lax.fori_loop(0, nc, fwd_scan, s0_f, unroll=math.gcd(nc, RET_UNROLL))
    lax.fori_loop(0, nc, bwd_scan, s0_b, unroll=math.gcd(nc, RET_UNROLL))

    def out_body(c, carry):
        rows = chunk_rows(c)
        q = qr_scr[rows, :]
        s = lax.dot_general(q.astype(BF16), kr_scr[rows, :].astype(BF16), (((1,), (1,)), ((), ())),
                            preferred_element_type=F32) * dmat
        qq = jnp.concatenate([q * qdec_f, q * qdec_b], axis=1).astype(BF16)
        o = (jnp.dot(s.astype(BF16), v_ref[0, rows, :], preferred_element_type=F32)
             + jnp.dot(qq, st_scr[c], preferred_element_type=F32))
        g = g_ref[0, rows, :].astype(F32)
        o_ref[0, rows, :] = (_rms(o) * _silu(g)).astype(o_ref.dtype)
        return carry

    lax.fori_loop(0, nc, out_body, 0, unroll=math.gcd(nc, RET_UNROLL))


def _ret_call(p, kvc, log_decay, cos2, sin2, ret_w):
    bsz, n, _ = p.shape
    n_ctx = kvc.shape[1]
    heads = ret_w // HEAD_DIM
    hd = HEAD_DIM
    chunk = _tile(n, RET_CHUNK)
    kern = functools.partial(_ret_kernel, n=n, n_ctx=n_ctx, chunk=chunk)
    col = lambda off: (lambda b, h, lg: (b, 0, off * heads + h))
    return pl.pallas_call(
        kern,
        grid_spec=pltpu.PrefetchScalarGridSpec(
            num_scalar_prefetch=1,
            grid=(bsz, heads),
            in_specs=[
                pl.BlockSpec((1, n, hd), col(0)),
                pl.BlockSpec((1, n, hd), col(1)),
                pl.BlockSpec((1, n, hd), col(2)),
                pl.BlockSpec((1, n, hd), col(3)),
                pl.BlockSpec((n, hd), lambda b, h, lg: (0, 0)),
                pl.BlockSpec((n, hd), lambda b, h, lg: (0, 0)),
                pl.BlockSpec((1, n_ctx, hd), col(0)),
                pl.BlockSpec((1, n_ctx, hd), col(1)),
            ],
            out_specs=pl.BlockSpec((1, n, hd), lambda b, h, lg: (b, 0, h)),
            scratch_shapes=[
                pltpu.VMEM((n, hd), F32),
                pltpu.VMEM((n, hd), F32),
                pltpu.VMEM((n // chunk, 2 * hd, hd), F32),
                pltpu.VMEM((n // chunk, 2 * hd, hd), BF16),
            ],
        ),
        out_shape=jax.ShapeDtypeStruct((bsz, n, ret_w), BF16),
        compiler_params=_params(("arbitrary", "arbitrary")),
        name="retention",
    )(log_decay, p, p, p, p, cos2, sin2, kvc, kvc)


def _filt_kernel(z_ref, w1_ref, b1_ref, w2_ref, b2_ref, w3_ref, b3_ref, fr_ref, w4_ref, dl_ref,
                 hs_ref, hd_ref, nyq_ref, *, n, tt, hw):
    i = pl.program_id(0)
    dot = functools.partial(jnp.dot, preferred_element_type=F32, precision=HIGHEST)
    fr = fr_ref[...]
    a = jnp.sin(fr[0:1] * (dot(z_ref[...], w1_ref[...]) + b1_ref[...]))
    a = jnp.sin(fr[1:2] * (dot(a, w2_ref[...]) + b2_ref[...]))
    a = jnp.sin(fr[2:3] * (dot(a, w3_ref[...]) + b3_ref[...]))
    hf = jnp.dot(a.astype(BF16), w4_ref[...], preferred_element_type=F32)
    ti = lax.broadcasted_iota(jnp.int32, (tt, hw), 0) + i * tt
    t = ti.astype(F32) * (1.0 / (n - 1.0))
    window = jnp.exp(-t * dl_ref[...])
    sign = (1 - 2 * (ti & 1)).astype(F32)

    @pl.when(i == 0)
    def _():
        nyq_ref[...] = jnp.zeros_like(nyq_ref)

    for o in range(HY_ORDER):
        h_f = hf[:, (2 * o) * hw:(2 * o + 1) * hw] * window
        h_b = jnp.where(ti == 0, 0.0, hf[:, (2 * o + 1) * hw:(2 * o + 2) * hw] * window)
        hsum = h_f + h_b
        hs_ref[:, o * hw:(o + 1) * hw] = hsum.astype(BF16)
        hd_ref[:, o * hw:(o + 1) * hw] = (h_f - h_b).astype(BF16)
        nyq_ref[:, o * hw:(o + 1) * hw] += jnp.sum(hsum * sign, axis=0, keepdims=True)


def _filt_call(n, fw1, fb1, fw2, fb2, fw3, fb3, freq, fw4, hw):
    hid = fw1.shape[1]
    pk = LANE
    bands = (HY_EMB_DIM - 1) // 2
    tpos = np.linspace(0.0, 1.0, n, dtype=np.float32)[:, None].astype(np.float64)
    fb = np.linspace(1e-4, bands - 1, bands, dtype=np.float32)[None, :].astype(np.float64)
    wv = (2.0 * math.pi * np.arange(n, dtype=np.float64)[:, None] / n)
    z = np.concatenate([tpos, np.cos(fb * wv), -np.sin(fb * wv)], axis=-1)
    z = np.pad(z, ((0, 0), (0, pk - HY_EMB_DIM))).astype(np.float32)
    max_decay = math.log(HY_DECAY_TARGET) / HY_DECAY_SHORT_PCT
    min_decay = math.log(HY_DECAY_TARGET) / HY_DECAY_LONG_PCT
    deltas = np.abs(np.linspace(min_decay, max_decay, hw, dtype=np.float32)).reshape(1, hw)

    padc = lambda w: jnp.pad(w, ((0, 0), (0, pk - hid)))
    w1 = jnp.pad(fw1, ((0, pk - HY_EMB_DIM), (0, pk - hid)))
    w2 = jnp.pad(fw2, ((0, pk - hid), (0, pk - hid)))
    w3 = jnp.pad(fw3, ((0, pk - hid), (0, pk - hid)))
    w4 = jnp.pad(fw4, ((0, pk - hid), (0, 0))).astype(BF16)
    b1, b2, b3 = (padc(b.reshape(1, hid)) for b in (fb1, fb2, fb3))
    fr = jnp.pad(freq, ((0, 8 - freq.shape[0]), (0, pk - hid)))
    tt = _tile(n, 256)
    nw = HY_ORDER * hw
    full = lambda shape: pl.BlockSpec(shape, lambda i: (0, 0))
    return pl.pallas_call(
        functools.partial(_filt_kernel, n=n, tt=tt, hw=hw),
        grid=(n // tt,),
        in_specs=[
            pl.BlockSpec((tt, pk), lambda i: (i, 0)),
            full((pk, pk)), full((1, pk)), full((pk, pk)), full((1, pk)),
            full((pk, pk)), full((1, pk)), full((8, pk)), full((pk, 2 * nw)), full((1, hw)),
        ],
        out_specs=[
            pl.BlockSpec((tt, nw), lambda i: (i, 0)),
            pl.BlockSpec((tt, nw), lambda i: (i, 0)),
            pl.BlockSpec((1, nw), lambda i: (0, 0)),
        ],
        out_shape=[
            jax.ShapeDtypeStruct((n, nw), BF16),
            jax.ShapeDtypeStruct((n, nw), BF16),
            jax.ShapeDtypeStruct((1, nw), F32),
        ],
        compiler_params=_params(("arbitrary",)),
        name="hyena_filters",
    )(jnp.asarray(z), w1, b1, w2, b2, w3, b3, fr, w4, jnp.asarray(deltas))


def _dftgen_kernel(ac_ref, as_ref, bc_ref, bs_ref, c_ref, sf_ref, si_ref, *, rows, n):
    i = pl.program_id(0)
    ac, as_ = ac_ref[0], as_ref[0]
    bc, bs = bc_ref[...], bs_ref[...]
    c = ac * bc - as_ * bs
    s = as_ * bc + ac * bs
    k = lax.broadcasted_iota(jnp.int32, (rows, n), 0) + i * rows
    t = lax.broadcasted_iota(jnp.int32, (rows, n), 1)
    sign = (1 - 2 * ((k + t) & 1)).astype(F32)
    c_ref[...] = c.astype(BF16)
    sf_ref[...] = jnp.where(k == 0, sign, s).astype(BF16)
    si_ref[...] = jnp.where(t == 0, sign, s).astype(BF16)


def _dftgen_call(n):
    rows = 64
    assert n % rows == 0
    big = 2 * n
    t = np.arange(n, dtype=np.int64)[None, :]
    khi = (np.arange(n // rows, dtype=np.int64) * rows)[:, None]
    klo = np.arange(rows, dtype=np.int64)[:, None]
    ang_a = 2.0 * math.pi * ((khi * t) % big) / big
    ang_b = 2.0 * math.pi * ((klo * t) % big) / big
    a_c = jnp.asarray(np.cos(ang_a).astype(np.float32).reshape(n // rows, 1, n))
    a_s = jnp.asarray(np.sin(ang_a).astype(np.float32).reshape(n // rows, 1, n))
    b_c = jnp.asarray(np.cos(ang_b).astype(np.float32))
    b_s = jnp.asarray(np.sin(ang_b).astype(np.float32))
    mat = jax.ShapeDtypeStruct((n, n), BF16)
    return pl.pallas_call(
        functools.partial(_dftgen_kernel, rows=rows, n=n),
        grid=(n // rows,),
        in_specs=[
            pl.BlockSpec((1, 1, n), lambda i: (i, 0, 0)),
            pl.BlockSpec((1, 1, n), lambda i: (i, 0, 0)),
            pl.BlockSpec((rows, n), lambda i: (0, 0)),
            pl.BlockSpec((rows, n), lambda i: (0, 0)),
        ],
        out_specs=[pl.BlockSpec((rows, n), lambda i: (i, 0))] * 3,
        out_shape=[mat, mat, mat],
        compiler_params=_params(("arbitrary",)),
        name="dft_matrices",
    )(a_c, a_s, b_c, b_s)


def _sconv_kernel(p_ref, w_ref, b_ref, o_ref, *, n, rc):
    w = w_ref[...]
    bias = b_ref[...]
    row = lax.broadcasted_iota(jnp.int32, (rc, p_ref.shape[2]), 0)
    for c in range(n // rc):
        cur = p_ref[0, c * rc:(c + 1) * rc, :].astype(F32)
        up = pltpu.roll(cur, 1, 0)
        dn = pltpu.roll(cur, rc - 1, 0)
        if c > 0:
            prev = p_ref[0, c * rc - 16:c * rc, :].astype(F32)[15:16]
        else:
            prev = jnp.zeros((1, cur.shape[1]), F32)
        if c < n // rc - 1:
            nxt = p_ref[0, (c + 1) * rc:(c + 1) * rc + 16, :].astype(F32)[0:1]
        else:
            nxt = jnp.zeros((1, cur.shape[1]), F32)
        up = jnp.where(row == 0, prev, up)
        dn = jnp.where(row == rc - 1, nxt, dn)
        out = bias + up * w[0:1] + cur * w[1:2] + dn * w[2:3]
        o_ref[0, c * rc:(c + 1) * rc, :] = out.astype(o_ref.dtype)


def _sconv_call(p, conv_w, conv_b, col0):
    bsz, n, _ = p.shape
    width = conv_w.shape[1]
    tc = _tile(width, 512)
    rc = _tile(n, 512)
    off = col0 // tc
    return pl.pallas_call(
        functools.partial(_sconv_kernel, n=n, rc=rc),
        grid=(bsz, width // tc),
        in_specs=[
            pl.BlockSpec((1, n, tc), lambda b, j: (b, 0, off + j)),
            pl.BlockSpec((HY_SHORT, tc), lambda b, j: (0, j)),
            pl.BlockSpec((1, tc), lambda b, j: (0, j)),
        ],
        out_specs=pl.BlockSpec((1, n, tc), lambda b, j: (b, 0, j)),
        out_shape=jax.ShapeDtypeStruct((bsz, n, width), BF16),
        compiler_params=_params(("arbitrary", "arbitrary")),
        name="short_conv",
    )(p, conv_w, conv_b.reshape(1, width))


def _dft_fwd_kernel(c_ref, s_ref, u_ref, ck_ref, sk_ref, nyq_ref, p_ref, q_ref, *, tm):
    i = pl.program_id(0)
    u = u_ref[0]
    cu = jnp.dot(c_ref[...], u, preferred_element_type=F32)
    su = jnp.dot(s_ref[...], u, preferred_element_type=F32)
    ck = ck_ref[...]
    sk = sk_ref[...]
    k = lax.broadcasted_iota(jnp.int32, cu.shape, 0) + i * tm
    p = jnp.where(k == 0, 0.5 * cu * ck, cu * ck - su * sk)
    q = jnp.where(k == 0, 0.5 * su * nyq_ref[...], cu * sk + su * ck)
    p_ref[0] = p.astype(p_ref.dtype)
    q_ref[0] = q.astype(q_ref.dtype)


def _dft_fwd_call(cmat, sfmat, u, u_col0, ck, sk, nyq, k_col0, hw):
    bsz, n, _ = u.shape
    tm = _tile(n, 512)
    tn = _tile(hw, 512)
    uo, ko = u_col0 // tn, k_col0 // tn
    return pl.pallas_call(
        functools.partial(_dft_fwd_kernel, tm=tm),
        grid=(n // tm, bsz, hw // tn),
        in_specs=[
            pl.BlockSpec((tm, n), lambda i, b, j: (i, 0)),
            pl.BlockSpec((tm, n), lambda i, b, j: (i, 0)),
            pl.BlockSpec((1, n, tn), lambda i, b, j: (b, 0, uo + j)),
            pl.BlockSpec((tm, tn), lambda i, b, j: (i, ko + j)),
            pl.BlockSpec((tm, tn), lambda i, b, j: (i, ko + j)),
            pl.BlockSpec((1, tn), lambda i, b, j: (0, ko + j)),
        ],
        out_specs=[pl.BlockSpec((1, tm, tn), lambda i, b, j: (b, i, j))] * 2,
        out_shape=[jax.ShapeDtypeStruct((bsz, n, hw), BF16)] * 2,
        compiler_params=_params(("arbitrary", "arbitrary", "arbitrary")),
        name="dft_forward",
    )(cmat, sfmat, u, ck, sk, nyq)


def _dft_inv_kernel(c_ref, s_ref, p_ref, q_ref, u_ref, gate_ref, bias_ref, o_ref, *, scale):
    y = (jnp.dot(c_ref[...], p_ref[0], preferred_element_type=F32)
         + jnp.dot(s_ref[...], q_ref[0], preferred_element_type=F32)) * scale
    y = y + u_ref[0].astype(F32) * bias_ref[...]
    o_ref[0] = (gate_ref[0].astype(F32) * y).astype(o_ref.dtype)


def _dft_inv_call(cmat, simat, p, q, u, u_col0, gate, gate_col0, bias):
    bsz, n, hw = p.shape
    tm = _tile(n, 512)
    tn = _tile(hw, 512)
    uo, go = u_col0 // tn, gate_col0 // tn
    return pl.pallas_call(
        functools.partial(_dft_inv_kernel, scale=1.0 / n),
        grid=(n // tm, bsz, hw // tn),
        in_specs=[
            pl.BlockSpec((tm, n), lambda i, b, j: (i, 0)),
            pl.BlockSpec((tm, n), lambda i, b, j: (i, 0)),
            pl.BlockSpec((1, n, tn), lambda i, b, j: (b, 0, j)),
            pl.BlockSpec((1, n, tn), lambda i, b, j: (b, 0, j)),
            pl.BlockSpec((1, tm, tn), lambda i, b, j: (b, i, uo + j)),
            pl.BlockSpec((1, tm, tn), lambda i, b, j: (b, i, go + j)),
            pl.BlockSpec((1, tn), lambda i, b, j: (0, j)),
        ],
        out_specs=pl.BlockSpec((1, tm, tn), lambda i, b, j: (b, i, j)),
        out_shape=jax.ShapeDtypeStruct((bsz, n, hw), BF16),
        compiler_params=_params(("arbitrary", "arbitrary", "arbitrary")),
        name="dft_inverse",
    )(cmat, simat, p, q, u, gate, bias.reshape(1, hw))


def _wout_kernel(r_ref, y_ref, wa_ref, wb_ref, x_ref, gate_ref, o_ref):
    y = (jnp.dot(r_ref[0], wa_ref[...], preferred_element_type=F32)
         + jnp.dot(y_ref[0], wb_ref[...], preferred_element_type=F32))
    o_ref[0] = x_ref[0] + gate_ref[0] * y


def _wout_call(r, yh, w_out, x, gate):
    bsz, n, d = x.shape
    rw = r.shape[2]
    tm = _tile(n, 512)
    return pl.pallas_call(
        _wout_kernel,
        grid=(bsz, n // tm),
        in_specs=[
            pl.BlockSpec((1, tm, rw), lambda b, i: (b, i, 0)),
            pl.BlockSpec((1, tm, d - rw), lambda b, i: (b, i, 0)),
            pl.BlockSpec((rw, d), lambda b, i: (0, 0)),
            pl.BlockSpec((d - rw, d), lambda b, i: (0, 0)),
            pl.BlockSpec((1, tm, d), lambda b, i: (b, i, 0)),
            pl.BlockSpec((1, 1, d), lambda b, i: (b, 0, 0)),
        ],
        out_specs=pl.BlockSpec((1, tm, d), lambda b, i: (b, i, 0)),
        out_shape=jax.ShapeDtypeStruct((bsz, n, d), F32),
        compiler_params=_params(("arbitrary", "arbitrary")),
        name="mixer_out",
    )(r, yh, w_out[:rw], w_out[rw:], x, gate)


def _pool_kernel(h_ref, m_ref, ic_ref, w_ref, sc_ref, x_ref, gate_ref, o_ref, *, tm, nt):
    i = pl.program_id(2)
    rows_c = pl.ds(pl.multiple_of(i * tm, tm), tm)
    rows_l = pl.ds(pl.multiple_of(jnp.maximum(i - 1, 0) * tm, tm), tm)
    rows_r = pl.ds(pl.multiple_of(jnp.minimum(i + 1, nt - 1) * tm, tm), tm)
    hc = h_ref[0, rows_c, :]
    acc = jnp.dot(m_ref[0, :, tm:2 * tm], hc, preferred_element_type=F32)
    left = jnp.dot(m_ref[0, :, 0:tm], h_ref[0, rows_l, :], preferred_element_type=F32)
    right = jnp.dot(m_ref[0, :, 2 * tm:3 * tm], h_ref[0, rows_r, :], preferred_element_type=F32)
    acc = acc + jnp.where(i > 0, left, 0.0) + jnp.where(i < nt - 1, right, 0.0)
    pooled = acc * ic_ref[0] - hc.astype(F32)
    y = jnp.dot(pooled.astype(BF16), w_ref[0], preferred_element_type=F32) * sc_ref[...]
    o_ref[0] = x_ref[0] + gate_ref[0] * y


def _pool_tables(n, tm):
    rows_per_tile = tm // GRID_W
    n_rows = n // GRID_W
    tok = np.arange(tm)
    r_out, c_out = tok // GRID_W, tok % GRID_W
    tin = np.arange(3 * tm)
    r_in, c_in = tin // GRID_W - rows_per_tile, tin % GRID_W
    mats, inv = [], []
    for win in POOL_WINDOWS:
        lo, hi = -(win // 2), win - 1 - win // 2
        dr = r_in[None, :] - r_out[:, None]
        dc = c_in[None, :] - c_out[:, None]
        mats.append(((dr >= lo) & (dr <= hi) & (dc >= lo) & (dc <= hi)).astype(np.float32))
        idx_r, idx_c = np.arange(n_rows), np.arange(GRID_W)
        cnt_r = np.minimum(idx_r + hi, n_rows - 1) + 1 - np.maximum(idx_r + lo, 0)
        cnt_c = np.minimum(idx_c + hi, GRID_W - 1) + 1 - np.maximum(idx_c + lo, 0)
        inv.append((1.0 / (cnt_r[:, None] * cnt_c[None, :]).astype(np.float64)).reshape(n, 1))
    return (jnp.asarray(np.stack(mats), dtype=BF16),
            jnp.asarray(np.stack(inv).astype(np.float32)))


def _pool_call(h, pool_w, pool_scale, x, gate):
    bsz, n, d = x.shape
    ng = len(POOL_WINDOWS)
    pg = d // ng
    tm = _tile(n, 512)
    nt = n // tm
    assert tm % GRID_W == 0 and tm // GRID_W >= max(POOL_WINDOWS) // 2
    mats, inv = _pool_tables(n, tm)
    return pl.pallas_call(
        functools.partial(_pool_kernel, tm=tm, nt=nt),
        grid=(bsz, ng, nt),
        in_specs=[
            pl.BlockSpec((1, n, pg), lambda b, g, i: (b, 0, g)),
            pl.BlockSpec((1, tm, 3 * tm), lambda b, g, i: (g, 0, 0)),
            pl.BlockSpec((1, tm, 1), lambda b, g, i: (g, i, 0)),
            pl.BlockSpec((1, pg, pg), lambda b, g, i: (g, 0, 0)),
            pl.BlockSpec((1, pg), lambda b, g, i: (0, g)),
            pl.BlockSpec((1, tm, pg), lambda b, g, i: (b, i, g)),
            pl.BlockSpec((1, 1, pg), lambda b, g, i: (b, 0, g)),
        ],
        out_specs=pl.BlockSpec((1, tm, pg), lambda b, g, i: (b, i, g)),
        out_shape=jax.ShapeDtypeStruct((bsz, n, d), F32),
        compiler_params=_params(("arbitrary", "arbitrary", "arbitrary")),
        name="pool_mixer",
    )(h, mats, inv, pool_w, pool_scale.reshape(1, d), x, gate)


def _rotary_tables(n):
    pos = np.arange(n)
    row = (pos // GRID_W).astype(np.float32)
    col = (pos % GRID_W).astype(np.float32)
    n_freq = HEAD_DIM // 4
    inv = (np.float32(ROPE_BASE) ** (-np.arange(n_freq, dtype=np.float32) / np.float32(n_freq))).astype(np.float32)
    ang = np.concatenate([row[:, None] * inv, col[:, None] * inv], axis=-1).astype(np.float32)
    cos, sin = np.cos(ang.astype(np.float64)), np.sin(ang.astype(np.float64))
    cos2 = np.concatenate([cos, cos], axis=-1).astype(np.float32)
    sin2 = np.concatenate([-sin, sin], axis=-1).astype(np.float32)
    return jnp.asarray(cos2), jnp.asarray(sin2)


def kernel(x, c, ctx, c_ctx, w_mod, b_mod, ffn1_w1, ffn1_w3, ffn1_w2, ffn2_w1, ffn2_w3, ffn2_w2, ab_w_in, ab_w_out, ret_log_decay, hy_conv_w, hy_conv_b, hy_f_w1, hy_f_b1, hy_f_w2, hy_f_b2, hy_f_w3, hy_f_b3, hy_f_freq, hy_f_w4, hy_bias, pool_w, pool_scale, final_gain):
    bsz, n, d = x.shape
    n_ctx = ctx.shape[1]
    depth = w_mod.shape[0]
    assert depth == 2, "layer 0 = retention/Hyena mixer, layer 1 = pooling mixer"
    ret_w = RET_HEADS * HEAD_DIM
    hw = d - ret_w
    assert hy_f_w4.shape[2] == HY_ORDER * 2 * hw

    c_all = jnp.concatenate([c, c_ctx[None, :], jnp.zeros((8 - bsz - 1, d), F32)], axis=0)
    mod = _mod_call(c_all, w_mod, b_mod).reshape(depth, 8, N_MOD, d)

    def mod_rows(layer, base, count=3):
        return mod[layer, :bsz, base:base + count]

    def mod_ctx(base, count=3):
        return jnp.broadcast_to(mod[0, bsz:bsz + 1, base:base + count], (bsz, count, d))

    bf = lambda w: w.astype(BF16)

    w1, w3, w2 = bf(ffn1_w1[0]), bf(ffn1_w3[0]), bf(ffn1_w2[0])
    x, h = _ffn_call(x, mod_rows(0, 0), w1, w3, w2, modn=mod_rows(0, 3))
    _, h_c = _ffn_call(ctx, mod_ctx(0), w1, w3, w2, modn=mod_ctx(3))

    w_in = bf(ab_w_in[0])
    p = _matmul_call(h.reshape(bsz * n, d), w_in, BF16).reshape(bsz, n, -1)
    kvc = _matmul_call(h_c.reshape(bsz * n_ctx, d), w_in[:, ret_w:3 * ret_w], BF16)
    kvc = kvc.reshape(bsz, n_ctx, 2 * ret_w)

    cos2, sin2 = _rotary_tables(n)
    r = _ret_call(p, kvc, ret_log_decay[0], cos2, sin2, ret_w)

    hsum, hdiff, nyq = _filt_call(n, hy_f_w1[0], hy_f_b1[0], hy_f_w2[0], hy_f_b2[0], hy_f_w3[0],
                                  hy_f_b3[0], hy_f_freq[0], hy_f_w4[0], hw)
    cmat, sfmat, simat = _dftgen_call(n)
    ck = _matmul_call(cmat, hsum, F32, 512, 512)
    sk = _matmul_call(sfmat, hdiff, F32, 512, 512)
    u = _sconv_call(p, hy_conv_w[0], hy_conv_b[0], 4 * ret_w)
    pq = _dft_fwd_call(cmat, sfmat, u, 0, ck, sk, nyq, 0, hw)
    z = _dft_inv_call(cmat, simat, pq[0], pq[1], u, 0, u, hw, hy_bias[0, 0])
    pq = _dft_fwd_call(cmat, sfmat, z, 0, ck, sk, nyq, hw, hw)
    yh = _dft_inv_call(cmat, simat, pq[0], pq[1], z, 0, u, 2 * hw, hy_bias[0, 1])

    x = _wout_call(r, yh, bf(ab_w_out[0]), x, mod_rows(0, 5, 1))
    x = _ffn_call(x, mod_rows(0, 6), bf(ffn2_w1[0]), bf(ffn2_w3[0]), bf(ffn2_w2[0]))

    x, h = _ffn_call(x, mod_rows(1, 0), bf(ffn1_w1[1]), bf(ffn1_w3[1]), bf(ffn1_w2[1]),
                     modn=mod_rows(1, 3))
    x = _pool_call(h, bf(pool_w[0]), pool_scale[0], x, mod_rows(1, 5, 1))
    return _ffn_call(x, mod_rows(1, 6), bf(ffn2_w1[1]), bf(ffn2_w3[1]), bf(ffn2_w2[1]),
                     gain=final_gain)
```

```python
import functools
import math

import numpy as np
import jax
import jax.numpy as jnp
from jax import lax
from jax.experimental import pallas as pl
from jax.experimental.pallas import tpu as pltpu

F32 = jnp.float32
BF16 = jnp.bfloat16
HIGHEST = lax.Precision.HIGHEST

EPS = 1e-6
GRID_W = 64
N_MOD = 9
RET_HEADS = 8
HEAD_DIM = 128
HY_ORDER = 2
HY_SHORT = 3
HY_EMB_DIM = 33
HY_DECAY_SHORT_PCT = 0.3
HY_DECAY_LONG_PCT = 1.5
HY_DECAY_TARGET = 1e-2
ROPE_BASE = 10000.0
POOL_WINDOWS = (2, 4, 8, 16)
RET_CHUNK = 256
RET_UNROLL = 4
LANE = 128
VMEM_LIMIT = 56 * 1024 * 1024


def _tile(dim, pref):
    if dim <= pref:
        return dim
    for t in range(pref - pref % LANE, 0, -LANE):
        if dim % t == 0:
            return t
    raise ValueError((dim, pref))


def _params(sem):
    return pltpu.CompilerParams(dimension_semantics=sem, vmem_limit_bytes=VMEM_LIMIT)


def _rms(x):
    return x * lax.rsqrt(jnp.mean(x * x, axis=-1, keepdims=True) + EPS)


def _silu(x):
    return x * jax.nn.sigmoid(x)


def _mod_kernel(c_ref, w_ref, b_ref, o_ref):
    sc = _silu(c_ref[...])
    o_ref[0] = jnp.dot(sc, w_ref[0], preferred_element_type=F32, precision=HIGHEST) + b_ref[0]


def _mod_call(c_all, w_mod, b_mod):
    depth, d, nd = w_mod.shape
    rows = c_all.shape[0]
    tn = _tile(nd, 1024)
    return pl.pallas_call(
        _mod_kernel,
        grid=(depth, nd // tn),
        in_specs=[
            pl.BlockSpec((rows, d), lambda l, j: (0, 0)),
            pl.BlockSpec((1, d, tn), lambda l, j: (l, 0, j)),
            pl.BlockSpec((1, 1, tn), lambda l, j: (l, 0, j)),
        ],
        out_specs=pl.BlockSpec((1, rows, tn), lambda l, j: (l, 0, j)),
        out_shape=jax.ShapeDtypeStruct((depth, rows, nd), F32),
        compiler_params=_params(("arbitrary", "arbitrary")),
        name="mod",
    )(c_all, w_mod, b_mod.reshape(depth, 1, nd))


def _ffn_kernel(*refs, n_f, emit_h, final):
    x_ref, mod_ref, w1_ref, w3_ref, w2_ref = refs[:5]
    pos = 5
    modn_ref = gain_ref = hn_ref = None
    if emit_h:
        modn_ref = refs[pos]
        pos += 1
    if final:
        gain_ref = refs[pos]
        pos += 1
    o_ref = refs[pos]
    pos += 1
    if emit_h:
        hn_ref = refs[pos]
        pos += 1
    h_scr, acc_scr = refs[pos], refs[pos + 1]
    f = pl.program_id(2)

    assert n_f >= 3

    def partial_out(h):
        a = jnp.dot(h, w1_ref[...], preferred_element_type=F32)
        b = jnp.dot(h, w3_ref[...], preferred_element_type=F32)
        g = (_silu(a) * b).astype(BF16)
        return jnp.dot(g, w2_ref[...], preferred_element_type=F32)

    @pl.when(f == 0)
    def _():
        m = mod_ref[0]
        h = (_rms(x_ref[0]) * (1.0 + m[1:2]) + m[0:1]).astype(BF16)
        h_scr[...] = h
        acc_scr[...] = partial_out(h)

    @pl.when(jnp.logical_and(f > 0, f < n_f - 1))
    def _():
        acc_scr[...] += partial_out(h_scr[...])

    @pl.when(f == n_f - 1)
    def _():
        m = mod_ref[0]
        xo = x_ref[0] + 0.5 * m[2:3] * (acc_scr[...] + partial_out(h_scr[...]))
        if emit_h:
            mn = modn_ref[0]
            hn_ref[0] = (_rms(xo) * (1.0 + mn[1:2]) + mn[0:1]).astype(BF16)
        if final:
            xo = _rms(xo) * gain_ref[...]
        o_ref[0] = xo


def _ffn_call(x, mod3, w1, w3, w2, layer, modn=None, gain=None):
    bsz, s, d = x.shape
    dff = w1.shape[2]
    tm = _tile(s, 512)
    tf = _tile(dff, 512)
    n_f = dff // tf
    emit_h = modn is not None
    final = gain is not None
    in_specs = [
        pl.BlockSpec((1, tm, d), lambda b, i, f: (b, i, 0)),
        pl.BlockSpec((1, 3, d), lambda b, i, f: (b, 0, 0)),
        pl.BlockSpec((None, d, tf), lambda b, i, f: (layer, 0, f)),
        pl.BlockSpec((None, d, tf), lambda b, i, f: (layer, 0, f)),
        pl.BlockSpec((None, tf, d), lambda b, i, f: (layer, f, 0)),
    ]
    args = [x, mod3, w1, w3, w2]
    if emit_h:
        in_specs.append(pl.BlockSpec((1, 3, d), lambda b, i, f: (b, 0, 0)))
        args.append(modn)
    if final:
        in_specs.append(pl.BlockSpec((1, d), lambda b, i, f: (0, 0)))
        args.append(gain.reshape(1, d))
    out_specs = [pl.BlockSpec((1, tm, d), lambda b, i, f: (b, i, 0))]
    out_shape = [jax.ShapeDtypeStruct((bsz, s, d), F32)]
    if emit_h:
        out_specs.append(pl.BlockSpec((1, tm, d), lambda b, i, f: (b, i, 0)))
        out_shape.append(jax.ShapeDtypeStruct((bsz, s, d), BF16))
    outs = pl.pallas_call(
        functools.partial(_ffn_kernel, n_f=n_f, emit_h=emit_h, final=final),
        grid=(bsz, s // tm, n_f),
        in_specs=in_specs,
        out_specs=out_specs,
        out_shape=out_shape,
        scratch_shapes=[pltpu.VMEM((tm, d), BF16), pltpu.VMEM((tm, d), F32)],
        compiler_params=_params(("arbitrary", "arbitrary", "arbitrary")),
        name="ffn",
    )(*args)
    return outs if emit_h else outs[0]


def _cast_kernel(*refs):
    half = len(refs) // 2
    for src, dst in zip(refs[:half], refs[half:]):
        dst[...] = src[...].astype(dst.dtype)


def _cast_call(arrays, steps):
    depth = arrays[0].shape[0]
    specs = []
    for a in arrays:
        _, r, c = a.shape
        if c >= r:
            specs.append(pl.BlockSpec((1, r, c // steps), lambda l, j: (l, 0, j)))
        else:
            specs.append(pl.BlockSpec((1, r // steps, c), lambda l, j: (l, j, 0)))
    return pl.pallas_call(
        _cast_kernel,
        grid=(depth, steps),
        in_specs=specs,
        out_specs=specs,
        out_shape=[jax.ShapeDtypeStruct(a.shape, BF16) for a in arrays],
        compiler_params=_params(("arbitrary", "arbitrary")),
        name="cast_weights",
    )(*arrays)


def _matmul_kernel(a_ref, b_ref, o_ref):
    o_ref[...] = jnp.dot(a_ref[...], b_ref[...], preferred_element_type=F32).astype(o_ref.dtype)


def _matmul_call(a, b, out_dtype, tm_pref=1024, tn_pref=1024, b_cols=None):
    m, k = a.shape
    c0, c1 = b_cols if b_cols is not None else (0, b.shape[1])
    n = c1 - c0
    tm = _tile(m, tm_pref)
    tn = math.gcd(_tile(n, tn_pref), c0) if c0 else _tile(n, tn_pref)
    assert tn % LANE == 0
    joff = c0 // tn
    return pl.pallas_call(
        _matmul_kernel,
        grid=(m // tm, n // tn),
        in_specs=[
            pl.BlockSpec((tm, k), lambda i, j: (i, 0)),
            pl.BlockSpec((k, tn), lambda i, j: (0, joff + j)),
        ],
        out_specs=pl.BlockSpec((tm, tn), lambda i, j: (i, j)),
        out_shape=jax.ShapeDtypeStruct((m, n), out_dtype),
        compiler_params=_params(("arbitrary", "arbitrary")),
        name="matmul",
    )(a, b)


def _ret_kernel(lg_ref, q_ref, k_ref, v_ref, g_ref, cos_ref, sin_ref, kc_ref, vc_ref, o_ref,
                qr_scr, kr_scr, upd_scr, st_scr, *, n, n_ctx, chunk):
    hd = HEAD_DIM
    head = pl.program_id(1)
    lgf = lg_ref[0, head]
    lgb = lg_ref[1, head]
    nc = n // chunk
    k_scale = hd ** -0.5

    posc = lax.broadcasted_iota(jnp.int32, (n_ctx, hd), 0).astype(F32)
    kc = kc_ref[0].astype(F32) * k_scale
    vc = vc_ref[0]
    kcf = (kc * jnp.exp((n_ctx - 1.0 - posc) * lgf)).T.astype(BF16)
    kcb = (kc * jnp.exp(posc * lgb)).T.astype(BF16)
    s0_f = jnp.dot(kcf, vc, preferred_element_type=F32)
    s0_b = jnp.dot(kcb, vc, preferred_element_type=F32)

    pos = lax.broadcasted_iota(jnp.int32, (chunk, hd), 0).astype(F32)
    qdec_f = jnp.exp((pos + 1.0) * lgf)
    kdec_f = jnp.exp((chunk - 1.0 - pos) * lgf)
    qdec_b = jnp.exp((chunk - pos) * lgb)
    kdec_b = jnp.exp(pos * lgb)
    cdec_f = jnp.exp(jnp.full((1, hd), chunk, F32) * lgf)
    cdec_b = jnp.exp(jnp.full((1, hd), chunk, F32) * lgb)
    ri = lax.broadcasted_iota(jnp.int32, (chunk, chunk), 0)
    ci = lax.broadcasted_iota(jnp.int32, (chunk, chunk), 1)
    diff = (ri - ci).astype(F32)
    dmat = (jnp.where(diff >= 0, jnp.exp(jnp.maximum(diff, 0.0) * lgf), 0.0)
            + jnp.where(diff <= 0, jnp.exp(jnp.maximum(-diff, 0.0) * lgb), 0.0))

    def rotary(x, rows):
        return x * cos_ref[rows, :] + pltpu.roll(x, hd // 2, 1) * sin_ref[rows, :]

    def chunk_rows(c):
        return pl.ds(pl.multiple_of(c * chunk, chunk), chunk)

    def prep_body(c, carry):
        rows = chunk_rows(c)
        qr_scr[rows, :] = rotary(q_ref[0, rows, :].astype(F32), rows)
        k = rotary(k_ref[0, rows, :].astype(F32), rows) * k_scale
        kr_scr[rows, :] = k
        kk = jnp.concatenate([k * kdec_f, k * kdec_b], axis=1).T.astype(BF16)
        upd_scr[c] = jnp.dot(kk, v_ref[0, rows, :], preferred_element_type=F32)
        return carry

    lax.fori_loop(0, nc, prep_body, 0, unroll=math.gcd(nc, RET_UNROLL))

    def fwd_scan(c, sf):
        st_scr[c, 0:hd, :] = sf.astype(BF16)
        return cdec_f * sf + upd_scr[c, 0:hd, :]

    def bwd_scan(j, sb):
        c = nc - 1 - j
        st_scr[c, hd:2 * hd, :] = sb.astype(BF16)
        return cdec_b * sb + upd_scr[c, hd:2 * hd, :]

    lax.fori_loop(0, nc, fwd_scan, s0_f, unroll=math.gcd(nc, RET_UNROLL))
    lax.fori_loop(0, nc, bwd_scan, s0_b, unroll=math.gcd(nc, RET_UNROLL))

    def out_body(c, carry):
        rows = chunk_rows(c)
        q = qr_scr[rows, :]
        s = lax.dot_general(q.astype(BF16), kr_scr[rows, :].astype(BF16), (((1,), (1,)), ((), ())),
                            preferred_element_type=F32) * dmat
        qq = jnp.concatenate([q * qdec_f, q * qdec_b], axis=1).astype(BF16)
        o = (jnp.dot(s.astype(BF16), v_ref[0, rows, :], preferred_element_type=F32)
             + jnp.dot(qq, st_scr[c], preferred_element_type=F32))
        g = g_ref[0, rows, :].astype(F32)
        o_ref[0, rows, :] = (_rms(o) * _silu(g)).astype(o_ref.dtype)
        return carry

    lax.fori_loop(0, nc, out_body, 0, unroll=math.gcd(nc, RET_UNROLL))


def _ret_call(p, kvc, log_decay, cos2, sin2, ret_w):
    bsz, n, _ = p.shape
    n_ctx = kvc.shape[1]
    heads = ret_w // HEAD_DIM
    hd = HEAD_DIM
    chunk = _tile(n, RET_CHUNK)
    kern = functools.partial(_ret_kernel, n=n, n_ctx=n_ctx, chunk=chunk)
    col = lambda off: (lambda b, h, lg: (b, 0, off * heads + h))
    return pl.pallas_call(
        kern,
        grid_spec=pltpu.PrefetchScalarGridSpec(
            num_scalar_prefetch=1,
            grid=(bsz, heads),
            in_specs=[
                pl.BlockSpec((1, n, hd), col(0)),
                pl.BlockSpec((1, n, hd), col(1)),
                pl.BlockSpec((1, n, hd), col(2)),
                pl.BlockSpec((1, n, hd), col(3)),
                pl.BlockSpec((n, hd), lambda b, h, lg: (0, 0)),
                pl.BlockSpec((n, hd), lambda b, h, lg: (0, 0)),
                pl.BlockSpec((1, n_ctx, hd), col(0)),
                pl.BlockSpec((1, n_ctx, hd), col(1)),
            ],
            out_specs=pl.BlockSpec((1, n, hd), lambda b, h, lg: (b, 0, h)),
            scratch_shapes=[
                pltpu.VMEM((n, hd), F32),
                pltpu.VMEM((n, hd), F32),
                pltpu.VMEM((n // chunk, 2 * hd, hd), F32),
                pltpu.VMEM((n // chunk, 2 * hd, hd), BF16),
            ],
        ),
        out_shape=jax.ShapeDtypeStruct((bsz, n, ret_w), BF16),
        compiler_params=_params(("arbitrary", "arbitrary")),
        name="retention",
    )(log_decay, p, p, p, p, cos2, sin2, kvc, kvc)


def _filt_kernel(z_ref, w1_ref, b1_ref, w2_ref, b2_ref, w3_ref, b3_ref, fr_ref, w4_ref, dl_ref,
                 hs_ref, hd_ref, nyq_ref, *, n, tt, hw):
    i = pl.program_id(0)
    dot = functools.partial(jnp.dot, preferred_element_type=F32, precision=HIGHEST)
    fr = fr_ref[...]
    a = jnp.sin(fr[0:1] * (dot(z_ref[...], w1_ref[...]) + b1_ref[...]))
    a = jnp.sin(fr[1:2] * (dot(a, w2_ref[...]) + b2_ref[...]))
    a = jnp.sin(fr[2:3] * (dot(a, w3_ref[...]) + b3_ref[...]))
    hf = jnp.dot(a.astype(BF16), w4_ref[...], preferred_element_type=F32)
    ti = lax.broadcasted_iota(jnp.int32, (tt, hw), 0) + i * tt
    t = ti.astype(F32) * (1.0 / (n - 1.0))
    window = jnp.exp(-t * dl_ref[...])
    sign = (1 - 2 * (ti & 1)).astype(F32)

    @pl.when(i == 0)
    def _():
        nyq_ref[...] = jnp.zeros_like(nyq_ref)

    for o in range(HY_ORDER):
        h_f = hf[:, (2 * o) * hw:(2 * o + 1) * hw] * window
        h_b = jnp.where(ti == 0, 0.0, hf[:, (2 * o + 1) * hw:(2 * o + 2) * hw] * window)
        hsum = h_f + h_b
        hs_ref[:, o * hw:(o + 1) * hw] = hsum.astype(BF16)
        hd_ref[:, o * hw:(o + 1) * hw] = (h_f - h_b).astype(BF16)
        nyq_ref[:, o * hw:(o + 1) * hw] += jnp.sum(hsum * sign, axis=0, keepdims=True)


def _filt_call(n, fw1, fb1, fw2, fb2, fw3, fb3, freq, fw4, hw):
    hid = fw1.shape[1]
    pk = LANE
    bands = (HY_EMB_DIM - 1) // 2
    tpos = np.linspace(0.0, 1.0, n, dtype=np.float32)[:, None].astype(np.float64)
    fb = np.linspace(1e-4, bands - 1, bands, dtype=np.float32)[None, :].astype(np.float64)
    wv = (2.0 * math.pi * np.arange(n, dtype=np.float64)[:, None] / n)
    z = np.concatenate([tpos, np.cos(fb * wv), -np.sin(fb * wv)], axis=-1)
    z = np.pad(z, ((0, 0), (0, pk - HY_EMB_DIM))).astype(np.float32)
    max_decay = math.log(HY_DECAY_TARGET) / HY_DECAY_SHORT_PCT
    min_decay = math.log(HY_DECAY_TARGET) / HY_DECAY_LONG_PCT
    deltas = np.abs(np.linspace(min_decay, max_decay, hw, dtype=np.float32)).reshape(1, hw)

    padc = lambda w: jnp.pad(w, ((0, 0), (0, pk - hid)))
    w1 = jnp.pad(fw1, ((0, pk - HY_EMB_DIM), (0, pk - hid)))
    w2 = jnp.pad(fw2, ((0, pk - hid), (0, pk - hid)))
    w3 = jnp.pad(fw3, ((0, pk - hid), (0, pk - hid)))
    w4 = jnp.pad(fw4, ((0, pk - hid), (0, 0))).astype(BF16)
    b1, b2, b3 = (padc(b.reshape(1, hid)) for b in (fb1, fb2, fb3))
    fr = jnp.pad(freq, ((0, 8 - freq.shape[0]), (0, pk - hid)))
    tt = _tile(n, 256)
    nw = HY_ORDER * hw
    full = lambda shape: pl.BlockSpec(shape, lambda i: (0, 0))
    return pl.pallas_call(
        functools.partial(_filt_kernel, n=n, tt=tt, hw=hw),
        grid=(n // tt,),
        in_specs=[
            pl.BlockSpec((tt, pk), lambda i: (i, 0)),
            full((pk, pk)), full((1, pk)), full((pk, pk)), full((1, pk)),
            full((pk, pk)), full((1, pk)), full((8, pk)), full((pk, 2 * nw)), full((1, hw)),
        ],
        out_specs=[
            pl.BlockSpec((tt, nw), lambda i: (i, 0)),
            pl.BlockSpec((tt, nw), lambda i: (i, 0)),
            pl.BlockSpec((1, nw), lambda i: (0, 0)),
        ],
        out_shape=[
            jax.ShapeDtypeStruct((n, nw), BF16),
            jax.ShapeDtypeStruct((n, nw), BF16),
            jax.ShapeDtypeStruct((1, nw), F32),
        ],
        compiler_params=_params(("arbitrary",)),
        name="hyena_filters",
    )(jnp.asarray(z), w1, b1, w2, b2, w3, b3, fr, w4, jnp.asarray(deltas))


def _dftgen_kernel(ac_ref, as_ref, bc_ref, bs_ref, c_ref, sf_ref, si_ref, *, rows, n):
    i = pl.program_id(0)
    ac, as_ = ac_ref[0], as_ref[0]
    bc, bs = bc_ref[...], bs_ref[...]
    c = ac * bc - as_ * bs
    s = as_ * bc + ac * bs
    k = lax.broadcasted_iota(jnp.int32, (rows, n), 0) + i * rows
    t = lax.broadcasted_iota(jnp.int32, (rows, n), 1)
    sign = (1 - 2 * ((k + t) & 1)).astype(F32)
    c_ref[...] = c.astype(BF16)
    sf_ref[...] = jnp.where(k == 0, sign, s).astype(BF16)
    si_ref[...] = jnp.where(t == 0, sign, s).astype(BF16)


def _dftgen_call(n):
    rows = 64
    assert n % rows == 0
    big = 2 * n
    t = np.arange(n, dtype=np.int64)[None, :]
    khi = (np.arange(n // rows, dtype=np.int64) * rows)[:, None]
    klo = np.arange(rows, dtype=np.int64)[:, None]
    ang_a = 2.0 * math.pi * ((khi * t) % big) / big
    ang_b = 2.0 * math.pi * ((klo * t) % big) / big
    a_c = jnp.asarray(np.cos(ang_a).astype(np.float32).reshape(n // rows, 1, n))
    a_s = jnp.asarray(np.sin(ang_a).astype(np.float32).reshape(n // rows, 1, n))
    b_c = jnp.asarray(np.cos(ang_b).astype(np.float32))
    b_s = jnp.asarray(np.sin(ang_b).astype(np.float32))
    mat = jax.ShapeDtypeStruct((n, n), BF16)
    return pl.pallas_call(
        functools.partial(_dftgen_kernel, rows=rows, n=n),
        grid=(n // rows,),
        in_specs=[
            pl.BlockSpec((1, 1, n), lambda i: (i, 0, 0)),
            pl.BlockSpec((1, 1, n), lambda i: (i, 0, 0)),
            pl.BlockSpec((rows, n), lambda i: (0, 0)),
            pl.BlockSpec((rows, n), lambda i: (0, 0)),
        ],
        out_specs=[pl.BlockSpec((rows, n), lambda i: (i, 0))] * 3,
        out_shape=[mat, mat, mat],
        compiler_params=_params(("arbitrary",)),
        name="dft_matrices",
    )(a_c, a_s, b_c, b_s)


def _sconv_kernel(p_ref, w_ref, b_ref, o_ref, *, n, rc):
    w = w_ref[...]
    bias = b_ref[...]
    row = lax.broadcasted_iota(jnp.int32, (rc, p_ref.shape[2]), 0)
    for c in range(n // rc):
        cur = p_ref[0, c * rc:(c + 1) * rc, :].astype(F32)
        up = pltpu.roll(cur, 1, 0)
        dn = pltpu.roll(cur, rc - 1, 0)
        if c > 0:
            prev = p_ref[0, c * rc - 16:c * rc, :].astype(F32)[15:16]
        else:
            prev = jnp.zeros((1, cur.shape[1]), F32)
        if c < n // rc - 1:
            nxt = p_ref[0, (c + 1) * rc:(c + 1) * rc + 16, :].astype(F32)[0:1]
        else:
            nxt = jnp.zeros((1, cur.shape[1]), F32)
        up = jnp.where(row == 0, prev, up)
        dn = jnp.where(row == rc - 1, nxt, dn)
        out = bias + up * w[0:1] + cur * w[1:2] + dn * w[2:3]
        o_ref[0, c * rc:(c + 1) * rc, :] = out.astype(o_ref.dtype)


def _sconv_call(p, conv_w, conv_b, col0):
    bsz, n, _ = p.shape
    width = conv_w.shape[1]
    tc = _tile(width, 512)
    rc = _tile(n, 512)
    off = col0 // tc
    return pl.pallas_call(
        functools.partial(_sconv_kernel, n=n, rc=rc),
        grid=(bsz, width // tc),
        in_specs=[
            pl.BlockSpec((1, n, tc), lambda b, j: (b, 0, off + j)),
            pl.BlockSpec((HY_SHORT, tc), lambda b, j: (0, j)),
            pl.BlockSpec((1, tc), lambda b, j: (0, j)),
        ],
        out_specs=pl.BlockSpec((1, n, tc), lambda b, j: (b, 0, j)),
        out_shape=jax.ShapeDtypeStruct((bsz, n, width), BF16),
        compiler_params=_params(("arbitrary", "arbitrary")),
        name="short_conv",
    )(p, conv_w, conv_b.reshape(1, width))


def _dft_fwd_kernel(c_ref, s_ref, u_ref, ck_ref, sk_ref, nyq_ref, p_ref, q_ref, *, tm):
    i = pl.program_id(0)
    u = u_ref[0]
    cu = jnp.dot(c_ref[...], u, preferred_element_type=F32)
    su = jnp.dot(s_ref[...], u, preferred_element_type=F32)
    ck = ck_ref[...]
    sk = sk_ref[...]
    k = lax.broadcasted_iota(jnp.int32, cu.shape, 0) + i * tm
    p = jnp.where(k == 0, 0.5 * cu * ck, cu * ck - su * sk)
    q = jnp.where(k == 0, 0.5 * su * nyq_ref[...], cu * sk + su * ck)
    p_ref[0] = p.astype(p_ref.dtype)
    q_ref[0] = q.astype(q_ref.dtype)


def _dft_fwd_call(cmat, sfmat, u, u_col0, ck, sk, nyq, k_col0, hw):
    bsz, n, _ = u.shape
    tm = _tile(n, 512)
    tn = _tile(hw, 512)
    uo, ko = u_col0 // tn, k_col0 // tn
    return pl.pallas_call(
        functools.partial(_dft_fwd_kernel, tm=tm),
        grid=(n // tm, bsz, hw // tn),
        in_specs=[
            pl.BlockSpec((tm, n), lambda i, b, j: (i, 0)),
            pl.BlockSpec((tm, n), lambda i, b, j: (i, 0)),
            pl.BlockSpec((1, n, tn), lambda i, b, j: (b, 0, uo + j)),
            pl.BlockSpec((tm, tn), lambda i, b, j: (i, ko + j)),
            pl.BlockSpec((tm, tn), lambda i, b, j: (i, ko + j)),
            pl.BlockSpec((1, tn), lambda i, b, j: (0, ko + j)),
        ],
        out_specs=[pl.BlockSpec((1, tm, tn), lambda i, b, j: (b, i, j))] * 2,
        out_shape=[jax.ShapeDtypeStruct((bsz, n, hw), BF16)] * 2,
        compiler_params=_params(("arbitrary", "arbitrary", "arbitrary")),
        name="dft_forward",
    )(cmat, sfmat, u, ck, sk, nyq)


def _dft_inv_kernel(c_ref, s_ref, p_ref, q_ref, u_ref, gate_ref, bias_ref, o_ref, *, scale):
    y = (jnp.dot(c_ref[...], p_ref[0], preferred_element_type=F32)
         + jnp.dot(s_ref[...], q_ref[0], preferred_element_type=F32)) * scale
    y = y + u_ref[0].astype(F32) * bias_ref[...]
    o_ref[0] = (gate_ref[0].astype(F32) * y).astype(o_ref.dtype)


def _dft_inv_call(cmat, simat, p, q, u, u_col0, gate, gate_col0, bias):
    bsz, n, hw = p.shape
    tm = _tile(n, 512)
    tn = _tile(hw, 512)
    uo, go = u_col0 // tn, gate_col0 // tn
    return pl.pallas_call(
        functools.partial(_dft_inv_kernel, scale=1.0 / n),
        grid=(n // tm, bsz, hw // tn),
        in_specs=[
            pl.BlockSpec((tm, n), lambda i, b, j: (i, 0)),
            pl.BlockSpec((tm, n), lambda i, b, j: (i, 0)),
            pl.BlockSpec((1, n, tn), lambda i, b, j: (b, 0, j)),
            pl.BlockSpec((1, n, tn), lambda i, b, j: (b, 0, j)),
            pl.BlockSpec((1, tm, tn), lambda i, b, j: (b, i, uo + j)),
            pl.BlockSpec((1, tm, tn), lambda i, b, j: (b, i, go + j)),
            pl.BlockSpec((1, tn), lambda i, b, j: (0, j)),
        ],
        out_specs=pl.BlockSpec((1, tm, tn), lambda i, b, j: (b, i, j)),
        out_shape=jax.ShapeDtypeStruct((bsz, n, hw), BF16),
        compiler_params=_params(("arbitrary", "arbitrary", "arbitrary")),
        name="dft_inverse",
    )(cmat, simat, p, q, u, gate, bias.reshape(1, hw))


def _wout_kernel(r_ref, y_ref, wa_ref, wb_ref, x_ref, gate_ref, o_ref):
    y = (jnp.dot(r_ref[0], wa_ref[...], preferred_element_type=F32)
         + jnp.dot(y_ref[0], wb_ref[...], preferred_element_type=F32))
    o_ref[0] = x_ref[0] + gate_ref[0] * y


def _wout_call(r, yh, w_out, x, gate):
    bsz, n, d = x.shape
    rw = r.shape[2]
    assert d == 2 * rw and yh.shape[2] == rw
    tm = _tile(n, 512)
    return pl.pallas_call(
        _wout_kernel,
        grid=(bsz, n // tm),
        in_specs=[
            pl.BlockSpec((1, tm, rw), lambda b, i: (b, i, 0)),
            pl.BlockSpec((1, tm, d - rw), lambda b, i: (b, i, 0)),
            pl.BlockSpec((rw, d), lambda b, i: (0, 0)),
            pl.BlockSpec((rw, d), lambda b, i: (1, 0)),
            pl.BlockSpec((1, tm, d), lambda b, i: (b, i, 0)),
            pl.BlockSpec((1, 1, d), lambda b, i: (b, 0, 0)),
        ],
        out_specs=pl.BlockSpec((1, tm, d), lambda b, i: (b, i, 0)),
        out_shape=jax.ShapeDtypeStruct((bsz, n, d), F32),
        compiler_params=_params(("arbitrary", "arbitrary")),
        name="mixer_out",
    )(r, yh, w_out, w_out, x, gate)


def _pool_kernel(h_ref, m_ref, ic_ref, w_ref, sc_ref, x_ref, gate_ref, o_ref, *, tm, nt):
    i = pl.program_id(2)
    rows_c = pl.ds(pl.multiple_of(i * tm, tm), tm)
    rows_l = pl.ds(pl.multiple_of(jnp.maximum(i - 1, 0) * tm, tm), tm)
    rows_r = pl.ds(pl.multiple_of(jnp.minimum(i + 1, nt - 1) * tm, tm), tm)
    hc = h_ref[0, rows_c, :]
    acc = jnp.dot(m_ref[0, :, tm:2 * tm], hc, preferred_element_type=F32)
    left = jnp.dot(m_ref[0, :, 0:tm], h_ref[0, rows_l, :], preferred_element_type=F32)
    right = jnp.dot(m_ref[0, :, 2 * tm:3 * tm], h_ref[0, rows_r, :], preferred_element_type=F32)
    acc = acc + jnp.where(i > 0, left, 0.0) + jnp.where(i < nt - 1, right, 0.0)
    pooled = acc * ic_ref[0] - hc.astype(F32)
    y = jnp.dot(pooled.astype(BF16), w_ref[0], preferred_element_type=F32) * sc_ref[...]
    o_ref[0] = x_ref[0] + gate_ref[0] * y


def _pool_tables(n, tm):
    rows_per_tile = tm // GRID_W
    n_rows = n // GRID_W
    tok = np.arange(tm)
    r_out, c_out = tok // GRID_W, tok % GRID_W
    tin = np.arange(3 * tm)
    r_in, c_in = tin // GRID_W - rows_per_tile, tin % GRID_W
    mats, inv = [], []
    for win in POOL_WINDOWS:
        lo, hi = -(win // 2), win - 1 - win // 2
        dr = r_in[None, :] - r_out[:, None]
        dc = c_in[None, :] - c_out[:, None]
        mats.append(((dr >= lo) & (dr <= hi) & (dc >= lo) & (dc <= hi)).astype(np.float32))
        idx_r, idx_c = np.arange(n_rows), np.arange(GRID_W)
        cnt_r = np.minimum(idx_r + hi, n_rows - 1) + 1 - np.maximum(idx_r + lo, 0)
        cnt_c = np.minimum(idx_c + hi, GRID_W - 1) + 1 - np.maximum(idx_c + lo, 0)
        inv.append((1.0 / (cnt_r[:, None] * cnt_c[None, :]).astype(np.float64)).reshape(n, 1))
    return (jnp.asarray(np.stack(mats), dtype=BF16),
            jnp.asarray(np.stack(inv).astype(np.float32)))


def _pool_call(h, pool_w, pool_scale, x, gate):
    bsz, n, d = x.shape
    ng = len(POOL_WINDOWS)
    pg = d // ng
    tm = _tile(n, 512)
    nt = n // tm
    assert tm % GRID_W == 0 and tm // GRID_W >= max(POOL_WINDOWS) // 2
    mats, inv = _pool_tables(n, tm)
    return pl.pallas_call(
        functools.partial(_pool_kernel, tm=tm, nt=nt),
        grid=(bsz, ng, nt),
        in_specs=[
            pl.BlockSpec((1, n, pg), lambda b, g, i: (b, 0, g)),
            pl.BlockSpec((1, tm, 3 * tm), lambda b, g, i: (g, 0, 0)),
            pl.BlockSpec((1, tm, 1), lambda b, g, i: (g, i, 0)),
            pl.BlockSpec((1, pg, pg), lambda b, g, i: (g, 0, 0)),
            pl.BlockSpec((1, pg), lambda b, g, i: (0, g)),
            pl.BlockSpec((1, tm, pg), lambda b, g, i: (b, i, g)),
            pl.BlockSpec((1, 1, pg), lambda b, g, i: (b, 0, g)),
        ],
        out_specs=pl.BlockSpec((1, tm, pg), lambda b, g, i: (b, i, g)),
        out_shape=jax.ShapeDtypeStruct((bsz, n, d), F32),
        compiler_params=_params(("arbitrary", "arbitrary", "arbitrary")),
        name="pool_mixer",
    )(h, mats, inv, pool_w, pool_scale.reshape(1, d), x, gate)


def _rotary_tables(n):
    pos = np.arange(n)
    row = (pos // GRID_W).astype(np.float32)
    col = (pos % GRID_W).astype(np.float32)
    n_freq = HEAD_DIM // 4
    inv = (np.float32(ROPE_BASE) ** (-np.arange(n_freq, dtype=np.float32) / np.float32(n_freq))).astype(np.float32)
    ang = np.concatenate([row[:, None] * inv, col[:, None] * inv], axis=-1).astype(np.float32)
    cos, sin = np.cos(ang.astype(np.float64)), np.sin(ang.astype(np.float64))
    cos2 = np.concatenate([cos, cos], axis=-1).astype(np.float32)
    sin2 = np.concatenate([-sin, sin], axis=-1).astype(np.float32)
    return jnp.asarray(cos2), jnp.asarray(sin2)


def kernel(x, c, ctx, c_ctx, w_mod, b_mod, ffn1_w1, ffn1_w3, ffn1_w2, ffn2_w1, ffn2_w3, ffn2_w2, ab_w_in, ab_w_out, ret_log_decay, hy_conv_w, hy_conv_b, hy_f_w1, hy_f_b1, hy_f_w2, hy_f_b2, hy_f_w3, hy_f_b3, hy_f_freq, hy_f_w4, hy_bias, pool_w, pool_scale, final_gain):
    bsz, n, d = x.shape
    n_ctx = ctx.shape[1]
    depth = w_mod.shape[0]
    assert depth == 2, "layer 0 = retention/Hyena mixer, layer 1 = pooling mixer"
    ret_w = RET_HEADS * HEAD_DIM
    hw = d - ret_w
    assert hy_f_w4.shape[2] == HY_ORDER * 2 * hw

    c_all = jnp.concatenate([c, c_ctx[None, :], jnp.zeros((8 - bsz - 1, d), F32)], axis=0)
    mod = _mod_call(c_all, w_mod, b_mod).reshape(depth, 8, N_MOD, d)

    def mod_rows(layer, base, count=3):
        return mod[layer, :bsz, base:base + count]

    def mod_ctx(base, count=3):
        return jnp.broadcast_to(mod[0, bsz:bsz + 1, base:base + count], (bsz, count, d))

    bf = lambda w: w.astype(BF16)

    ffn_w = _cast_call([ffn1_w1, ffn1_w3, ffn1_w2, ffn2_w1, ffn2_w3, ffn2_w2], ffn1_w1.shape[2] // 256)
    ffn1_w, ffn2_w = ffn_w[:3], ffn_w[3:]

    x, h = _ffn_call(x, mod_rows(0, 0), *ffn1_w, 0, modn=mod_rows(0, 3))
    _, h_c = _ffn_call(ctx, mod_ctx(0), *ffn1_w, 0, modn=mod_ctx(3))

    w_in, w_out = _cast_call([ab_w_in, ab_w_out], 8)
    w_in, w_out = w_in[0], w_out[0]
    p = _matmul_call(h.reshape(bsz * n, d), w_in, BF16).reshape(bsz, n, -1)
    kvc = _matmul_call(h_c.reshape(bsz * n_ctx, d), w_in, BF16, b_cols=(ret_w, 3 * ret_w))
    kvc = kvc.reshape(bsz, n_ctx, 2 * ret_w)

    cos2, sin2 = _rotary_tables(n)
    r = _ret_call(p, kvc, ret_log_decay[0], cos2, sin2, ret_w)

    hsum, hdiff, nyq = _filt_call(n, hy_f_w1[0], hy_f_b1[0], hy_f_w2[0], hy_f_b2[0], hy_f_w3[0],
                                  hy_f_b3[0], hy_f_freq[0], hy_f_w4[0], hw)
    cmat, sfmat, simat = _dftgen_call(n)
    ck = _matmul_call(cmat, hsum, F32, 512, 512)
    sk = _matmul_call(sfmat, hdiff, F32, 512, 512)
    u = _sconv_call(p, hy_conv_w[0], hy_conv_b[0], 4 * ret_w)
    pq = _dft_fwd_call(cmat, sfmat, u, 0, ck, sk, nyq, 0, hw)
    z = _dft_inv_call(cmat, simat, pq[0], pq[1], u, 0, u, hw, hy_bias[0, 0])
    pq = _dft_fwd_call(cmat, sfmat, z, 0, ck, sk, nyq, hw, hw)
    yh = _dft_inv_call(cmat, simat, pq[0], pq[1], z, 0, u, 2 * hw, hy_bias[0, 1])

    x = _wout_call(r, yh, w_out, x, mod_rows(0, 5, 1))
    x = _ffn_call(x, mod_rows(0, 6), *ffn2_w, 0)

    x, h = _ffn_call(x, mod_rows(1, 0), *ffn1_w, 1, modn=mod_rows(1, 3))
    x = _pool_call(h, bf(pool_w[0]), pool_scale[0], x, mod_rows(1, 5, 1))
    return _ffn_call(x, mod_rows(1, 6), *ffn2_w, 1, gain=final_gain)
```

```python
import functools
import math

import numpy as np
import jax
import jax.numpy as jnp
from jax import lax
from jax.experimental import pallas as pl
from jax.experimental.pallas import tpu as pltpu

F32 = jnp.float32
BF16 = jnp.bfloat16
HIGHEST = lax.Precision.HIGHEST

EPS = 1e-6
GRID_W = 64
N_MOD = 9
RET_HEADS = 8
HEAD_DIM = 128
HY_ORDER = 2
HY_SHORT = 3
HY_EMB_DIM = 33
HY_DECAY_SHORT_PCT = 0.3
HY_DECAY_LONG_PCT = 1.5
HY_DECAY_TARGET = 1e-2
ROPE_BASE = 10000.0
POOL_WINDOWS = (2, 4, 8, 16)
RET_CHUNK = 256
RET_UNROLL = 4
LANE = 128
VMEM_LIMIT = 56 * 1024 * 1024


def _tile(dim, pref):
    if dim <= pref:
        return dim
    for t in range(pref - pref % LANE, 0, -LANE):
        if dim % t == 0:
            return t
    raise ValueError((dim, pref))


def _params(sem):
    return pltpu.CompilerParams(dimension_semantics=sem, vmem_limit_bytes=VMEM_LIMIT)


def _rms(x):
    return x * lax.rsqrt(jnp.mean(x * x, axis=-1, keepdims=True) + EPS)


def _silu(x):
    return x * jax.nn.sigmoid(x)


def _mod_kernel(c_ref, w_ref, b_ref, o_ref):
    sc = _silu(c_ref[...])
    o_ref[0] = jnp.dot(sc, w_ref[0], preferred_element_type=F32, precision=HIGHEST) + b_ref[0]


def _mod_call(c_all, w_mod, b_mod):
    depth, d, nd = w_mod.shape
    rows = c_all.shape[0]
    tn = _tile(nd, 1024)
    return pl.pallas_call(
        _mod_kernel,
        grid=(depth, nd // tn),
        in_specs=[
            pl.BlockSpec((rows, d), lambda l, j: (0, 0)),
            pl.BlockSpec((1, d, tn), lambda l, j: (l, 0, j)),
            pl.BlockSpec((1, 1, tn), lambda l, j: (l, 0, j)),
        ],
        out_specs=pl.BlockSpec((1, rows, tn), lambda l, j: (l, 0, j)),
        out_shape=jax.ShapeDtypeStruct((depth, rows, nd), F32),
        compiler_params=_params(("arbitrary", "arbitrary")),
        name="mod",
    )(c_all, w_mod, b_mod.reshape(depth, 1, nd))


def _ffn_kernel(*refs, n_f, emit_h, final):
    x_ref, mod_ref, w1_ref, w3_ref, w2_ref = refs[:5]
    pos = 5
    modn_ref = gain_ref = hn_ref = None
    if emit_h:
        modn_ref = refs[pos]
        pos += 1
    if final:
        gain_ref = refs[pos]
        pos += 1
    o_ref = refs[pos]
    pos += 1
    if emit_h:
        hn_ref = refs[pos]
        pos += 1
    h_scr, acc_scr = refs[pos], refs[pos + 1]
    f = pl.program_id(2)

    assert n_f >= 3

    def partial_out(h):
        a = jnp.dot(h, w1_ref[...], preferred_element_type=F32)
        b = jnp.dot(h, w3_ref[...], preferred_element_type=F32)
        g = (_silu(a) * b).astype(BF16)
        return jnp.dot(g, w2_ref[...], preferred_element_type=F32)

    @pl.when(f == 0)
    def _():
        m = mod_ref[0]
        h = (_rms(x_ref[0]) * (1.0 + m[1:2]) + m[0:1]).astype(BF16)
        h_scr[...] = h
        acc_scr[...] = partial_out(h)

    @pl.when(jnp.logical_and(f > 0, f < n_f - 1))
    def _():
        acc_scr[...] += partial_out(h_scr[...])

    @pl.when(f == n_f - 1)
    def _():
        m = mod_ref[0]
        xo = x_ref[0] + 0.5 * m[2:3] * (acc_scr[...] + partial_out(h_scr[...]))
        if emit_h:
            mn = modn_ref[0]
            hn_ref[0] = (_rms(xo) * (1.0 + mn[1:2]) + mn[0:1]).astype(BF16)
        if final:
            xo = _rms(xo) * gain_ref[...]
        o_ref[0] = xo


def _ffn_call(x, mod3, w1, w3, w2, layer, modn=None, gain=None):
    bsz, s, d = x.shape
    n_f, tf = w1.shape[1], w1.shape[3]
    tm = _tile(s, 512)
    emit_h = modn is not None
    final = gain is not None
    in_specs = [
        pl.BlockSpec((1, tm, d), lambda b, i, f: (b, i, 0)),
        pl.BlockSpec((1, 3, d), lambda b, i, f: (b, 0, 0)),
        pl.BlockSpec((None, None, d, tf), lambda b, i, f: (layer, f, 0, 0)),
        pl.BlockSpec((None, None, d, tf), lambda b, i, f: (layer, f, 0, 0)),
        pl.BlockSpec((None, tf, d), lambda b, i, f: (layer, f, 0)),
    ]
    args = [x, mod3, w1, w3, w2]
    if emit_h:
        in_specs.append(pl.BlockSpec((1, 3, d), lambda b, i, f: (b, 0, 0)))
        args.append(modn)
    if final:
        in_specs.append(pl.BlockSpec((1, d), lambda b, i, f: (0, 0)))
        args.append(gain.reshape(1, d))
    out_specs = [pl.BlockSpec((1, tm, d), lambda b, i, f: (b, i, 0))]
    out_shape = [jax.ShapeDtypeStruct((bsz, s, d), F32)]
    if emit_h:
        out_specs.append(pl.BlockSpec((1, tm, d), lambda b, i, f: (b, i, 0)))
        out_shape.append(jax.ShapeDtypeStruct((bsz, s, d), BF16))
    outs = pl.pallas_call(
        functools.partial(_ffn_kernel, n_f=n_f, emit_h=emit_h, final=final),
        grid=(bsz, s // tm, n_f),
        in_specs=in_specs,
        out_specs=out_specs,
        out_shape=out_shape,
        scratch_shapes=[pltpu.VMEM((tm, d), BF16), pltpu.VMEM((tm, d), F32)],
        compiler_params=_params(("arbitrary", "arbitrary", "arbitrary")),
        name="ffn",
    )(*args)
    return outs if emit_h else outs[0]


def _cast_kernel(*refs):
    half = len(refs) // 2
    for src, dst in zip(refs[:half], refs[half:]):
        dst[...] = src[...].astype(dst.dtype).reshape(dst.shape)


def _cast_call(arrays, steps, tile_cols=False):
    depth = arrays[0].shape[0]
    specs, out_specs, out_shape = [], [], []
    for a in arrays:
        _, r, c = a.shape
        if c >= r:
            specs.append(pl.BlockSpec((1, r, c // steps), lambda l, j: (l, 0, j)))
        else:
            assert not tile_cols
            specs.append(pl.BlockSpec((1, r // steps, c), lambda l, j: (l, j, 0)))
        if tile_cols:
            out_specs.append(pl.BlockSpec((1, 1, r, c // steps), lambda l, j: (l, j, 0, 0)))
            out_shape.append(jax.ShapeDtypeStruct((depth, steps, r, c // steps), BF16))
        else:
            out_specs.append(specs[-1])
            out_shape.append(jax.ShapeDtypeStruct(a.shape, BF16))
    return pl.pallas_call(
        _cast_kernel,
        grid=(depth, steps),
        in_specs=specs,
        out_specs=out_specs,
        out_shape=out_shape,
        compiler_params=_params(("arbitrary", "arbitrary")),
        name="cast_weights",
    )(*arrays)


def _matmul_kernel(a_ref, b_ref, o_ref):
    o_ref[...] = jnp.dot(a_ref[...], b_ref[...], preferred_element_type=F32).astype(o_ref.dtype)


def _matmul_call(a, b, out_dtype, tm_pref=1024, tn_pref=1024, b_cols=None):
    m, k = a.shape
    c0, c1 = b_cols if b_cols is not None else (0, b.shape[1])
    n = c1 - c0
    tm = _tile(m, tm_pref)
    tn = math.gcd(_tile(n, tn_pref), c0) if c0 else _tile(n, tn_pref)
    assert tn % LANE == 0
    joff = c0 // tn
    return pl.pallas_call(
        _matmul_kernel,
        grid=(m // tm, n // tn),
        in_specs=[
            pl.BlockSpec((tm, k), lambda i, j: (i, 0)),
            pl.BlockSpec((k, tn), lambda i, j: (0, joff + j)),
        ],
        out_specs=pl.BlockSpec((tm, tn), lambda i, j: (i, j)),
        out_shape=jax.ShapeDtypeStruct((m, n), out_dtype),
        compiler_params=_params(("arbitrary", "arbitrary")),
        name="matmul",
    )(a, b)


def _ret_kernel(lg_ref, q_ref, k_ref, v_ref, g_ref, cos_ref, sin_ref, kc_ref, vc_ref, o_ref,
                qr_scr, kr_scr, upd_scr, st_scr, *, n, n_ctx, chunk):
    hd = HEAD_DIM
    head = pl.program_id(1)
    lgf = lg_ref[0, head]
    lgb = lg_ref[1, head]
    nc = n // chunk
    k_scale = hd ** -0.5

    posc = lax.broadcasted_iota(jnp.int32, (n_ctx, hd), 0).astype(F32)
    kc = kc_ref[0].astype(F32) * k_scale
    vc = vc_ref[0]
    kcf = (kc * jnp.exp((n_ctx - 1.0 - posc) * lgf)).T.astype(BF16)
    kcb = (kc * jnp.exp(posc * lgb)).T.astype(BF16)
    s0_f = jnp.dot(kcf, vc, preferred_element_type=F32)
    s0_b = jnp.dot(kcb, vc, preferred_element_type=F32)

    pos = lax.broadcasted_iota(jnp.int32, (chunk, hd), 0).astype(F32)
    qdec_f = jnp.exp((pos + 1.0) * lgf)
    kdec_f = jnp.exp((chunk - 1.0 - pos) * lgf)
    qdec_b = jnp.exp((chunk - pos) * lgb)
    kdec_b = jnp.exp(pos * lgb)
    cdec_f = jnp.exp(jnp.full((1, hd), chunk, F32) * lgf)
    cdec_b = jnp.exp(jnp.full((1, hd), chunk, F32) * lgb)
    ri = lax.broadcasted_iota(jnp.int32, (chunk, chunk), 0)
    ci = lax.broadcasted_iota(jnp.int32, (chunk, chunk), 1)
    diff = (ri - ci).astype(F32)
    dmat = (jnp.where(diff >= 0, jnp.exp(jnp.maximum(diff, 0.0) * lgf), 0.0)
            + jnp.where(diff <= 0, jnp.exp(jnp.maximum(-diff, 0.0) * lgb), 0.0))

    def rotary(x, rows):
        return x * cos_ref[rows, :] + pltpu.roll(x, hd // 2, 1) * sin_ref[rows, :]

    def chunk_rows(c):
        return pl.ds(pl.multiple_of(c * chunk, chunk), chunk)

    def prep_body(c, carry):
        rows = chunk_rows(c)
        qr_scr[rows, :] = rotary(q_ref[0, rows, :].astype(F32), rows)
        k = rotary(k_ref[0, rows, :].astype(F32), rows) * k_scale
        kr_scr[rows, :] = k
        kk = jnp.concatenate([k * kdec_f, k * kdec_b], axis=1).T.astype(BF16)
        upd_scr[c] = jnp.dot(kk, v_ref[0, rows, :], preferred_element_type=F32)
        return carry

    lax.fori_loop(0, nc, prep_body, 0, unroll=math.gcd(nc, RET_UNROLL))

    def fwd_scan(c, sf):
        st_scr[c, 0:hd, :] = sf.astype(BF16)
        return cdec_f * sf + upd_scr[c, 0:hd, :]

    def bwd_scan(j, sb):
        c = nc - 1 - j
        st_scr[c, hd:2 * hd, :] = sb.astype(BF16)
        return cdec_b * sb + upd_scr[c, hd:2 * hd, :]

    lax.fori_loop(0, nc, fwd_scan, s0_f, unroll=math.gcd(nc, RET_UNROLL))
    lax.fori_loop(0, nc, bwd_scan, s0_b, unroll=math.gcd(nc, RET_UNROLL))

    def out_body(c, carry):
        rows = chunk_rows(c)
        q = qr_scr[rows, :]
        s = lax.dot_general(q.astype(BF16), kr_scr[rows, :].astype(BF16), (((1,), (1,)), ((), ())),
                            preferred_element_type=F32) * dmat
        qq = jnp.concatenate([q * qdec_f, q * qdec_b], axis=1).astype(BF16)
        o = (jnp.dot(s.astype(BF16), v_ref[0, rows, :], preferred_element_type=F32)
             + jnp.dot(qq, st_scr[c], preferred_element_type=F32))
        g = g_ref[0, rows, :].astype(F32)
        o_ref[0, rows, :] = (_rms(o) * _silu(g)).astype(o_ref.dtype)
        return carry

    lax.fori_loop(0, nc, out_body, 0, unroll=math.gcd(nc, RET_UNROLL))


def _ret_call(p, kvc, log_decay, cos2, sin2, ret_w):
    bsz, n, _ = p.shape
    n_ctx = kvc.shape[1]
    heads = ret_w // HEAD_DIM
    hd = HEAD_DIM
    chunk = _tile(n, RET_CHUNK)
    kern = functools.partial(_ret_kernel, n=n, n_ctx=n_ctx, chunk=chunk)
    col = lambda off: (lambda b, h, lg: (b, 0, off * heads + h))
    return pl.pallas_call(
        kern,
        grid_spec=pltpu.PrefetchScalarGridSpec(
            num_scalar_prefetch=1,
            grid=(bsz, heads),
            in_specs=[
                pl.BlockSpec((1, n, hd), col(0)),
                pl.BlockSpec((1, n, hd), col(1)),
                pl.BlockSpec((1, n, hd), col(2)),
                pl.BlockSpec((1, n, hd), col(3)),
                pl.BlockSpec((n, hd), lambda b, h, lg: (0, 0)),
                pl.BlockSpec((n, hd), lambda b, h, lg: (0, 0)),
                pl.BlockSpec((1, n_ctx, hd), col(0)),
                pl.BlockSpec((1, n_ctx, hd), col(1)),
            ],
            out_specs=pl.BlockSpec((1, n, hd), lambda b, h, lg: (b, 0, h)),
            scratch_shapes=[
                pltpu.VMEM((n, hd), F32),
                pltpu.VMEM((n, hd), F32),
                pltpu.VMEM((n // chunk, 2 * hd, hd), F32),
                pltpu.VMEM((n // chunk, 2 * hd, hd), BF16),
            ],
        ),
        out_shape=jax.ShapeDtypeStruct((bsz, n, ret_w), BF16),
        compiler_params=_params(("arbitrary", "arbitrary")),
        name="retention",
    )(log_decay, p, p, p, p, cos2, sin2, kvc, kvc)


def _filt_kernel(z_ref, w1_ref, b1_ref, w2_ref, b2_ref, w3_ref, b3_ref, fr_ref, w4_ref, dl_ref,
                 hs_ref, hd_ref, nyq_ref, *, n, tt, hw):
    i = pl.program_id(0)
    dot = functools.partial(jnp.dot, preferred_element_type=F32, precision=HIGHEST)
    fr = fr_ref[...]
    a = jnp.sin(fr[0:1] * (dot(z_ref[...], w1_ref[...]) + b1_ref[...]))
    a = jnp.sin(fr[1:2] * (dot(a, w2_ref[...]) + b2_ref[...]))
    a = jnp.sin(fr[2:3] * (dot(a, w3_ref[...]) + b3_ref[...]))
    hf = jnp.dot(a.astype(BF16), w4_ref[...], preferred_element_type=F32)
    ti = lax.broadcasted_iota(jnp.int32, (tt, hw), 0) + i * tt
    t = ti.astype(F32) * (1.0 / (n - 1.0))
    window = jnp.exp(-t * dl_ref[...])
    sign = (1 - 2 * (ti & 1)).astype(F32)

    @pl.when(i == 0)
    def _():
        nyq_ref[...] = jnp.zeros_like(nyq_ref)

    for o in range(HY_ORDER):
        h_f = hf[:, (2 * o) * hw:(2 * o + 1) * hw] * window
        h_b = jnp.where(ti == 0, 0.0, hf[:, (2 * o + 1) * hw:(2 * o + 2) * hw] * window)
        hsum = h_f + h_b
        hs_ref[:, o * hw:(o + 1) * hw] = hsum.astype(BF16)
        hd_ref[:, o * hw:(o + 1) * hw] = (h_f - h_b).astype(BF16)
        nyq_ref[:, o * hw:(o + 1) * hw] += jnp.sum(hsum * sign, axis=0, keepdims=True)


def _filt_call(n, fw1, fb1, fw2, fb2, fw3, fb3, freq, fw4, hw):
    hid = fw1.shape[1]
    pk = LANE
    bands = (HY_EMB_DIM - 1) // 2
    tpos = np.linspace(0.0, 1.0, n, dtype=np.float32)[:, None].astype(np.float64)
    fb = np.linspace(1e-4, bands - 1, bands, dtype=np.float32)[None, :].astype(np.float64)
    wv = (2.0 * math.pi * np.arange(n, dtype=np.float64)[:, None] / n)
    z = np.concatenate([tpos, np.cos(fb * wv), -np.sin(fb * wv)], axis=-1)
    z = np.pad(z, ((0, 0), (0, pk - HY_EMB_DIM))).astype(np.float32)
    max_decay = math.log(HY_DECAY_TARGET) / HY_DECAY_SHORT_PCT
    min_decay = math.log(HY_DECAY_TARGET) / HY_DECAY_LONG_PCT
    deltas = np.abs(np.linspace(min_decay, max_decay, hw, dtype=np.float32)).reshape(1, hw)

    padc = lambda w: jnp.pad(w, ((0, 0), (0, pk - hid)))
    w1 = jnp.pad(fw1, ((0, pk - HY_EMB_DIM), (0, pk - hid)))
    w2 = jnp.pad(fw2, ((0, pk - hid), (0, pk - hid)))
    w3 = jnp.pad(fw3, ((0, pk - hid), (0, pk - hid)))
    w4 = jnp.pad(fw4, ((0, pk - hid), (0, 0))).astype(BF16)
    b1, b2, b3 = (padc(b.reshape(1, hid)) for b in (fb1, fb2, fb3))
    fr = jnp.pad(freq, ((0, 8 - freq.shape[0]), (0, pk - hid)))
    tt = _tile(n, 256)
    nw = HY_ORDER * hw
    full = lambda shape: pl.BlockSpec(shape, lambda i: (0, 0))
    return pl.pallas_call(
        functools.partial(_filt_kernel, n=n, tt=tt, hw=hw),
        grid=(n // tt,),
        in_specs=[
            pl.BlockSpec((tt, pk), lambda i: (i, 0)),
            full((pk, pk)), full((1, pk)), full((pk, pk)), full((1, pk)),
            full((pk, pk)), full((1, pk)), full((8, pk)), full((pk, 2 * nw)), full((1, hw)),
        ],
        out_specs=[
            pl.BlockSpec((tt, nw), lambda i: (i, 0)),
            pl.BlockSpec((tt, nw), lambda i: (i, 0)),
            pl.BlockSpec((1, nw), lambda i: (0, 0)),
        ],
        out_shape=[
            jax.ShapeDtypeStruct((n, nw), BF16),
            jax.ShapeDtypeStruct((n, nw), BF16),
            jax.ShapeDtypeStruct((1, nw), F32),
        ],
        compiler_params=_params(("arbitrary",)),
        name="hyena_filters",
    )(jnp.asarray(z), w1, b1, w2, b2, w3, b3, fr, w4, jnp.asarray(deltas))


def _dftgen_kernel(ac_ref, as_ref, bc_ref, bs_ref, c_ref, sf_ref, si_ref, *, rows, n):
    i = pl.program_id(0)
    ac, as_ = ac_ref[0], as_ref[0]
    bc, bs = bc_ref[...], bs_ref[...]
    c = ac * bc - as_ * bs
    s = as_ * bc + ac * bs
    k = lax.broadcasted_iota(jnp.int32, (rows, n), 0) + i * rows
    t = lax.broadcasted_iota(jnp.int32, (rows, n), 1)
    sign = (1 - 2 * ((k + t) & 1)).astype(F32)
    c_ref[...] = c.astype(BF16)
    sf_ref[...] = jnp.where(k == 0, sign, s).astype(BF16)
    si_ref[...] = jnp.where(t == 0, sign, s).astype(BF16)


def _dftgen_call(n):
    rows = 64
    assert n % rows == 0
    big = 2 * n
    t = np.arange(n, dtype=np.int64)[None, :]
    khi = (np.arange(n // rows, dtype=np.int64) * rows)[:, None]
    klo = np.arange(rows, dtype=np.int64)[:, None]
    ang_a = 2.0 * math.pi * ((khi * t) % big) / big
    ang_b = 2.0 * math.pi * ((klo * t) % big) / big
    a_c = jnp.asarray(np.cos(ang_a).astype(np.float32).reshape(n // rows, 1, n))
    a_s = jnp.asarray(np.sin(ang_a).astype(np.float32).reshape(n // rows, 1, n))
    b_c = jnp.asarray(np.cos(ang_b).astype(np.float32))
    b_s = jnp.asarray(np.sin(ang_b).astype(np.float32))
    mat = jax.ShapeDtypeStruct((n, n), BF16)
    return pl.pallas_call(
        functools.partial(_dftgen_kernel, rows=rows, n=n),
        grid=(n // rows,),
        in_specs=[
            pl.BlockSpec((1, 1, n), lambda i: (i, 0, 0)),
            pl.BlockSpec((1, 1, n), lambda i: (i, 0, 0)),
            pl.BlockSpec((rows, n), lambda i: (0, 0)),
            pl.BlockSpec((rows, n), lambda i: (0, 0)),
        ],
        out_specs=[pl.BlockSpec((rows, n), lambda i: (i, 0))] * 3,
        out_shape=[mat, mat, mat],
        compiler_params=_params(("arbitrary",)),
        name="dft_matrices",
    )(a_c, a_s, b_c, b_s)


def _sconv_kernel(p_ref, w_ref, b_ref, o_ref, *, n, rc):
    w = w_ref[...]
    bias = b_ref[...]
    row = lax.broadcasted_iota(jnp.int32, (rc, p_ref.shape[2]), 0)
    for c in range(n // rc):
        cur = p_ref[0, c * rc:(c + 1) * rc, :].astype(F32)
        up = pltpu.roll(cur, 1, 0)
        dn = pltpu.roll(cur, rc - 1, 0)
        if c > 0:
            prev = p_ref[0, c * rc - 16:c * rc, :].astype(F32)[15:16]
        else:
            prev = jnp.zeros((1, cur.shape[1]), F32)
        if c < n // rc - 1:
            nxt = p_ref[0, (c + 1) * rc:(c + 1) * rc + 16, :].astype(F32)[0:1]
        else:
            nxt = jnp.zeros((1, cur.shape[1]), F32)
        up = jnp.where(row == 0, prev, up)
        dn = jnp.where(row == rc - 1, nxt, dn)
        out = bias + up * w[0:1] + cur * w[1:2] + dn * w[2:3]
        o_ref[0, c * rc:(c + 1) * rc, :] = out.astype(o_ref.dtype)


def _sconv_call(p, conv_w, conv_b, col0):
    bsz, n, _ = p.shape
    width = conv_w.shape[1]
    tc = _tile(width, 512)
    rc = _tile(n, 512)
    off = col0 // tc
    return pl.pallas_call(
        functools.partial(_sconv_kernel, n=n, rc=rc),
        grid=(bsz, width // tc),
        in_specs=[
            pl.BlockSpec((1, n, tc), lambda b, j: (b, 0, off + j)),
            pl.BlockSpec((HY_SHORT, tc), lambda b, j: (0, j)),
            pl.BlockSpec((1, tc), lambda b, j: (0, j)),
        ],
        out_specs=pl.BlockSpec((1, n, tc), lambda b, j: (b, 0, j)),
        out_shape=jax.ShapeDtypeStruct((bsz, n, width), BF16),
        compiler_params=_params(("arbitrary", "arbitrary")),
        name="short_conv",
    )(p, conv_w, conv_b.reshape(1, width))


def _dft_fwd_kernel(c_ref, s_ref, u_ref, ck_ref, sk_ref, nyq_ref, p_ref, q_ref, *, tm):
    i = pl.program_id(0)
    u = u_ref[0]
    cu = jnp.dot(c_ref[...], u, preferred_element_type=F32)
    su = jnp.dot(s_ref[...], u, preferred_element_type=F32)
    ck = ck_ref[...]
    sk = sk_ref[...]
    k = lax.broadcasted_iota(jnp.int32, cu.shape, 0) + i * tm
    p = jnp.where(k == 0, 0.5 * cu * ck, cu * ck - su * sk)
    q = jnp.where(k == 0, 0.5 * su * nyq_ref[...], cu * sk + su * ck)
    p_ref[0] = p.astype(p_ref.dtype)
    q_ref[0] = q.astype(q_ref.dtype)


def _dft_fwd_call(cmat, sfmat, u, u_col0, ck, sk, nyq, k_col0, hw):
    bsz, n, _ = u.shape
    tm = _tile(n, 512)
    tn = _tile(hw, 512)
    uo, ko = u_col0 // tn, k_col0 // tn
    return pl.pallas_call(
        functools.partial(_dft_fwd_kernel, tm=tm),
        grid=(n // tm, bsz, hw // tn),
        in_specs=[
            pl.BlockSpec((tm, n), lambda i, b, j: (i, 0)),
            pl.BlockSpec((tm, n), lambda i, b, j: (i, 0)),
            pl.BlockSpec((1, n, tn), lambda i, b, j: (b, 0, uo + j)),
            pl.BlockSpec((tm, tn), lambda i, b, j: (i, ko + j)),
            pl.BlockSpec((tm, tn), lambda i, b, j: (i, ko + j)),
            pl.BlockSpec((1, tn), lambda i, b, j: (0, ko + j)),
        ],
        out_specs=[pl.BlockSpec((1, tm, tn), lambda i, b, j: (b, i, j))] * 2,
        out_shape=[jax.ShapeDtypeStruct((bsz, n, hw), BF16)] * 2,
        compiler_params=_params(("arbitrary", "arbitrary", "arbitrary")),
        name="dft_forward",
    )(cmat, sfmat, u, ck, sk, nyq)


def _dft_inv_kernel(c_ref, s_ref, p_ref, q_ref, u_ref, gate_ref, bias_ref, o_ref, *, scale):
    y = (jnp.dot(c_ref[...], p_ref[0], preferred_element_type=F32)
         + jnp.dot(s_ref[...], q_ref[0], preferred_element_type=F32)) * scale
    y = y + u_ref[0].astype(F32) * bias_ref[...]
    o_ref[0] = (gate_ref[0].astype(F32) * y).astype(o_ref.dtype)


def _dft_inv_call(cmat, simat, p, q, u, u_col0, gate, gate_col0, bias):
    bsz, n, hw = p.shape
    tm = _tile(n, 512)
    tn = _tile(hw, 512)
    uo, go = u_col0 // tn, gate_col0 // tn
    return pl.pallas_call(
        functools.partial(_dft_inv_kernel, scale=1.0 / n),
        grid=(n // tm, bsz, hw // tn),
        in_specs=[
            pl.BlockSpec((tm, n), lambda i, b, j: (i, 0)),
            pl.BlockSpec((tm, n), lambda i, b, j: (i, 0)),
            pl.BlockSpec((1, n, tn), lambda i, b, j: (b, 0, j)),
            pl.BlockSpec((1, n, tn), lambda i, b, j: (b, 0, j)),
            pl.BlockSpec((1, tm, tn), lambda i, b, j: (b, i, uo + j)),
            pl.BlockSpec((1, tm, tn), lambda i, b, j: (b, i, go + j)),
            pl.BlockSpec((1, tn), lambda i, b, j: (0, j)),
        ],
        out_specs=pl.BlockSpec((1, tm, tn), lambda i, b, j: (b, i, j)),
        out_shape=jax.ShapeDtypeStruct((bsz, n, hw), BF16),
        compiler_params=_params(("arbitrary", "arbitrary", "arbitrary")),
        name="dft_inverse",
    )(cmat, simat, p, q, u, gate, bias.reshape(1, hw))


def _wout_kernel(r_ref, y_ref, wa_ref, wb_ref, x_ref, gate_ref, o_ref):
    y = (jnp.dot(r_ref[0], wa_ref[...], preferred_element_type=F32)
         + jnp.dot(y_ref[0], wb_ref[...], preferred_element_type=F32))
    o_ref[0] = x_ref[0] + gate_ref[0] * y


def _wout_call(r, yh, w_out, x, gate):
    bsz, n, d = x.shape
    rw = r.shape[2]
    assert d == 2 * rw and yh.shape[2] == rw
    tm = _tile(n, 512)
    return pl.pallas_call(
        _wout_kernel,
        grid=(bsz, n // tm),
        in_specs=[
            pl.BlockSpec((1, tm, rw), lambda b, i: (b, i, 0)),
            pl.BlockSpec((1, tm, d - rw), lambda b, i: (b, i, 0)),
            pl.BlockSpec((rw, d), lambda b, i: (0, 0)),
            pl.BlockSpec((rw, d), lambda b, i: (1, 0)),
            pl.BlockSpec((1, tm, d), lambda b, i: (b, i, 0)),
            pl.BlockSpec((1, 1, d), lambda b, i: (b, 0, 0)),
        ],
        out_specs=pl.BlockSpec((1, tm, d), lambda b, i: (b, i, 0)),
        out_shape=jax.ShapeDtypeStruct((bsz, n, d), F32),
        compiler_params=_params(("arbitrary", "arbitrary")),
        name="mixer_out",
    )(r, yh, w_out, w_out, x, gate)


def _pool_kernel(h_ref, m_ref, ic_ref, w_ref, sc_ref, x_ref, gate_ref, o_ref, *, tm, nt):
    i = pl.program_id(2)
    rows_c = pl.ds(pl.multiple_of(i * tm, tm), tm)
    rows_l = pl.ds(pl.multiple_of(jnp.maximum(i - 1, 0) * tm, tm), tm)
    rows_r = pl.ds(pl.multiple_of(jnp.minimum(i + 1, nt - 1) * tm, tm), tm)
    hc = h_ref[0, rows_c, :]
    acc = jnp.dot(m_ref[0, :, tm:2 * tm], hc, preferred_element_type=F32)
    left = jnp.dot(m_ref[0, :, 0:tm], h_ref[0, rows_l, :], preferred_element_type=F32)
    right = jnp.dot(m_ref[0, :, 2 * tm:3 * tm], h_ref[0, rows_r, :], preferred_element_type=F32)
    acc = acc + jnp.where(i > 0, left, 0.0) + jnp.where(i < nt - 1, right, 0.0)
    pooled = acc * ic_ref[0] - hc.astype(F32)
    y = jnp.dot(pooled.astype(BF16), w_ref[0], preferred_element_type=F32) * sc_ref[...]
    o_ref[0] = x_ref[0] + gate_ref[0] * y


def _pool_tables(n, tm):
    rows_per_tile = tm // GRID_W
    n_rows = n // GRID_W
    tok = np.arange(tm)
    r_out, c_out = tok // GRID_W, tok % GRID_W
    tin = np.arange(3 * tm)
    r_in, c_in = tin // GRID_W - rows_per_tile, tin % GRID_W
    mats, inv = [], []
    for win in POOL_WINDOWS:
        lo, hi = -(win // 2), win - 1 - win // 2
        dr = r_in[None, :] - r_out[:, None]
        dc = c_in[None, :] - c_out[:, None]
        mats.append(((dr >= lo) & (dr <= hi) & (dc >= lo) & (dc <= hi)).astype(np.float32))
        idx_r, idx_c = np.arange(n_rows), np.arange(GRID_W)
        cnt_r = np.minimum(idx_r + hi, n_rows - 1) + 1 - np.maximum(idx_r + lo, 0)
        cnt_c = np.minimum(idx_c + hi, GRID_W - 1) + 1 - np.maximum(idx_c + lo, 0)
        inv.append((1.0 / (cnt_r[:, None] * cnt_c[None, :]).astype(np.float64)).reshape(n, 1))
    return (jnp.asarray(np.stack(mats), dtype=BF16),
            jnp.asarray(np.stack(inv).astype(np.float32)))


def _pool_call(h, pool_w, pool_scale, x, gate):
    bsz, n, d = x.shape
    ng = len(POOL_WINDOWS)
    pg = d // ng
    tm = _tile(n, 512)
    nt = n // tm
    assert tm % GRID_W == 0 and tm // GRID_W >= max(POOL_WINDOWS) // 2
    mats, inv = _pool_tables(n, tm)
    return pl.pallas_call(
        functools.partial(_pool_kernel, tm=tm, nt=nt),
        grid=(bsz, ng, nt),
        in_specs=[
            pl.BlockSpec((1, n, pg), lambda b, g, i: (b, 0, g)),
            pl.BlockSpec((1, tm, 3 * tm), lambda b, g, i: (g, 0, 0)),
            pl.BlockSpec((1, tm, 1), lambda b, g, i: (g, i, 0)),
            pl.BlockSpec((1, pg, pg), lambda b, g, i: (g, 0, 0)),
            pl.BlockSpec((1, pg), lambda b, g, i: (0, g)),
            pl.BlockSpec((1, tm, pg), lambda b, g, i: (b, i, g)),
            pl.BlockSpec((1, 1, pg), lambda b, g, i: (b, 0, g)),
        ],
        out_specs=pl.BlockSpec((1, tm, pg), lambda b, g, i: (b, i, g)),
        out_shape=jax.ShapeDtypeStruct((bsz, n, d), F32),
        compiler_params=_params(("arbitrary", "arbitrary", "arbitrary")),
        name="pool_mixer",
    )(h, mats, inv, pool_w, pool_scale.reshape(1, d), x, gate)


def _rotary_tables(n):
    pos = np.arange(n)
    row = (pos // GRID_W).astype(np.float32)
    col = (pos % GRID_W).astype(np.float32)
    n_freq = HEAD_DIM // 4
    inv = (np.float32(ROPE_BASE) ** (-np.arange(n_freq, dtype=np.float32) / np.float32(n_freq))).astype(np.float32)
    ang = np.concatenate([row[:, None] * inv, col[:, None] * inv], axis=-1).astype(np.float32)
    cos, sin = np.cos(ang.astype(np.float64)), np.sin(ang.astype(np.float64))
    cos2 = np.concatenate([cos, cos], axis=-1).astype(np.float32)
    sin2 = np.concatenate([-sin, sin], axis=-1).astype(np.float32)
    return jnp.asarray(cos2), jnp.asarray(sin2)


def kernel(x, c, ctx, c_ctx, w_mod, b_mod, ffn1_w1, ffn1_w3, ffn1_w2, ffn2_w1, ffn2_w3, ffn2_w2, ab_w_in, ab_w_out, ret_log_decay, hy_conv_w, hy_conv_b, hy_f_w1, hy_f_b1, hy_f_w2, hy_f_b2, hy_f_w3, hy_f_b3, hy_f_freq, hy_f_w4, hy_bias, pool_w, pool_scale, final_gain):
    bsz, n, d = x.shape
    n_ctx = ctx.shape[1]
    depth = w_mod.shape[0]
    assert depth == 2, "layer 0 = retention/Hyena mixer, layer 1 = pooling mixer"
    ret_w = RET_HEADS * HEAD_DIM
    hw = d - ret_w
    assert hy_f_w4.shape[2] == HY_ORDER * 2 * hw

    c_all = jnp.concatenate([c, c_ctx[None, :], jnp.zeros((8 - bsz - 1, d), F32)], axis=0)
    mod = _mod_call(c_all, w_mod, b_mod).reshape(depth, 8, N_MOD, d)

    def mod_rows(layer, base, count=3):
        return mod[layer, :bsz, base:base + count]

    def mod_ctx(base, count=3):
        return jnp.broadcast_to(mod[0, bsz:bsz + 1, base:base + count], (bsz, count, d))

    bf = lambda w: w.astype(BF16)

    n_f = ffn1_w1.shape[2] // _tile(ffn1_w1.shape[2], 512)
    w13 = _cast_call([ffn1_w1, ffn1_w3, ffn2_w1, ffn2_w3], n_f, tile_cols=True)
    w2s = _cast_call([ffn1_w2, ffn2_w2], n_f)
    ffn1_w, ffn2_w = (w13[0], w13[1], w2s[0]), (w13[2], w13[3], w2s[1])

    x, h = _ffn_call(x, mod_rows(0, 0), *ffn1_w, 0, modn=mod_rows(0, 3))
    _, h_c = _ffn_call(ctx, mod_ctx(0), *ffn1_w, 0, modn=mod_ctx(3))

    w_in, w_out = _cast_call([ab_w_in, ab_w_out], 8)
    w_in, w_out = w_in[0], w_out[0]
    p = _matmul_call(h.reshape(bsz * n, d), w_in, BF16).reshape(bsz, n, -1)
    kvc = _matmul_call(h_c.reshape(bsz * n_ctx, d), w_in, BF16, b_cols=(ret_w, 3 * ret_w))
    kvc = kvc.reshape(bsz, n_ctx, 2 * ret_w)

    cos2, sin2 = _rotary_tables(n)
    r = _ret_call(p, kvc, ret_log_decay[0], cos2, sin2, ret_w)

    hsum, hdiff, nyq = _filt_call(n, hy_f_w1[0], hy_f_b1[0], hy_f_w2[0], hy_f_b2[0], hy_f_w3[0],
                                  hy_f_b3[0], hy_f_freq[0], hy_f_w4[0], hw)
    cmat, sfmat, simat = _dftgen_call(n)
    ck = _matmul_call(cmat, hsum, F32, 512, 512)
    sk = _matmul_call(sfmat, hdiff, F32, 512, 512)
    u = _sconv_call(p, hy_conv_w[0], hy_conv_b[0], 4 * ret_w)
    pq = _dft_fwd_call(cmat, sfmat, u, 0, ck, sk, nyq, 0, hw)
    z = _dft_inv_call(cmat, simat, pq[0], pq[1], u, 0, u, hw, hy_bias[0, 0])
    pq = _dft_fwd_call(cmat, sfmat, z, 0, ck, sk, nyq, hw, hw)
    yh = _dft_inv_call(cmat, simat, pq[0], pq[1], z, 0, u, 2 * hw, hy_bias[0, 1])

    x = _wout_call(r, yh, w_out, x, mod_rows(0, 5, 1))
    x = _ffn_call(x, mod_rows(0, 6), *ffn2_w, 0)

    x, h = _ffn_call(x, mod_rows(1, 0), *ffn1_w, 1, modn=mod_rows(1, 3))
    x = _pool_call(h, bf(pool_w[0]), pool_scale[0], x, mod_rows(1, 5, 1))
    return _ffn_call(x, mod_rows(1, 6), *ffn2_w, 1, gain=final_gain)
```

```python
import functools
import math

import numpy as np
import jax
import jax.numpy as jnp
from jax import lax
from jax.experimental import pallas as pl
from jax.experimental.pallas import tpu as pltpu

F32 = jnp.float32
BF16 = jnp.bfloat16
HIGHEST = lax.Precision.HIGHEST

EPS = 1e-6
GRID_W = 64
N_MOD = 9
RET_HEADS = 8
HEAD_DIM = 128
HY_ORDER = 2
HY_SHORT = 3
HY_EMB_DIM = 33
HY_DECAY_SHORT_PCT = 0.3
HY_DECAY_LONG_PCT = 1.5
HY_DECAY_TARGET = 1e-2
ROPE_BASE = 10000.0
POOL_WINDOWS = (2, 4, 8, 16)
RET_CHUNK = 256
RET_UNROLL = 4
HY_CLASSES = 16
LANE = 128
VMEM_LIMIT = 56 * 1024 * 1024


def _tile(dim, pref):
    if dim <= pref:
        return dim
    for t in range(pref - pref % LANE, 0, -LANE):
        if dim % t == 0:
            return t
    raise ValueError((dim, pref))


def _params(sem):
    return pltpu.CompilerParams(dimension_semantics=sem, vmem_limit_bytes=VMEM_LIMIT)


def _rms(x):
    return x * lax.rsqrt(jnp.mean(x * x, axis=-1, keepdims=True) + EPS)


def _silu(x):
    return x * jax.nn.sigmoid(x)


def _mod_kernel(c_ref, w_ref, b_ref, o_ref):
    sc = _silu(c_ref[...])
    o_ref[0] = jnp.dot(sc, w_ref[0], preferred_element_type=F32, precision=HIGHEST) + b_ref[0]


def _mod_call(c_all, w_mod, b_mod):
    depth, d, nd = w_mod.shape
    rows = c_all.shape[0]
    tn = _tile(nd, 1024)
    return pl.pallas_call(
        _mod_kernel,
        grid=(depth, nd // tn),
        in_specs=[
            pl.BlockSpec((rows, d), lambda l, j: (0, 0)),
            pl.BlockSpec((1, d, tn), lambda l, j: (l, 0, j)),
            pl.BlockSpec((1, 1, tn), lambda l, j: (l, 0, j)),
        ],
        out_specs=pl.BlockSpec((1, rows, tn), lambda l, j: (l, 0, j)),
        out_shape=jax.ShapeDtypeStruct((depth, rows, nd), F32),
        compiler_params=_params(("arbitrary", "arbitrary")),
        name="mod",
    )(c_all, w_mod, b_mod.reshape(depth, 1, nd))


def _ffn_kernel(*refs, n_f, emit_h, final):
    x_ref, mod_ref, w1_ref, w3_ref, w2_ref = refs[:5]
    pos = 5
    modn_ref = gain_ref = hn_ref = None
    if emit_h:
        modn_ref = refs[pos]
        pos += 1
    if final:
        gain_ref = refs[pos]
        pos += 1
    o_ref = refs[pos]
    pos += 1
    if emit_h:
        hn_ref = refs[pos]
        pos += 1
    h_scr, acc_scr = refs[pos], refs[pos + 1]
    f = pl.program_id(2)

    assert n_f >= 3

    def partial_out(h):
        a = jnp.dot(h, w1_ref[...], preferred_element_type=F32)
        b = jnp.dot(h, w3_ref[...], preferred_element_type=F32)
        g = (_silu(a) * b).astype(BF16)
        return jnp.dot(g, w2_ref[...], preferred_element_type=F32)

    @pl.when(f == 0)
    def _():
        m = mod_ref[0]
        h = (_rms(x_ref[0]) * (1.0 + m[1:2]) + m[0:1]).astype(BF16)
        h_scr[...] = h
        acc_scr[...] = partial_out(h)

    @pl.when(jnp.logical_and(f > 0, f < n_f - 1))
    def _():
        acc_scr[...] += partial_out(h_scr[...])

    @pl.when(f == n_f - 1)
    def _():
        m = mod_ref[0]
        xo = x_ref[0] + 0.5 * m[2:3] * (acc_scr[...] + partial_out(h_scr[...]))
        if emit_h:
            mn = modn_ref[0]
            hn_ref[0] = (_rms(xo) * (1.0 + mn[1:2]) + mn[0:1]).astype(BF16)
        if final:
            xo = _rms(xo) * gain_ref[...]
        o_ref[0] = xo


def _ffn_call(x, mod3, w1, w3, w2, layer, modn=None, gain=None):
    bsz, s, d = x.shape
    n_f, tf = w1.shape[1], w1.shape[3]
    tm = _tile(s, 512)
    emit_h = modn is not None
    final = gain is not None
    in_specs = [
        pl.BlockSpec((1, tm, d), lambda b, i, f: (b, i, 0)),
        pl.BlockSpec((1, 3, d), lambda b, i, f: (b, 0, 0)),
        pl.BlockSpec((None, None, d, tf), lambda b, i, f: (layer, f, 0, 0)),
        pl.BlockSpec((None, None, d, tf), lambda b, i, f: (layer, f, 0, 0)),
        pl.BlockSpec((None, tf, d), lambda b, i, f: (layer, f, 0)),
    ]
    args = [x, mod3, w1, w3, w2]
    if emit_h:
        in_specs.append(pl.BlockSpec((1, 3, d), lambda b, i, f: (b, 0, 0)))
        args.append(modn)
    if final:
        in_specs.append(pl.BlockSpec((1, d), lambda b, i, f: (0, 0)))
        args.append(gain.reshape(1, d))
    out_specs = [pl.BlockSpec((1, tm, d), lambda b, i, f: (b, i, 0))]
    out_shape = [jax.ShapeDtypeStruct((bsz, s, d), F32)]
    if emit_h:
        out_specs.append(pl.BlockSpec((1, tm, d), lambda b, i, f: (b, i, 0)))
        out_shape.append(jax.ShapeDtypeStruct((bsz, s, d), BF16))
    outs = pl.pallas_call(
        functools.partial(_ffn_kernel, n_f=n_f, emit_h=emit_h, final=final),
        grid=(bsz, s // tm, n_f),
        in_specs=in_specs,
        out_specs=out_specs,
        out_shape=out_shape,
        scratch_shapes=[pltpu.VMEM((tm, d), BF16), pltpu.VMEM((tm, d), F32)],
        compiler_params=_params(("arbitrary", "arbitrary", "arbitrary")),
        name="ffn",
    )(*args)
    return outs if emit_h else outs[0]


def _cast_kernel(*refs):
    half = len(refs) // 2
    for src, dst in zip(refs[:half], refs[half:]):
        dst[...] = src[...].astype(dst.dtype).reshape(dst.shape)


def _cast_call(arrays, steps, tile_cols=False):
    depth = arrays[0].shape[0]
    specs, out_specs, out_shape = [], [], []
    for a in arrays:
        _, r, c = a.shape
        if c >= r:
            specs.append(pl.BlockSpec((1, r, c // steps), lambda l, j: (l, 0, j)))
        else:
            assert not tile_cols
            specs.append(pl.BlockSpec((1, r // steps, c), lambda l, j: (l, j, 0)))
        if tile_cols:
            out_specs.append(pl.BlockSpec((1, 1, r, c // steps), lambda l, j: (l, j, 0, 0)))
            out_shape.append(jax.ShapeDtypeStruct((depth, steps, r, c // steps), BF16))
        else:
            out_specs.append(specs[-1])
            out_shape.append(jax.ShapeDtypeStruct(a.shape, BF16))
    return pl.pallas_call(
        _cast_kernel,
        grid=(depth, steps),
        in_specs=specs,
        out_specs=out_specs,
        out_shape=out_shape,
        compiler_params=_params(("arbitrary", "arbitrary")),
        name="cast_weights",
    )(*arrays)


def _matmul_kernel(a_ref, b_ref, o_ref):
    o_ref[...] = jnp.dot(a_ref[...], b_ref[...], preferred_element_type=F32).astype(o_ref.dtype)


def _matmul_call(a, b, out_dtype, tm_pref=1024, tn_pref=1024, b_cols=None):
    m, k = a.shape
    c0, c1 = b_cols if b_cols is not None else (0, b.shape[1])
    n = c1 - c0
    tm = _tile(m, tm_pref)
    tn = math.gcd(_tile(n, tn_pref), c0) if c0 else _tile(n, tn_pref)
    assert tn % LANE == 0
    joff = c0 // tn
    return pl.pallas_call(
        _matmul_kernel,
        grid=(m // tm, n // tn),
        in_specs=[
            pl.BlockSpec((tm, k), lambda i, j: (i, 0)),
            pl.BlockSpec((k, tn), lambda i, j: (0, joff + j)),
        ],
        out_specs=pl.BlockSpec((tm, tn), lambda i, j: (i, j)),
        out_shape=jax.ShapeDtypeStruct((m, n), out_dtype),
        compiler_params=_params(("arbitrary", "arbitrary")),
        name="matmul",
    )(a, b)


def _ret_kernel(lg_ref, q_ref, k_ref, v_ref, g_ref, cos_ref, sin_ref, kc_ref, vc_ref, o_ref,
                qr_scr, kr_scr, upd_scr, st_scr, *, n, n_ctx, chunk):
    hd = HEAD_DIM
    head = pl.program_id(1)
    lgf = lg_ref[0, head]
    lgb = lg_ref[1, head]
    nc = n // chunk
    k_scale = hd ** -0.5

    posc = lax.broadcasted_iota(jnp.int32, (n_ctx, hd), 0).astype(F32)
    kc = kc_ref[0].astype(F32) * k_scale
    vc = vc_ref[0]
    kcf = (kc * jnp.exp((n_ctx - 1.0 - posc) * lgf)).T.astype(BF16)
    kcb = (kc * jnp.exp(posc * lgb)).T.astype(BF16)
    s0_f = jnp.dot(kcf, vc, preferred_element_type=F32)
    s0_b = jnp.dot(kcb, vc, preferred_element_type=F32)

    pos = lax.broadcasted_iota(jnp.int32, (chunk, hd), 0).astype(F32)
    qdec_f = jnp.exp((pos + 1.0) * lgf)
    kdec_f = jnp.exp((chunk - 1.0 - pos) * lgf)
    qdec_b = jnp.exp((chunk - pos) * lgb)
    kdec_b = jnp.exp(pos * lgb)
    cdec_f = jnp.exp(jnp.full((1, hd), chunk, F32) * lgf)
    cdec_b = jnp.exp(jnp.full((1, hd), chunk, F32) * lgb)
    ri = lax.broadcasted_iota(jnp.int32, (chunk, chunk), 0)
    ci = lax.broadcasted_iota(jnp.int32, (chunk, chunk), 1)
    diff = (ri - ci).astype(F32)
    dmat = (jnp.where(diff >= 0, jnp.exp(jnp.maximum(diff, 0.0) * lgf), 0.0)
            + jnp.where(diff <= 0, jnp.exp(jnp.maximum(-diff, 0.0) * lgb), 0.0))

    def rotary(x, rows):
        return x * cos_ref[rows, :] + pltpu.roll(x, hd // 2, 1) * sin_ref[rows, :]

    def chunk_rows(c):
        return pl.ds(pl.multiple_of(c * chunk, chunk), chunk)

    def prep_body(c, carry):
        rows = chunk_rows(c)
        qr_scr[rows, :] = rotary(q_ref[0, rows, :].astype(F32), rows)
        k = rotary(k_ref[0, rows, :].astype(F32), rows) * k_scale
        kr_scr[rows, :] = k
        kk = jnp.concatenate([k * kdec_f, k * kdec_b], axis=1).T.astype(BF16)
        upd_scr[c] = jnp.dot(kk, v_ref[0, rows, :], preferred_element_type=F32)
        return carry

    lax.fori_loop(0, nc, prep_body, 0, unroll=math.gcd(nc, RET_UNROLL))

    def fwd_scan(c, sf):
        st_scr[c, 0:hd, :] = sf.astype(BF16)
        return cdec_f * sf + upd_scr[c, 0:hd, :]

    def bwd_scan(j, sb):
        c = nc - 1 - j
        st_scr[c, hd:2 * hd, :] = sb.astype(BF16)
        return cdec_b * sb + upd_scr[c, hd:2 * hd, :]

    lax.fori_loop(0, nc, fwd_scan, s0_f, unroll=math.gcd(nc, RET_UNROLL))
    lax.fori_loop(0, nc, bwd_scan, s0_b, unroll=math.gcd(nc, RET_UNROLL))

    def out_body(c, carry):
        rows = chunk_rows(c)
        q = qr_scr[rows, :]
        s = lax.dot_general(q.astype(BF16), kr_scr[rows, :].astype(BF16), (((1,), (1,)), ((), ())),
                            preferred_element_type=F32) * dmat
        qq = jnp.concatenate([q * qdec_f, q * qdec_b], axis=1).astype(BF16)
        o = (jnp.dot(s.astype(BF16), v_ref[0, rows, :], preferred_element_type=F32)
             + jnp.dot(qq, st_scr[c], preferred_element_type=F32))
        g = g_ref[0, rows, :].astype(F32)
        o_ref[0, rows, :] = (_rms(o) * _silu(g)).astype(o_ref.dtype)
        return carry

    lax.fori_loop(0, nc, out_body, 0, unroll=math.gcd(nc, RET_UNROLL))


def _ret_call(p, kvc, log_decay, cos2, sin2, ret_w):
    bsz, n, _ = p.shape
    n_ctx = kvc.shape[1]
    heads = ret_w // HEAD_DIM
    hd = HEAD_DIM
    chunk = _tile(n, RET_CHUNK)
    kern = functools.partial(_ret_kernel, n=n, n_ctx=n_ctx, chunk=chunk)
    col = lambda off: (lambda b, h, lg: (b, 0, off * heads + h))
    return pl.pallas_call(
        kern,
        grid_spec=pltpu.PrefetchScalarGridSpec(
            num_scalar_prefetch=1,
            grid=(bsz, heads),
            in_specs=[
                pl.BlockSpec((1, n, hd), col(0)),
                pl.BlockSpec((1, n, hd), col(1)),
                pl.BlockSpec((1, n, hd), col(2)),
                pl.BlockSpec((1, n, hd), col(3)),
                pl.BlockSpec((n, hd), lambda b, h, lg: (0, 0)),
                pl.BlockSpec((n, hd), lambda b, h, lg: (0, 0)),
                pl.BlockSpec((1, n_ctx, hd), col(0)),
                pl.BlockSpec((1, n_ctx, hd), col(1)),
            ],
            out_specs=pl.BlockSpec((1, n, hd), lambda b, h, lg: (b, 0, h)),
            scratch_shapes=[
                pltpu.VMEM((n, hd), F32),
                pltpu.VMEM((n, hd), F32),
                pltpu.VMEM((n // chunk, 2 * hd, hd), F32),
                pltpu.VMEM((n // chunk, 2 * hd, hd), BF16),
            ],
        ),
        out_shape=jax.ShapeDtypeStruct((bsz, n, ret_w), BF16),
        compiler_params=_params(("arbitrary", "arbitrary")),
        name="retention",
    )(log_decay, p, p, p, p, cos2, sin2, kvc, kvc)


def _cmul(a, b):
    return a[0] * b[0] - a[1] * b[1], a[0] * b[1] + a[1] * b[0]


def _cmul_conj(a, b):
    return a[0] * b[0] + a[1] * b[1], a[1] * b[0] - a[0] * b[1]


def _cmul_const(a, c, s):
    eps = 1e-12
    if abs(s) < eps:
        return (a[0], a[1]) if c > 0 else (-a[0], -a[1])
    if abs(c) < eps:
        return (-a[1], a[0]) if s > 0 else (a[1], -a[0])
    return a[0] * c - a[1] * s, a[0] * s + a[1] * c


def _fft_list(xs, sign):
    size = len(xs)
    if size == 1:
        return xs
    even = _fft_list(xs[0::2], sign)
    odd = _fft_list(xs[1::2], sign)
    out = [None] * size
    for k in range(size // 2):
        ang = sign * 2.0 * math.pi * k / size
        tr, ti = _cmul_const(odd[k], math.cos(ang), math.sin(ang))
        out[k] = (even[k][0] + tr, even[k][1] + ti)
        out[k + size // 2] = (even[k][0] - tr, even[k][1] - ti)
    return out


def _fft_tables(n, lanes):
    j_cls = HY_CLASSES
    k0n, mcls = 2 * n // j_cls, n // j_cls
    ang = 2.0 * math.pi * np.outer(np.arange(k0n), np.arange(mcls)) / k0n
    c, s = np.cos(ang), np.sin(ang)
    a0 = np.block([[c, s], [-s, c]])
    tw = 2.0 * math.pi * np.outer(np.arange(j_cls), np.arange(k0n)) / (2 * n)
    twr = np.broadcast_to(np.cos(tw)[:, :, None], (j_cls, k0n, lanes))
    twi = np.broadcast_to(-np.sin(tw)[:, :, None], (j_cls, k0n, lanes))
    return (jnp.asarray(a0, dtype=BF16), jnp.asarray(a0.T, dtype=BF16),
            jnp.asarray(twr, dtype=F32), jnp.asarray(twi, dtype=F32))


def _filt_kernel(z_ref, w1_ref, b1_ref, w2_ref, b2_ref, w3_ref, b3_ref, fr_ref, w4_ref, dl_ref,
                 hf_ref, hb_ref, *, n, tt, hw):
    i = pl.program_id(0)
    dot = functools.partial(jnp.dot, preferred_element_type=F32, precision=HIGHEST)
    fr = fr_ref[...]
    a = jnp.sin(fr[0:1] * (dot(z_ref[...], w1_ref[...]) + b1_ref[...]))
    a = jnp.sin(fr[1:2] * (dot(a, w2_ref[...]) + b2_ref[...]))
    a = jnp.sin(fr[2:3] * (dot(a, w3_ref[...]) + b3_ref[...]))
    hf = jnp.dot(a.astype(BF16), w4_ref[...], preferred_element_type=F32)
    mcls = n // HY_CLASSES
    rho = lax.broadcasted_iota(jnp.int32, (tt, hw), 0) + i * tt
    ti = HY_CLASSES * (rho & (mcls - 1)) + lax.shift_right_logical(rho, int(math.log2(mcls)))
    t = ti.astype(F32) * (1.0 / (n - 1.0))
    window = jnp.exp(-t * dl_ref[...])
    for o in range(HY_ORDER):
        hf_ref[:, o * hw:(o + 1) * hw] = (hf[:, (2 * o) * hw:(2 * o + 1) * hw] * window).astype(BF16)
        h_b = jnp.where(ti == 0, 0.0, hf[:, (2 * o + 1) * hw:(2 * o + 2) * hw] * window)
        hb_ref[:, o * hw:(o + 1) * hw] = h_b.astype(BF16)


def _filt_call(n, fw1, fb1, fw2, fb2, fw3, fb3, freq, fw4, hw):
    hid = fw1.shape[1]
    pk = LANE
    mcls = n // HY_CLASSES
    assert mcls & (mcls - 1) == 0
    bands = (HY_EMB_DIM - 1) // 2
    tpos = np.linspace(0.0, 1.0, n, dtype=np.float32)[:, None].astype(np.float64)
    fb = np.linspace(1e-4, bands - 1, bands, dtype=np.float32)[None, :].astype(np.float64)
    wv = (2.0 * math.pi * np.arange(n, dtype=np.float64)[:, None] / n)
    z = np.concatenate([tpos, np.cos(fb * wv), -np.sin(fb * wv)], axis=-1)
    rho = np.arange(n)
    t_of_row = HY_CLASSES * (rho % mcls) + rho // mcls
    z = np.pad(z, ((0, 0), (0, pk - HY_EMB_DIM))).astype(np.float32)[t_of_row]
    max_decay = math.log(HY_DECAY_TARGET) / HY_DECAY_SHORT_PCT
    min_decay = math.log(HY_DECAY_TARGET) / HY_DECAY_LONG_PCT
    deltas = np.abs(np.linspace(min_decay, max_decay, hw, dtype=np.float32)).reshape(1, hw)

    padc = lambda w: jnp.pad(w, ((0, 0), (0, pk - hid)))
    w1 = jnp.pad(fw1, ((0, pk - HY_EMB_DIM), (0, pk - hid)))
    w2 = jnp.pad(fw2, ((0, pk - hid), (0, pk - hid)))
    w3 = jnp.pad(fw3, ((0, pk - hid), (0, pk - hid)))
    w4 = jnp.pad(fw4, ((0, pk - hid), (0, 0))).astype(BF16)
    b1, b2, b3 = (padc(b.reshape(1, hid)) for b in (fb1, fb2, fb3))
    fr = jnp.pad(freq, ((0, 8 - freq.shape[0]), (0, pk - hid)))
    tt = _tile(n, 256)
    nw = HY_ORDER * hw
    full = lambda shape: pl.BlockSpec(shape, lambda i: (0, 0))
    return pl.pallas_call(
        functools.partial(_filt_kernel, n=n, tt=tt, hw=hw),
        grid=(n // tt,),
        in_specs=[
            pl.BlockSpec((tt, pk), lambda i: (i, 0)),
            full((pk, pk)), full((1, pk)), full((pk, pk)), full((1, pk)),
            full((pk, pk)), full((1, pk)), full((8, pk)), full((pk, 2 * nw)), full((1, hw)),
        ],
        out_specs=[pl.BlockSpec((tt, nw), lambda i: (i, 0))] * 2,
        out_shape=[jax.ShapeDtypeStruct((n, nw), BF16)] * 2,
        compiler_params=_params(("arbitrary",)),
        name="hyena_filters",
    )(jnp.asarray(z), w1, b1, w2, b2, w3, b3, fr, w4, jnp.asarray(deltas))


def _sconv_kernel(p_ref, w_ref, b_ref, o_ref, *, n):
    w = w_ref[...]
    bias = b_ref[...]
    mcls = n // HY_CLASSES
    row = lax.broadcasted_iota(jnp.int32, (mcls, p_ref.shape[2]), 0)

    def cls(j):
        return p_ref[0, j * mcls:(j + 1) * mcls, :].astype(F32)

    for j in range(HY_CLASSES):
        cur = cls(j)
        if j > 0:
            prev = cls(j - 1)
        else:
            prev = jnp.where(row == 0, 0.0, pltpu.roll(cls(HY_CLASSES - 1), 1, 0))
        if j < HY_CLASSES - 1:
            nxt = cls(j + 1)
        else:
            nxt = jnp.where(row == mcls - 1, 0.0, pltpu.roll(cls(0), mcls - 1, 0))
        out = bias + prev * w[0:1] + cur * w[1:2] + nxt * w[2:3]
        o_ref[0, j * mcls:(j + 1) * mcls, :] = out.astype(o_ref.dtype)


def _sconv_call(p, conv_w, conv_b):
    bsz, n, width = p.shape
    tc = _tile(width, 512)
    return pl.pallas_call(
        functools.partial(_sconv_kernel, n=n),
        grid=(bsz, width // tc),
        in_specs=[
            pl.BlockSpec((1, n, tc), lambda b, j: (b, 0, j)),
            pl.BlockSpec((HY_SHORT, tc), lambda b, j: (0, j)),
            pl.BlockSpec((1, tc), lambda b, j: (0, j)),
        ],
        out_specs=pl.BlockSpec((1, n, tc), lambda b, j: (b, 0, j)),
        out_shape=jax.ShapeDtypeStruct((bsz, n, width), BF16),
        compiler_params=_params(("arbitrary", "arbitrary")),
        name="short_conv",
    )(p, conv_w, conv_b.reshape(1, width))


def _spec_kernel(a0_ref, twr_ref, twi_ref, hf_ref, hb_ref, kre_ref, kim_ref, wf_scr, wb_scr,
                 *, k0n, mcls):
    j_cls = HY_CLASSES
    a0r = a0_ref[:, 0:mcls]
    for j in range(j_cls):
        rows = slice(j * mcls, (j + 1) * mcls)
        wf_scr[j] = jnp.dot(a0r, hf_ref[rows, :], preferred_element_type=F32)
        wb_scr[j] = jnp.dot(a0r, hb_ref[rows, :], preferred_element_type=F32)

    def body(c, carry):
        halves = []
        for half in range(2):
            r8 = pl.ds(pl.multiple_of(c * 16 + half * 8, 8), 8)
            i8 = pl.ds(pl.multiple_of(k0n + c * 16 + half * 8, 8), 8)
            xs, ys = [], []
            for j in range(j_cls):
                tw = (twr_ref[j, r8, :], twi_ref[j, r8, :])
                xs.append(_cmul((wf_scr[j, r8, :], wf_scr[j, i8, :]), tw))
                ys.append(_cmul((wb_scr[j, r8, :], wb_scr[j, i8, :]), tw))
            halves.append((_fft_list(xs, -1), _fft_list(ys, -1)))
        for q in range(j_cls):
            rows = pl.ds(pl.multiple_of(q * k0n + c * 16, 16), 16)
            kre = jnp.concatenate([h[0][q][0] + h[1][q][0] for h in halves], axis=0)
            kim = jnp.concatenate([h[0][q][1] - h[1][q][1] for h in halves], axis=0)
            kre_ref[rows, :] = kre.astype(kre_ref.dtype)
            kim_ref[rows, :] = kim.astype(kim_ref.dtype)
        return carry

    lax.fori_loop(0, k0n // 16, body, 0)


def _spec_call(tables, hf, hb):
    a0, _, twr, twi = tables
    n, width = hf.shape
    tn = twr.shape[2]
    k0n, mcls = a0.shape[0] // 2, a0.shape[1] // 2
    const = lambda shape: pl.BlockSpec(shape, lambda j: (0,) * len(shape))
    return pl.pallas_call(
        functools.partial(_spec_kernel, k0n=k0n, mcls=mcls),
        grid=(width // tn,),
        in_specs=[
            const(a0.shape), const(twr.shape), const(twi.shape),
            pl.BlockSpec((n, tn), lambda j: (0, j)),
            pl.BlockSpec((n, tn), lambda j: (0, j)),
        ],
        out_specs=[pl.BlockSpec((2 * n, tn), lambda j: (0, j))] * 2,
        out_shape=[jax.ShapeDtypeStruct((2 * n, width), BF16)] * 2,
        scratch_shapes=[pltpu.VMEM((HY_CLASSES, 2 * k0n, tn), F32)] * 2,
        compiler_params=_params(("arbitrary",)),
        name="filter_spectrum",
    )(a0, twr, twi, hf, hb)


def _fftconv_kernel(a0_ref, a0t_ref, twr_ref, twi_ref, z_ref, g_ref, kre_ref, kim_ref, bias_ref,
                    o_ref, w_scr, *, k0n, mcls, scale):
    j_cls = HY_CLASSES
    for j in range(j_cls):
        rows = slice(j * mcls, (j + 1) * mcls)
        zj = jnp.concatenate([z_ref[0, 0, rows, :], z_ref[0, 1, rows, :]], axis=0)
        w_scr[j] = jnp.dot(a0_ref[...], zj, preferred_element_type=F32)

    def body(c, carry):
        for half in range(2):
            r8 = pl.ds(pl.multiple_of(c * 16 + half * 8, 8), 8)
            i8 = pl.ds(pl.multiple_of(k0n + c * 16 + half * 8, 8), 8)
            tws = [(twr_ref[j, r8, :], twi_ref[j, r8, :]) for j in range(j_cls)]
            xs = [_cmul((w_scr[j, r8, :], w_scr[j, i8, :]), tws[j]) for j in range(j_cls)]
            spec = _fft_list(xs, -1)
            prod = []
            for q in range(j_cls):
                rows = pl.ds(pl.multiple_of(q * k0n + c * 16, 16), 16)
                kq = (kre_ref[rows, :].astype(F32)[half * 8:(half + 1) * 8],
                      kim_ref[rows, :].astype(F32)[half * 8:(half + 1) * 8])
                prod.append(_cmul(spec[q], kq))
            back = _fft_list(prod, 1)
            for j in range(j_cls):
                vr, vi = _cmul_conj(back[j], tws[j])
                w_scr[j, r8, :] = vr
                w_scr[j, i8, :] = vi
        return carry

    lax.fori_loop(0, k0n // 16, body, 0)

    bias = bias_ref[...]
    for j in range(j_cls):
        rows = slice(j * mcls, (j + 1) * mcls)
        y = jnp.dot(a0t_ref[...], w_scr[j].astype(BF16), preferred_element_type=F32) * scale
        for b2 in range(2):
            conv = y[b2 * mcls:(b2 + 1) * mcls] + z_ref[0, b2, rows, :].astype(F32) * bias
            o_ref[0, b2, rows, :] = (g_ref[0, b2, rows, :].astype(F32) * conv).astype(o_ref.dtype)


def _fftconv_call(tables, z, z_col0, gate, gate_col0, kre, kim, k_col0, bias):
    a0, a0t, twr, twi = tables
    bsz, n, _ = z.shape
    hw = bias.shape[0]
    tn = twr.shape[2]
    k0n, mcls = a0.shape[0] // 2, a0.shape[1] // 2
    zo, go, ko = z_col0 // tn, gate_col0 // tn, k_col0 // tn
    z4 = z.reshape(bsz // 2, 2, n, z.shape[2])
    g4 = gate.reshape(bsz // 2, 2, n, gate.shape[2])
    const = lambda shape: pl.BlockSpec(shape, lambda j, p: (0,) * len(shape))
    out = pl.pallas_call(
        functools.partial(_fftconv_kernel, k0n=k0n, mcls=mcls, scale=1.0 / (2 * n)),
        grid=(hw // tn, bsz // 2),
        in_specs=[
            const(a0.shape), const(a0t.shape), const(twr.shape), const(twi.shape),
            pl.BlockSpec((1, 2, n, tn), lambda j, p: (p, 0, 0, zo + j)),
            pl.BlockSpec((1, 2, n, tn), lambda j, p: (p, 0, 0, go + j)),
            pl.BlockSpec((2 * n, tn), lambda j, p: (0, ko + j)),
            pl.BlockSpec((2 * n, tn), lambda j, p: (0, ko + j)),
            pl.BlockSpec((1, tn), lambda j, p: (0, j)),
        ],
        out_specs=pl.BlockSpec((1, 2, n, tn), lambda j, p: (p, 0, 0, j)),
        out_shape=jax.ShapeDtypeStruct((bsz // 2, 2, n, hw), BF16),
        scratch_shapes=[pltpu.VMEM((HY_CLASSES, 2 * k0n, tn), F32)],
        compiler_params=_params(("arbitrary", "arbitrary")),
        name="fft_conv",
    )(a0, a0t, twr, twi, z4, g4, kre, kim, bias.reshape(1, hw))
    return out.reshape(bsz, n, hw)


def _wout_kernel(r_ref, y_ref, wa_ref, wb_ref, x_ref, gate_ref, o_ref):
    y = (jnp.dot(r_ref[0], wa_ref[...], preferred_element_type=F32)
         + jnp.dot(y_ref[0], wb_ref[...], preferred_element_type=F32))
    o_ref[0] = x_ref[0] + gate_ref[0] * y


def _wout_call(r, yh, w_out, x, gate):
    bsz, n, d = x.shape
    rw = r.shape[2]
    assert d == 2 * rw and yh.shape[2] == rw
    tm = _tile(n, 512)
    return pl.pallas_call(
        _wout_kernel,
        grid=(bsz, n // tm),
        in_specs=[
            pl.BlockSpec((1, tm, rw), lambda b, i: (b, i, 0)),
            pl.BlockSpec((1, tm, d - rw), lambda b, i: (b, i, 0)),
            pl.BlockSpec((rw, d), lambda b, i: (0, 0)),
            pl.BlockSpec((rw, d), lambda b, i: (1, 0)),
            pl.BlockSpec((1, tm, d), lambda b, i: (b, i, 0)),
            pl.BlockSpec((1, 1, d), lambda b, i: (b, 0, 0)),
        ],
        out_specs=pl.BlockSpec((1, tm, d), lambda b, i: (b, i, 0)),
        out_shape=jax.ShapeDtypeStruct((bsz, n, d), F32),
        compiler_params=_params(("arbitrary", "arbitrary")),
        name="mixer_out",
    )(r, yh, w_out, w_out, x, gate)


def _pool_kernel(h_ref, m_ref, ic_ref, w_ref, sc_ref, x_ref, gate_ref, o_ref, *, tm, nt):
    i = pl.program_id(2)
    rows_c = pl.ds(pl.multiple_of(i * tm, tm), tm)
    rows_l = pl.ds(pl.multiple_of(jnp.maximum(i - 1, 0) * tm, tm), tm)
    rows_r = pl.ds(pl.multiple_of(jnp.minimum(i + 1, nt - 1) * tm, tm), tm)
    hc = h_ref[0, rows_c, :]
    acc = jnp.dot(m_ref[0, :, tm:2 * tm], hc, preferred_element_type=F32)
    left = jnp.dot(m_ref[0, :, 0:tm], h_ref[0, rows_l, :], preferred_element_type=F32)
    right = jnp.dot(m_ref[0, :, 2 * tm:3 * tm], h_ref[0, rows_r, :], preferred_element_type=F32)
    acc = acc + jnp.where(i > 0, left, 0.0) + jnp.where(i < nt - 1, right, 0.0)
    pooled = acc * ic_ref[0] - hc.astype(F32)
    y = jnp.dot(pooled.astype(BF16), w_ref[0], preferred_element_type=F32) * sc_ref[...]
    o_ref[0] = x_ref[0] + gate_ref[0] * y


def _pool_tables(n, tm):
    rows_per_tile = tm // GRID_W
    n_rows = n // GRID_W
    tok = np.arange(tm)
    r_out, c_out = tok // GRID_W, tok % GRID_W
    tin = np.arange(3 * tm)
    r_in, c_in = tin // GRID_W - rows_per_tile, tin % GRID_W
    mats, inv = [], []
    for win in POOL_WINDOWS:
        lo, hi = -(win // 2), win - 1 - win // 2
        dr = r_in[None, :] - r_out[:, None]
        dc = c_in[None, :] - c_out[:, None]
        mats.append(((dr >= lo) & (dr <= hi) & (dc >= lo) & (dc <= hi)).astype(np.float32))
        idx_r, idx_c = np.arange(n_rows), np.arange(GRID_W)
        cnt_r = np.minimum(idx_r + hi, n_rows - 1) + 1 - np.maximum(idx_r + lo, 0)
        cnt_c = np.minimum(idx_c + hi, GRID_W - 1) + 1 - np.maximum(idx_c + lo, 0)
        inv.append((1.0 / (cnt_r[:, None] * cnt_c[None, :]).astype(np.float64)).reshape(n, 1))
    return (jnp.asarray(np.stack(mats), dtype=BF16),
            jnp.asarray(np.stack(inv).astype(np.float32)))


def _pool_call(h, pool_w, pool_scale, x, gate):
    bsz, n, d = x.shape
    ng = len(POOL_WINDOWS)
    pg = d // ng
    tm = _tile(n, 512)
    nt = n // tm
    assert tm % GRID_W == 0 and tm // GRID_W >= max(POOL_WINDOWS) // 2
    mats, inv = _pool_tables(n, tm)
    return pl.pallas_call(
        functools.partial(_pool_kernel, tm=tm, nt=nt),
        grid=(bsz, ng, nt),
        in_specs=[
            pl.BlockSpec((1, n, pg), lambda b, g, i: (b, 0, g)),
            pl.BlockSpec((1, tm, 3 * tm), lambda b, g, i: (g, 0, 0)),
            pl.BlockSpec((1, tm, 1), lambda b, g, i: (g, i, 0)),
            pl.BlockSpec((1, pg, pg), lambda b, g, i: (g, 0, 0)),
            pl.BlockSpec((1, pg), lambda b, g, i: (0, g)),
            pl.BlockSpec((1, tm, pg), lambda b, g, i: (b, i, g)),
            pl.BlockSpec((1, 1, pg), lambda b, g, i: (b, 0, g)),
        ],
        out_specs=pl.BlockSpec((1, tm, pg), lambda b, g, i: (b, i, g)),
        out_shape=jax.ShapeDtypeStruct((bsz, n, d), F32),
        compiler_params=_params(("arbitrary", "arbitrary", "arbitrary")),
        name="pool_mixer",
    )(h, mats, inv, pool_w, pool_scale.reshape(1, d), x, gate)


def _rotary_tables(n):
    pos = np.arange(n)
    row = (pos // GRID_W).astype(np.float32)
    col = (pos % GRID_W).astype(np.float32)
    n_freq = HEAD_DIM // 4
    inv = (np.float32(ROPE_BASE) ** (-np.arange(n_freq, dtype=np.float32) / np.float32(n_freq))).astype(np.float32)
    ang = np.concatenate([row[:, None] * inv, col[:, None] * inv], axis=-1).astype(np.float32)
    cos, sin = np.cos(ang.astype(np.float64)), np.sin(ang.astype(np.float64))
    cos2 = np.concatenate([cos, cos], axis=-1).astype(np.float32)
    sin2 = np.concatenate([-sin, sin], axis=-1).astype(np.float32)
    return jnp.asarray(cos2), jnp.asarray(sin2)


def kernel(x, c, ctx, c_ctx, w_mod, b_mod, ffn1_w1, ffn1_w3, ffn1_w2, ffn2_w1, ffn2_w3, ffn2_w2, ab_w_in, ab_w_out, ret_log_decay, hy_conv_w, hy_conv_b, hy_f_w1, hy_f_b1, hy_f_w2, hy_f_b2, hy_f_w3, hy_f_b3, hy_f_freq, hy_f_w4, hy_bias, pool_w, pool_scale, final_gain):
    bsz, n, d = x.shape
    n_ctx = ctx.shape[1]
    depth = w_mod.shape[0]
    assert depth == 2, "layer 0 = retention/Hyena mixer, layer 1 = pooling mixer"
    ret_w = RET_HEADS * HEAD_DIM
    hw = d - ret_w
    assert hy_f_w4.shape[2] == HY_ORDER * 2 * hw and bsz % 2 == 0

    c_all = jnp.concatenate([c, c_ctx[None, :], jnp.zeros((8 - bsz - 1, d), F32)], axis=0)
    mod = _mod_call(c_all, w_mod, b_mod).reshape(depth, 8, N_MOD, d)

    def mod_rows(layer, base, count=3):
        return mod[layer, :bsz, base:base + count]

    def mod_ctx(base, count=3):
        return jnp.broadcast_to(mod[0, bsz:bsz + 1, base:base + count], (bsz, count, d))

    n_f = ffn1_w1.shape[2] // _tile(ffn1_w1.shape[2], 512)
    w13 = _cast_call([ffn1_w1, ffn1_w3, ffn2_w1, ffn2_w3], n_f, tile_cols=True)
    w2s = _cast_call([ffn1_w2, ffn2_w2], n_f)
    ffn1_w, ffn2_w = (w13[0], w13[1], w2s[0]), (w13[2], w13[3], w2s[1])

    x, h = _ffn_call(x, mod_rows(0, 0), *ffn1_w, 0, modn=mod_rows(0, 3))
    _, h_c = _ffn_call(ctx, mod_ctx(0), *ffn1_w, 0, modn=mod_ctx(3))

    w_in, w_out = _cast_call([ab_w_in, ab_w_out], 8)
    w_in, w_out = w_in[0], w_out[0]
    p = _matmul_call(h.reshape(bsz * n, d), w_in, BF16, b_cols=(0, 4 * ret_w)).reshape(bsz, n, -1)
    kvc = _matmul_call(h_c.reshape(bsz * n_ctx, d), w_in, BF16, b_cols=(ret_w, 3 * ret_w))
    kvc = kvc.reshape(bsz, n_ctx, 2 * ret_w)

    cos2, sin2 = _rotary_tables(n)
    r = _ret_call(p, kvc, ret_log_decay[0], cos2, sin2, ret_w)

    mcls = n // HY_CLASSES
    h_cm = h.reshape(bsz, mcls, HY_CLASSES, d).transpose(0, 2, 1, 3).reshape(bsz * n, d)
    p_hy = _matmul_call(h_cm, w_in, BF16, b_cols=(4 * ret_w, w_in.shape[1])).reshape(bsz, n, -1)
    hf, hb = _filt_call(n, hy_f_w1[0], hy_f_b1[0], hy_f_w2[0], hy_f_b2[0], hy_f_w3[0],
                        hy_f_b3[0], hy_f_freq[0], hy_f_w4[0], hw)
    tables = _fft_tables(n, LANE)
    kre, kim = _spec_call(tables, hf, hb)
    u = _sconv_call(p_hy, hy_conv_w[0], hy_conv_b[0])
    z = _fftconv_call(tables, u, 0, u, hw, kre, kim, 0, hy_bias[0, 0])
    yh = _fftconv_call(tables, z, 0, u, 2 * hw, kre, kim, hw, hy_bias[0, 1])
    yh = yh.reshape(bsz, HY_CLASSES, mcls, hw).transpose(0, 2, 1, 3).reshape(bsz, n, hw)

    x = _wout_call(r, yh, w_out, x, mod_rows(0, 5, 1))
    x = _ffn_call(x, mod_rows(0, 6), *ffn2_w, 0)

    x, h = _ffn_call(x, mod_rows(1, 0), *ffn1_w, 1, modn=mod_rows(1, 3))
    x = _pool_call(h, pool_w[0].astype(BF16), pool_scale[0], x, mod_rows(1, 5, 1))
    return _ffn_call(x, mod_rows(1, 6), *ffn2_w, 1, gain=final_gain)
```

```python
import functools
import math

import numpy as np
import jax
import jax.numpy as jnp
from jax import lax
from jax.experimental import pallas as pl
from jax.experimental.pallas import tpu as pltpu

F32 = jnp.float32
BF16 = jnp.bfloat16
HIGHEST = lax.Precision.HIGHEST

EPS = 1e-6
GRID_W = 64
N_MOD = 9
RET_HEADS = 8
HEAD_DIM = 128
HY_ORDER = 2
HY_SHORT = 3
HY_EMB_DIM = 33
HY_DECAY_SHORT_PCT = 0.3
HY_DECAY_LONG_PCT = 1.5
HY_DECAY_TARGET = 1e-2
ROPE_BASE = 10000.0
POOL_WINDOWS = (2, 4, 8, 16)
RET_CHUNK = 256
RET_UNROLL = 4
HY_CLASSES = 16
LANE = 128
VMEM_LIMIT = 56 * 1024 * 1024


def _tile(dim, pref):
    if dim <= pref:
        return dim
    for t in range(pref - pref % LANE, 0, -LANE):
        if dim % t == 0:
            return t
    raise ValueError((dim, pref))


def _params(sem):
    return pltpu.CompilerParams(dimension_semantics=sem, vmem_limit_bytes=VMEM_LIMIT)


def _rms(x):
    return x * lax.rsqrt(jnp.mean(x * x, axis=-1, keepdims=True) + EPS)


def _silu(x):
    return x * jax.nn.sigmoid(x)


def _mod_kernel(c_ref, w_ref, b_ref, o_ref):
    sc = _silu(c_ref[...])
    o_ref[0] = jnp.dot(sc, w_ref[0], preferred_element_type=F32, precision=HIGHEST) + b_ref[0]


def _mod_call(c_all, w_mod, b_mod):
    depth, d, nd = w_mod.shape
    rows = c_all.shape[0]
    tn = _tile(nd, 1024)
    return pl.pallas_call(
        _mod_kernel,
        grid=(depth, nd // tn),
        in_specs=[
            pl.BlockSpec((rows, d), lambda l, j: (0, 0)),
            pl.BlockSpec((1, d, tn), lambda l, j: (l, 0, j)),
            pl.BlockSpec((1, 1, tn), lambda l, j: (l, 0, j)),
        ],
        out_specs=pl.BlockSpec((1, rows, tn), lambda l, j: (l, 0, j)),
        out_shape=jax.ShapeDtypeStruct((depth, rows, nd), F32),
        compiler_params=_params(("arbitrary", "arbitrary")),
        name="mod",
    )(c_all, w_mod, b_mod.reshape(depth, 1, nd))


def _ffn_kernel(*refs, n_f, emit_h, final):
    x_ref, mod_ref, w1_ref, w3_ref, w2_ref = refs[:5]
    pos = 5
    modn_ref = gain_ref = hn_ref = None
    if emit_h:
        modn_ref = refs[pos]
        pos += 1
    if final:
        gain_ref = refs[pos]
        pos += 1
    o_ref = refs[pos]
    pos += 1
    if emit_h:
        hn_ref = refs[pos]
        pos += 1
    h_scr, acc_scr = refs[pos], refs[pos + 1]
    f = pl.program_id(2)

    assert n_f >= 3

    def partial_out(h):
        a = jnp.dot(h, w1_ref[...], preferred_element_type=F32)
        b = jnp.dot(h, w3_ref[...], preferred_element_type=F32)
        g = (_silu(a) * b).astype(BF16)
        return jnp.dot(g, w2_ref[...], preferred_element_type=F32)

    @pl.when(f == 0)
    def _():
        m = mod_ref[0]
        h = (_rms(x_ref[0]) * (1.0 + m[1:2]) + m[0:1]).astype(BF16)
        h_scr[...] = h
        acc_scr[...] = partial_out(h)

    @pl.when(jnp.logical_and(f > 0, f < n_f - 1))
    def _():
        acc_scr[...] += partial_out(h_scr[...])

    @pl.when(f == n_f - 1)
    def _():
        m = mod_ref[0]
        xo = x_ref[0] + 0.5 * m[2:3] * (acc_scr[...] + partial_out(h_scr[...]))
        if emit_h:
            mn = modn_ref[0]
            hn_ref[0] = (_rms(xo) * (1.0 + mn[1:2]) + mn[0:1]).astype(BF16)
        if final:
            xo = _rms(xo) * gain_ref[...]
        o_ref[0] = xo


def _ffn_call(x, mod3, w1, w3, w2, layer, modn=None, gain=None):
    bsz, s, d = x.shape
    n_f, tf = w1.shape[1], w1.shape[3]
    tm = _tile(s, 512)
    emit_h = modn is not None
    final = gain is not None
    in_specs = [
        pl.BlockSpec((1, tm, d), lambda b, i, f: (b, i, 0)),
        pl.BlockSpec((1, 3, d), lambda b, i, f: (b, 0, 0)),
        pl.BlockSpec((None, None, d, tf), lambda b, i, f: (layer, f, 0, 0)),
        pl.BlockSpec((None, None, d, tf), lambda b, i, f: (layer, f, 0, 0)),
        pl.BlockSpec((None, tf, d), lambda b, i, f: (layer, f, 0)),
    ]
    args = [x, mod3, w1, w3, w2]
    if emit_h:
        in_specs.append(pl.BlockSpec((1, 3, d), lambda b, i, f: (b, 0, 0)))
        args.append(modn)
    if final:
        in_specs.append(pl.BlockSpec((1, d), lambda b, i, f: (0, 0)))
        args.append(gain.reshape(1, d))
    out_specs = [pl.BlockSpec((1, tm, d), lambda b, i, f: (b, i, 0))]
    out_shape = [jax.ShapeDtypeStruct((bsz, s, d), F32)]
    if emit_h:
        out_specs.append(pl.BlockSpec((1, tm, d), lambda b, i, f: (b, i, 0)))
        out_shape.append(jax.ShapeDtypeStruct((bsz, s, d), BF16))
    outs = pl.pallas_call(
        functools.partial(_ffn_kernel, n_f=n_f, emit_h=emit_h, final=final),
        grid=(bsz, s // tm, n_f),
        in_specs=in_specs,
        out_specs=out_specs,
        out_shape=out_shape,
        scratch_shapes=[pltpu.VMEM((tm, d), BF16), pltpu.VMEM((tm, d), F32)],
        compiler_params=_params(("arbitrary", "arbitrary", "arbitrary")),
        name="ffn",
    )(*args)
    return outs if emit_h else outs[0]


def _cast_kernel(*refs):
    half = len(refs) // 2
    for src, dst in zip(refs[:half], refs[half:]):
        dst[...] = src[...].astype(dst.dtype).reshape(dst.shape)


def _cast_call(arrays, steps, tile_cols=False):
    depth = arrays[0].shape[0]
    specs, out_specs, out_shape = [], [], []
    for a in arrays:
        _, r, c = a.shape
        if c >= r:
            specs.append(pl.BlockSpec((1, r, c // steps), lambda l, j: (l, 0, j)))
        else:
            assert not tile_cols
            specs.append(pl.BlockSpec((1, r // steps, c), lambda l, j: (l, j, 0)))
        if tile_cols:
            out_specs.append(pl.BlockSpec((1, 1, r, c // steps), lambda l, j: (l, j, 0, 0)))
            out_shape.append(jax.ShapeDtypeStruct((depth, steps, r, c // steps), BF16))
        else:
            out_specs.append(specs[-1])
            out_shape.append(jax.ShapeDtypeStruct(a.shape, BF16))
    return pl.pallas_call(
        _cast_kernel,
        grid=(depth, steps),
        in_specs=specs,
        out_specs=out_specs,
        out_shape=out_shape,
        compiler_params=_params(("arbitrary", "arbitrary")),
        name="cast_weights",
    )(*arrays)


def _matmul_kernel(a_ref, b_ref, o_ref):
    o_ref[...] = jnp.dot(a_ref[...], b_ref[...], preferred_element_type=F32).astype(o_ref.dtype)


def _matmul_call(a, b, out_dtype, tm_pref=1024, tn_pref=1024, b_cols=None):
    m, k = a.shape
    c0, c1 = b_cols if b_cols is not None else (0, b.shape[1])
    n = c1 - c0
    tm = _tile(m, tm_pref)
    tn = math.gcd(_tile(n, tn_pref), c0) if c0 else _tile(n, tn_pref)
    assert tn % LANE == 0
    joff = c0 // tn
    return pl.pallas_call(
        _matmul_kernel,
        grid=(m // tm, n // tn),
        in_specs=[
            pl.BlockSpec((tm, k), lambda i, j: (i, 0)),
            pl.BlockSpec((k, tn), lambda i, j: (0, joff + j)),
        ],
        out_specs=pl.BlockSpec((tm, tn), lambda i, j: (i, j)),
        out_shape=jax.ShapeDtypeStruct((m, n), out_dtype),
        compiler_params=_params(("arbitrary", "arbitrary")),
        name="matmul",
    )(a, b)


def _ret_kernel(lg_ref, q_ref, k_ref, v_ref, g_ref, cos_ref, sin_ref, kc_ref, vc_ref, o_ref,
                qr_scr, kr_scr, upd_scr, st_scr, *, n, n_ctx, chunk):
    hd = HEAD_DIM
    head = pl.program_id(1)
    lgf = lg_ref[0, head]
    lgb = lg_ref[1, head]
    nc = n // chunk
    k_scale = hd ** -0.5

    posc = lax.broadcasted_iota(jnp.int32, (n_ctx, hd), 0).astype(F32)
    kc = kc_ref[0].astype(F32) * k_scale
    vc = vc_ref[0]
    kcf = (kc * jnp.exp((n_ctx - 1.0 - posc) * lgf)).T.astype(BF16)
    kcb = (kc * jnp.exp(posc * lgb)).T.astype(BF16)
    s0_f = jnp.dot(kcf, vc, preferred_element_type=F32)
    s0_b = jnp.dot(kcb, vc, preferred_element_type=F32)

    pos = lax.broadcasted_iota(jnp.int32, (chunk, hd), 0).astype(F32)
    qdec_f = jnp.exp((pos + 1.0) * lgf)
    kdec_f = jnp.exp((chunk - 1.0 - pos) * lgf)
    qdec_b = jnp.exp((chunk - pos) * lgb)
    kdec_b = jnp.exp(pos * lgb)
    cdec_f = jnp.exp(jnp.full((1, hd), chunk, F32) * lgf)
    cdec_b = jnp.exp(jnp.full((1, hd), chunk, F32) * lgb)
    ri = lax.broadcasted_iota(jnp.int32, (chunk, chunk), 0)
    ci = lax.broadcasted_iota(jnp.int32, (chunk, chunk), 1)
    diff = (ri - ci).astype(F32)
    dmat = (jnp.where(diff >= 0, jnp.exp(jnp.maximum(diff, 0.0) * lgf), 0.0)
            + jnp.where(diff <= 0, jnp.exp(jnp.maximum(-diff, 0.0) * lgb), 0.0))

    def rotary(x, rows):
        return x * cos_ref[rows, :] + pltpu.roll(x, hd // 2, 1) * sin_ref[rows, :]

    def chunk_rows(c):
        return pl.ds(pl.multiple_of(c * chunk, chunk), chunk)

    def prep_body(c, carry):
        rows = chunk_rows(c)
        qr_scr[rows, :] = rotary(q_ref[0, rows, :].astype(F32), rows)
        k = rotary(k_ref[0, rows, :].astype(F32), rows) * k_scale
        kr_scr[rows, :] = k
        kk = jnp.concatenate([k * kdec_f, k * kdec_b], axis=1).T.astype(BF16)
        upd_scr[c] = jnp.dot(kk, v_ref[0, rows, :], preferred_element_type=F32)
        return carry

    lax.fori_loop(0, nc, prep_body, 0, unroll=math.gcd(nc, RET_UNROLL))

    def fwd_scan(c, sf):
        st_scr[c, 0:hd, :] = sf.astype(BF16)
        return cdec_f * sf + upd_scr[c, 0:hd, :]

    def bwd_scan(j, sb):
        c = nc - 1 - j
        st_scr[c, hd:2 * hd, :] = sb.astype(BF16)
        return cdec_b * sb + upd_scr[c, hd:2 * hd, :]

    lax.fori_loop(0, nc, fwd_scan, s0_f, unroll=math.gcd(nc, RET_UNROLL))
    lax.fori_loop(0, nc, bwd_scan, s0_b, unroll=math.gcd(nc, RET_UNROLL))

    def out_body(c, carry):
        rows = chunk_rows(c)
        q = qr_scr[rows, :]
        s = lax.dot_general(q.astype(BF16), kr_scr[rows, :].astype(BF16), (((1,), (1,)), ((), ())),
                            preferred_element_type=F32) * dmat
        qq = jnp.concatenate([q * qdec_f, q * qdec_b], axis=1).astype(BF16)
        o = (jnp.dot(s.astype(BF16), v_ref[0, rows, :], preferred_element_type=F32)
             + jnp.dot(qq, st_scr[c], preferred_element_type=F32))
        g = g_ref[0, rows, :].astype(F32)
        o_ref[0, rows, :] = (_rms(o) * _silu(g)).astype(o_ref.dtype)
        return carry

    lax.fori_loop(0, nc, out_body, 0, unroll=math.gcd(nc, RET_UNROLL))


def _ret_call(p, kvc, log_decay, cos2, sin2, ret_w):
    bsz, n, _ = p.shape
    n_ctx = kvc.shape[1]
    heads = ret_w // HEAD_DIM
    hd = HEAD_DIM
    chunk = _tile(n, RET_CHUNK)
    kern = functools.partial(_ret_kernel, n=n, n_ctx=n_ctx, chunk=chunk)
    col = lambda off: (lambda b, h, lg: (b, 0, off * heads + h))
    return pl.pallas_call(
        kern,
        grid_spec=pltpu.PrefetchScalarGridSpec(
            num_scalar_prefetch=1,
            grid=(bsz, heads),
            in_specs=[
                pl.BlockSpec((1, n, hd), col(0)),
                pl.BlockSpec((1, n, hd), col(1)),
                pl.BlockSpec((1, n, hd), col(2)),
                pl.BlockSpec((1, n, hd), col(3)),
                pl.BlockSpec((n, hd), lambda b, h, lg: (0, 0)),
                pl.BlockSpec((n, hd), lambda b, h, lg: (0, 0)),
                pl.BlockSpec((1, n_ctx, hd), col(0)),
                pl.BlockSpec((1, n_ctx, hd), col(1)),
            ],
            out_specs=pl.BlockSpec((1, n, hd), lambda b, h, lg: (b, 0, h)),
            scratch_shapes=[
                pltpu.VMEM((n, hd), F32),
                pltpu.VMEM((n, hd), F32),
                pltpu.VMEM((n // chunk, 2 * hd, hd), F32),
                pltpu.VMEM((n // chunk, 2 * hd, hd), BF16),
            ],
        ),
        out_shape=jax.ShapeDtypeStruct((bsz, n, ret_w), BF16),
        compiler_params=_params(("arbitrary", "arbitrary")),
        name="retention",
    )(log_decay, p, p, p, p, cos2, sin2, kvc, kvc)


def _cmul(a, b):
    return a[0] * b[0] - a[1] * b[1], a[0] * b[1] + a[1] * b[0]


def _cmul_const(a, c, s):
    eps = 1e-12
    if abs(s) < eps:
        return (a[0], a[1]) if c > 0 else (-a[0], -a[1])
    if abs(c) < eps:
        return (-a[1], a[0]) if s > 0 else (a[1], -a[0])
    return a[0] * c - a[1] * s, a[0] * s + a[1] * c


def _fft_list(xs, sign):
    size = len(xs)
    if size == 1:
        return xs
    even = _fft_list(xs[0::2], sign)
    odd = _fft_list(xs[1::2], sign)
    out = [None] * size
    for k in range(size // 2):
        ang = sign * 2.0 * math.pi * k / size
        tr, ti = _cmul_const(odd[k], math.cos(ang), math.sin(ang))
        out[k] = (even[k][0] + tr, even[k][1] + ti)
        out[k + size // 2] = (even[k][0] - tr, even[k][1] - ti)
    return out


def _fft_tables(n):
    j_cls = HY_CLASSES
    k0n, mcls = 2 * n // j_cls, n // j_cls
    t = j_cls * np.arange(mcls)[None, None, :] + np.arange(j_cls)[:, None, None]
    ang = 2.0 * math.pi * ((np.arange(k0n)[None, :, None] * t) % (2 * n)) / (2 * n)
    cs = np.concatenate([np.cos(ang), np.sin(ang)], axis=1)
    cst = np.concatenate([np.cos(ang).transpose(0, 2, 1), np.sin(ang).transpose(0, 2, 1)], axis=1)
    return jnp.asarray(cs, dtype=BF16), jnp.asarray(cst, dtype=BF16)


def _filt_kernel(z_ref, w1_ref, b1_ref, w2_ref, b2_ref, w3_ref, b3_ref, fr_ref, w4_ref, dl_ref,
                 hf_ref, hb_ref, *, n, tt, hw):
    i = pl.program_id(0)
    dot = functools.partial(jnp.dot, preferred_element_type=F32, precision=HIGHEST)
    fr = fr_ref[...]
    a = jnp.sin(fr[0:1] * (dot(z_ref[...], w1_ref[...]) + b1_ref[...]))
    a = jnp.sin(fr[1:2] * (dot(a, w2_ref[...]) + b2_ref[...]))
    a = jnp.sin(fr[2:3] * (dot(a, w3_ref[...]) + b3_ref[...]))
    hf = jnp.dot(a.astype(BF16), w4_ref[...], preferred_element_type=F32)
    mcls = n // HY_CLASSES
    rho = lax.broadcasted_iota(jnp.int32, (tt, hw), 0) + i * tt
    ti = HY_CLASSES * (rho & (mcls - 1)) + lax.shift_right_logical(rho, int(math.log2(mcls)))
    t = ti.astype(F32) * (1.0 / (n - 1.0))
    window = jnp.exp(-t * dl_ref[...])
    for o in range(HY_ORDER):
        hf_ref[:, o * hw:(o + 1) * hw] = (hf[:, (2 * o) * hw:(2 * o + 1) * hw] * window).astype(BF16)
        h_b = jnp.where(ti == 0, 0.0, hf[:, (2 * o + 1) * hw:(2 * o + 2) * hw] * window)
        hb_ref[:, o * hw:(o + 1) * hw] = h_b.astype(BF16)


def _filt_call(n, fw1, fb1, fw2, fb2, fw3, fb3, freq, fw4, hw):
    hid = fw1.shape[1]
    pk = LANE
    mcls = n // HY_CLASSES
    assert mcls & (mcls - 1) == 0
    bands = (HY_EMB_DIM - 1) // 2
    tpos = np.linspace(0.0, 1.0, n, dtype=np.float32)[:, None].astype(np.float64)
    fb = np.linspace(1e-4, bands - 1, bands, dtype=np.float32)[None, :].astype(np.float64)
    wv = (2.0 * math.pi * np.arange(n, dtype=np.float64)[:, None] / n)
    z = np.concatenate([tpos, np.cos(fb * wv), -np.sin(fb * wv)], axis=-1)
    rho = np.arange(n)
    t_of_row = HY_CLASSES * (rho % mcls) + rho // mcls
    z = np.pad(z, ((0, 0), (0, pk - HY_EMB_DIM))).astype(np.float32)[t_of_row]
    max_decay = math.log(HY_DECAY_TARGET) / HY_DECAY_SHORT_PCT
    min_decay = math.log(HY_DECAY_TARGET) / HY_DECAY_LONG_PCT
    deltas = np.abs(np.linspace(min_decay, max_decay, hw, dtype=np.float32)).reshape(1, hw)

    padc = lambda w: jnp.pad(w, ((0, 0), (0, pk - hid)))
    w1 = jnp.pad(fw1, ((0, pk - HY_EMB_DIM), (0, pk - hid)))
    w2 = jnp.pad(fw2, ((0, pk - hid), (0, pk - hid)))
    w3 = jnp.pad(fw3, ((0, pk - hid), (0, pk - hid)))
    w4 = jnp.pad(fw4, ((0, pk - hid), (0, 0))).astype(BF16)
    b1, b2, b3 = (padc(b.reshape(1, hid)) for b in (fb1, fb2, fb3))
    fr = jnp.pad(freq, ((0, 8 - freq.shape[0]), (0, pk - hid)))
    tt = _tile(n, 256)
    nw = HY_ORDER * hw
    full = lambda shape: pl.BlockSpec(shape, lambda i: (0, 0))
    return pl.pallas_call(
        functools.partial(_filt_kernel, n=n, tt=tt, hw=hw),
        grid=(n // tt,),
        in_specs=[
            pl.BlockSpec((tt, pk), lambda i: (i, 0)),
            full((pk, pk)), full((1, pk)), full((pk, pk)), full((1, pk)),
            full((pk, pk)), full((1, pk)), full((8, pk)), full((pk, 2 * nw)), full((1, hw)),
        ],
        out_specs=[pl.BlockSpec((tt, nw), lambda i: (i, 0))] * 2,
        out_shape=[jax.ShapeDtypeStruct((n, nw), BF16)] * 2,
        compiler_params=_params(("arbitrary",)),
        name="hyena_filters",
    )(jnp.asarray(z), w1, b1, w2, b2, w3, b3, fr, w4, jnp.asarray(deltas))


def _sconv_kernel(p_ref, w_ref, b_ref, o_ref, *, n):
    w = w_ref[...]
    bias = b_ref[...]
    mcls = n // HY_CLASSES
    row = lax.broadcasted_iota(jnp.int32, (mcls, p_ref.shape[2]), 0)

    def cls(j):
        return p_ref[0, j * mcls:(j + 1) * mcls, :].astype(F32)

    for j in range(HY_CLASSES):
        cur = cls(j)
        if j > 0:
            prev = cls(j - 1)
        else:
            prev = jnp.where(row == 0, 0.0, pltpu.roll(cls(HY_CLASSES - 1), 1, 0))
        if j < HY_CLASSES - 1:
            nxt = cls(j + 1)
        else:
            nxt = jnp.where(row == mcls - 1, 0.0, pltpu.roll(cls(0), mcls - 1, 0))
        out = bias + prev * w[0:1] + cur * w[1:2] + nxt * w[2:3]
        o_ref[0, j * mcls:(j + 1) * mcls, :] = out.astype(o_ref.dtype)


def _sconv_call(p, conv_w, conv_b):
    bsz, n, width = p.shape
    tc = _tile(width, 512)
    return pl.pallas_call(
        functools.partial(_sconv_kernel, n=n),
        grid=(bsz, width // tc),
        in_specs=[
            pl.BlockSpec((1, n, tc), lambda b, j: (b, 0, j)),
            pl.BlockSpec((HY_SHORT, tc), lambda b, j: (0, j)),
            pl.BlockSpec((1, tc), lambda b, j: (0, j)),
        ],
        out_specs=pl.BlockSpec((1, n, tc), lambda b, j: (b, 0, j)),
        out_shape=jax.ShapeDtypeStruct((bsz, n, width), BF16),
        compiler_params=_params(("arbitrary", "arbitrary")),
        name="short_conv",
    )(p, conv_w, conv_b.reshape(1, width))


def _spec_kernel(cs_ref, hf_ref, hb_ref, kre_ref, kim_ref, wf_scr, wb_scr,
                 *, k0n, mcls):
    j_cls = HY_CLASSES
    tn = hf_ref.shape[1]
    for j in range(j_cls):
        rows = slice(j * mcls, (j + 1) * mcls)
        prod = jnp.dot(cs_ref[j], jnp.concatenate([hf_ref[rows, :], hb_ref[rows, :]], axis=1),
                       preferred_element_type=F32)
        wf_scr[j, 0:k0n, :] = prod[0:k0n, 0:tn]
        wf_scr[j, k0n:2 * k0n, :] = -prod[k0n:2 * k0n, 0:tn]
        wb_scr[j, 0:k0n, :] = prod[0:k0n, tn:2 * tn]
        wb_scr[j, k0n:2 * k0n, :] = -prod[k0n:2 * k0n, tn:2 * tn]

    def body(c, carry):
        halves = []
        for half in range(2):
            r8 = pl.ds(pl.multiple_of(c * 16 + half * 8, 8), 8)
            i8 = pl.ds(pl.multiple_of(k0n + c * 16 + half * 8, 8), 8)
            xs = [(wf_scr[j, r8, :], wf_scr[j, i8, :]) for j in range(j_cls)]
            ys = [(wb_scr[j, r8, :], wb_scr[j, i8, :]) for j in range(j_cls)]
            halves.append((_fft_list(xs, -1), _fft_list(ys, -1)))
        for q in range(j_cls):
            rows = pl.ds(pl.multiple_of(q * k0n + c * 16, 16), 16)
            kre = jnp.concatenate([h[0][q][0] + h[1][q][0] for h in halves], axis=0)
            kim = jnp.concatenate([h[0][q][1] - h[1][q][1] for h in halves], axis=0)
            kre_ref[rows, :] = kre.astype(kre_ref.dtype)
            kim_ref[rows, :] = kim.astype(kim_ref.dtype)
        return carry

    lax.fori_loop(0, k0n // 16, body, 0)


def _spec_call(tables, hf, hb):
    cs, _ = tables
    n, width = hf.shape
    tn = LANE
    k0n, mcls = cs.shape[1] // 2, cs.shape[2]
    const = lambda shape: pl.BlockSpec(shape, lambda j: (0,) * len(shape),
                                       pipeline_mode=pl.Buffered(1))
    return pl.pallas_call(
        functools.partial(_spec_kernel, k0n=k0n, mcls=mcls),
        grid=(width // tn,),
        in_specs=[
            const(cs.shape),
            pl.BlockSpec((n, tn), lambda j: (0, j)),
            pl.BlockSpec((n, tn), lambda j: (0, j)),
        ],
        out_specs=[pl.BlockSpec((2 * n, tn), lambda j: (0, j))] * 2,
        out_shape=[jax.ShapeDtypeStruct((2 * n, width), BF16)] * 2,
        scratch_shapes=[pltpu.VMEM((HY_CLASSES, 2 * k0n, tn), F32)] * 2,
        compiler_params=_params(("arbitrary",)),
        name="filter_spectrum",
    )(cs, hf, hb)


def _fftconv_kernel(cs_ref, cst_ref, z_ref, g_ref, kre_ref, kim_ref, bias_ref,
                    o_ref, w_scr, *, k0n, mcls, scale):
    j_cls = HY_CLASSES
    tn = o_ref.shape[3]
    for j in range(j_cls):
        rows = slice(j * mcls, (j + 1) * mcls)
        prod = jnp.dot(cs_ref[j], jnp.concatenate([z_ref[0, 0, rows, :], z_ref[0, 1, rows, :]], axis=1),
                       preferred_element_type=F32)
        w_scr[j, 0:k0n, :] = prod[0:k0n, 0:tn] + prod[k0n:2 * k0n, tn:2 * tn]
        w_scr[j, k0n:2 * k0n, :] = prod[0:k0n, tn:2 * tn] - prod[k0n:2 * k0n, 0:tn]

    def body(c, carry):
        for half in range(2):
            r8 = pl.ds(pl.multiple_of(c * 16 + half * 8, 8), 8)
            i8 = pl.ds(pl.multiple_of(k0n + c * 16 + half * 8, 8), 8)
            xs = [(w_scr[j, r8, :], w_scr[j, i8, :]) for j in range(j_cls)]
            spec = _fft_list(xs, -1)
            prod = []
            for q in range(j_cls):
                rows = pl.ds(pl.multiple_of(q * k0n + c * 16, 16), 16)
                kq = (kre_ref[rows, :].astype(F32)[half * 8:(half + 1) * 8],
                      kim_ref[rows, :].astype(F32)[half * 8:(half + 1) * 8])
                prod.append(_cmul(spec[q], kq))
            back = _fft_list(prod, 1)
            for j in range(j_cls):
                w_scr[j, r8, :] = back[j][0]
                w_scr[j, i8, :] = back[j][1]
        return carry

    lax.fori_loop(0, k0n // 16, body, 0)

    bias = bias_ref[...]
    for j in range(j_cls):
        rows = slice(j * mcls, (j + 1) * mcls)
        v2 = jnp.concatenate([w_scr[j, 0:k0n, :], w_scr[j, k0n:2 * k0n, :]], axis=1).astype(BF16)
        prod = jnp.dot(cst_ref[j], v2, preferred_element_type=F32) * scale
        ys = (prod[0:mcls, 0:tn] - prod[mcls:2 * mcls, tn:2 * tn],
              prod[mcls:2 * mcls, 0:tn] + prod[0:mcls, tn:2 * tn])
        for b2 in range(2):
            conv = ys[b2] + z_ref[0, b2, rows, :].astype(F32) * bias
            o_ref[0, b2, rows, :] = (g_ref[0, b2, rows, :].astype(F32) * conv).astype(o_ref.dtype)


def _fftconv_call(tables, z, z_col0, gate, gate_col0, kre, kim, k_col0, bias):
    cs, cst = tables
    bsz, n, _ = z.shape
    hw = bias.shape[0]
    tn = LANE
    k0n, mcls = cs.shape[1] // 2, cs.shape[2]
    zo, go, ko = z_col0 // tn, gate_col0 // tn, k_col0 // tn
    z4 = z.reshape(bsz // 2, 2, n, z.shape[2])
    g4 = gate.reshape(bsz // 2, 2, n, gate.shape[2])
    const = lambda shape: pl.BlockSpec(shape, lambda j, p: (0,) * len(shape),
                                       pipeline_mode=pl.Buffered(1))
    out = pl.pallas_call(
        functools.partial(_fftconv_kernel, k0n=k0n, mcls=mcls, scale=1.0 / (2 * n)),
        grid=(hw // tn, bsz // 2),
        in_specs=[
            const(cs.shape), const(cst.shape),
            pl.BlockSpec((1, 2, n, tn), lambda j, p: (p, 0, 0, zo + j)),
            pl.BlockSpec((1, 2, n, tn), lambda j, p: (p, 0, 0, go + j)),
            pl.BlockSpec((2 * n, tn), lambda j, p: (0, ko + j)),
            pl.BlockSpec((2 * n, tn), lambda j, p: (0, ko + j)),
            pl.BlockSpec((1, tn), lambda j, p: (0, j)),
        ],
        out_specs=pl.BlockSpec((1, 2, n, tn), lambda j, p: (p, 0, 0, j)),
        out_shape=jax.ShapeDtypeStruct((bsz // 2, 2, n, hw), BF16),
        scratch_shapes=[pltpu.VMEM((HY_CLASSES, 2 * k0n, tn), F32)],
        compiler_params=_params(("arbitrary", "arbitrary")),
        name="fft_conv",
    )(cs, cst, z4, g4, kre, kim, bias.reshape(1, hw))
    return out.reshape(bsz, n, hw)


def _wout_kernel(r_ref, y_ref, wa_ref, wb_ref, x_ref, gate_ref, o_ref):
    y = (jnp.dot(r_ref[0], wa_ref[...], preferred_element_type=F32)
         + jnp.dot(y_ref[0], wb_ref[...], preferred_element_type=F32))
    o_ref[0] = x_ref[0] + gate_ref[0] * y


def _wout_call(r, yh, w_out, x, gate):
    bsz, n, d = x.shape
    rw = r.shape[2]
    assert d == 2 * rw and yh.shape[2] == rw
    tm = _tile(n, 512)
    return pl.pallas_call(
        _wout_kernel,
        grid=(bsz, n // tm),
        in_specs=[
            pl.BlockSpec((1, tm, rw), lambda b, i: (b, i, 0)),
            pl.BlockSpec((1, tm, d - rw), lambda b, i: (b, i, 0)),
            pl.BlockSpec((rw, d), lambda b, i: (0, 0)),
            pl.BlockSpec((rw, d), lambda b, i: (1, 0)),
            pl.BlockSpec((1, tm, d), lambda b, i: (b, i, 0)),
            pl.BlockSpec((1, 1, d), lambda b, i: (b, 0, 0)),
        ],
        out_specs=pl.BlockSpec((1, tm, d), lambda b, i: (b, i, 0)),
        out_shape=jax.ShapeDtypeStruct((bsz, n, d), F32),
        compiler_params=_params(("arbitrary", "arbitrary")),
        name="mixer_out",
    )(r, yh, w_out, w_out, x, gate)


def _pool_kernel(h_ref, m_ref, ic_ref, w_ref, sc_ref, x_ref, gate_ref, o_ref, *, tm, nt):
    i = pl.program_id(2)
    rows_c = pl.ds(pl.multiple_of(i * tm, tm), tm)
    rows_l = pl.ds(pl.multiple_of(jnp.maximum(i - 1, 0) * tm, tm), tm)
    rows_r = pl.ds(pl.multiple_of(jnp.minimum(i + 1, nt - 1) * tm, tm), tm)
    hc = h_ref[0, rows_c, :]
    acc = jnp.dot(m_ref[0, :, tm:2 * tm], hc, preferred_element_type=F32)
    left = jnp.dot(m_ref[0, :, 0:tm], h_ref[0, rows_l, :], preferred_element_type=F32)
    right = jnp.dot(m_ref[0, :, 2 * tm:3 * tm], h_ref[0, rows_r, :], preferred_element_type=F32)
    acc = acc + jnp.where(i > 0, left, 0.0) + jnp.where(i < nt - 1, right, 0.0)
    pooled = acc * ic_ref[0] - hc.astype(F32)
    y = jnp.dot(pooled.astype(BF16), w_ref[0], preferred_element_type=F32) * sc_ref[...]
    o_ref[0] = x_ref[0] + gate_ref[0] * y


def _pool_tables(n, tm):
    rows_per_tile = tm // GRID_W
    n_rows = n // GRID_W
    tok = np.arange(tm)
    r_out, c_out = tok // GRID_W, tok % GRID_W
    tin = np.arange(3 * tm)
    r_in, c_in = tin // GRID_W - rows_per_tile, tin % GRID_W
    mats, inv = [], []
    for win in POOL_WINDOWS:
        lo, hi = -(win // 2), win - 1 - win // 2
        dr = r_in[None, :] - r_out[:, None]
        dc = c_in[None, :] - c_out[:, None]
        mats.append(((dr >= lo) & (dr <= hi) & (dc >= lo) & (dc <= hi)).astype(np.float32))
        idx_r, idx_c = np.arange(n_rows), np.arange(GRID_W)
        cnt_r = np.minimum(idx_r + hi, n_rows - 1) + 1 - np.maximum(idx_r + lo, 0)
        cnt_c = np.minimum(idx_c + hi, GRID_W - 1) + 1 - np.maximum(idx_c + lo, 0)
        inv.append((1.0 / (cnt_r[:, None] * cnt_c[None, :]).astype(np.float64)).reshape(n, 1))
    return (jnp.asarray(np.stack(mats), dtype=BF16),
            jnp.asarray(np.stack(inv).astype(np.float32)))


def _pool_call(h, pool_w, pool_scale, x, gate):
    bsz, n, d = x.shape
    ng = len(POOL_WINDOWS)
    pg = d // ng
    tm = _tile(n, 512)
    nt = n // tm
    assert tm % GRID_W == 0 and tm // GRID_W >= max(POOL_WINDOWS) // 2
    mats, inv = _pool_tables(n, tm)
    return pl.pallas_call(
        functools.partial(_pool_kernel, tm=tm, nt=nt),
        grid=(bsz, ng, nt),
        in_specs=[
            pl.BlockSpec((1, n, pg), lambda b, g, i: (b, 0, g)),
            pl.BlockSpec((1, tm, 3 * tm), lambda b, g, i: (g, 0, 0)),
            pl.BlockSpec((1, tm, 1), lambda b, g, i: (g, i, 0)),
            pl.BlockSpec((1, pg, pg), lambda b, g, i: (g, 0, 0)),
            pl.BlockSpec((1, pg), lambda b, g, i: (0, g)),
            pl.BlockSpec((1, tm, pg), lambda b, g, i: (b, i, g)),
            pl.BlockSpec((1, 1, pg), lambda b, g, i: (b, 0, g)),
        ],
        out_specs=pl.BlockSpec((1, tm, pg), lambda b, g, i: (b, i, g)),
        out_shape=jax.ShapeDtypeStruct((bsz, n, d), F32),
        compiler_params=_params(("arbitrary", "arbitrary", "arbitrary")),
        name="pool_mixer",
    )(h, mats, inv, pool_w, pool_scale.reshape(1, d), x, gate)


def _rotary_tables(n):
    pos = np.arange(n)
    row = (pos // GRID_W).astype(np.float32)
    col = (pos % GRID_W).astype(np.float32)
    n_freq = HEAD_DIM // 4
    inv = (np.float32(ROPE_BASE) ** (-np.arange(n_freq, dtype=np.float32) / np.float32(n_freq))).astype(np.float32)
    ang = np.concatenate([row[:, None] * inv, col[:, None] * inv], axis=-1).astype(np.float32)
    cos, sin = np.cos(ang.astype(np.float64)), np.sin(ang.astype(np.float64))
    cos2 = np.concatenate([cos, cos], axis=-1).astype(np.float32)
    sin2 = np.concatenate([-sin, sin], axis=-1).astype(np.float32)
    return jnp.asarray(cos2), jnp.asarray(sin2)


def kernel(x, c, ctx, c_ctx, w_mod, b_mod, ffn1_w1, ffn1_w3, ffn1_w2, ffn2_w1, ffn2_w3, ffn2_w2, ab_w_in, ab_w_out, ret_log_decay, hy_conv_w, hy_conv_b, hy_f_w1, hy_f_b1, hy_f_w2, hy_f_b2, hy_f_w3, hy_f_b3, hy_f_freq, hy_f_w4, hy_bias, pool_w, pool_scale, final_gain):
    bsz, n, d = x.shape
    n_ctx = ctx.shape[1]
    depth = w_mod.shape[0]
    assert depth == 2, "layer 0 = retention/Hyena mixer, layer 1 = pooling mixer"
    ret_w = RET_HEADS * HEAD_DIM
    hw = d - ret_w
    assert hy_f_w4.shape[2] == HY_ORDER * 2 * hw and bsz % 2 == 0

    c_all = jnp.concatenate([c, c_ctx[None, :], jnp.zeros((8 - bsz - 1, d), F32)], axis=0)
    mod = _mod_call(c_all, w_mod, b_mod).reshape(depth, 8, N_MOD, d)

    def mod_rows(layer, base, count=3):
        return mod[layer, :bsz, base:base + count]

    def mod_ctx(base, count=3):
        return jnp.broadcast_to(mod[0, bsz:bsz + 1, base:base + count], (bsz, count, d))

    n_f = ffn1_w1.shape[2] // _tile(ffn1_w1.shape[2], 512)
    w13 = _cast_call([ffn1_w1, ffn1_w3, ffn2_w1, ffn2_w3], n_f, tile_cols=True)
    w2s = _cast_call([ffn1_w2, ffn2_w2], n_f)
    ffn1_w, ffn2_w = (w13[0], w13[1], w2s[0]), (w13[2], w13[3], w2s[1])

    x, h = _ffn_call(x, mod_rows(0, 0), *ffn1_w, 0, modn=mod_rows(0, 3))
    _, h_c = _ffn_call(ctx, mod_ctx(0), *ffn1_w, 0, modn=mod_ctx(3))

    w_in, w_out = _cast_call([ab_w_in, ab_w_out], 8)
    w_in, w_out = w_in[0], w_out[0]
    p = _matmul_call(h.reshape(bsz * n, d), w_in, BF16, b_cols=(0, 4 * ret_w)).reshape(bsz, n, -1)
    kvc = _matmul_call(h_c.reshape(bsz * n_ctx, d), w_in, BF16, b_cols=(ret_w, 3 * ret_w))
    kvc = kvc.reshape(bsz, n_ctx, 2 * ret_w)

    cos2, sin2 = _rotary_tables(n)
    r = _ret_call(p, kvc, ret_log_decay[0], cos2, sin2, ret_w)

    mcls = n // HY_CLASSES
    h_cm = h.reshape(bsz, mcls, HY_CLASSES, d).transpose(0, 2, 1, 3).reshape(bsz * n, d)
    p_hy = _matmul_call(h_cm, w_in, BF16, b_cols=(4 * ret_w, w_in.shape[1])).reshape(bsz, n, -1)
    hf, hb = _filt_call(n, hy_f_w1[0], hy_f_b1[0], hy_f_w2[0], hy_f_b2[0], hy_f_w3[0],
                        hy_f_b3[0], hy_f_freq[0], hy_f_w4[0], hw)
    tables = _fft_tables(n)
    kre, kim = _spec_call(tables, hf, hb)
    u = _sconv_call(p_hy, hy_conv_w[0], hy_conv_b[0])
    z = _fftconv_call(tables, u, 0, u, hw, kre, kim, 0, hy_bias[0, 0])
    yh = _fftconv_call(tables, z, 0, u, 2 * hw, kre, kim, hw, hy_bias[0, 1])
    yh = yh.reshape(bsz, HY_CLASSES, mcls, hw).transpose(0, 2, 1, 3).reshape(bsz, n, hw)

    x = _wout_call(r, yh, w_out, x, mod_rows(0, 5, 1))
    x = _ffn_call(x, mod_rows(0, 6), *ffn2_w, 0)

    x, h = _ffn_call(x, mod_rows(1, 0), *ffn1_w, 1, modn=mod_rows(1, 3))
    x = _pool_call(h, pool_w[0].astype(BF16), pool_scale[0], x, mod_rows(1, 5, 1))
    return _ffn_call(x, mod_rows(1, 6), *ffn2_w, 1, gain=final_gain)
```

```python
import functools
import math

import numpy as np
import jax
import jax.numpy as jnp
from jax import lax
from jax.experimental import pallas as pl
from jax.experimental.pallas import tpu as pltpu

F32 = jnp.float32
BF16 = jnp.bfloat16
HIGHEST = lax.Precision.HIGHEST

EPS = 1e-6
GRID_W = 64
N_MOD = 9
RET_HEADS = 8
HEAD_DIM = 128
HY_ORDER = 2
HY_SHORT = 3
HY_EMB_DIM = 33
HY_DECAY_SHORT_PCT = 0.3
HY_DECAY_LONG_PCT = 1.5
HY_DECAY_TARGET = 1e-2
ROPE_BASE = 10000.0
POOL_WINDOWS = (2, 4, 8, 16)
RET_CHUNK = 256
RET_UNROLL = 4
HY_CLASSES = 16
LANE = 128
VMEM_LIMIT = 56 * 1024 * 1024


def _tile(dim, pref):
    if dim <= pref:
        return dim
    for t in range(pref - pref % LANE, 0, -LANE):
        if dim % t == 0:
            return t
    raise ValueError((dim, pref))


def _params(sem):
    return pltpu.CompilerParams(dimension_semantics=sem, vmem_limit_bytes=VMEM_LIMIT)


def _rms(x):
    return x * lax.rsqrt(jnp.mean(x * x, axis=-1, keepdims=True) + EPS)


def _silu(x):
    return x * jax.nn.sigmoid(x)


def _mod_kernel(c_ref, w_ref, b_ref, o_ref):
    sc = _silu(c_ref[...])
    w = w_ref[0]
    sc_hi = sc.astype(BF16)
    sc_lo = (sc - sc_hi.astype(F32)).astype(BF16)
    w_hi = w.astype(BF16)
    w_lo = (w - w_hi.astype(F32)).astype(BF16)
    rows = sc.shape[0]
    top = jnp.dot(jnp.concatenate([sc_hi, sc_lo], axis=0), w_hi, preferred_element_type=F32)
    o_ref[0] = (top[0:rows] + top[rows:2 * rows]
                + jnp.dot(sc_hi, w_lo, preferred_element_type=F32) + b_ref[0])


def _mod_call(c_all, w_mod, b_mod):
    depth, d, nd = w_mod.shape
    rows = c_all.shape[0]
    tn = _tile(nd, 1024)
    return pl.pallas_call(
        _mod_kernel,
        grid=(depth, nd // tn),
        in_specs=[
            pl.BlockSpec((rows, d), lambda l, j: (0, 0)),
            pl.BlockSpec((1, d, tn), lambda l, j: (l, 0, j)),
            pl.BlockSpec((1, 1, tn), lambda l, j: (l, 0, j)),
        ],
        out_specs=pl.BlockSpec((1, rows, tn), lambda l, j: (l, 0, j)),
        out_shape=jax.ShapeDtypeStruct((depth, rows, nd), F32),
        compiler_params=_params(("arbitrary", "arbitrary")),
        name="mod",
    )(c_all, w_mod, b_mod.reshape(depth, 1, nd))


def _ffn_kernel(*refs, n_f, emit_h, final):
    x_ref, mod_ref, w13_ref, w2_ref = refs[:4]
    pos = 4
    modn_ref = gain_ref = hn_ref = None
    if emit_h:
        modn_ref = refs[pos]
        pos += 1
    if final:
        gain_ref = refs[pos]
        pos += 1
    o_ref = refs[pos]
    pos += 1
    if emit_h:
        hn_ref = refs[pos]
        pos += 1
    h_scr, acc_scr = refs[pos], refs[pos + 1]
    f = pl.program_id(2)

    assert n_f >= 3

    tf = w2_ref.shape[0]

    def partial_out(h):
        ab = jnp.dot(h, w13_ref[...], preferred_element_type=F32)
        g = (_silu(ab[:, 0:tf]) * ab[:, tf:2 * tf]).astype(BF16)
        return jnp.dot(g, w2_ref[...], preferred_element_type=F32)

    @pl.when(f == 0)
    def _():
        m = mod_ref[0]
        h = (_rms(x_ref[0]) * (1.0 + m[1:2]) + m[0:1]).astype(BF16)
        h_scr[...] = h
        acc_scr[...] = partial_out(h)

    @pl.when(jnp.logical_and(f > 0, f < n_f - 1))
    def _():
        acc_scr[...] += partial_out(h_scr[...])

    @pl.when(f == n_f - 1)
    def _():
        m = mod_ref[0]
        xo = x_ref[0] + 0.5 * m[2:3] * (acc_scr[...] + partial_out(h_scr[...]))
        if emit_h:
            mn = modn_ref[0]
            hn_ref[0] = (_rms(xo) * (1.0 + mn[1:2]) + mn[0:1]).astype(BF16)
        if final:
            xo = _rms(xo) * gain_ref[...]
        o_ref[0] = xo


def _ffn_call(x, mod3, w13, w2, layer, modn=None, gain=None):
    bsz, s, d = x.shape
    n_f, tf = w13.shape[1], w13.shape[3] // 2
    tm = _tile(s, 512)
    emit_h = modn is not None
    final = gain is not None
    in_specs = [
        pl.BlockSpec((1, tm, d), lambda b, i, f: (b, i, 0)),
        pl.BlockSpec((1, 3, d), lambda b, i, f: (b, 0, 0)),
        pl.BlockSpec((None, None, d, 2 * tf), lambda b, i, f: (layer, f, 0, 0)),
        pl.BlockSpec((None, tf, d), lambda b, i, f: (layer, f, 0)),
    ]
    args = [x, mod3, w13, w2]
    if emit_h:
        in_specs.append(pl.BlockSpec((1, 3, d), lambda b, i, f: (b, 0, 0)))
        args.append(modn)
    if final:
        in_specs.append(pl.BlockSpec((1, d), lambda b, i, f: (0, 0)))
        args.append(gain.reshape(1, d))
    out_specs = [pl.BlockSpec((1, tm, d), lambda b, i, f: (b, i, 0))]
    out_shape = [jax.ShapeDtypeStruct((bsz, s, d), F32)]
    if emit_h:
        out_specs.append(pl.BlockSpec((1, tm, d), lambda b, i, f: (b, i, 0)))
        out_shape.append(jax.ShapeDtypeStruct((bsz, s, d), BF16))
    outs = pl.pallas_call(
        functools.partial(_ffn_kernel, n_f=n_f, emit_h=emit_h, final=final),
        grid=(bsz, s // tm, n_f),
        in_specs=in_specs,
        out_specs=out_specs,
        out_shape=out_shape,
        scratch_shapes=[pltpu.VMEM((tm, d), BF16), pltpu.VMEM((tm, d), F32)],
        compiler_params=_params(("arbitrary", "arbitrary", "arbitrary")),
        name="ffn",
    )(*args)
    return outs if emit_h else outs[0]


def _cast_kernel(*refs):
    half = len(refs) // 2
    for src, dst in zip(refs[:half], refs[half:]):
        dst[...] = src[...].astype(dst.dtype)


def _cast_call(arrays, steps):
    depth = arrays[0].shape[0]
    specs = []
    for a in arrays:
        _, r, c = a.shape
        if c >= r:
            specs.append(pl.BlockSpec((1, r, c // steps), lambda l, j: (l, 0, j)))
        else:
            specs.append(pl.BlockSpec((1, r // steps, c), lambda l, j: (l, j, 0)))
    return pl.pallas_call(
        _cast_kernel,
        grid=(depth, steps),
        in_specs=specs,
        out_specs=specs,
        out_shape=[jax.ShapeDtypeStruct(a.shape, BF16) for a in arrays],
        compiler_params=_params(("arbitrary", "arbitrary")),
        name="cast_weights",
    )(*arrays)


def _cast_pairs_kernel(*refs):
    n_out = len(refs) // 3
    for k in range(n_out):
        wa, wb, dst = refs[2 * k], refs[2 * k + 1], refs[2 * n_out + k]
        tf = wa.shape[2]
        dst[0, 0, :, 0:tf] = wa[0].astype(dst.dtype)
        dst[0, 0, :, tf:2 * tf] = wb[0].astype(dst.dtype)


def _cast_pairs_call(pairs, steps):
    depth, r, c = pairs[0][0].shape
    tf = c // steps
    in_spec = pl.BlockSpec((1, r, tf), lambda l, j: (l, 0, j))
    out_spec = pl.BlockSpec((1, 1, r, 2 * tf), lambda l, j: (l, j, 0, 0))
    return pl.pallas_call(
        _cast_pairs_kernel,
        grid=(depth, steps),
        in_specs=[in_spec] * (2 * len(pairs)),
        out_specs=[out_spec] * len(pairs),
        out_shape=[jax.ShapeDtypeStruct((depth, steps, r, 2 * tf), BF16)] * len(pairs),
        compiler_params=_params(("arbitrary", "arbitrary")),
        name="cast_weight_pairs",
    )(*[w for pair in pairs for w in pair])


def _matmul_kernel(a_ref, b_ref, o_ref):
    o_ref[...] = jnp.dot(a_ref[...], b_ref[...], preferred_element_type=F32).astype(o_ref.dtype)


def _matmul_call(a, b, out_dtype, tm_pref=1024, tn_pref=1024, b_cols=None):
    m, k = a.shape
    c0, c1 = b_cols if b_cols is not None else (0, b.shape[1])
    n = c1 - c0
    tm = _tile(m, tm_pref)
    tn = math.gcd(_tile(n, tn_pref), c0) if c0 else _tile(n, tn_pref)
    assert tn % LANE == 0
    joff = c0 // tn
    return pl.pallas_call(
        _matmul_kernel,
        grid=(m // tm, n // tn),
        in_specs=[
            pl.BlockSpec((tm, k), lambda i, j: (i, 0)),
            pl.BlockSpec((k, tn), lambda i, j: (0, joff + j)),
        ],
        out_specs=pl.BlockSpec((tm, tn), lambda i, j: (i, j)),
        out_shape=jax.ShapeDtypeStruct((m, n), out_dtype),
        compiler_params=_params(("arbitrary", "arbitrary")),
        name="matmul",
    )(a, b)


def _ret_kernel(lg_ref, q_ref, k_ref, v_ref, g_ref, cos_ref, sin_ref, kc_ref, vc_ref, o_ref,
                qr_scr, kr_scr, upd_scr, st_scr, *, n, n_ctx, chunk):
    hd = HEAD_DIM
    head = pl.program_id(1)
    lgf = lg_ref[0, head]
    lgb = lg_ref[1, head]
    nc = n // chunk
    k_scale = hd ** -0.5

    posc = lax.broadcasted_iota(jnp.int32, (n_ctx, hd), 0).astype(F32)
    kc = kc_ref[0].astype(F32) * k_scale
    vc = vc_ref[0]
    kcf = (kc * jnp.exp((n_ctx - 1.0 - posc) * lgf)).T.astype(BF16)
    kcb = (kc * jnp.exp(posc * lgb)).T.astype(BF16)
    s0_f = jnp.dot(kcf, vc, preferred_element_type=F32)
    s0_b = jnp.dot(kcb, vc, preferred_element_type=F32)

    pos = lax.broadcasted_iota(jnp.int32, (chunk, hd), 0).astype(F32)
    qdec_f = jnp.exp((pos + 1.0) * lgf)
    kdec_f = jnp.exp((chunk - 1.0 - pos) * lgf)
    qdec_b = jnp.exp((chunk - pos) * lgb)
    kdec_b = jnp.exp(pos * lgb)
    cdec_f = jnp.exp(jnp.full((1, hd), chunk, F32) * lgf)
    cdec_b = jnp.exp(jnp.full((1, hd), chunk, F32) * lgb)
    ri = lax.broadcasted_iota(jnp.int32, (chunk, chunk), 0)
    ci = lax.broadcasted_iota(jnp.int32, (chunk, chunk), 1)
    diff = (ri - ci).astype(F32)
    dmat = (jnp.where(diff >= 0, jnp.exp(jnp.maximum(diff, 0.0) * lgf), 0.0)
            + jnp.where(diff <= 0, jnp.exp(jnp.maximum(-diff, 0.0) * lgb), 0.0))

    def rotary(x, rows):
        return x * cos_ref[rows, :] + pltpu.roll(x, hd // 2, 1) * sin_ref[rows, :]

    def chunk_rows(c):
        return pl.ds(pl.multiple_of(c * chunk, chunk), chunk)

    def prep_body(c, carry):
        rows = chunk_rows(c)
        qr_scr[rows, :] = rotary(q_ref[0, rows, :].astype(F32), rows)
        k = rotary(k_ref[0, rows, :].astype(F32), rows) * k_scale
        kr_scr[rows, :] = k
        kk = jnp.concatenate([k * kdec_f, k * kdec_b], axis=1).T.astype(BF16)
        upd_scr[c] = jnp.dot(kk, v_ref[0, rows, :], preferred_element_type=F32)
        return carry

    lax.fori_loop(0, nc, prep_body, 0, unroll=math.gcd(nc, RET_UNROLL))

    def fwd_scan(c, sf):
        st_scr[c, 0:hd, :] = sf.astype(BF16)
        return cdec_f * sf + upd_scr[c, 0:hd, :]

    def bwd_scan(j, sb):
        c = nc - 1 - j
        st_scr[c, hd:2 * hd, :] = sb.astype(BF16)
        return cdec_b * sb + upd_scr[c, hd:2 * hd, :]

    lax.fori_loop(0, nc, fwd_scan, s0_f, unroll=math.gcd(nc, RET_UNROLL))
    lax.fori_loop(0, nc, bwd_scan, s0_b, unroll=math.gcd(nc, RET_UNROLL))

    def out_body(c, carry):
        rows = chunk_rows(c)
        q = qr_scr[rows, :]
        s = lax.dot_general(q.astype(BF16), kr_scr[rows, :].astype(BF16), (((1,), (1,)), ((), ())),
                            preferred_element_type=F32) * dmat
        qq = jnp.concatenate([q * qdec_f, q * qdec_b], axis=1).astype(BF16)
        o = (jnp.dot(s.astype(BF16), v_ref[0, rows, :], preferred_element_type=F32)
             + jnp.dot(qq, st_scr[c], preferred_element_type=F32))
        g = g_ref[0, rows, :].astype(F32)
        o_ref[0, rows, :] = (_rms(o) * _silu(g)).astype(o_ref.dtype)
        return carry

    lax.fori_loop(0, nc, out_body, 0, unroll=math.gcd(nc, RET_UNROLL))


def _ret_call(p, kvc, log_decay, cos2, sin2, ret_w):
    bsz, n, _ = p.shape
    n_ctx = kvc.shape[1]
    heads = ret_w // HEAD_DIM
    hd = HEAD_DIM
    chunk = _tile(n, RET_CHUNK)
    kern = functools.partial(_ret_kernel, n=n, n_ctx=n_ctx, chunk=chunk)
    col = lambda off: (lambda b, h, lg: (b, 0, off * heads + h))
    return pl.pallas_call(
        kern,
        grid_spec=pltpu.PrefetchScalarGridSpec(
            num_scalar_prefetch=1,
            grid=(bsz, heads),
            in_specs=[
                pl.BlockSpec((1, n, hd), col(0)),
                pl.BlockSpec((1, n, hd), col(1)),
                pl.BlockSpec((1, n, hd), col(2)),
                pl.BlockSpec((1, n, hd), col(3)),
                pl.BlockSpec((n, hd), lambda b, h, lg: (0, 0)),
                pl.BlockSpec((n, hd), lambda b, h, lg: (0, 0)),
                pl.BlockSpec((1, n_ctx, hd), col(0)),
                pl.BlockSpec((1, n_ctx, hd), col(1)),
            ],
            out_specs=pl.BlockSpec((1, n, hd), lambda b, h, lg: (b, 0, h)),
            scratch_shapes=[
                pltpu.VMEM((n, hd), F32),
                pltpu.VMEM((n, hd), F32),
                pltpu.VMEM((n // chunk, 2 * hd, hd), F32),
                pltpu.VMEM((n // chunk, 2 * hd, hd), BF16),
            ],
        ),
        out_shape=jax.ShapeDtypeStruct((bsz, n, ret_w), BF16),
        compiler_params=_params(("arbitrary", "arbitrary")),
        name="retention",
    )(log_decay, p, p, p, p, cos2, sin2, kvc, kvc)


def _cmul(a, b):
    return a[0] * b[0] - a[1] * b[1], a[0] * b[1] + a[1] * b[0]


def _cmul_const(a, c, s):
    eps = 1e-12
    if abs(s) < eps:
        return (a[0], a[1]) if c > 0 else (-a[0], -a[1])
    if abs(c) < eps:
        return (-a[1], a[0]) if s > 0 else (a[1], -a[0])
    return a[0] * c - a[1] * s, a[0] * s + a[1] * c


def _fft_list(xs, sign):
    size = len(xs)
    if size == 1:
        return xs
    even = _fft_list(xs[0::2], sign)
    odd = _fft_list(xs[1::2], sign)
    out = [None] * size
    for k in range(size // 2):
        ang = sign * 2.0 * math.pi * k / size
        tr, ti = _cmul_const(odd[k], math.cos(ang), math.sin(ang))
        out[k] = (even[k][0] + tr, even[k][1] + ti)
        out[k + size // 2] = (even[k][0] - tr, even[k][1] - ti)
    return out


def _fft_tables(n):
    j_cls = HY_CLASSES
    k0n, mcls = 2 * n // j_cls, n // j_cls
    t = j_cls * np.arange(mcls)[None, None, :] + np.arange(j_cls)[:, None, None]
    ang = 2.0 * math.pi * ((np.arange(k0n)[None, :, None] * t) % (2 * n)) / (2 * n)
    cs = np.concatenate([np.cos(ang), np.sin(ang)], axis=1)
    cst = np.concatenate([np.cos(ang).transpose(0, 2, 1), np.sin(ang).transpose(0, 2, 1)], axis=1)
    return jnp.asarray(cs, dtype=BF16), jnp.asarray(cst, dtype=BF16)


def _filt_kernel(z_ref, w1_ref, b1_ref, w2_ref, b2_ref, w3_ref, b3_ref, fr_ref, w4_ref, dl_ref,
                 hf_ref, hb_ref, *, n, tt, hw):
    i = pl.program_id(0)
    dot = functools.partial(jnp.dot, preferred_element_type=F32, precision=HIGHEST)
    fr = fr_ref[...]
    a = jnp.sin(fr[0:1] * (dot(z_ref[...], w1_ref[...]) + b1_ref[...]))
    a = jnp.sin(fr[1:2] * (dot(a, w2_ref[...]) + b2_ref[...]))
    a = jnp.sin(fr[2:3] * (dot(a, w3_ref[...]) + b3_ref[...]))
    hf = jnp.dot(a.astype(BF16), w4_ref[...], preferred_element_type=F32)
    mcls = n // HY_CLASSES
    rho = lax.broadcasted_iota(jnp.int32, (tt, hw), 0) + i * tt
    ti = HY_CLASSES * (rho & (mcls - 1)) + lax.shift_right_logical(rho, int(math.log2(mcls)))
    t = ti.astype(F32) * (1.0 / (n - 1.0))
    window = jnp.exp(-t * dl_ref[...])
    for o in range(HY_ORDER):
        hf_ref[:, o * hw:(o + 1) * hw] = (hf[:, (2 * o) * hw:(2 * o + 1) * hw] * window).astype(BF16)
        h_b = jnp.where(ti == 0, 0.0, hf[:, (2 * o + 1) * hw:(2 * o + 2) * hw] * window)
        hb_ref[:, o * hw:(o + 1) * hw] = h_b.astype(BF16)


def _filt_call(n, fw1, fb1, fw2, fb2, fw3, fb3, freq, fw4, hw):
    hid = fw1.shape[1]
    pk = LANE
    mcls = n // HY_CLASSES
    assert mcls & (mcls - 1) == 0
    bands = (HY_EMB_DIM - 1) // 2
    tpos = np.linspace(0.0, 1.0, n, dtype=np.float32)[:, None].astype(np.float64)
    fb = np.linspace(1e-4, bands - 1, bands, dtype=np.float32)[None, :].astype(np.float64)
    wv = (2.0 * math.pi * np.arange(n, dtype=np.float64)[:, None] / n)
    z = np.concatenate([tpos, np.cos(fb * wv), -np.sin(fb * wv)], axis=-1)
    rho = np.arange(n)
    t_of_row = HY_CLASSES * (rho % mcls) + rho // mcls
    z = np.pad(z, ((0, 0), (0, pk - HY_EMB_DIM))).astype(np.float32)[t_of_row]
    max_decay = math.log(HY_DECAY_TARGET) / HY_DECAY_SHORT_PCT
    min_decay = math.log(HY_DECAY_TARGET) / HY_DECAY_LONG_PCT
    deltas = np.abs(np.linspace(min_decay, max_decay, hw, dtype=np.float32)).reshape(1, hw)

    padc = lambda w: jnp.pad(w, ((0, 0), (0, pk - hid)))
    w1 = jnp.pad(fw1, ((0, pk - HY_EMB_DIM), (0, pk - hid)))
    w2 = jnp.pad(fw2, ((0, pk - hid), (0, pk - hid)))
    w3 = jnp.pad(fw3, ((0, pk - hid), (0, pk - hid)))
    w4 = jnp.pad(fw4, ((0, pk - hid), (0, 0))).astype(BF16)
    b1, b2, b3 = (padc(b.reshape(1, hid)) for b in (fb1, fb2, fb3))
    fr = jnp.pad(freq, ((0, 8 - freq.shape[0]), (0, pk - hid)))
    tt = _tile(n, 256)
    nw = HY_ORDER * hw
    full = lambda shape: pl.BlockSpec(shape, lambda i: (0, 0))
    return pl.pallas_call(
        functools.partial(_filt_kernel, n=n, tt=tt, hw=hw),
        grid=(n // tt,),
        in_specs=[
            pl.BlockSpec((tt, pk), lambda i: (i, 0)),
            full((pk, pk)), full((1, pk)), full((pk, pk)), full((1, pk)),
            full((pk, pk)), full((1, pk)), full((8, pk)), full((pk, 2 * nw)), full((1, hw)),
        ],
        out_specs=[pl.BlockSpec((tt, nw), lambda i: (i, 0))] * 2,
        out_shape=[jax.ShapeDtypeStruct((n, nw), BF16)] * 2,
        compiler_params=_params(("arbitrary",)),
        name="hyena_filters",
    )(jnp.asarray(z), w1, b1, w2, b2, w3, b3, fr, w4, jnp.asarray(deltas))


def _sconv_kernel(p_ref, w_ref, b_ref, o_ref, *, n):
    w = w_ref[...]
    bias = b_ref[...]
    mcls = n // HY_CLASSES
    row = lax.broadcasted_iota(jnp.int32, (mcls, p_ref.shape[2]), 0)

    def cls(j):
        return p_ref[0, j * mcls:(j + 1) * mcls, :].astype(F32)

    for j in range(HY_CLASSES):
        cur = cls(j)
        if j > 0:
            prev = cls(j - 1)
        else:
            prev = jnp.where(row == 0, 0.0, pltpu.roll(cls(HY_CLASSES - 1), 1, 0))
        if j < HY_CLASSES - 1:
            nxt = cls(j + 1)
        else:
            nxt = jnp.where(row == mcls - 1, 0.0, pltpu.roll(cls(0), mcls - 1, 0))
        out = bias + prev * w[0:1] + cur * w[1:2] + nxt * w[2:3]
        o_ref[0, j * mcls:(j + 1) * mcls, :] = out.astype(o_ref.dtype)


def _sconv_call(p, conv_w, conv_b):
    bsz, n, width = p.shape
    tc = _tile(width, 512)
    return pl.pallas_call(
        functools.partial(_sconv_kernel, n=n),
        grid=(bsz, width // tc),
        in_specs=[
            pl.BlockSpec((1, n, tc), lambda b, j: (b, 0, j)),
            pl.BlockSpec((HY_SHORT, tc), lambda b, j: (0, j)),
            pl.BlockSpec((1, tc), lambda b, j: (0, j)),
        ],
        out_specs=pl.BlockSpec((1, n, tc), lambda b, j: (b, 0, j)),
        out_shape=jax.ShapeDtypeStruct((bsz, n, width), BF16),
        compiler_params=_params(("arbitrary", "arbitrary")),
        name="short_conv",
    )(p, conv_w, conv_b.reshape(1, width))


def _spec_kernel(cs_ref, hf_ref, hb_ref, kre_ref, kim_ref, wf_scr, wb_scr,
                 *, k0n, mcls):
    j_cls = HY_CLASSES
    tn = hf_ref.shape[1]
    for j in range(j_cls):
        rows = slice(j * mcls, (j + 1) * mcls)
        prod = jnp.dot(cs_ref[j], jnp.concatenate([hf_ref[rows, :], hb_ref[rows, :]], axis=1),
                       preferred_element_type=F32)
        wf_scr[j, 0:k0n, :] = prod[0:k0n, 0:tn]
        wf_scr[j, k0n:2 * k0n, :] = -prod[k0n:2 * k0n, 0:tn]
        wb_scr[j, 0:k0n, :] = prod[0:k0n, tn:2 * tn]
        wb_scr[j, k0n:2 * k0n, :] = -prod[k0n:2 * k0n, tn:2 * tn]

    def body(c, carry):
        halves = []
        for half in range(2):
            r8 = pl.ds(pl.multiple_of(c * 16 + half * 8, 8), 8)
            i8 = pl.ds(pl.multiple_of(k0n + c * 16 + half * 8, 8), 8)
            xs = [(wf_scr[j, r8, :], wf_scr[j, i8, :]) for j in range(j_cls)]
            ys = [(wb_scr[j, r8, :], wb_scr[j, i8, :]) for j in range(j_cls)]
            halves.append((_fft_list(xs, -1), _fft_list(ys, -1)))
        for q in range(j_cls):
            rows = pl.ds(pl.multiple_of(q * k0n + c * 16, 16), 16)
            kre = jnp.concatenate([h[0][q][0] + h[1][q][0] for h in halves], axis=0)
            kim = jnp.concatenate([h[0][q][1] - h[1][q][1] for h in halves], axis=0)
            kre_ref[rows, :] = kre.astype(kre_ref.dtype)
            kim_ref[rows, :] = kim.astype(kim_ref.dtype)
        return carry

    lax.fori_loop(0, k0n // 16, body, 0)


def _spec_call(tables, hf, hb):
    cs, _ = tables
    n, width = hf.shape
    tn = LANE
    k0n, mcls = cs.shape[1] // 2, cs.shape[2]
    const = lambda shape: pl.BlockSpec(shape, lambda j: (0,) * len(shape),
                                       pipeline_mode=pl.Buffered(1))
    return pl.pallas_call(
        functools.partial(_spec_kernel, k0n=k0n, mcls=mcls),
        grid=(width // tn,),
        in_specs=[
            const(cs.shape),
            pl.BlockSpec((n, tn), lambda j: (0, j)),
            pl.BlockSpec((n, tn), lambda j: (0, j)),
        ],
        out_specs=[pl.BlockSpec((2 * n, tn), lambda j: (0, j))] * 2,
        out_shape=[jax.ShapeDtypeStruct((2 * n, width), BF16)] * 2,
        scratch_shapes=[pltpu.VMEM((HY_CLASSES, 2 * k0n, tn), F32)] * 2,
        compiler_params=_params(("arbitrary",)),
        name="filter_spectrum",
    )(cs, hf, hb)


def _fftconv_kernel(cs_ref, cst_ref, z_ref, g_ref, kre_ref, kim_ref, bias_ref,
                    o_ref, w_scr, *, k0n, mcls, scale):
    j_cls = HY_CLASSES
    tn = o_ref.shape[3]
    for j in range(j_cls):
        rows = slice(j * mcls, (j + 1) * mcls)
        prod = jnp.dot(cs_ref[j], jnp.concatenate([z_ref[0, 0, rows, :], z_ref[0, 1, rows, :]], axis=1),
                       preferred_element_type=F32)
        w_scr[j, 0:k0n, :] = prod[0:k0n, 0:tn] + prod[k0n:2 * k0n, tn:2 * tn]
        w_scr[j, k0n:2 * k0n, :] = prod[0:k0n, tn:2 * tn] - prod[k0n:2 * k0n, 0:tn]

    def body(c, carry):
        for half in range(2):
            r8 = pl.ds(pl.multiple_of(c * 16 + half * 8, 8), 8)
            i8 = pl.ds(pl.multiple_of(k0n + c * 16 + half * 8, 8), 8)
            xs = [(w_scr[j, r8, :], w_scr[j, i8, :]) for j in range(j_cls)]
            spec = _fft_list(xs, -1)
            prod = []
            for q in range(j_cls):
                rows = pl.ds(pl.multiple_of(q * k0n + c * 16, 16), 16)
                kq = (kre_ref[rows, :].astype(F32)[half * 8:(half + 1) * 8],
                      kim_ref[rows, :].astype(F32)[half * 8:(half + 1) * 8])
                prod.append(_cmul(spec[q], kq))
            back = _fft_list(prod, 1)
            for j in range(j_cls):
                w_scr[j, r8, :] = back[j][0]
                w_scr[j, i8, :] = back[j][1]
        return carry

    lax.fori_loop(0, k0n // 16, body, 0)

    bias = bias_ref[...]
    for j in range(j_cls):
        rows = slice(j * mcls, (j + 1) * mcls)
        v2 = jnp.concatenate([w_scr[j, 0:k0n, :], w_scr[j, k0n:2 * k0n, :]], axis=1).astype(BF16)
        prod = jnp.dot(cst_ref[j], v2, preferred_element_type=F32) * scale
        ys = (prod[0:mcls, 0:tn] - prod[mcls:2 * mcls, tn:2 * tn],
              prod[mcls:2 * mcls, 0:tn] + prod[0:mcls, tn:2 * tn])
        for b2 in range(2):
            conv = ys[b2] + z_ref[0, b2, rows, :].astype(F32) * bias
            o_ref[0, b2, rows, :] = (g_ref[0, b2, rows, :].astype(F32) * conv).astype(o_ref.dtype)


def _fftconv_call(tables, z, z_col0, gate, gate_col0, kre, kim, k_col0, bias):
    cs, cst = tables
    bsz, n, _ = z.shape
    hw = bias.shape[0]
    tn = LANE
    k0n, mcls = cs.shape[1] // 2, cs.shape[2]
    zo, go, ko = z_col0 // tn, gate_col0 // tn, k_col0 // tn
    z4 = z.reshape(bsz // 2, 2, n, z.shape[2])
    g4 = gate.reshape(bsz // 2, 2, n, gate.shape[2])
    const = lambda shape: pl.BlockSpec(shape, lambda j, p: (0,) * len(shape),
                                       pipeline_mode=pl.Buffered(1))
    out = pl.pallas_call(
        functools.partial(_fftconv_kernel, k0n=k0n, mcls=mcls, scale=1.0 / (2 * n)),
        grid=(hw // tn, bsz // 2),
        in_specs=[
            const(cs.shape), const(cst.shape),
            pl.BlockSpec((1, 2, n, tn), lambda j, p: (p, 0, 0, zo + j)),
            pl.BlockSpec((1, 2, n, tn), lambda j, p: (p, 0, 0, go + j)),
            pl.BlockSpec((2 * n, tn), lambda j, p: (0, ko + j)),
            pl.BlockSpec((2 * n, tn), lambda j, p: (0, ko + j)),
            pl.BlockSpec((1, tn), lambda j, p: (0, j)),
        ],
        out_specs=pl.BlockSpec((1, 2, n, tn), lambda j, p: (p, 0, 0, j)),
        out_shape=jax.ShapeDtypeStruct((bsz // 2, 2, n, hw), BF16),
        scratch_shapes=[pltpu.VMEM((HY_CLASSES, 2 * k0n, tn), F32)],
        compiler_params=_params(("arbitrary", "arbitrary")),
        name="fft_conv",
    )(cs, cst, z4, g4, kre, kim, bias.reshape(1, hw))
    return out.reshape(bsz, n, hw)


def _wout_kernel(r_ref, y_ref, wa_ref, wb_ref, x_ref, gate_ref, o_ref):
    y = (jnp.dot(r_ref[0], wa_ref[...], preferred_element_type=F32)
         + jnp.dot(y_ref[0], wb_ref[...], preferred_element_type=F32))
    o_ref[0] = x_ref[0] + gate_ref[0] * y


def _wout_call(r, yh, w_out, x, gate):
    bsz, n, d = x.shape
    rw = r.shape[2]
    assert d == 2 * rw and yh.shape[2] == rw
    tm = _tile(n, 512)
    return pl.pallas_call(
        _wout_kernel,
        grid=(bsz, n // tm),
        in_specs=[
            pl.BlockSpec((1, tm, rw), lambda b, i: (b, i, 0)),
            pl.BlockSpec((1, tm, d - rw), lambda b, i: (b, i, 0)),
            pl.BlockSpec((rw, d), lambda b, i: (0, 0)),
            pl.BlockSpec((rw, d), lambda b, i: (1, 0)),
            pl.BlockSpec((1, tm, d), lambda b, i: (b, i, 0)),
            pl.BlockSpec((1, 1, d), lambda b, i: (b, 0, 0)),
        ],
        out_specs=pl.BlockSpec((1, tm, d), lambda b, i: (b, i, 0)),
        out_shape=jax.ShapeDtypeStruct((bsz, n, d), F32),
        compiler_params=_params(("arbitrary", "arbitrary")),
        name="mixer_out",
    )(r, yh, w_out, w_out, x, gate)


def _pool_kernel(h_ref, m_ref, ic_ref, w_ref, sc_ref, x_ref, gate_ref, o_ref, *, tm, nt):
    i = pl.program_id(2)
    rows_c = pl.ds(pl.multiple_of(i * tm, tm), tm)
    rows_l = pl.ds(pl.multiple_of(jnp.maximum(i - 1, 0) * tm, tm), tm)
    rows_r = pl.ds(pl.multiple_of(jnp.minimum(i + 1, nt - 1) * tm, tm), tm)
    hc = h_ref[0, rows_c, :]
    acc = jnp.dot(m_ref[0, :, tm:2 * tm], hc, preferred_element_type=F32)
    left = jnp.dot(m_ref[0, :, 0:tm], h_ref[0, rows_l, :], preferred_element_type=F32)
    right = jnp.dot(m_ref[0, :, 2 * tm:3 * tm], h_ref[0, rows_r, :], preferred_element_type=F32)
    acc = acc + jnp.where(i > 0, left, 0.0) + jnp.where(i < nt - 1, right, 0.0)
    pooled = acc * ic_ref[0] - hc.astype(F32)
    y = jnp.dot(pooled.astype(BF16), w_ref[0], preferred_element_type=F32) * sc_ref[...]
    o_ref[0] = x_ref[0] + gate_ref[0] * y


def _pool_tables(n, tm):
    rows_per_tile = tm // GRID_W
    n_rows = n // GRID_W
    tok = np.arange(tm)
    r_out, c_out = tok // GRID_W, tok % GRID_W
    tin = np.arange(3 * tm)
    r_in, c_in = tin // GRID_W - rows_per_tile, tin % GRID_W
    mats, inv = [], []
    for win in POOL_WINDOWS:
        lo, hi = -(win // 2), win - 1 - win // 2
        dr = r_in[None, :] - r_out[:, None]
        dc = c_in[None, :] - c_out[:, None]
        mats.append(((dr >= lo) & (dr <= hi) & (dc >= lo) & (dc <= hi)).astype(np.float32))
        idx_r, idx_c = np.arange(n_rows), np.arange(GRID_W)
        cnt_r = np.minimum(idx_r + hi, n_rows - 1) + 1 - np.maximum(idx_r + lo, 0)
        cnt_c = np.minimum(idx_c + hi, GRID_W - 1) + 1 - np.maximum(idx_c + lo, 0)
        inv.append((1.0 / (cnt_r[:, None] * cnt_c[None, :]).astype(np.float64)).reshape(n, 1))
    return (jnp.asarray(np.stack(mats), dtype=BF16),
            jnp.asarray(np.stack(inv).astype(np.float32)))


def _pool_call(h, pool_w, pool_scale, x, gate):
    bsz, n, d = x.shape
    ng = len(POOL_WINDOWS)
    pg = d // ng
    tm = _tile(n, 512)
    nt = n // tm
    assert tm % GRID_W == 0 and tm // GRID_W >= max(POOL_WINDOWS) // 2
    mats, inv = _pool_tables(n, tm)
    return pl.pallas_call(
        functools.partial(_pool_kernel, tm=tm, nt=nt),
        grid=(bsz, ng, nt),
        in_specs=[
            pl.BlockSpec((1, n, pg), lambda b, g, i: (b, 0, g)),
            pl.BlockSpec((1, tm, 3 * tm), lambda b, g, i: (g, 0, 0)),
            pl.BlockSpec((1, tm, 1), lambda b, g, i: (g, i, 0)),
            pl.BlockSpec((1, pg, pg), lambda b, g, i: (g, 0, 0)),
            pl.BlockSpec((1, pg), lambda b, g, i: (0, g)),
            pl.BlockSpec((1, tm, pg), lambda b, g, i: (b, i, g)),
            pl.BlockSpec((1, 1, pg), lambda b, g, i: (b, 0, g)),
        ],
        out_specs=pl.BlockSpec((1, tm, pg), lambda b, g, i: (b, i, g)),
        out_shape=jax.ShapeDtypeStruct((bsz, n, d), F32),
        compiler_params=_params(("arbitrary", "arbitrary", "arbitrary")),
        name="pool_mixer",
    )(h, mats, inv, pool_w, pool_scale.reshape(1, d), x, gate)


def _rotary_tables(n):
    pos = np.arange(n)
    row = (pos // GRID_W).astype(np.float32)
    col = (pos % GRID_W).astype(np.float32)
    n_freq = HEAD_DIM // 4
    inv = (np.float32(ROPE_BASE) ** (-np.arange(n_freq, dtype=np.float32) / np.float32(n_freq))).astype(np.float32)
    ang = np.concatenate([row[:, None] * inv, col[:, None] * inv], axis=-1).astype(np.float32)
    cos, sin = np.cos(ang.astype(np.float64)), np.sin(ang.astype(np.float64))
    cos2 = np.concatenate([cos, cos], axis=-1).astype(np.float32)
    sin2 = np.concatenate([-sin, sin], axis=-1).astype(np.float32)
    return jnp.asarray(cos2), jnp.asarray(sin2)


def kernel(x, c, ctx, c_ctx, w_mod, b_mod, ffn1_w1, ffn1_w3, ffn1_w2, ffn2_w1, ffn2_w3, ffn2_w2, ab_w_in, ab_w_out, ret_log_decay, hy_conv_w, hy_conv_b, hy_f_w1, hy_f_b1, hy_f_w2, hy_f_b2, hy_f_w3, hy_f_b3, hy_f_freq, hy_f_w4, hy_bias, pool_w, pool_scale, final_gain):
    bsz, n, d = x.shape
    n_ctx = ctx.shape[1]
    depth = w_mod.shape[0]
    assert depth == 2, "layer 0 = retention/Hyena mixer, layer 1 = pooling mixer"
    ret_w = RET_HEADS * HEAD_DIM
    hw = d - ret_w
    assert hy_f_w4.shape[2] == HY_ORDER * 2 * hw and bsz % 2 == 0

    c_all = jnp.concatenate([c, c_ctx[None, :], jnp.zeros((8 - bsz - 1, d), F32)], axis=0)
    mod = _mod_call(c_all, w_mod, b_mod).reshape(depth, 8, N_MOD, d)

    def mod_rows(layer, base, count=3):
        return mod[layer, :bsz, base:base + count]

    def mod_ctx(base, count=3):
        return jnp.broadcast_to(mod[0, bsz:bsz + 1, base:base + count], (bsz, count, d))

    n_f = ffn1_w1.shape[2] // _tile(ffn1_w1.shape[2], 512)
    w13 = _cast_pairs_call([(ffn1_w1, ffn1_w3), (ffn2_w1, ffn2_w3)], n_f)
    w2s = _cast_call([ffn1_w2, ffn2_w2], n_f)
    ffn1_w, ffn2_w = (w13[0], w2s[0]), (w13[1], w2s[1])

    x, h = _ffn_call(x, mod_rows(0, 0), *ffn1_w, 0, modn=mod_rows(0, 3))
    _, h_c = _ffn_call(ctx, mod_ctx(0), *ffn1_w, 0, modn=mod_ctx(3))

    w_in, w_out = _cast_call([ab_w_in, ab_w_out], 8)
    w_in, w_out = w_in[0], w_out[0]
    p = _matmul_call(h.reshape(bsz * n, d), w_in, BF16, b_cols=(0, 4 * ret_w)).reshape(bsz, n, -1)
    kvc = _matmul_call(h_c.reshape(bsz * n_ctx, d), w_in, BF16, b_cols=(ret_w, 3 * ret_w))
    kvc = kvc.reshape(bsz, n_ctx, 2 * ret_w)

    cos2, sin2 = _rotary_tables(n)
    r = _ret_call(p, kvc, ret_log_decay[0], cos2, sin2, ret_w)

    mcls = n // HY_CLASSES
    h_cm = h.reshape(bsz, mcls, HY_CLASSES, d).transpose(0, 2, 1, 3).reshape(bsz * n, d)
    p_hy = _matmul_call(h_cm, w_in, BF16, b_cols=(4 * ret_w, w_in.shape[1])).reshape(bsz, n, -1)
    hf, hb = _filt_call(n, hy_f_w1[0], hy_f_b1[0], hy_f_w2[0], hy_f_b2[0], hy_f_w3[0],
                        hy_f_b3[0], hy_f_freq[0], hy_f_w4[0], hw)
    tables = _fft_tables(n)
    kre, kim = _spec_call(tables, hf, hb)
    u = _sconv_call(p_hy, hy_conv_w[0], hy_conv_b[0])
    z = _fftconv_call(tables, u, 0, u, hw, kre, kim, 0, hy_bias[0, 0])
    yh = _fftconv_call(tables, z, 0, u, 2 * hw, kre, kim, hw, hy_bias[0, 1])
    yh = yh.reshape(bsz, HY_CLASSES, mcls, hw).transpose(0, 2, 1, 3).reshape(bsz, n, hw)

    x = _wout_call(r, yh, w_out, x, mod_rows(0, 5, 1))
    x = _ffn_call(x, mod_rows(0, 6), *ffn2_w, 0)

    x, h = _ffn_call(x, mod_rows(1, 0), *ffn1_w, 1, modn=mod_rows(1, 3))
    x = _pool_call(h, pool_w[0].astype(BF16), pool_scale[0], x, mod_rows(1, 5, 1))
    return _ffn_call(x, mod_rows(1, 6), *ffn2_w, 1, gain=final_gain)
```

```python
import functools
import math

import numpy as np
import jax
import jax.numpy as jnp
from jax import lax
from jax.experimental import pallas as pl
from jax.experimental.pallas import tpu as pltpu

F32 = jnp.float32
BF16 = jnp.bfloat16
HIGHEST = lax.Precision.HIGHEST

EPS = 1e-6
GRID_W = 64
N_MOD = 9
RET_HEADS = 8
HEAD_DIM = 128
HY_ORDER = 2
HY_SHORT = 3
HY_EMB_DIM = 33
HY_DECAY_SHORT_PCT = 0.3
HY_DECAY_LONG_PCT = 1.5
HY_DECAY_TARGET = 1e-2
ROPE_BASE = 10000.0
POOL_WINDOWS = (2, 4, 8, 16)
RET_CHUNK = 256
RET_UNROLL = 4
HY_CLASSES = 16
LANE = 128
VMEM_LIMIT = 56 * 1024 * 1024


def _tile(dim, pref):
    if dim <= pref:
        return dim
    for t in range(pref - pref % LANE, 0, -LANE):
        if dim % t == 0:
            return t
    raise ValueError((dim, pref))


def _params(sem):
    return pltpu.CompilerParams(dimension_semantics=sem, vmem_limit_bytes=VMEM_LIMIT)


def _rms(x):
    return x * lax.rsqrt(jnp.mean(x * x, axis=-1, keepdims=True) + EPS)


def _silu(x):
    return x * jax.nn.sigmoid(x)


def _mod_kernel(c_ref, w_ref, b_ref, o_ref):
    sc = _silu(c_ref[...])
    w = w_ref[0]
    sc_hi = sc.astype(BF16)
    sc_lo = (sc - sc_hi.astype(F32)).astype(BF16)
    w_hi = w.astype(BF16)
    w_lo = (w - w_hi.astype(F32)).astype(BF16)
    rows = sc.shape[0]
    top = jnp.dot(jnp.concatenate([sc_hi, sc_lo], axis=0), w_hi, preferred_element_type=F32)
    o_ref[0] = (top[0:rows] + top[rows:2 * rows]
                + jnp.dot(sc_hi, w_lo, preferred_element_type=F32) + b_ref[0])


def _mod_call(c_all, w_mod, b_mod):
    depth, d, nd = w_mod.shape
    rows = c_all.shape[0]
    tn = _tile(nd, 1024)
    return pl.pallas_call(
        _mod_kernel,
        grid=(depth, nd // tn),
        in_specs=[
            pl.BlockSpec((rows, d), lambda l, j: (0, 0)),
            pl.BlockSpec((1, d, tn), lambda l, j: (l, 0, j)),
            pl.BlockSpec((1, 1, tn), lambda l, j: (l, 0, j)),
        ],
        out_specs=pl.BlockSpec((1, rows, tn), lambda l, j: (l, 0, j)),
        out_shape=jax.ShapeDtypeStruct((depth, rows, nd), F32),
        compiler_params=_params(("arbitrary", "arbitrary")),
        name="mod",
    )(c_all, w_mod, b_mod.reshape(depth, 1, nd))


def _ffn_kernel(*refs, n_f, emit_h, final):
    x_ref, mod_ref, w13_ref, w2_ref = refs[:4]
    pos = 4
    modn_ref = gain_ref = hn_ref = None
    if emit_h:
        modn_ref = refs[pos]
        pos += 1
    if final:
        gain_ref = refs[pos]
        pos += 1
    o_ref = refs[pos]
    pos += 1
    if emit_h:
        hn_ref = refs[pos]
        pos += 1
    h_scr, acc_scr = refs[pos], refs[pos + 1]
    f = pl.program_id(2)

    assert n_f >= 3

    tf = w2_ref.shape[0]

    def partial_out(h):
        ab = jnp.dot(h, w13_ref[...], preferred_element_type=F32)
        g = (_silu(ab[:, 0:tf]) * ab[:, tf:2 * tf]).astype(BF16)
        return jnp.dot(g, w2_ref[...], preferred_element_type=F32)

    @pl.when(f == 0)
    def _():
        m = mod_ref[0]
        h = (_rms(x_ref[0]) * (1.0 + m[1:2]) + m[0:1]).astype(BF16)
        h_scr[...] = h
        acc_scr[...] = partial_out(h)

    @pl.when(jnp.logical_and(f > 0, f < n_f - 1))
    def _():
        acc_scr[...] += partial_out(h_scr[...])

    @pl.when(f == n_f - 1)
    def _():
        m = mod_ref[0]
        xo = x_ref[0] + 0.5 * m[2:3] * (acc_scr[...] + partial_out(h_scr[...]))
        if emit_h:
            mn = modn_ref[0]
            hn_ref[0] = (_rms(xo) * (1.0 + mn[1:2]) + mn[0:1]).astype(BF16)
        if final:
            xo = _rms(xo) * gain_ref[...]
        o_ref[0] = xo


def _ffn_call(x, mod3, w13, w2, layer, modn=None, gain=None):
    bsz, s, d = x.shape
    n_f, tf = w13.shape[1], w13.shape[3] // 2
    tm = _tile(s, 512)
    emit_h = modn is not None
    final = gain is not None
    in_specs = [
        pl.BlockSpec((1, tm, d), lambda b, i, f: (b, i, 0)),
        pl.BlockSpec((1, 3, d), lambda b, i, f: (b, 0, 0)),
        pl.BlockSpec((None, None, d, 2 * tf), lambda b, i, f: (layer, f, 0, 0)),
        pl.BlockSpec((None, tf, d), lambda b, i, f: (layer, f, 0)),
    ]
    args = [x, mod3, w13, w2]
    if emit_h:
        in_specs.append(pl.BlockSpec((1, 3, d), lambda b, i, f: (b, 0, 0)))
        args.append(modn)
    if final:
        in_specs.append(pl.BlockSpec((1, d), lambda b, i, f: (0, 0)))
        args.append(gain.reshape(1, d))
    out_specs = [pl.BlockSpec((1, tm, d), lambda b, i, f: (b, i, 0))]
    out_shape = [jax.ShapeDtypeStruct((bsz, s, d), F32)]
    if emit_h:
        out_specs.append(pl.BlockSpec((1, tm, d), lambda b, i, f: (b, i, 0)))
        out_shape.append(jax.ShapeDtypeStruct((bsz, s, d), BF16))
    outs = pl.pallas_call(
        functools.partial(_ffn_kernel, n_f=n_f, emit_h=emit_h, final=final),
        grid=(bsz, s // tm, n_f),
        in_specs=in_specs,
        out_specs=out_specs,
        out_shape=out_shape,
        scratch_shapes=[pltpu.VMEM((tm, d), BF16), pltpu.VMEM((tm, d), F32)],
        compiler_params=_params(("arbitrary", "arbitrary", "arbitrary")),
        name="ffn",
    )(*args)
    return outs if emit_h else outs[0]


def _cast_kernel(*refs):
    half = len(refs) // 2
    for src, dst in zip(refs[:half], refs[half:]):
        dst[...] = src[...].astype(dst.dtype)


def _cast_call(arrays, steps):
    depth = arrays[0].shape[0]
    specs = []
    for a in arrays:
        _, r, c = a.shape
        assert r % (16 * steps) == 0
        specs.append(pl.BlockSpec((1, r // steps, c), lambda l, j: (l, j, 0)))
    return pl.pallas_call(
        _cast_kernel,
        grid=(depth, steps),
        in_specs=specs,
        out_specs=specs,
        out_shape=[jax.ShapeDtypeStruct(a.shape, BF16) for a in arrays],
        compiler_params=_params(("arbitrary", "arbitrary")),
        name="cast_weights",
    )(*arrays)


def _cast_pairs_kernel(*refs):
    n_out = len(refs) // 3
    for k in range(n_out):
        wa, wb, dst = refs[2 * k], refs[2 * k + 1], refs[2 * n_out + k]
        tf = wa.shape[2]
        dst[0, 0, :, 0:tf] = wa[0].astype(dst.dtype)
        dst[0, 0, :, tf:2 * tf] = wb[0].astype(dst.dtype)


def _cast_pairs_call(pairs, steps):
    depth, r, c = pairs[0][0].shape
    tf = c // steps
    in_spec = pl.BlockSpec((1, r, tf), lambda l, j: (l, 0, j))
    out_spec = pl.BlockSpec((1, 1, r, 2 * tf), lambda l, j: (l, j, 0, 0))
    return pl.pallas_call(
        _cast_pairs_kernel,
        grid=(depth, steps),
        in_specs=[in_spec] * (2 * len(pairs)),
        out_specs=[out_spec] * len(pairs),
        out_shape=[jax.ShapeDtypeStruct((depth, steps, r, 2 * tf), BF16)] * len(pairs),
        compiler_params=_params(("arbitrary", "arbitrary")),
        name="cast_weight_pairs",
    )(*[w for pair in pairs for w in pair])


def _matmul_kernel(a_ref, b_ref, o_ref):
    o_ref[...] = jnp.dot(a_ref[...], b_ref[...], preferred_element_type=F32).astype(o_ref.dtype)


def _matmul_call(a, b, out_dtype, tm_pref=1024, tn_pref=1024, b_cols=None):
    m, k = a.shape
    c0, c1 = b_cols if b_cols is not None else (0, b.shape[1])
    n = c1 - c0
    tm = _tile(m, tm_pref)
    tn = math.gcd(_tile(n, tn_pref), c0) if c0 else _tile(n, tn_pref)
    assert tn % LANE == 0
    joff = c0 // tn
    return pl.pallas_call(
        _matmul_kernel,
        grid=(m // tm, n // tn),
        in_specs=[
            pl.BlockSpec((tm, k), lambda i, j: (i, 0)),
            pl.BlockSpec((k, tn), lambda i, j: (0, joff + j)),
        ],
        out_specs=pl.BlockSpec((tm, tn), lambda i, j: (i, j)),
        out_shape=jax.ShapeDtypeStruct((m, n), out_dtype),
        compiler_params=_params(("arbitrary", "arbitrary")),
        name="matmul",
    )(a, b)


def _ret_kernel(lg_ref, q_ref, k_ref, v_ref, g_ref, cos_ref, sin_ref, kc_ref, vc_ref, o_ref,
                qr_scr, kr_scr, upd_scr, st_scr, *, n, n_ctx, chunk):
    hd = HEAD_DIM
    head = pl.program_id(1)
    lgf = lg_ref[0, head]
    lgb = lg_ref[1, head]
    nc = n // chunk
    k_scale = hd ** -0.5

    posc = lax.broadcasted_iota(jnp.int32, (n_ctx, hd), 0).astype(F32)
    kc = kc_ref[0].astype(F32) * k_scale
    vc = vc_ref[0]
    kcf = (kc * jnp.exp((n_ctx - 1.0 - posc) * lgf)).T.astype(BF16)
    kcb = (kc * jnp.exp(posc * lgb)).T.astype(BF16)
    s0_f = jnp.dot(kcf, vc, preferred_element_type=F32)
    s0_b = jnp.dot(kcb, vc, preferred_element_type=F32)

    pos = lax.broadcasted_iota(jnp.int32, (chunk, hd), 0).astype(F32)
    qdec_f = jnp.exp((pos + 1.0) * lgf)
    kdec_f = jnp.exp((chunk - 1.0 - pos) * lgf)
    qdec_b = jnp.exp((chunk - pos) * lgb)
    kdec_b = jnp.exp(pos * lgb)
    cdec_f = jnp.exp(jnp.full((1, hd), chunk, F32) * lgf)
    cdec_b = jnp.exp(jnp.full((1, hd), chunk, F32) * lgb)
    ri = lax.broadcasted_iota(jnp.int32, (chunk, chunk), 0)
    ci = lax.broadcasted_iota(jnp.int32, (chunk, chunk), 1)
    diff = (ri - ci).astype(F32)
    dmat = (jnp.where(diff >= 0, jnp.exp(jnp.maximum(diff, 0.0) * lgf), 0.0)
            + jnp.where(diff <= 0, jnp.exp(jnp.maximum(-diff, 0.0) * lgb), 0.0))

    def rotary(x, rows):
        return x * cos_ref[rows, :] + pltpu.roll(x, hd // 2, 1) * sin_ref[rows, :]

    def chunk_rows(c):
        return pl.ds(pl.multiple_of(c * chunk, chunk), chunk)

    def prep_body(c, carry):
        rows = chunk_rows(c)
        qr_scr[rows, :] = rotary(q_ref[0, rows, :].astype(F32), rows)
        k = rotary(k_ref[0, rows, :].astype(F32), rows) * k_scale
        kr_scr[rows, :] = k
        kk = jnp.concatenate([k * kdec_f, k * kdec_b], axis=1).T.astype(BF16)
        upd_scr[c] = jnp.dot(kk, v_ref[0, rows, :], preferred_element_type=F32)
        return carry

    lax.fori_loop(0, nc, prep_body, 0, unroll=math.gcd(nc, RET_UNROLL))

    def fwd_scan(c, sf):
        st_scr[c, 0:hd, :] = sf.astype(BF16)
        return cdec_f * sf + upd_scr[c, 0:hd, :]

    def bwd_scan(j, sb):
        c = nc - 1 - j
        st_scr[c, hd:2 * hd, :] = sb.astype(BF16)
        return cdec_b * sb + upd_scr[c, hd:2 * hd, :]

    lax.fori_loop(0, nc, fwd_scan, s0_f, unroll=math.gcd(nc, RET_UNROLL))
    lax.fori_loop(0, nc, bwd_scan, s0_b, unroll=math.gcd(nc, RET_UNROLL))

    def out_body(c, carry):
        rows = chunk_rows(c)
        q = qr_scr[rows, :]
        s = lax.dot_general(q.astype(BF16), kr_scr[rows, :].astype(BF16), (((1,), (1,)), ((), ())),
                            preferred_element_type=F32) * dmat
        qq = jnp.concatenate([q * qdec_f, q * qdec_b], axis=1).astype(BF16)
        o = (jnp.dot(s.astype(BF16), v_ref[0, rows, :], preferred_element_type=F32)
             + jnp.dot(qq, st_scr[c], preferred_element_type=F32))
        g = g_ref[0, rows, :].astype(F32)
        o_ref[0, rows, :] = (_rms(o) * _silu(g)).astype(o_ref.dtype)
        return carry

    lax.fori_loop(0, nc, out_body, 0, unroll=math.gcd(nc, RET_UNROLL))


def _ret_call(p, kvc, log_decay, cos2, sin2, ret_w):
    bsz, n, _ = p.shape
    n_ctx = kvc.shape[1]
    heads = ret_w // HEAD_DIM
    hd = HEAD_DIM
    chunk = _tile(n, RET_CHUNK)
    kern = functools.partial(_ret_kernel, n=n, n_ctx=n_ctx, chunk=chunk)
    col = lambda off: (lambda b, h, lg: (b, 0, off * heads + h))
    return pl.pallas_call(
        kern,
        grid_spec=pltpu.PrefetchScalarGridSpec(
            num_scalar_prefetch=1,
            grid=(bsz, heads),
            in_specs=[
                pl.BlockSpec((1, n, hd), col(0)),
                pl.BlockSpec((1, n, hd), col(1)),
                pl.BlockSpec((1, n, hd), col(2)),
                pl.BlockSpec((1, n, hd), col(3)),
                pl.BlockSpec((n, hd), lambda b, h, lg: (0, 0)),
                pl.BlockSpec((n, hd), lambda b, h, lg: (0, 0)),
                pl.BlockSpec((1, n_ctx, hd), col(0)),
                pl.BlockSpec((1, n_ctx, hd), col(1)),
            ],
            out_specs=pl.BlockSpec((1, n, hd), lambda b, h, lg: (b, 0, h)),
            scratch_shapes=[
                pltpu.VMEM((n, hd), F32),
                pltpu.VMEM((n, hd), F32),
                pltpu.VMEM((n // chunk, 2 * hd, hd), F32),
                pltpu.VMEM((n // chunk, 2 * hd, hd), BF16),
            ],
        ),
        out_shape=jax.ShapeDtypeStruct((bsz, n, ret_w), BF16),
        compiler_params=_params(("arbitrary", "arbitrary")),
        name="retention",
    )(log_decay, p, p, p, p, cos2, sin2, kvc, kvc)


def _cmul(a, b):
    return a[0] * b[0] - a[1] * b[1], a[0] * b[1] + a[1] * b[0]


def _cmul_const(a, c, s):
    eps = 1e-12
    if abs(s) < eps:
        return (a[0], a[1]) if c > 0 else (-a[0], -a[1])
    if abs(c) < eps:
        return (-a[1], a[0]) if s > 0 else (a[1], -a[0])
    return a[0] * c - a[1] * s, a[0] * s + a[1] * c


def _fft_list(xs, sign):
    size = len(xs)
    if size == 1:
        return xs
    even = _fft_list(xs[0::2], sign)
    odd = _fft_list(xs[1::2], sign)
    out = [None] * size
    for k in range(size // 2):
        ang = sign * 2.0 * math.pi * k / size
        tr, ti = _cmul_const(odd[k], math.cos(ang), math.sin(ang))
        out[k] = (even[k][0] + tr, even[k][1] + ti)
        out[k + size // 2] = (even[k][0] - tr, even[k][1] - ti)
    return out


def _fft_tables(n):
    j_cls = HY_CLASSES
    k0n, mcls = 2 * n // j_cls, n // j_cls
    t = j_cls * np.arange(mcls)[None, None, :] + np.arange(j_cls)[:, None, None]
    ang = 2.0 * math.pi * ((np.arange(k0n)[None, :, None] * t) % (2 * n)) / (2 * n)
    cs = np.concatenate([np.cos(ang), np.sin(ang)], axis=1)
    cst = np.concatenate([np.cos(ang).transpose(0, 2, 1), np.sin(ang).transpose(0, 2, 1)], axis=1)
    return jnp.asarray(cs, dtype=BF16), jnp.asarray(cst, dtype=BF16)


def _filt_kernel(z_ref, w1_ref, b1_ref, w2_ref, b2_ref, w3_ref, b3_ref, fr_ref, w4_ref, dl_ref,
                 hf_ref, hb_ref, *, n, tt, hw):
    i = pl.program_id(0)
    dot = functools.partial(jnp.dot, preferred_element_type=F32, precision=HIGHEST)
    fr = fr_ref[...]
    a = jnp.sin(fr[0:1] * (dot(z_ref[...], w1_ref[...]) + b1_ref[...]))
    a = jnp.sin(fr[1:2] * (dot(a, w2_ref[...]) + b2_ref[...]))
    a = jnp.sin(fr[2:3] * (dot(a, w3_ref[...]) + b3_ref[...]))
    hf = jnp.dot(a.astype(BF16), w4_ref[...], preferred_element_type=F32)
    mcls = n // HY_CLASSES
    rho = lax.broadcasted_iota(jnp.int32, (tt, hw), 0) + i * tt
    ti = HY_CLASSES * (rho & (mcls - 1)) + lax.shift_right_logical(rho, int(math.log2(mcls)))
    t = ti.astype(F32) * (1.0 / (n - 1.0))
    window = jnp.exp(-t * dl_ref[...])
    for o in range(HY_ORDER):
        hf_ref[:, o * hw:(o + 1) * hw] = (hf[:, (2 * o) * hw:(2 * o + 1) * hw] * window).astype(BF16)
        h_b = jnp.where(ti == 0, 0.0, hf[:, (2 * o + 1) * hw:(2 * o + 2) * hw] * window)
        hb_ref[:, o * hw:(o + 1) * hw] = h_b.astype(BF16)


def _filt_call(n, fw1, fb1, fw2, fb2, fw3, fb3, freq, fw4, hw):
    hid = fw1.shape[1]
    pk = LANE
    mcls = n // HY_CLASSES
    assert mcls & (mcls - 1) == 0
    bands = (HY_EMB_DIM - 1) // 2
    tpos = np.linspace(0.0, 1.0, n, dtype=np.float32)[:, None].astype(np.float64)
    fb = np.linspace(1e-4, bands - 1, bands, dtype=np.float32)[None, :].astype(np.float64)
    wv = (2.0 * math.pi * np.arange(n, dtype=np.float64)[:, None] / n)
    z = np.concatenate([tpos, np.cos(fb * wv), -np.sin(fb * wv)], axis=-1)
    rho = np.arange(n)
    t_of_row = HY_CLASSES * (rho % mcls) + rho // mcls
    z = np.pad(z, ((0, 0), (0, pk - HY_EMB_DIM))).astype(np.float32)[t_of_row]
    max_decay = math.log(HY_DECAY_TARGET) / HY_DECAY_SHORT_PCT
    min_decay = math.log(HY_DECAY_TARGET) / HY_DECAY_LONG_PCT
    deltas = np.abs(np.linspace(min_decay, max_decay, hw, dtype=np.float32)).reshape(1, hw)

    padc = lambda w: jnp.pad(w, ((0, 0), (0, pk - hid)))
    w1 = jnp.pad(fw1, ((0, pk - HY_EMB_DIM), (0, pk - hid)))
    w2 = jnp.pad(fw2, ((0, pk - hid), (0, pk - hid)))
    w3 = jnp.pad(fw3, ((0, pk - hid), (0, pk - hid)))
    w4 = jnp.pad(fw4, ((0, pk - hid), (0, 0))).astype(BF16)
    b1, b2, b3 = (padc(b.reshape(1, hid)) for b in (fb1, fb2, fb3))
    fr = jnp.pad(freq, ((0, 8 - freq.shape[0]), (0, pk - hid)))
    tt = _tile(n, 256)
    nw = HY_ORDER * hw
    full = lambda shape: pl.BlockSpec(shape, lambda i: (0, 0))
    return pl.pallas_call(
        functools.partial(_filt_kernel, n=n, tt=tt, hw=hw),
        grid=(n // tt,),
        in_specs=[
            pl.BlockSpec((tt, pk), lambda i: (i, 0)),
            full((pk, pk)), full((1, pk)), full((pk, pk)), full((1, pk)),
            full((pk, pk)), full((1, pk)), full((8, pk)), full((pk, 2 * nw)), full((1, hw)),
        ],
        out_specs=[pl.BlockSpec((tt, nw), lambda i: (i, 0))] * 2,
        out_shape=[jax.ShapeDtypeStruct((n, nw), BF16)] * 2,
        compiler_params=_params(("arbitrary",)),
        name="hyena_filters",
    )(jnp.asarray(z), w1, b1, w2, b2, w3, b3, fr, w4, jnp.asarray(deltas))


def _sconv_kernel(p_ref, w_ref, b_ref, o_ref, *, n):
    w = w_ref[...]
    bias = b_ref[...]
    mcls = n // HY_CLASSES
    row = lax.broadcasted_iota(jnp.int32, (mcls, p_ref.shape[2]), 0)

    def cls(j):
        return p_ref[0, j * mcls:(j + 1) * mcls, :].astype(F32)

    for j in range(HY_CLASSES):
        cur = cls(j)
        if j > 0:
            prev = cls(j - 1)
        else:
            prev = jnp.where(row == 0, 0.0, pltpu.roll(cls(HY_CLASSES - 1), 1, 0))
        if j < HY_CLASSES - 1:
            nxt = cls(j + 1)
        else:
            nxt = jnp.where(row == mcls - 1, 0.0, pltpu.roll(cls(0), mcls - 1, 0))
        out = bias + prev * w[0:1] + cur * w[1:2] + nxt * w[2:3]
        o_ref[0, j * mcls:(j + 1) * mcls, :] = out.astype(o_ref.dtype)


def _sconv_call(p, conv_w, conv_b):
    bsz, n, width = p.shape
    tc = _tile(width, 512)
    return pl.pallas_call(
        functools.partial(_sconv_kernel, n=n),
        grid=(bsz, width // tc),
        in_specs=[
            pl.BlockSpec((1, n, tc), lambda b, j: (b, 0, j)),
            pl.BlockSpec((HY_SHORT, tc), lambda b, j: (0, j)),
            pl.BlockSpec((1, tc), lambda b, j: (0, j)),
        ],
        out_specs=pl.BlockSpec((1, n, tc), lambda b, j: (b, 0, j)),
        out_shape=jax.ShapeDtypeStruct((bsz, n, width), BF16),
        compiler_params=_params(("arbitrary", "arbitrary")),
        name="short_conv",
    )(p, conv_w, conv_b.reshape(1, width))


def _spec_kernel(cs_ref, hf_ref, hb_ref, kre_ref, kim_ref, wf_scr, wb_scr,
                 *, k0n, mcls):
    j_cls = HY_CLASSES
    tn = hf_ref.shape[1]
    for j in range(j_cls):
        rows = slice(j * mcls, (j + 1) * mcls)
        prod = jnp.dot(cs_ref[j], jnp.concatenate([hf_ref[rows, :], hb_ref[rows, :]], axis=1),
                       preferred_element_type=F32)
        wf_scr[j, 0:k0n, :] = prod[0:k0n, 0:tn]
        wf_scr[j, k0n:2 * k0n, :] = -prod[k0n:2 * k0n, 0:tn]
        wb_scr[j, 0:k0n, :] = prod[0:k0n, tn:2 * tn]
        wb_scr[j, k0n:2 * k0n, :] = -prod[k0n:2 * k0n, tn:2 * tn]

    def body(c, carry):
        halves = []
        for half in range(2):
            r8 = pl.ds(pl.multiple_of(c * 16 + half * 8, 8), 8)
            i8 = pl.ds(pl.multiple_of(k0n + c * 16 + half * 8, 8), 8)
            xs = [(wf_scr[j, r8, :], wf_scr[j, i8, :]) for j in range(j_cls)]
            ys = [(wb_scr[j, r8, :], wb_scr[j, i8, :]) for j in range(j_cls)]
            halves.append((_fft_list(xs, -1), _fft_list(ys, -1)))
        for q in range(j_cls):
            rows = pl.ds(pl.multiple_of(q * k0n + c * 16, 16), 16)
            kre = jnp.concatenate([h[0][q][0] + h[1][q][0] for h in halves], axis=0)
            kim = jnp.concatenate([h[0][q][1] - h[1][q][1] for h in halves], axis=0)
            kre_ref[rows, :] = kre.astype(kre_ref.dtype)
            kim_ref[rows, :] = kim.astype(kim_ref.dtype)
        return carry

    lax.fori_loop(0, k0n // 16, body, 0)


def _spec_call(tables, hf, hb):
    cs, _ = tables
    n, width = hf.shape
    tn = LANE
    k0n, mcls = cs.shape[1] // 2, cs.shape[2]
    const = lambda shape: pl.BlockSpec(shape, lambda j: (0,) * len(shape),
                                       pipeline_mode=pl.Buffered(1))
    return pl.pallas_call(
        functools.partial(_spec_kernel, k0n=k0n, mcls=mcls),
        grid=(width // tn,),
        in_specs=[
            const(cs.shape),
            pl.BlockSpec((n, tn), lambda j: (0, j)),
            pl.BlockSpec((n, tn), lambda j: (0, j)),
        ],
        out_specs=[pl.BlockSpec((2 * n, tn), lambda j: (0, j))] * 2,
        out_shape=[jax.ShapeDtypeStruct((2 * n, width), BF16)] * 2,
        scratch_shapes=[pltpu.VMEM((HY_CLASSES, 2 * k0n, tn), F32)] * 2,
        compiler_params=_params(("arbitrary",)),
        name="filter_spectrum",
    )(cs, hf, hb)


def _fftconv_kernel(cs_ref, cst_ref, z_ref, g_ref, kre_ref, kim_ref, bias_ref,
                    o_ref, w_scr, *, k0n, mcls, scale):
    j_cls = HY_CLASSES
    tn = o_ref.shape[3]
    for j in range(j_cls):
        rows = slice(j * mcls, (j + 1) * mcls)
        prod = jnp.dot(cs_ref[j], jnp.concatenate([z_ref[0, 0, rows, :], z_ref[0, 1, rows, :]], axis=1),
                       preferred_element_type=F32)
        w_scr[j, 0:k0n, :] = prod[0:k0n, 0:tn] + prod[k0n:2 * k0n, tn:2 * tn]
        w_scr[j, k0n:2 * k0n, :] = prod[0:k0n, tn:2 * tn] - prod[k0n:2 * k0n, 0:tn]

    def body(c, carry):
        for half in range(2):
            r8 = pl.ds(pl.multiple_of(c * 16 + half * 8, 8), 8)
            i8 = pl.ds(pl.multiple_of(k0n + c * 16 + half * 8, 8), 8)
            xs = [(w_scr[j, r8, :], w_scr[j, i8, :]) for j in range(j_cls)]
            spec = _fft_list(xs, -1)
            prod = []
            for q in range(j_cls):
                rows = pl.ds(pl.multiple_of(q * k0n + c * 16, 16), 16)
                kq = (kre_ref[rows, :].astype(F32)[half * 8:(half + 1) * 8],
                      kim_ref[rows, :].astype(F32)[half * 8:(half + 1) * 8])
                prod.append(_cmul(spec[q], kq))
            back = _fft_list(prod, 1)
            for j in range(j_cls):
                w_scr[j, r8, :] = back[j][0]
                w_scr[j, i8, :] = back[j][1]
        return carry

    lax.fori_loop(0, k0n // 16, body, 0)

    bias = bias_ref[...]
    for j in range(j_cls):
        rows = slice(j * mcls, (j + 1) * mcls)
        v2 = jnp.concatenate([w_scr[j, 0:k0n, :], w_scr[j, k0n:2 * k0n, :]], axis=1).astype(BF16)
        prod = jnp.dot(cst_ref[j], v2, preferred_element_type=F32) * scale
        ys = (prod[0:mcls, 0:tn] - prod[mcls:2 * mcls, tn:2 * tn],
              prod[mcls:2 * mcls, 0:tn] + prod[0:mcls, tn:2 * tn])
        for b2 in range(2):
            conv = ys[b2] + z_ref[0, b2, rows, :].astype(F32) * bias
            o_ref[0, b2, rows, :] = (g_ref[0, b2, rows, :].astype(F32) * conv).astype(o_ref.dtype)


def _fftconv_call(tables, z, z_col0, gate, gate_col0, kre, kim, k_col0, bias):
    cs, cst = tables
    bsz, n, _ = z.shape
    hw = bias.shape[0]
    tn = LANE
    k0n, mcls = cs.shape[1] // 2, cs.shape[2]
    zo, go, ko = z_col0 // tn, gate_col0 // tn, k_col0 // tn
    z4 = z.reshape(bsz // 2, 2, n, z.shape[2])
    g4 = gate.reshape(bsz // 2, 2, n, gate.shape[2])
    const = lambda shape: pl.BlockSpec(shape, lambda j, p: (0,) * len(shape),
                                       pipeline_mode=pl.Buffered(1))
    out = pl.pallas_call(
        functools.partial(_fftconv_kernel, k0n=k0n, mcls=mcls, scale=1.0 / (2 * n)),
        grid=(hw // tn, bsz // 2),
        in_specs=[
            const(cs.shape), const(cst.shape),
            pl.BlockSpec((1, 2, n, tn), lambda j, p: (p, 0, 0, zo + j)),
            pl.BlockSpec((1, 2, n, tn), lambda j, p: (p, 0, 0, go + j)),
            pl.BlockSpec((2 * n, tn), lambda j, p: (0, ko + j)),
            pl.BlockSpec((2 * n, tn), lambda j, p: (0, ko + j)),
            pl.BlockSpec((1, tn), lambda j, p: (0, j)),
        ],
        out_specs=pl.BlockSpec((1, 2, n, tn), lambda j, p: (p, 0, 0, j)),
        out_shape=jax.ShapeDtypeStruct((bsz // 2, 2, n, hw), BF16),
        scratch_shapes=[pltpu.VMEM((HY_CLASSES, 2 * k0n, tn), F32)],
        compiler_params=_params(("arbitrary", "arbitrary")),
        name="fft_conv",
    )(cs, cst, z4, g4, kre, kim, bias.reshape(1, hw))
    return out.reshape(bsz, n, hw)


def _wout_kernel(r_ref, y_ref, wa_ref, wb_ref, x_ref, gate_ref, o_ref):
    y = (jnp.dot(r_ref[0], wa_ref[...], preferred_element_type=F32)
         + jnp.dot(y_ref[0], wb_ref[...], preferred_element_type=F32))
    o_ref[0] = x_ref[0] + gate_ref[0] * y


def _wout_call(r, yh, w_out, x, gate):
    bsz, n, d = x.shape
    rw = r.shape[2]
    assert d == 2 * rw and yh.shape[2] == rw
    tm = _tile(n, 512)
    return pl.pallas_call(
        _wout_kernel,
        grid=(bsz, n // tm),
        in_specs=[
            pl.BlockSpec((1, tm, rw), lambda b, i: (b, i, 0)),
            pl.BlockSpec((1, tm, d - rw), lambda b, i: (b, i, 0)),
            pl.BlockSpec((rw, d), lambda b, i: (0, 0)),
            pl.BlockSpec((rw, d), lambda b, i: (1, 0)),
            pl.BlockSpec((1, tm, d), lambda b, i: (b, i, 0)),
            pl.BlockSpec((1, 1, d), lambda b, i: (b, 0, 0)),
        ],
        out_specs=pl.BlockSpec((1, tm, d), lambda b, i: (b, i, 0)),
        out_shape=jax.ShapeDtypeStruct((bsz, n, d), F32),
        compiler_params=_params(("arbitrary", "arbitrary")),
        name="mixer_out",
    )(r, yh, w_out, w_out, x, gate)


def _pool_kernel(h_ref, m_ref, ic_ref, w_ref, sc_ref, x_ref, gate_ref, o_ref, *, tm, nt):
    i = pl.program_id(2)
    rows_c = pl.ds(pl.multiple_of(i * tm, tm), tm)
    rows_l = pl.ds(pl.multiple_of(jnp.maximum(i - 1, 0) * tm, tm), tm)
    rows_r = pl.ds(pl.multiple_of(jnp.minimum(i + 1, nt - 1) * tm, tm), tm)
    hc = h_ref[0, rows_c, :]
    acc = jnp.dot(m_ref[0, :, tm:2 * tm], hc, preferred_element_type=F32)
    left = jnp.dot(m_ref[0, :, 0:tm], h_ref[0, rows_l, :], preferred_element_type=F32)
    right = jnp.dot(m_ref[0, :, 2 * tm:3 * tm], h_ref[0, rows_r, :], preferred_element_type=F32)
    acc = acc + jnp.where(i > 0, left, 0.0) + jnp.where(i < nt - 1, right, 0.0)
    pooled = acc * ic_ref[0] - hc.astype(F32)
    y = jnp.dot(pooled.astype(BF16), w_ref[0], preferred_element_type=F32) * sc_ref[...]
    o_ref[0] = x_ref[0] + gate_ref[0] * y


def _pool_tables(n, tm):
    rows_per_tile = tm // GRID_W
    n_rows = n // GRID_W
    tok = np.arange(tm)
    r_out, c_out = tok // GRID_W, tok % GRID_W
    tin = np.arange(3 * tm)
    r_in, c_in = tin // GRID_W - rows_per_tile, tin % GRID_W
    mats, inv = [], []
    for win in POOL_WINDOWS:
        lo, hi = -(win // 2), win - 1 - win // 2
        dr = r_in[None, :] - r_out[:, None]
        dc = c_in[None, :] - c_out[:, None]
        mats.append(((dr >= lo) & (dr <= hi) & (dc >= lo) & (dc <= hi)).astype(np.float32))
        idx_r, idx_c = np.arange(n_rows), np.arange(GRID_W)
        cnt_r = np.minimum(idx_r + hi, n_rows - 1) + 1 - np.maximum(idx_r + lo, 0)
        cnt_c = np.minimum(idx_c + hi, GRID_W - 1) + 1 - np.maximum(idx_c + lo, 0)
        inv.append((1.0 / (cnt_r[:, None] * cnt_c[None, :]).astype(np.float64)).reshape(n, 1))
    return (jnp.asarray(np.stack(mats), dtype=BF16),
            jnp.asarray(np.stack(inv).astype(np.float32)))


def _pool_call(h, pool_w, pool_scale, x, gate):
    bsz, n, d = x.shape
    ng = len(POOL_WINDOWS)
    pg = d // ng
    tm = _tile(n, 512)
    nt = n // tm
    assert tm % GRID_W == 0 and tm // GRID_W >= max(POOL_WINDOWS) // 2
    mats, inv = _pool_tables(n, tm)
    return pl.pallas_call(
        functools.partial(_pool_kernel, tm=tm, nt=nt),
        grid=(bsz, ng, nt),
        in_specs=[
            pl.BlockSpec((1, n, pg), lambda b, g, i: (b, 0, g)),
            pl.BlockSpec((1, tm, 3 * tm), lambda b, g, i: (g, 0, 0)),
            pl.BlockSpec((1, tm, 1), lambda b, g, i: (g, i, 0)),
            pl.BlockSpec((1, pg, pg), lambda b, g, i: (g, 0, 0)),
            pl.BlockSpec((1, pg), lambda b, g, i: (0, g)),
            pl.BlockSpec((1, tm, pg), lambda b, g, i: (b, i, g)),
            pl.BlockSpec((1, 1, pg), lambda b, g, i: (b, 0, g)),
        ],
        out_specs=pl.BlockSpec((1, tm, pg), lambda b, g, i: (b, i, g)),
        out_shape=jax.ShapeDtypeStruct((bsz, n, d), F32),
        compiler_params=_params(("arbitrary", "arbitrary", "arbitrary")),
        name="pool_mixer",
    )(h, mats, inv, pool_w, pool_scale.reshape(1, d), x, gate)


def _rotary_tables(n):
    pos = np.arange(n)
    row = (pos // GRID_W).astype(np.float32)
    col = (pos % GRID_W).astype(np.float32)
    n_freq = HEAD_DIM // 4
    inv = (np.float32(ROPE_BASE) ** (-np.arange(n_freq, dtype=np.float32) / np.float32(n_freq))).astype(np.float32)
    ang = np.concatenate([row[:, None] * inv, col[:, None] * inv], axis=-1).astype(np.float32)
    cos, sin = np.cos(ang.astype(np.float64)), np.sin(ang.astype(np.float64))
    cos2 = np.concatenate([cos, cos], axis=-1).astype(np.float32)
    sin2 = np.concatenate([-sin, sin], axis=-1).astype(np.float32)
    return jnp.asarray(cos2), jnp.asarray(sin2)


def kernel(x, c, ctx, c_ctx, w_mod, b_mod, ffn1_w1, ffn1_w3, ffn1_w2, ffn2_w1, ffn2_w3, ffn2_w2, ab_w_in, ab_w_out, ret_log_decay, hy_conv_w, hy_conv_b, hy_f_w1, hy_f_b1, hy_f_w2, hy_f_b2, hy_f_w3, hy_f_b3, hy_f_freq, hy_f_w4, hy_bias, pool_w, pool_scale, final_gain):
    bsz, n, d = x.shape
    n_ctx = ctx.shape[1]
    depth = w_mod.shape[0]
    assert depth == 2, "layer 0 = retention/Hyena mixer, layer 1 = pooling mixer"
    ret_w = RET_HEADS * HEAD_DIM
    hw = d - ret_w
    assert hy_f_w4.shape[2] == HY_ORDER * 2 * hw and bsz % 2 == 0

    c_all = jnp.concatenate([c, c_ctx[None, :], jnp.zeros((8 - bsz - 1, d), F32)], axis=0)
    mod = _mod_call(c_all, w_mod, b_mod).reshape(depth, 8, N_MOD, d)

    def mod_rows(layer, base, count=3):
        return mod[layer, :bsz, base:base + count]

    def mod_ctx(base, count=3):
        return mod[0, bsz:bsz + 1, base:base + count]

    n_f = ffn1_w1.shape[2] // _tile(ffn1_w1.shape[2], 512)
    w13 = _cast_pairs_call([(ffn1_w1, ffn1_w3), (ffn2_w1, ffn2_w3)], n_f)
    w2s = _cast_call([ffn1_w2, ffn2_w2], n_f)
    ffn1_w, ffn2_w = (w13[0], w2s[0]), (w13[1], w2s[1])

    x, h = _ffn_call(x, mod_rows(0, 0), *ffn1_w, 0, modn=mod_rows(0, 3))
    _, h_c = _ffn_call(ctx.reshape(1, bsz * n_ctx, d), mod_ctx(0), *ffn1_w, 0, modn=mod_ctx(3))

    w_in, w_out = _cast_call([ab_w_in, ab_w_out], 8)
    w_in, w_out = w_in[0], w_out[0]
    p = _matmul_call(h.reshape(bsz * n, d), w_in, BF16, b_cols=(0, 4 * ret_w)).reshape(bsz, n, -1)
    kvc = _matmul_call(h_c.reshape(bsz * n_ctx, d), w_in, BF16, b_cols=(ret_w, 3 * ret_w))
    kvc = kvc.reshape(bsz, n_ctx, 2 * ret_w)

    cos2, sin2 = _rotary_tables(n)
    r = _ret_call(p, kvc, ret_log_decay[0], cos2, sin2, ret_w)

    mcls = n // HY_CLASSES
    h_cm = h.reshape(bsz, mcls, HY_CLASSES, d).transpose(0, 2, 1, 3).reshape(bsz * n, d)
    p_hy = _matmul_call(h_cm, w_in, BF16, b_cols=(4 * ret_w, w_in.shape[1])).reshape(bsz, n, -1)
    hf, hb = _filt_call(n, hy_f_w1[0], hy_f_b1[0], hy_f_w2[0], hy_f_b2[0], hy_f_w3[0],
                        hy_f_b3[0], hy_f_freq[0], hy_f_w4[0], hw)
    tables = _fft_tables(n)
    kre, kim = _spec_call(tables, hf, hb)
    u = _sconv_call(p_hy, hy_conv_w[0], hy_conv_b[0])
    z = _fftconv_call(tables, u, 0, u, hw, kre, kim, 0, hy_bias[0, 0])
    yh = _fftconv_call(tables, z, 0, u, 2 * hw, kre, kim, hw, hy_bias[0, 1])
    yh = yh.reshape(bsz, HY_CLASSES, mcls, hw).transpose(0, 2, 1, 3).reshape(bsz, n, hw)

    x = _wout_call(r, yh, w_out, x, mod_rows(0, 5, 1))
    x = _ffn_call(x, mod_rows(0, 6), *ffn2_w, 0)

    x, h = _ffn_call(x, mod_rows(1, 0), *ffn1_w, 1, modn=mod_rows(1, 3))
    x = _pool_call(h, pool_w[0].astype(BF16), pool_scale[0], x, mod_rows(1, 5, 1))
    return _ffn_call(x, mod_rows(1, 6), *ffn2_w, 1, gain=final_gain)
```

```python
import functools
import math

import numpy as np
import jax
import jax.numpy as jnp
from jax import lax
from jax.experimental import pallas as pl
from jax.experimental.pallas import tpu as pltpu

F32 = jnp.float32
BF16 = jnp.bfloat16
HIGHEST = lax.Precision.HIGHEST

EPS = 1e-6
GRID_W = 64
N_MOD = 9
RET_HEADS = 8
HEAD_DIM = 128
HY_ORDER = 2
HY_SHORT = 3
HY_EMB_DIM = 33
HY_DECAY_SHORT_PCT = 0.3
HY_DECAY_LONG_PCT = 1.5
HY_DECAY_TARGET = 1e-2
ROPE_BASE = 10000.0
POOL_WINDOWS = (2, 4, 8, 16)
RET_CHUNK = 256
RET_UNROLL = 4
HY_CLASSES = 16
LANE = 128
VMEM_LIMIT = 56 * 1024 * 1024


def _tile(dim, pref):
    if dim <= pref:
        return dim
    for t in range(pref - pref % LANE, 0, -LANE):
        if dim % t == 0:
            return t
    raise ValueError((dim, pref))


def _params(sem):
    return pltpu.CompilerParams(dimension_semantics=sem, vmem_limit_bytes=VMEM_LIMIT)


def _rms(x):
    return x * lax.rsqrt(jnp.mean(x * x, axis=-1, keepdims=True) + EPS)


def _silu(x):
    return x * jax.nn.sigmoid(x)


def _mod_kernel(c_ref, w_ref, b_ref, o_ref):
    sc = _silu(c_ref[...])
    w = w_ref[0]
    sc_hi = sc.astype(BF16)
    sc_lo = (sc - sc_hi.astype(F32)).astype(BF16)
    w_hi = w.astype(BF16)
    w_lo = (w - w_hi.astype(F32)).astype(BF16)
    rows = sc.shape[0]
    top = jnp.dot(jnp.concatenate([sc_hi, sc_lo], axis=0), w_hi, preferred_element_type=F32)
    o_ref[0] = (top[0:rows] + top[rows:2 * rows]
                + jnp.dot(sc_hi, w_lo, preferred_element_type=F32) + b_ref[0])


def _mod_call(c_all, w_mod, b_mod):
    depth, d, nd = w_mod.shape
    rows = c_all.shape[0]
    tn = _tile(nd, 1024)
    return pl.pallas_call(
        _mod_kernel,
        grid=(depth, nd // tn),
        in_specs=[
            pl.BlockSpec((rows, d), lambda l, j: (0, 0)),
            pl.BlockSpec((1, d, tn), lambda l, j: (l, 0, j)),
            pl.BlockSpec((1, 1, tn), lambda l, j: (l, 0, j)),
        ],
        out_specs=pl.BlockSpec((1, rows, tn), lambda l, j: (l, 0, j)),
        out_shape=jax.ShapeDtypeStruct((depth, rows, nd), F32),
        compiler_params=_params(("arbitrary", "arbitrary")),
        name="mod",
    )(c_all, w_mod, b_mod.reshape(depth, 1, nd))


def _ffn_kernel(*refs, n_f, emit_h, final):
    x_ref, mod_ref, w13_ref, w2_ref = refs[:4]
    pos = 4
    modn_ref = gain_ref = hn_ref = None
    if emit_h:
        modn_ref = refs[pos]
        pos += 1
    if final:
        gain_ref = refs[pos]
        pos += 1
    o_ref = refs[pos]
    pos += 1
    if emit_h:
        hn_ref = refs[pos]
        pos += 1
    h_scr, acc_scr = refs[pos], refs[pos + 1]
    f = pl.program_id(2)

    assert n_f >= 3

    tf = w2_ref.shape[0]

    def partial_out(h):
        ab = jnp.dot(h, w13_ref[...], preferred_element_type=F32)
        g = (_silu(ab[:, 0:tf]) * ab[:, tf:2 * tf]).astype(BF16)
        return jnp.dot(g, w2_ref[...], preferred_element_type=F32)

    @pl.when(f == 0)
    def _():
        m = mod_ref[0]
        h = (_rms(x_ref[0]) * (1.0 + m[1:2]) + m[0:1]).astype(BF16)
        h_scr[...] = h
        acc_scr[...] = partial_out(h)

    @pl.when(jnp.logical_and(f > 0, f < n_f - 1))
    def _():
        acc_scr[...] += partial_out(h_scr[...])

    @pl.when(f == n_f - 1)
    def _():
        m = mod_ref[0]
        xo = x_ref[0] + 0.5 * m[2:3] * (acc_scr[...] + partial_out(h_scr[...]))
        if emit_h:
            mn = modn_ref[0]
            hn_ref[0] = (_rms(xo) * (1.0 + mn[1:2]) + mn[0:1]).astype(BF16)
        if final:
            xo = _rms(xo) * gain_ref[...]
        o_ref[0] = xo


def _ffn_call(x, mod3, w13, w2, layer, modn=None, gain=None):
    bsz, s, d = x.shape
    n_f, tf = w13.shape[1], w13.shape[3] // 2
    tm = _tile(s, 512)
    emit_h = modn is not None
    final = gain is not None
    in_specs = [
        pl.BlockSpec((1, tm, d), lambda b, i, f: (b, i, 0)),
        pl.BlockSpec((1, 3, d), lambda b, i, f: (b, 0, 0)),
        pl.BlockSpec((None, None, d, 2 * tf), lambda b, i, f: (layer, f, 0, 0)),
        pl.BlockSpec((None, tf, d), lambda b, i, f: (layer, f, 0)),
    ]
    args = [x, mod3, w13, w2]
    if emit_h:
        in_specs.append(pl.BlockSpec((1, 3, d), lambda b, i, f: (b, 0, 0)))
        args.append(modn)
    if final:
        in_specs.append(pl.BlockSpec((1, d), lambda b, i, f: (0, 0)))
        args.append(gain.reshape(1, d))
    out_specs = [pl.BlockSpec((1, tm, d), lambda b, i, f: (b, i, 0))]
    out_shape = [jax.ShapeDtypeStruct((bsz, s, d), F32)]
    if emit_h:
        out_specs.append(pl.BlockSpec((1, tm, d), lambda b, i, f: (b, i, 0)))
        out_shape.append(jax.ShapeDtypeStruct((bsz, s, d), BF16))
    outs = pl.pallas_call(
        functools.partial(_ffn_kernel, n_f=n_f, emit_h=emit_h, final=final),
        grid=(bsz, s // tm, n_f),
        in_specs=in_specs,
        out_specs=out_specs,
        out_shape=out_shape,
        scratch_shapes=[pltpu.VMEM((tm, d), BF16), pltpu.VMEM((tm, d), F32)],
        compiler_params=_params(("arbitrary", "arbitrary", "arbitrary")),
        name="ffn",
    )(*args)
    return outs if emit_h else outs[0]


CAST_STREAMS = 2


def _cast_kernel(*refs):
    n_out = len(refs) // (CAST_STREAMS + 1)
    for k in range(n_out):
        dst = refs[CAST_STREAMS * n_out + k]
        cw = dst.shape[2] // CAST_STREAMS
        for part in range(CAST_STREAMS):
            dst[:, :, part * cw:(part + 1) * cw] = refs[CAST_STREAMS * k + part][...].astype(dst.dtype)


def _cast_call(arrays, steps):
    depth = arrays[0].shape[0]
    in_specs, out_specs, args = [], [], []
    for a in arrays:
        _, r, c = a.shape
        assert r % (16 * steps) == 0 and c % (CAST_STREAMS * LANE) == 0
        for part in range(CAST_STREAMS):
            in_specs.append(pl.BlockSpec((1, r // steps, c // CAST_STREAMS),
                                         lambda l, j, part=part: (l, j, part)))
            args.append(a)
        out_specs.append(pl.BlockSpec((1, r // steps, c), lambda l, j: (l, j, 0)))
    return pl.pallas_call(
        _cast_kernel,
        grid=(depth, steps),
        in_specs=in_specs,
        out_specs=out_specs,
        out_shape=[jax.ShapeDtypeStruct(a.shape, BF16) for a in arrays],
        compiler_params=_params(("arbitrary", "arbitrary")),
        name="cast_weights",
    )(*args)


def _cast_pairs_kernel(*refs):
    n_out = len(refs) // 3
    for k in range(n_out):
        wa, wb, dst = refs[2 * k], refs[2 * k + 1], refs[2 * n_out + k]
        tf = wa.shape[2]
        dst[0, 0, :, 0:tf] = wa[0].astype(dst.dtype)
        dst[0, 0, :, tf:2 * tf] = wb[0].astype(dst.dtype)


def _cast_pairs_call(pairs, steps):
    depth, r, c = pairs[0][0].shape
    tf = c // steps
    in_spec = pl.BlockSpec((1, r, tf), lambda l, j: (l, 0, j))
    out_spec = pl.BlockSpec((1, 1, r, 2 * tf), lambda l, j: (l, j, 0, 0))
    return pl.pallas_call(
        _cast_pairs_kernel,
        grid=(depth, steps),
        in_specs=[in_spec] * (2 * len(pairs)),
        out_specs=[out_spec] * len(pairs),
        out_shape=[jax.ShapeDtypeStruct((depth, steps, r, 2 * tf), BF16)] * len(pairs),
        compiler_params=_params(("arbitrary", "arbitrary")),
        name="cast_weight_pairs",
    )(*[w for pair in pairs for w in pair])


def _matmul_kernel(a_ref, b_ref, o_ref):
    o_ref[...] = jnp.dot(a_ref[...], b_ref[...], preferred_element_type=F32).astype(o_ref.dtype)


def _matmul_call(a, b, out_dtype, tm_pref=1024, tn_pref=1024, b_cols=None):
    m, k = a.shape
    c0, c1 = b_cols if b_cols is not None else (0, b.shape[1])
    n = c1 - c0
    tm = _tile(m, tm_pref)
    tn = math.gcd(_tile(n, tn_pref), c0) if c0 else _tile(n, tn_pref)
    assert tn % LANE == 0
    joff = c0 // tn
    return pl.pallas_call(
        _matmul_kernel,
        grid=(m // tm, n // tn),
        in_specs=[
            pl.BlockSpec((tm, k), lambda i, j: (i, 0)),
            pl.BlockSpec((k, tn), lambda i, j: (0, joff + j)),
        ],
        out_specs=pl.BlockSpec((tm, tn), lambda i, j: (i, j)),
        out_shape=jax.ShapeDtypeStruct((m, n), out_dtype),
        compiler_params=_params(("arbitrary", "arbitrary")),
        name="matmul",
    )(a, b)


def _ret_kernel(lg_ref, q_ref, k_ref, v_ref, g_ref, cos_ref, sin_ref, kc_ref, vc_ref, o_ref,
                qr_scr, kr_scr, upd_scr, st_scr, *, n, n_ctx, chunk):
    hd = HEAD_DIM
    head = pl.program_id(1)
    lgf = lg_ref[0, head]
    lgb = lg_ref[1, head]
    nc = n // chunk
    k_scale = hd ** -0.5

    posc = lax.broadcasted_iota(jnp.int32, (n_ctx, hd), 0).astype(F32)
    kc = kc_ref[0].astype(F32) * k_scale
    vc = vc_ref[0]
    kcf = (kc * jnp.exp((n_ctx - 1.0 - posc) * lgf)).T.astype(BF16)
    kcb = (kc * jnp.exp(posc * lgb)).T.astype(BF16)
    s0_f = jnp.dot(kcf, vc, preferred_element_type=F32)
    s0_b = jnp.dot(kcb, vc, preferred_element_type=F32)

    pos = lax.broadcasted_iota(jnp.int32, (chunk, hd), 0).astype(F32)
    qdec_f = jnp.exp((pos + 1.0) * lgf)
    kdec_f = jnp.exp((chunk - 1.0 - pos) * lgf)
    qdec_b = jnp.exp((chunk - pos) * lgb)
    kdec_b = jnp.exp(pos * lgb)
    cdec_f = jnp.exp(jnp.full((1, hd), chunk, F32) * lgf)
    cdec_b = jnp.exp(jnp.full((1, hd), chunk, F32) * lgb)
    ri = lax.broadcasted_iota(jnp.int32, (chunk, chunk), 0)
    ci = lax.broadcasted_iota(jnp.int32, (chunk, chunk), 1)
    diff = (ri - ci).astype(F32)
    dmat = (jnp.where(diff >= 0, jnp.exp(jnp.maximum(diff, 0.0) * lgf), 0.0)
            + jnp.where(diff <= 0, jnp.exp(jnp.maximum(-diff, 0.0) * lgb), 0.0))

    def rotary(x, rows):
        return x * cos_ref[rows, :] + pltpu.roll(x, hd // 2, 1) * sin_ref[rows, :]

    def chunk_rows(c):
        return pl.ds(pl.multiple_of(c * chunk, chunk), chunk)

    def prep_body(c, carry):
        rows = chunk_rows(c)
        qr_scr[rows, :] = rotary(q_ref[0, rows, :].astype(F32), rows)
        k = rotary(k_ref[0, rows, :].astype(F32), rows) * k_scale
        kr_scr[rows, :] = k
        kk = jnp.concatenate([k * kdec_f, k * kdec_b], axis=1).T.astype(BF16)
        upd_scr[c] = jnp.dot(kk, v_ref[0, rows, :], preferred_element_type=F32)
        return carry

    lax.fori_loop(0, nc, prep_body, 0, unroll=math.gcd(nc, RET_UNROLL))

    def fwd_scan(c, sf):
        st_scr[c, 0:hd, :] = sf.astype(BF16)
        return cdec_f * sf + upd_scr[c, 0:hd, :]

    def bwd_scan(j, sb):
        c = nc - 1 - j
        st_scr[c, hd:2 * hd, :] = sb.astype(BF16)
        return cdec_b * sb + upd_scr[c, hd:2 * hd, :]

    lax.fori_loop(0, nc, fwd_scan, s0_f, unroll=math.gcd(nc, RET_UNROLL))
    lax.fori_loop(0, nc, bwd_scan, s0_b, unroll=math.gcd(nc, RET_UNROLL))

    def out_body(c, carry):
        rows = chunk_rows(c)
        q = qr_scr[rows, :]
        s = lax.dot_general(q.astype(BF16), kr_scr[rows, :].astype(BF16), (((1,), (1,)), ((), ())),
                            preferred_element_type=F32) * dmat
        qq = jnp.concatenate([q * qdec_f, q * qdec_b], axis=1).astype(BF16)
        o = (jnp.dot(s.astype(BF16), v_ref[0, rows, :], preferred_element_type=F32)
             + jnp.dot(qq, st_scr[c], preferred_element_type=F32))
        g = g_ref[0, rows, :].astype(F32)
        o_ref[0, rows, :] = (_rms(o) * _silu(g)).astype(o_ref.dtype)
        return carry

    lax.fori_loop(0, nc, out_body, 0, unroll=math.gcd(nc, RET_UNROLL))


def _ret_call(p, kvc, log_decay, cos2, sin2, ret_w):
    bsz, n, _ = p.shape
    n_ctx = kvc.shape[1]
    heads = ret_w // HEAD_DIM
    hd = HEAD_DIM
    chunk = _tile(n, RET_CHUNK)
    kern = functools.partial(_ret_kernel, n=n, n_ctx=n_ctx, chunk=chunk)
    col = lambda off: (lambda b, h, lg: (b, 0, off * heads + h))
    return pl.pallas_call(
        kern,
        grid_spec=pltpu.PrefetchScalarGridSpec(
            num_scalar_prefetch=1,
            grid=(bsz, heads),
            in_specs=[
                pl.BlockSpec((1, n, hd), col(0)),
                pl.BlockSpec((1, n, hd), col(1)),
                pl.BlockSpec((1, n, hd), col(2)),
                pl.BlockSpec((1, n, hd), col(3)),
                pl.BlockSpec((n, hd), lambda b, h, lg: (0, 0)),
                pl.BlockSpec((n, hd), lambda b, h, lg: (0, 0)),
                pl.BlockSpec((1, n_ctx, hd), col(0)),
                pl.BlockSpec((1, n_ctx, hd), col(1)),
            ],
            out_specs=pl.BlockSpec((1, n, hd), lambda b, h, lg: (b, 0, h)),
            scratch_shapes=[
                pltpu.VMEM((n, hd), F32),
                pltpu.VMEM((n, hd), F32),
                pltpu.VMEM((n // chunk, 2 * hd, hd), F32),
                pltpu.VMEM((n // chunk, 2 * hd, hd), BF16),
            ],
        ),
        out_shape=jax.ShapeDtypeStruct((bsz, n, ret_w), BF16),
        compiler_params=_params(("arbitrary", "arbitrary")),
        name="retention",
    )(log_decay, p, p, p, p, cos2, sin2, kvc, kvc)


def _cmul(a, b):
    return a[0] * b[0] - a[1] * b[1], a[0] * b[1] + a[1] * b[0]


def _cmul_const(a, c, s):
    eps = 1e-12
    if abs(s) < eps:
        return (a[0], a[1]) if c > 0 else (-a[0], -a[1])
    if abs(c) < eps:
        return (-a[1], a[0]) if s > 0 else (a[1], -a[0])
    return a[0] * c - a[1] * s, a[0] * s + a[1] * c


def _fft_list(xs, sign):
    size = len(xs)
    if size == 1:
        return xs
    even = _fft_list(xs[0::2], sign)
    odd = _fft_list(xs[1::2], sign)
    out = [None] * size
    for k in range(size // 2):
        ang = sign * 2.0 * math.pi * k / size
        tr, ti = _cmul_const(odd[k], math.cos(ang), math.sin(ang))
        out[k] = (even[k][0] + tr, even[k][1] + ti)
        out[k + size // 2] = (even[k][0] - tr, even[k][1] - ti)
    return out


def _fft_tables(n):
    j_cls = HY_CLASSES
    k0n, mcls = 2 * n // j_cls, n // j_cls
    t = j_cls * np.arange(mcls)[None, None, :] + np.arange(j_cls)[:, None, None]
    ang = 2.0 * math.pi * ((np.arange(k0n)[None, :, None] * t) % (2 * n)) / (2 * n)
    cs = np.concatenate([np.cos(ang), np.sin(ang)], axis=1)
    cst = np.concatenate([np.cos(ang).transpose(0, 2, 1), np.sin(ang).transpose(0, 2, 1)], axis=1)
    return jnp.asarray(cs, dtype=BF16), jnp.asarray(cst, dtype=BF16)


def _filt_kernel(z_ref, w1_ref, b1_ref, w2_ref, b2_ref, w3_ref, b3_ref, fr_ref, w4_ref, dl_ref,
                 hf_ref, hb_ref, *, n, tt, hw):
    i = pl.program_id(0)
    dot = functools.partial(jnp.dot, preferred_element_type=F32, precision=HIGHEST)
    fr = fr_ref[...]
    a = jnp.sin(fr[0:1] * (dot(z_ref[...], w1_ref[...]) + b1_ref[...]))
    a = jnp.sin(fr[1:2] * (dot(a, w2_ref[...]) + b2_ref[...]))
    a = jnp.sin(fr[2:3] * (dot(a, w3_ref[...]) + b3_ref[...]))
    hf = jnp.dot(a.astype(BF16), w4_ref[...], preferred_element_type=F32)
    mcls = n // HY_CLASSES
    rho = lax.broadcasted_iota(jnp.int32, (tt, hw), 0) + i * tt
    ti = HY_CLASSES * (rho & (mcls - 1)) + lax.shift_right_logical(rho, int(math.log2(mcls)))
    t = ti.astype(F32) * (1.0 / (n - 1.0))
    window = jnp.exp(-t * dl_ref[...])
    for o in range(HY_ORDER):
        hf_ref[:, o * hw:(o + 1) * hw] = (hf[:, (2 * o) * hw:(2 * o + 1) * hw] * window).astype(BF16)
        h_b = jnp.where(ti == 0, 0.0, hf[:, (2 * o + 1) * hw:(2 * o + 2) * hw] * window)
        hb_ref[:, o * hw:(o + 1) * hw] = h_b.astype(BF16)


def _filt_call(n, fw1, fb1, fw2, fb2, fw3, fb3, freq, fw4, hw):
    hid = fw1.shape[1]
    pk = LANE
    mcls = n // HY_CLASSES
    assert mcls & (mcls - 1) == 0
    bands = (HY_EMB_DIM - 1) // 2
    tpos = np.linspace(0.0, 1.0, n, dtype=np.float32)[:, None].astype(np.float64)
    fb = np.linspace(1e-4, bands - 1, bands, dtype=np.float32)[None, :].astype(np.float64)
    wv = (2.0 * math.pi * np.arange(n, dtype=np.float64)[:, None] / n)
    z = np.concatenate([tpos, np.cos(fb * wv), -np.sin(fb * wv)], axis=-1)
    rho = np.arange(n)
    t_of_row = HY_CLASSES * (rho % mcls) + rho // mcls
    z = np.pad(z, ((0, 0), (0, pk - HY_EMB_DIM))).astype(np.float32)[t_of_row]
    max_decay = math.log(HY_DECAY_TARGET) / HY_DECAY_SHORT_PCT
    min_decay = math.log(HY_DECAY_TARGET) / HY_DECAY_LONG_PCT
    deltas = np.abs(np.linspace(min_decay, max_decay, hw, dtype=np.float32)).reshape(1, hw)

    padc = lambda w: jnp.pad(w, ((0, 0), (0, pk - hid)))
    w1 = jnp.pad(fw1, ((0, pk - HY_EMB_DIM), (0, pk - hid)))
    w2 = jnp.pad(fw2, ((0, pk - hid), (0, pk - hid)))
    w3 = jnp.pad(fw3, ((0, pk - hid), (0, pk - hid)))
    w4 = jnp.pad(fw4, ((0, pk - hid), (0, 0))).astype(BF16)
    b1, b2, b3 = (padc(b.reshape(1, hid)) for b in (fb1, fb2, fb3))
    fr = jnp.pad(freq, ((0, 8 - freq.shape[0]), (0, pk - hid)))
    tt = _tile(n, 256)
    nw = HY_ORDER * hw
    full = lambda shape: pl.BlockSpec(shape, lambda i: (0, 0))
    return pl.pallas_call(
        functools.partial(_filt_kernel, n=n, tt=tt, hw=hw),
        grid=(n // tt,),
        in_specs=[
            pl.BlockSpec((tt, pk), lambda i: (i, 0)),
            full((pk, pk)), full((1, pk)), full((pk, pk)), full((1, pk)),
            full((pk, pk)), full((1, pk)), full((8, pk)), full((pk, 2 * nw)), full((1, hw)),
        ],
        out_specs=[pl.BlockSpec((tt, nw), lambda i: (i, 0))] * 2,
        out_shape=[jax.ShapeDtypeStruct((n, nw), BF16)] * 2,
        compiler_params=_params(("arbitrary",)),
        name="hyena_filters",
    )(jnp.asarray(z), w1, b1, w2, b2, w3, b3, fr, w4, jnp.asarray(deltas))


def _sconv_kernel(p_ref, w_ref, b_ref, o_ref, *, n):
    w = w_ref[...]
    bias = b_ref[...]
    mcls = n // HY_CLASSES
    row = lax.broadcasted_iota(jnp.int32, (mcls, p_ref.shape[2]), 0)

    def cls(j):
        return p_ref[0, j * mcls:(j + 1) * mcls, :].astype(F32)

    for j in range(HY_CLASSES):
        cur = cls(j)
        if j > 0:
            prev = cls(j - 1)
        else:
            prev = jnp.where(row == 0, 0.0, pltpu.roll(cls(HY_CLASSES - 1), 1, 0))
        if j < HY_CLASSES - 1:
            nxt = cls(j + 1)
        else:
            nxt = jnp.where(row == mcls - 1, 0.0, pltpu.roll(cls(0), mcls - 1, 0))
        out = bias + prev * w[0:1] + cur * w[1:2] + nxt * w[2:3]
        o_ref[0, j * mcls:(j + 1) * mcls, :] = out.astype(o_ref.dtype)


def _sconv_call(p, conv_w, conv_b):
    bsz, n, width = p.shape
    tc = _tile(width, 512)
    return pl.pallas_call(
        functools.partial(_sconv_kernel, n=n),
        grid=(bsz, width // tc),
        in_specs=[
            pl.BlockSpec((1, n, tc), lambda b, j: (b, 0, j)),
            pl.BlockSpec((HY_SHORT, tc), lambda b, j: (0, j)),
            pl.BlockSpec((1, tc), lambda b, j: (0, j)),
        ],
        out_specs=pl.BlockSpec((1, n, tc), lambda b, j: (b, 0, j)),
        out_shape=jax.ShapeDtypeStruct((bsz, n, width), BF16),
        compiler_params=_params(("arbitrary", "arbitrary")),
        name="short_conv",
    )(p, conv_w, conv_b.reshape(1, width))


def _spec_kernel(cs_ref, hf_ref, hb_ref, kre_ref, kim_ref, wf_scr, wb_scr,
                 *, k0n, mcls):
    j_cls = HY_CLASSES
    tn = hf_ref.shape[1]
    for j in range(j_cls):
        rows = slice(j * mcls, (j + 1) * mcls)
        prod = jnp.dot(cs_ref[j], jnp.concatenate([hf_ref[rows, :], hb_ref[rows, :]], axis=1),
                       preferred_element_type=F32)
        wf_scr[j, 0:k0n, :] = prod[0:k0n, 0:tn]
        wf_scr[j, k0n:2 * k0n, :] = -prod[k0n:2 * k0n, 0:tn]
        wb_scr[j, 0:k0n, :] = prod[0:k0n, tn:2 * tn]
        wb_scr[j, k0n:2 * k0n, :] = -prod[k0n:2 * k0n, tn:2 * tn]

    def body(c, carry):
        halves = []
        for half in range(2):
            r8 = pl.ds(pl.multiple_of(c * 16 + half * 8, 8), 8)
            i8 = pl.ds(pl.multiple_of(k0n + c * 16 + half * 8, 8), 8)
            xs = [(wf_scr[j, r8, :], wf_scr[j, i8, :]) for j in range(j_cls)]
            ys = [(wb_scr[j, r8, :], wb_scr[j, i8, :]) for j in range(j_cls)]
            halves.append((_fft_list(xs, -1), _fft_list(ys, -1)))
        for q in range(j_cls):
            rows = pl.ds(pl.multiple_of(q * k0n + c * 16, 16), 16)
            kre = jnp.concatenate([h[0][q][0] + h[1][q][0] for h in halves], axis=0)
            kim = jnp.concatenate([h[0][q][1] - h[1][q][1] for h in halves], axis=0)
            kre_ref[rows, :] = kre.astype(kre_ref.dtype)
            kim_ref[rows, :] = kim.astype(kim_ref.dtype)
        return carry

    lax.fori_loop(0, k0n // 16, body, 0)


def _spec_call(tables, hf, hb):
    cs, _ = tables
    n, width = hf.shape
    tn = LANE
    k0n, mcls = cs.shape[1] // 2, cs.shape[2]
    const = lambda shape: pl.BlockSpec(shape, lambda j: (0,) * len(shape),
                                       pipeline_mode=pl.Buffered(1))
    return pl.pallas_call(
        functools.partial(_spec_kernel, k0n=k0n, mcls=mcls),
        grid=(width // tn,),
        in_specs=[
            const(cs.shape),
            pl.BlockSpec((n, tn), lambda j: (0, j)),
            pl.BlockSpec((n, tn), lambda j: (0, j)),
        ],
        out_specs=[pl.BlockSpec((2 * n, tn), lambda j: (0, j))] * 2,
        out_shape=[jax.ShapeDtypeStruct((2 * n, width), BF16)] * 2,
        scratch_shapes=[pltpu.VMEM((HY_CLASSES, 2 * k0n, tn), F32)] * 2,
        compiler_params=_params(("arbitrary",)),
        name="filter_spectrum",
    )(cs, hf, hb)


def _fftconv_kernel(cs_ref, cst_ref, z_ref, g_ref, kre_ref, kim_ref, bias_ref,
                    o_ref, w_scr, *, k0n, mcls, scale):
    j_cls = HY_CLASSES
    tn = o_ref.shape[3]
    for j in range(j_cls):
        rows = slice(j * mcls, (j + 1) * mcls)
        prod = jnp.dot(cs_ref[j], jnp.concatenate([z_ref[0, 0, rows, :], z_ref[0, 1, rows, :]], axis=1),
                       preferred_element_type=F32)
        w_scr[j, 0:k0n, :] = prod[0:k0n, 0:tn] + prod[k0n:2 * k0n, tn:2 * tn]
        w_scr[j, k0n:2 * k0n, :] = prod[0:k0n, tn:2 * tn] - prod[k0n:2 * k0n, 0:tn]

    def body(c, carry):
        for half in range(2):
            r8 = pl.ds(pl.multiple_of(c * 16 + half * 8, 8), 8)
            i8 = pl.ds(pl.multiple_of(k0n + c * 16 + half * 8, 8), 8)
            xs = [(w_scr[j, r8, :], w_scr[j, i8, :]) for j in range(j_cls)]
            spec = _fft_list(xs, -1)
            prod = []
            for q in range(j_cls):
                rows = pl.ds(pl.multiple_of(q * k0n + c * 16, 16), 16)
                kq = (kre_ref[rows, :].astype(F32)[half * 8:(half + 1) * 8],
                      kim_ref[rows, :].astype(F32)[half * 8:(half + 1) * 8])
                prod.append(_cmul(spec[q], kq))
            back = _fft_list(prod, 1)
            for j in range(j_cls):
                w_scr[j, r8, :] = back[j][0]
                w_scr[j, i8, :] = back[j][1]
        return carry

    lax.fori_loop(0, k0n // 16, body, 0)

    bias = bias_ref[...]
    for j in range(j_cls):
        rows = slice(j * mcls, (j + 1) * mcls)
        v2 = jnp.concatenate([w_scr[j, 0:k0n, :], w_scr[j, k0n:2 * k0n, :]], axis=1).astype(BF16)
        prod = jnp.dot(cst_ref[j], v2, preferred_element_type=F32) * scale
        ys = (prod[0:mcls, 0:tn] - prod[mcls:2 * mcls, tn:2 * tn],
              prod[mcls:2 * mcls, 0:tn] + prod[0:mcls, tn:2 * tn])
        for b2 in range(2):
            conv = ys[b2] + z_ref[0, b2, rows, :].astype(F32) * bias
            o_ref[0, b2, rows, :] = (g_ref[0, b2, rows, :].astype(F32) * conv).astype(o_ref.dtype)


def _fftconv_call(tables, z, z_col0, gate, gate_col0, kre, kim, k_col0, bias):
    cs, cst = tables
    bsz, n, _ = z.shape
    hw = bias.shape[0]
    tn = LANE
    k0n, mcls = cs.shape[1] // 2, cs.shape[2]
    zo, go, ko = z_col0 // tn, gate_col0 // tn, k_col0 // tn
    z4 = z.reshape(bsz // 2, 2, n, z.shape[2])
    g4 = gate.reshape(bsz // 2, 2, n, gate.shape[2])
    const = lambda shape: pl.BlockSpec(shape, lambda j, p: (0,) * len(shape),
                                       pipeline_mode=pl.Buffered(1))
    out = pl.pallas_call(
        functools.partial(_fftconv_kernel, k0n=k0n, mcls=mcls, scale=1.0 / (2 * n)),
        grid=(hw // tn, bsz // 2),
        in_specs=[
            const(cs.shape), const(cst.shape),
            pl.BlockSpec((1, 2, n, tn), lambda j, p: (p, 0, 0, zo + j)),
            pl.BlockSpec((1, 2, n, tn), lambda j, p: (p, 0, 0, go + j)),
            pl.BlockSpec((2 * n, tn), lambda j, p: (0, ko + j)),
            pl.BlockSpec((2 * n, tn), lambda j, p: (0, ko + j)),
            pl.BlockSpec((1, tn), lambda j, p: (0, j)),
        ],
        out_specs=pl.BlockSpec((1, 2, n, tn), lambda j, p: (p, 0, 0, j)),
        out_shape=jax.ShapeDtypeStruct((bsz // 2, 2, n, hw), BF16),
        scratch_shapes=[pltpu.VMEM((HY_CLASSES, 2 * k0n, tn), F32)],
        compiler_params=_params(("arbitrary", "arbitrary")),
        name="fft_conv",
    )(cs, cst, z4, g4, kre, kim, bias.reshape(1, hw))
    return out.reshape(bsz, n, hw)


def _wout_kernel(r_ref, y_ref, wa_ref, wb_ref, x_ref, gate_ref, o_ref):
    y = (jnp.dot(r_ref[0], wa_ref[...], preferred_element_type=F32)
         + jnp.dot(y_ref[0], wb_ref[...], preferred_element_type=F32))
    o_ref[0] = x_ref[0] + gate_ref[0] * y


def _wout_call(r, yh, w_out, x, gate):
    bsz, n, d = x.shape
    rw = r.shape[2]
    assert d == 2 * rw and yh.shape[2] == rw
    tm = _tile(n, 512)
    return pl.pallas_call(
        _wout_kernel,
        grid=(bsz, n // tm),
        in_specs=[
            pl.BlockSpec((1, tm, rw), lambda b, i: (b, i, 0)),
            pl.BlockSpec((1, tm, d - rw), lambda b, i: (b, i, 0)),
            pl.BlockSpec((rw, d), lambda b, i: (0, 0)),
            pl.BlockSpec((rw, d), lambda b, i: (1, 0)),
            pl.BlockSpec((1, tm, d), lambda b, i: (b, i, 0)),
            pl.BlockSpec((1, 1, d), lambda b, i: (b, 0, 0)),
        ],
        out_specs=pl.BlockSpec((1, tm, d), lambda b, i: (b, i, 0)),
        out_shape=jax.ShapeDtypeStruct((bsz, n, d), F32),
        compiler_params=_params(("arbitrary", "arbitrary")),
        name="mixer_out",
    )(r, yh, w_out, w_out, x, gate)


def _pool_kernel(h_ref, mc_ref, icol_ref, w_ref, sc_ref, x_ref, gate_ref, o_ref, cp_scr, pm_scr,
                 *, n, tm, to):
    g = pl.program_id(1)
    i = pl.program_id(2)
    n_rows = n // GRID_W

    @pl.when(i == 0)
    def _():
        for t in range(n // tm):
            rows = slice(t * tm, (t + 1) * tm)
            cp_scr[rows, :] = jnp.dot(mc_ref[0], h_ref[0, rows, :], preferred_element_type=F32)

    def grid_row(r):
        return slice(r * GRID_W, (r + 1) * GRID_W)

    for gi, win in enumerate(POOL_WINDOWS):
        @pl.when(jnp.logical_and(i == 0, g == gi))
        def _(win=win):
            lo, hi = win // 2, win - 1 - win // 2
            icol = icol_ref[0]
            acc = cp_scr[grid_row(0), :]
            for r in range(1, hi + 1):
                acc = acc + cp_scr[grid_row(r), :]
            for r in range(n_rows):
                if r > 0 and r + hi < n_rows:
                    acc = acc + cp_scr[grid_row(r + hi), :]
                if r - lo - 1 >= 0:
                    acc = acc - cp_scr[grid_row(r - lo - 1), :]
                cnt_r = min(r + hi, n_rows - 1) + 1 - max(r - lo, 0)
                mean = acc * (icol * (1.0 / cnt_r))
                pm_scr[grid_row(r), :] = (mean - h_ref[0, grid_row(r), :].astype(F32)).astype(BF16)

    rows = pl.ds(pl.multiple_of(i * to, to), to)
    y = jnp.dot(pm_scr[rows, :], w_ref[0], preferred_element_type=F32) * sc_ref[...]
    o_ref[0] = x_ref[0] + gate_ref[0] * y


def _pool_tables(tm, pg):
    tok = np.arange(tm)
    r_t, c_t = tok // GRID_W, tok % GRID_W
    mats, inv = [], []
    for win in POOL_WINDOWS:
        lo, hi = -(win // 2), win - 1 - win // 2
        dc = c_t[None, :] - c_t[:, None]
        mats.append(((r_t[None, :] == r_t[:, None]) & (dc >= lo) & (dc <= hi)).astype(np.float32))
        idx_c = np.arange(GRID_W)
        cnt_c = np.minimum(idx_c + hi, GRID_W - 1) + 1 - np.maximum(idx_c + lo, 0)
        inv.append(np.broadcast_to((1.0 / cnt_c)[:, None], (GRID_W, pg)))
    return (jnp.asarray(np.stack(mats), dtype=BF16),
            jnp.asarray(np.stack(inv).astype(np.float32)))


def _pool_call(h, pool_w, pool_scale, x, gate):
    bsz, n, d = x.shape
    ng = len(POOL_WINDOWS)
    pg = d // ng
    tm = _tile(n, 512)
    to = _tile(n, 1024)
    assert tm % GRID_W == 0 and n % GRID_W == 0
    mats, icol = _pool_tables(tm, pg)
    return pl.pallas_call(
        functools.partial(_pool_kernel, n=n, tm=tm, to=to),
        grid=(bsz, ng, n // to),
        in_specs=[
            pl.BlockSpec((1, n, pg), lambda b, g, i: (b, 0, g)),
            pl.BlockSpec((1, tm, tm), lambda b, g, i: (g, 0, 0)),
            pl.BlockSpec((1, GRID_W, pg), lambda b, g, i: (g, 0, 0)),
            pl.BlockSpec((1, pg, pg), lambda b, g, i: (g, 0, 0)),
            pl.BlockSpec((1, pg), lambda b, g, i: (0, g)),
            pl.BlockSpec((1, to, pg), lambda b, g, i: (b, i, g)),
            pl.BlockSpec((1, 1, pg), lambda b, g, i: (b, 0, g)),
        ],
        out_specs=pl.BlockSpec((1, to, pg), lambda b, g, i: (b, i, g)),
        out_shape=jax.ShapeDtypeStruct((bsz, n, d), F32),
        scratch_shapes=[pltpu.VMEM((n, pg), F32), pltpu.VMEM((n, pg), BF16)],
        compiler_params=_params(("arbitrary", "arbitrary", "arbitrary")),
        name="pool_mixer",
    )(h, mats, icol, pool_w, pool_scale.reshape(1, d), x, gate)


def _rotary_tables(n):
    pos = np.arange(n)
    row = (pos // GRID_W).astype(np.float32)
    col = (pos % GRID_W).astype(np.float32)
    n_freq = HEAD_DIM // 4
    inv = (np.float32(ROPE_BASE) ** (-np.arange(n_freq, dtype=np.float32) / np.float32(n_freq))).astype(np.float32)
    ang = np.concatenate([row[:, None] * inv, col[:, None] * inv], axis=-1).astype(np.float32)
    cos, sin = np.cos(ang.astype(np.float64)), np.sin(ang.astype(np.float64))
    cos2 = np.concatenate([cos, cos], axis=-1).astype(np.float32)
    sin2 = np.concatenate([-sin, sin], axis=-1).astype(np.float32)
    return jnp.asarray(cos2), jnp.asarray(sin2)


def kernel(x, c, ctx, c_ctx, w_mod, b_mod, ffn1_w1, ffn1_w3, ffn1_w2, ffn2_w1, ffn2_w3, ffn2_w2, ab_w_in, ab_w_out, ret_log_decay, hy_conv_w, hy_conv_b, hy_f_w1, hy_f_b1, hy_f_w2, hy_f_b2, hy_f_w3, hy_f_b3, hy_f_freq, hy_f_w4, hy_bias, pool_w, pool_scale, final_gain):
    bsz, n, d = x.shape
    n_ctx = ctx.shape[1]
    depth = w_mod.shape[0]
    assert depth == 2, "layer 0 = retention/Hyena mixer, layer 1 = pooling mixer"
    ret_w = RET_HEADS * HEAD_DIM
    hw = d - ret_w
    assert hy_f_w4.shape[2] == HY_ORDER * 2 * hw and bsz % 2 == 0

    c_all = jnp.concatenate([c, c_ctx[None, :], jnp.zeros((8 - bsz - 1, d), F32)], axis=0)
    mod = _mod_call(c_all, w_mod, b_mod).reshape(depth, 8, N_MOD, d)

    def mod_rows(layer, base, count=3):
        return mod[layer, :bsz, base:base + count]

    def mod_ctx(base, count=3):
        return mod[0, bsz:bsz + 1, base:base + count]

    n_f = ffn1_w1.shape[2] // _tile(ffn1_w1.shape[2], 512)
    w13 = _cast_pairs_call([(ffn1_w1, ffn1_w3), (ffn2_w1, ffn2_w3)], n_f)
    w2s = _cast_call([ffn1_w2, ffn2_w2], 2 * n_f)
    ffn1_w, ffn2_w = (w13[0], w2s[0]), (w13[1], w2s[1])

    x, h = _ffn_call(x, mod_rows(0, 0), *ffn1_w, 0, modn=mod_rows(0, 3))
    _, h_c = _ffn_call(ctx.reshape(1, bsz * n_ctx, d), mod_ctx(0), *ffn1_w, 0, modn=mod_ctx(3))

    w_in, w_out = _cast_call([ab_w_in, ab_w_out], 16)
    w_in, w_out = w_in[0], w_out[0]
    p = _matmul_call(h.reshape(bsz * n, d), w_in, BF16, b_cols=(0, 4 * ret_w)).reshape(bsz, n, -1)
    kvc = _matmul_call(h_c.reshape(bsz * n_ctx, d), w_in, BF16, b_cols=(ret_w, 3 * ret_w))
    kvc = kvc.reshape(bsz, n_ctx, 2 * ret_w)

    cos2, sin2 = _rotary_tables(n)
    r = _ret_call(p, kvc, ret_log_decay[0], cos2, sin2, ret_w)

    mcls = n // HY_CLASSES
    h_cm = h.reshape(bsz, mcls, HY_CLASSES, d).transpose(0, 2, 1, 3).reshape(bsz * n, d)
    p_hy = _matmul_call(h_cm, w_in, BF16, b_cols=(4 * ret_w, w_in.shape[1])).reshape(bsz, n, -1)
    hf, hb = _filt_call(n, hy_f_w1[0], hy_f_b1[0], hy_f_w2[0], hy_f_b2[0], hy_f_w3[0],
                        hy_f_b3[0], hy_f_freq[0], hy_f_w4[0], hw)
    tables = _fft_tables(n)
    kre, kim = _spec_call(tables, hf, hb)
    u = _sconv_call(p_hy, hy_conv_w[0], hy_conv_b[0])
    z = _fftconv_call(tables, u, 0, u, hw, kre, kim, 0, hy_bias[0, 0])
    yh = _fftconv_call(tables, z, 0, u, 2 * hw, kre, kim, hw, hy_bias[0, 1])
    yh = yh.reshape(bsz, HY_CLASSES, mcls, hw).transpose(0, 2, 1, 3).reshape(bsz, n, hw)

    x = _wout_call(r, yh, w_out, x, mod_rows(0, 5, 1))
    x = _ffn_call(x, mod_rows(0, 6), *ffn2_w, 0)

    x, h = _ffn_call(x, mod_rows(1, 0), *ffn1_w, 1, modn=mod_rows(1, 3))
    x = _pool_call(h, pool_w[0].astype(BF16), pool_scale[0], x, mod_rows(1, 5, 1))
    return _ffn_call(x, mod_rows(1, 6), *ffn2_w, 1, gain=final_gain)
```

```python
import functools
import math

import numpy as np
import jax
import jax.numpy as jnp
from jax import lax
from jax.experimental import pallas as pl
from jax.experimental.pallas import tpu as pltpu

F32 = jnp.float32
BF16 = jnp.bfloat16
HIGHEST = lax.Precision.HIGHEST

EPS = 1e-6
GRID_W = 64
N_MOD = 9
RET_HEADS = 8
HEAD_DIM = 128
HY_ORDER = 2
HY_SHORT = 3
HY_EMB_DIM = 33
HY_DECAY_SHORT_PCT = 0.3
HY_DECAY_LONG_PCT = 1.5
HY_DECAY_TARGET = 1e-2
ROPE_BASE = 10000.0
POOL_WINDOWS = (2, 4, 8, 16)
RET_CHUNK = 256
RET_UNROLL = 16
HY_CLASSES = 16
LANE = 128
VMEM_LIMIT = 56 * 1024 * 1024


def _tile(dim, pref):
    if dim <= pref:
        return dim
    for t in range(pref - pref % LANE, 0, -LANE):
        if dim % t == 0:
            return t
    raise ValueError((dim, pref))


def _params(sem):
    return pltpu.CompilerParams(dimension_semantics=sem, vmem_limit_bytes=VMEM_LIMIT)


def _rms(x):
    return x * lax.rsqrt(jnp.mean(x * x, axis=-1, keepdims=True) + EPS)


def _silu(x):
    return x * jax.nn.sigmoid(x)


def _mod_kernel(c_ref, w_ref, b_ref, o_ref):
    sc = _silu(c_ref[...])
    w = w_ref[0]
    sc_hi = sc.astype(BF16)
    sc_lo = (sc - sc_hi.astype(F32)).astype(BF16)
    w_hi = w.astype(BF16)
    w_lo = (w - w_hi.astype(F32)).astype(BF16)
    rows = sc.shape[0]
    top = jnp.dot(jnp.concatenate([sc_hi, sc_lo], axis=0), w_hi, preferred_element_type=F32)
    o_ref[0] = (top[0:rows] + top[rows:2 * rows]
                + jnp.dot(sc_hi, w_lo, preferred_element_type=F32) + b_ref[0])


def _mod_call(c_all, w_mod, b_mod):
    depth, d, nd = w_mod.shape
    rows = c_all.shape[0]
    tn = _tile(nd, 1024)
    return pl.pallas_call(
        _mod_kernel,
        grid=(depth, nd // tn),
        in_specs=[
            pl.BlockSpec((rows, d), lambda l, j: (0, 0)),
            pl.BlockSpec((1, d, tn), lambda l, j: (l, 0, j)),
            pl.BlockSpec((1, 1, tn), lambda l, j: (l, 0, j)),
        ],
        out_specs=pl.BlockSpec((1, rows, tn), lambda l, j: (l, 0, j)),
        out_shape=jax.ShapeDtypeStruct((depth, rows, nd), F32),
        compiler_params=_params(("arbitrary", "arbitrary")),
        name="mod",
    )(c_all, w_mod, b_mod.reshape(depth, 1, nd))


def _ffn_kernel(*refs, n_f, emit_h, final):
    x_ref, mod_ref, w13_ref, w2_ref = refs[:4]
    pos = 4
    modn_ref = gain_ref = hn_ref = None
    if emit_h:
        modn_ref = refs[pos]
        pos += 1
    if final:
        gain_ref = refs[pos]
        pos += 1
    o_ref = refs[pos]
    pos += 1
    if emit_h:
        hn_ref = refs[pos]
        pos += 1
    h_scr, acc_scr = refs[pos], refs[pos + 1]
    f = pl.program_id(2)

    assert n_f >= 3

    tf = w2_ref.shape[0]

    def partial_out(h):
        ab = jnp.dot(h, w13_ref[...], preferred_element_type=F32)
        g = (_silu(ab[:, 0:tf]) * ab[:, tf:2 * tf]).astype(BF16)
        return jnp.dot(g, w2_ref[...], preferred_element_type=F32)

    @pl.when(f == 0)
    def _():
        m = mod_ref[0]
        h = (_rms(x_ref[0]) * (1.0 + m[1:2]) + m[0:1]).astype(BF16)
        h_scr[...] = h
        acc_scr[...] = partial_out(h)

    @pl.when(jnp.logical_and(f > 0, f < n_f - 1))
    def _():
        acc_scr[...] += partial_out(h_scr[...])

    @pl.when(f == n_f - 1)
    def _():
        m = mod_ref[0]
        xo = x_ref[0] + 0.5 * m[2:3] * (acc_scr[...] + partial_out(h_scr[...]))
        if emit_h:
            mn = modn_ref[0]
            hn_ref[0] = (_rms(xo) * (1.0 + mn[1:2]) + mn[0:1]).astype(BF16)
        if final:
            xo = _rms(xo) * gain_ref[...]
        o_ref[0] = xo


def _ffn_call(x, mod3, w13, w2, layer, modn=None, gain=None):
    bsz, s, d = x.shape
    n_f, tf = w13.shape[1], w13.shape[3] // 2
    tm = _tile(s, 512)
    emit_h = modn is not None
    final = gain is not None
    in_specs = [
        pl.BlockSpec((1, tm, d), lambda b, i, f: (b, i, 0)),
        pl.BlockSpec((1, 3, d), lambda b, i, f: (b, 0, 0)),
        pl.BlockSpec((None, None, d, 2 * tf), lambda b, i, f: (layer, f, 0, 0)),
        pl.BlockSpec((None, tf, d), lambda b, i, f: (layer, f, 0)),
    ]
    args = [x, mod3, w13, w2]
    if emit_h:
        in_specs.append(pl.BlockSpec((1, 3, d), lambda b, i, f: (b, 0, 0)))
        args.append(modn)
    if final:
        in_specs.append(pl.BlockSpec((1, d), lambda b, i, f: (0, 0)))
        args.append(gain.reshape(1, d))
    out_specs = [pl.BlockSpec((1, tm, d), lambda b, i, f: (b, i, 0))]
    out_shape = [jax.ShapeDtypeStruct((bsz, s, d), F32)]
    if emit_h:
        out_specs.append(pl.BlockSpec((1, tm, d), lambda b, i, f: (b, i, 0)))
        out_shape.append(jax.ShapeDtypeStruct((bsz, s, d), BF16))
    outs = pl.pallas_call(
        functools.partial(_ffn_kernel, n_f=n_f, emit_h=emit_h, final=final),
        grid=(bsz, s // tm, n_f),
        in_specs=in_specs,
        out_specs=out_specs,
        out_shape=out_shape,
        scratch_shapes=[pltpu.VMEM((tm, d), BF16), pltpu.VMEM((tm, d), F32)],
        compiler_params=_params(("arbitrary", "arbitrary", "arbitrary")),
        name="ffn",
    )(*args)
    return outs if emit_h else outs[0]


def _cast_kernel(*refs):
    half = len(refs) // 2
    for src, dst in zip(refs[:half], refs[half:]):
        dst[...] = src[...].astype(dst.dtype)


def _cast_call(arrays, steps):
    depth = arrays[0].shape[0]
    specs = []
    for a in arrays:
        assert a.shape[0] == depth and a.shape[1] % (16 * steps) == 0
        specs.append(pl.BlockSpec((1, a.shape[1] // steps, a.shape[2]), lambda l, j: (l, j, 0)))
    return pl.pallas_call(
        _cast_kernel,
        grid=(depth, steps),
        in_specs=specs,
        out_specs=specs,
        out_shape=[jax.ShapeDtypeStruct(a.shape, BF16) for a in arrays],
        compiler_params=_params(("arbitrary", "arbitrary")),
        name="cast_weights",
    )(*arrays)


def _cast_pairs_kernel(*refs):
    n_out = len(refs) // 3
    for k in range(n_out):
        wa, wb, dst = refs[2 * k], refs[2 * k + 1], refs[2 * n_out + k]
        tf = wa.shape[2]
        dst[0, 0, :, 0:tf] = wa[0].astype(dst.dtype)
        dst[0, 0, :, tf:2 * tf] = wb[0].astype(dst.dtype)


def _cast_pairs_call(pairs, steps):
    depth, r, c = pairs[0][0].shape
    tf = c // steps
    in_spec = pl.BlockSpec((1, r, tf), lambda l, j: (l, 0, j))
    out_spec = pl.BlockSpec((1, 1, r, 2 * tf), lambda l, j: (l, j, 0, 0))
    return pl.pallas_call(
        _cast_pairs_kernel,
        grid=(depth, steps),
        in_specs=[in_spec] * (2 * len(pairs)),
        out_specs=[out_spec] * len(pairs),
        out_shape=[jax.ShapeDtypeStruct((depth, steps, r, 2 * tf), BF16)] * len(pairs),
        compiler_params=_params(("arbitrary", "arbitrary")),
        name="cast_weight_pairs",
    )(*[w for pair in pairs for w in pair])


def _matmul_kernel(a_ref, b_ref, o_ref):
    o_ref[...] = jnp.dot(a_ref[...], b_ref[...], preferred_element_type=F32).astype(o_ref.dtype)


def _matmul_call(a, b, out_dtype, tm_pref=1024, tn_pref=1024, b_cols=None):
    m, k = a.shape
    c0, c1 = b_cols if b_cols is not None else (0, b.shape[1])
    n = c1 - c0
    tm = _tile(m, tm_pref)
    tn = math.gcd(_tile(n, tn_pref), c0) if c0 else _tile(n, tn_pref)
    assert tn % LANE == 0
    joff = c0 // tn
    return pl.pallas_call(
        _matmul_kernel,
        grid=(m // tm, n // tn),
        in_specs=[
            pl.BlockSpec((tm, k), lambda i, j: (i, 0)),
            pl.BlockSpec((k, tn), lambda i, j: (0, joff + j)),
        ],
        out_specs=pl.BlockSpec((tm, tn), lambda i, j: (i, j)),
        out_shape=jax.ShapeDtypeStruct((m, n), out_dtype),
        compiler_params=_params(("arbitrary", "arbitrary")),
        name="matmul",
    )(a, b)


def _ret_kernel(lg_ref, q_ref, k_ref, v_ref, g_ref, cos_ref, sin_ref, kc_ref, vc_ref, o_ref,
                qr_scr, kr_scr, upd_scr, st_scr, *, n, n_ctx, chunk):
    hd = HEAD_DIM
    head = pl.program_id(1)
    lgf = lg_ref[0, head]
    lgb = lg_ref[1, head]
    nc = n // chunk
    k_scale = hd ** -0.5

    posc = lax.broadcasted_iota(jnp.int32, (n_ctx, hd), 0).astype(F32)
    kc = kc_ref[0].astype(F32) * k_scale
    vc = vc_ref[0]
    kcf = (kc * jnp.exp((n_ctx - 1.0 - posc) * lgf)).T.astype(BF16)
    kcb = (kc * jnp.exp(posc * lgb)).T.astype(BF16)
    s0_f = jnp.dot(kcf, vc, preferred_element_type=F32)
    s0_b = jnp.dot(kcb, vc, preferred_element_type=F32)

    pos = lax.broadcasted_iota(jnp.int32, (chunk, hd), 0).astype(F32)
    qdec_f = jnp.exp((pos + 1.0) * lgf)
    kdec_f = jnp.exp((chunk - 1.0 - pos) * lgf)
    qdec_b = jnp.exp((chunk - pos) * lgb)
    kdec_b = jnp.exp(pos * lgb)
    cdec_f = jnp.exp(jnp.full((1, hd), chunk, F32) * lgf)
    cdec_b = jnp.exp(jnp.full((1, hd), chunk, F32) * lgb)
    ri = lax.broadcasted_iota(jnp.int32, (chunk, chunk), 0)
    ci = lax.broadcasted_iota(jnp.int32, (chunk, chunk), 1)
    diff = (ri - ci).astype(F32)
    dmat = (jnp.where(diff >= 0, jnp.exp(jnp.maximum(diff, 0.0) * lgf), 0.0)
            + jnp.where(diff <= 0, jnp.exp(jnp.maximum(-diff, 0.0) * lgb), 0.0))

    def rotary(x, rows):
        return x * cos_ref[rows, :] + pltpu.roll(x, hd // 2, 1) * sin_ref[rows, :]

    def chunk_rows(c):
        return pl.ds(pl.multiple_of(c * chunk, chunk), chunk)

    def prep_body(c, carry):
        rows = chunk_rows(c)
        qr_scr[rows, :] = rotary(q_ref[0, rows, :].astype(F32), rows)
        k = rotary(k_ref[0, rows, :].astype(F32), rows) * k_scale
        kr_scr[rows, :] = k
        kk = jnp.concatenate([k * kdec_f, k * kdec_b], axis=1).T.astype(BF16)
        upd_scr[c] = jnp.dot(kk, v_ref[0, rows, :], preferred_element_type=F32)
        return carry

    lax.fori_loop(0, nc, prep_body, 0, unroll=math.gcd(nc, RET_UNROLL))

    def fwd_scan(c, sf):
        st_scr[c, 0:hd, :] = sf.astype(BF16)
        return cdec_f * sf + upd_scr[c, 0:hd, :]

    def bwd_scan(j, sb):
        c = nc - 1 - j
        st_scr[c, hd:2 * hd, :] = sb.astype(BF16)
        return cdec_b * sb + upd_scr[c, hd:2 * hd, :]

    lax.fori_loop(0, nc, fwd_scan, s0_f, unroll=math.gcd(nc, RET_UNROLL))
    lax.fori_loop(0, nc, bwd_scan, s0_b, unroll=math.gcd(nc, RET_UNROLL))

    def out_body(c, carry):
        rows = chunk_rows(c)
        q = qr_scr[rows, :]
        s = lax.dot_general(q.astype(BF16), kr_scr[rows, :].astype(BF16), (((1,), (1,)), ((), ())),
                            preferred_element_type=F32) * dmat
        qq = jnp.concatenate([q * qdec_f, q * qdec_b], axis=1).astype(BF16)
        o = (jnp.dot(s.astype(BF16), v_ref[0, rows, :], preferred_element_type=F32)
             + jnp.dot(qq, st_scr[c], preferred_element_type=F32))
        g = g_ref[0, rows, :].astype(F32)
        o_ref[0, rows, :] = (_rms(o) * _silu(g)).astype(o_ref.dtype)
        return carry

    lax.fori_loop(0, nc, out_body, 0, unroll=math.gcd(nc, RET_UNROLL))


def _ret_call(p, kvc, log_decay, cos2, sin2, ret_w):
    bsz, n, _ = p.shape
    n_ctx = kvc.shape[1]
    heads = ret_w // HEAD_DIM
    hd = HEAD_DIM
    chunk = _tile(n, RET_CHUNK)
    kern = functools.partial(_ret_kernel, n=n, n_ctx=n_ctx, chunk=chunk)
    col = lambda off: (lambda b, h, lg: (b, 0, off * heads + h))
    return pl.pallas_call(
        kern,
        grid_spec=pltpu.PrefetchScalarGridSpec(
            num_scalar_prefetch=1,
            grid=(bsz, heads),
            in_specs=[
                pl.BlockSpec((1, n, hd), col(0)),
                pl.BlockSpec((1, n, hd), col(1)),
                pl.BlockSpec((1, n, hd), col(2)),
                pl.BlockSpec((1, n, hd), col(3)),
                pl.BlockSpec((n, hd), lambda b, h, lg: (0, 0)),
                pl.BlockSpec((n, hd), lambda b, h, lg: (0, 0)),
                pl.BlockSpec((1, n_ctx, hd), col(0)),
                pl.BlockSpec((1, n_ctx, hd), col(1)),
            ],
            out_specs=pl.BlockSpec((1, n, hd), lambda b, h, lg: (b, 0, h)),
            scratch_shapes=[
                pltpu.VMEM((n, hd), F32),
                pltpu.VMEM((n, hd), F32),
                pltpu.VMEM((n // chunk, 2 * hd, hd), F32),
                pltpu.VMEM((n // chunk, 2 * hd, hd), BF16),
            ],
        ),
        out_shape=jax.ShapeDtypeStruct((bsz, n, ret_w), BF16),
        compiler_params=_params(("arbitrary", "arbitrary")),
        name="retention",
    )(log_decay, p, p, p, p, cos2, sin2, kvc, kvc)


def _cmul(a, b):
    return a[0] * b[0] - a[1] * b[1], a[0] * b[1] + a[1] * b[0]


def _cmul_const(a, c, s):
    eps = 1e-12
    if abs(s) < eps:
        return (a[0], a[1]) if c > 0 else (-a[0], -a[1])
    if abs(c) < eps:
        return (-a[1], a[0]) if s > 0 else (a[1], -a[0])
    return a[0] * c - a[1] * s, a[0] * s + a[1] * c


def _fft_list(xs, sign):
    size = len(xs)
    if size == 1:
        return xs
    even = _fft_list(xs[0::2], sign)
    odd = _fft_list(xs[1::2], sign)
    out = [None] * size
    for k in range(size // 2):
        ang = sign * 2.0 * math.pi * k / size
        tr, ti = _cmul_const(odd[k], math.cos(ang), math.sin(ang))
        out[k] = (even[k][0] + tr, even[k][1] + ti)
        out[k + size // 2] = (even[k][0] - tr, even[k][1] - ti)
    return out


def _fft_tables(n):
    j_cls = HY_CLASSES
    k0n, mcls = 2 * n // j_cls, n // j_cls
    t = j_cls * np.arange(mcls)[None, None, :] + np.arange(j_cls)[:, None, None]
    ang = 2.0 * math.pi * ((np.arange(k0n)[None, :, None] * t) % (2 * n)) / (2 * n)
    cs = np.concatenate([np.cos(ang), np.sin(ang)], axis=1)
    cst = np.concatenate([np.cos(ang).transpose(0, 2, 1), np.sin(ang).transpose(0, 2, 1)], axis=1)
    return jnp.asarray(cs, dtype=BF16), jnp.asarray(cst, dtype=BF16)


def _filt_kernel(z_ref, w1_ref, b1_ref, w2_ref, b2_ref, w3_ref, b3_ref, fr_ref, w4_ref, dl_ref,
                 hf_ref, hb_ref, *, n, tt, hw):
    i = pl.program_id(0)
    dot = functools.partial(jnp.dot, preferred_element_type=F32, precision=HIGHEST)
    fr = fr_ref[...]
    a = jnp.sin(fr[0:1] * (dot(z_ref[...], w1_ref[...]) + b1_ref[...]))
    a = jnp.sin(fr[1:2] * (dot(a, w2_ref[...]) + b2_ref[...]))
    a = jnp.sin(fr[2:3] * (dot(a, w3_ref[...]) + b3_ref[...]))
    hf = jnp.dot(a.astype(BF16), w4_ref[...], preferred_element_type=F32)
    mcls = n // HY_CLASSES
    rho = lax.broadcasted_iota(jnp.int32, (tt, hw), 0) + i * tt
    ti = HY_CLASSES * (rho & (mcls - 1)) + lax.shift_right_logical(rho, int(math.log2(mcls)))
    t = ti.astype(F32) * (1.0 / (n - 1.0))
    window = jnp.exp(-t * dl_ref[...])
    for o in range(HY_ORDER):
        hf_ref[:, o * hw:(o + 1) * hw] = (hf[:, (2 * o) * hw:(2 * o + 1) * hw] * window).astype(BF16)
        h_b = jnp.where(ti == 0, 0.0, hf[:, (2 * o + 1) * hw:(2 * o + 2) * hw] * window)
        hb_ref[:, o * hw:(o + 1) * hw] = h_b.astype(BF16)


def _filt_call(n, fw1, fb1, fw2, fb2, fw3, fb3, freq, fw4, hw):
    hid = fw1.shape[1]
    pk = LANE
    mcls = n // HY_CLASSES
    assert mcls & (mcls - 1) == 0
    bands = (HY_EMB_DIM - 1) // 2
    tpos = np.linspace(0.0, 1.0, n, dtype=np.float32)[:, None].astype(np.float64)
    fb = np.linspace(1e-4, bands - 1, bands, dtype=np.float32)[None, :].astype(np.float64)
    wv = (2.0 * math.pi * np.arange(n, dtype=np.float64)[:, None] / n)
    z = np.concatenate([tpos, np.cos(fb * wv), -np.sin(fb * wv)], axis=-1)
    rho = np.arange(n)
    t_of_row = HY_CLASSES * (rho % mcls) + rho // mcls
    z = np.pad(z, ((0, 0), (0, pk - HY_EMB_DIM))).astype(np.float32)[t_of_row]
    max_decay = math.log(HY_DECAY_TARGET) / HY_DECAY_SHORT_PCT
    min_decay = math.log(HY_DECAY_TARGET) / HY_DECAY_LONG_PCT
    deltas = np.abs(np.linspace(min_decay, max_decay, hw, dtype=np.float32)).reshape(1, hw)

    padc = lambda w: jnp.pad(w, ((0, 0), (0, pk - hid)))
    w1 = jnp.pad(fw1, ((0, pk - HY_EMB_DIM), (0, pk - hid)))
    w2 = jnp.pad(fw2, ((0, pk - hid), (0, pk - hid)))
    w3 = jnp.pad(fw3, ((0, pk - hid), (0, pk - hid)))
    w4 = jnp.pad(fw4, ((0, pk - hid), (0, 0))).astype(BF16)
    b1, b2, b3 = (padc(b.reshape(1, hid)) for b in (fb1, fb2, fb3))
    fr = jnp.pad(freq, ((0, 8 - freq.shape[0]), (0, pk - hid)))
    tt = _tile(n, 256)
    nw = HY_ORDER * hw
    full = lambda shape: pl.BlockSpec(shape, lambda i: (0, 0))
    return pl.pallas_call(
        functools.partial(_filt_kernel, n=n, tt=tt, hw=hw),
        grid=(n // tt,),
        in_specs=[
            pl.BlockSpec((tt, pk), lambda i: (i, 0)),
            full((pk, pk)), full((1, pk)), full((pk, pk)), full((1, pk)),
            full((pk, pk)), full((1, pk)), full((8, pk)), full((pk, 2 * nw)), full((1, hw)),
        ],
        out_specs=[pl.BlockSpec((tt, nw), lambda i: (i, 0))] * 2,
        out_shape=[jax.ShapeDtypeStruct((n, nw), BF16)] * 2,
        compiler_params=_params(("arbitrary",)),
        name="hyena_filters",
    )(jnp.asarray(z), w1, b1, w2, b2, w3, b3, fr, w4, jnp.asarray(deltas))


def _sconv_kernel(p_ref, w_ref, b_ref, o_ref, *, n):
    w = w_ref[...]
    bias = b_ref[...]
    mcls = n // HY_CLASSES
    row = lax.broadcasted_iota(jnp.int32, (mcls, p_ref.shape[2]), 0)

    def cls(j):
        return p_ref[0, j * mcls:(j + 1) * mcls, :].astype(F32)

    for j in range(HY_CLASSES):
        cur = cls(j)
        if j > 0:
            prev = cls(j - 1)
        else:
            prev = jnp.where(row == 0, 0.0, pltpu.roll(cls(HY_CLASSES - 1), 1, 0))
        if j < HY_CLASSES - 1:
            nxt = cls(j + 1)
        else:
            nxt = jnp.where(row == mcls - 1, 0.0, pltpu.roll(cls(0), mcls - 1, 0))
        out = bias + prev * w[0:1] + cur * w[1:2] + nxt * w[2:3]
        o_ref[0, j * mcls:(j + 1) * mcls, :] = out.astype(o_ref.dtype)


def _sconv_call(p, conv_w, conv_b):
    bsz, n, width = p.shape
    tc = _tile(width, 512)
    return pl.pallas_call(
        functools.partial(_sconv_kernel, n=n),
        grid=(bsz, width // tc),
        in_specs=[
            pl.BlockSpec((1, n, tc), lambda b, j: (b, 0, j)),
            pl.BlockSpec((HY_SHORT, tc), lambda b, j: (0, j)),
            pl.BlockSpec((1, tc), lambda b, j: (0, j)),
        ],
        out_specs=pl.BlockSpec((1, n, tc), lambda b, j: (b, 0, j)),
        out_shape=jax.ShapeDtypeStruct((bsz, n, width), BF16),
        compiler_params=_params(("arbitrary", "arbitrary")),
        name="short_conv",
    )(p, conv_w, conv_b.reshape(1, width))


def _spec_kernel(cs_ref, hf_ref, hb_ref, kre_ref, kim_ref, wf_scr, wb_scr,
                 *, k0n, mcls):
    j_cls = HY_CLASSES
    tn = hf_ref.shape[1]
    for j in range(j_cls):
        rows = slice(j * mcls, (j + 1) * mcls)
        prod = jnp.dot(cs_ref[j], jnp.concatenate([hf_ref[rows, :], hb_ref[rows, :]], axis=1),
                       preferred_element_type=F32)
        wf_scr[j, 0:k0n, :] = prod[0:k0n, 0:tn]
        wf_scr[j, k0n:2 * k0n, :] = -prod[k0n:2 * k0n, 0:tn]
        wb_scr[j, 0:k0n, :] = prod[0:k0n, tn:2 * tn]
        wb_scr[j, k0n:2 * k0n, :] = -prod[k0n:2 * k0n, tn:2 * tn]

    def body(c, carry):
        halves = []
        for half in range(2):
            r8 = pl.ds(pl.multiple_of(c * 16 + half * 8, 8), 8)
            i8 = pl.ds(pl.multiple_of(k0n + c * 16 + half * 8, 8), 8)
            xs = [(wf_scr[j, r8, :], wf_scr[j, i8, :]) for j in range(j_cls)]
            ys = [(wb_scr[j, r8, :], wb_scr[j, i8, :]) for j in range(j_cls)]
            halves.append((_fft_list(xs, -1), _fft_list(ys, -1)))
        for q in range(j_cls):
            rows = pl.ds(pl.multiple_of(q * k0n + c * 16, 16), 16)
            kre = jnp.concatenate([h[0][q][0] + h[1][q][0] for h in halves], axis=0)
            kim = jnp.concatenate([h[0][q][1] - h[1][q][1] for h in halves], axis=0)
            kre_ref[rows, :] = kre.astype(kre_ref.dtype)
            kim_ref[rows, :] = kim.astype(kim_ref.dtype)
        return carry

    lax.fori_loop(0, k0n // 16, body, 0)


def _spec_call(tables, hf, hb):
    cs, _ = tables
    n, width = hf.shape
    tn = LANE
    k0n, mcls = cs.shape[1] // 2, cs.shape[2]
    const = lambda shape: pl.BlockSpec(shape, lambda j: (0,) * len(shape),
                                       pipeline_mode=pl.Buffered(1))
    return pl.pallas_call(
        functools.partial(_spec_kernel, k0n=k0n, mcls=mcls),
        grid=(width // tn,),
        in_specs=[
            const(cs.shape),
            pl.BlockSpec((n, tn), lambda j: (0, j)),
            pl.BlockSpec((n, tn), lambda j: (0, j)),
        ],
        out_specs=[pl.BlockSpec((2 * n, tn), lambda j: (0, j))] * 2,
        out_shape=[jax.ShapeDtypeStruct((2 * n, width), BF16)] * 2,
        scratch_shapes=[pltpu.VMEM((HY_CLASSES, 2 * k0n, tn), F32)] * 2,
        compiler_params=_params(("arbitrary",)),
        name="filter_spectrum",
    )(cs, hf, hb)


def _fftconv_kernel(cs_ref, cst_ref, z_ref, g_ref, kre_ref, kim_ref, bias_ref,
                    o_ref, w_scr, *, k0n, mcls, scale):
    j_cls = HY_CLASSES
    tn = o_ref.shape[3]
    for j in range(j_cls):
        rows = slice(j * mcls, (j + 1) * mcls)
        prod = jnp.dot(cs_ref[j], jnp.concatenate([z_ref[0, 0, rows, :], z_ref[0, 1, rows, :]], axis=1),
                       preferred_element_type=F32)
        w_scr[j, 0:k0n, :] = prod[0:k0n, 0:tn] + prod[k0n:2 * k0n, tn:2 * tn]
        w_scr[j, k0n:2 * k0n, :] = prod[0:k0n, tn:2 * tn] - prod[k0n:2 * k0n, 0:tn]

    def body(c, carry):
        for half in range(2):
            r8 = pl.ds(pl.multiple_of(c * 16 + half * 8, 8), 8)
            i8 = pl.ds(pl.multiple_of(k0n + c * 16 + half * 8, 8), 8)
            xs = [(w_scr[j, r8, :], w_scr[j, i8, :]) for j in range(j_cls)]
            spec = _fft_list(xs, -1)
            prod = []
            for q in range(j_cls):
                rows = pl.ds(pl.multiple_of(q * k0n + c * 16, 16), 16)
                kq = (kre_ref[rows, :].astype(F32)[half * 8:(half + 1) * 8],
                      kim_ref[rows, :].astype(F32)[half * 8:(half + 1) * 8])
                prod.append(_cmul(spec[q], kq))
            back = _fft_list(prod, 1)
            for j in range(j_cls):
                w_scr[j, r8, :] = back[j][0]
                w_scr[j, i8, :] = back[j][1]
        return carry

    lax.fori_loop(0, k0n // 16, body, 0)

    bias = bias_ref[...]
    for j in range(j_cls):
        rows = slice(j * mcls, (j + 1) * mcls)
        v2 = jnp.concatenate([w_scr[j, 0:k0n, :], w_scr[j, k0n:2 * k0n, :]], axis=1).astype(BF16)
        prod = jnp.dot(cst_ref[j], v2, preferred_element_type=F32) * scale
        ys = (prod[0:mcls, 0:tn] - prod[mcls:2 * mcls, tn:2 * tn],
              prod[mcls:2 * mcls, 0:tn] + prod[0:mcls, tn:2 * tn])
        for b2 in range(2):
            conv = ys[b2] + z_ref[0, b2, rows, :].astype(F32) * bias
            o_ref[0, b2, rows, :] = (g_ref[0, b2, rows, :].astype(F32) * conv).astype(o_ref.dtype)


def _fftconv_call(tables, z, z_col0, gate, gate_col0, kre, kim, k_col0, bias):
    cs, cst = tables
    bsz, n, _ = z.shape
    hw = bias.shape[0]
    tn = LANE
    k0n, mcls = cs.shape[1] // 2, cs.shape[2]
    zo, go, ko = z_col0 // tn, gate_col0 // tn, k_col0 // tn
    z4 = z.reshape(bsz // 2, 2, n, z.shape[2])
    g4 = gate.reshape(bsz // 2, 2, n, gate.shape[2])
    const = lambda shape: pl.BlockSpec(shape, lambda j, p: (0,) * len(shape),
                                       pipeline_mode=pl.Buffered(1))
    out = pl.pallas_call(
        functools.partial(_fftconv_kernel, k0n=k0n, mcls=mcls, scale=1.0 / (2 * n)),
        grid=(hw // tn, bsz // 2),
        in_specs=[
            const(cs.shape), const(cst.shape),
            pl.BlockSpec((1, 2, n, tn), lambda j, p: (p, 0, 0, zo + j)),
            pl.BlockSpec((1, 2, n, tn), lambda j, p: (p, 0, 0, go + j)),
            pl.BlockSpec((2 * n, tn), lambda j, p: (0, ko + j)),
            pl.BlockSpec((2 * n, tn), lambda j, p: (0, ko + j)),
            pl.BlockSpec((1, tn), lambda j, p: (0, j)),
        ],
        out_specs=pl.BlockSpec((1, 2, n, tn), lambda j, p: (p, 0, 0, j)),
        out_shape=jax.ShapeDtypeStruct((bsz // 2, 2, n, hw), BF16),
        scratch_shapes=[pltpu.VMEM((HY_CLASSES, 2 * k0n, tn), F32)],
        compiler_params=_params(("arbitrary", "arbitrary")),
        name="fft_conv",
    )(cs, cst, z4, g4, kre, kim, bias.reshape(1, hw))
    return out.reshape(bsz, n, hw)


def _wout_kernel(r_ref, y_ref, wa_ref, wb_ref, x_ref, gate_ref, o_ref):
    y = (jnp.dot(r_ref[0], wa_ref[...], preferred_element_type=F32)
         + jnp.dot(y_ref[0], wb_ref[...], preferred_element_type=F32))
    o_ref[0] = x_ref[0] + gate_ref[0] * y


def _wout_call(r, yh, w_out, x, gate):
    bsz, n, d = x.shape
    rw = r.shape[2]
    assert d == 2 * rw and yh.shape[2] == rw
    tm = _tile(n, 512)
    return pl.pallas_call(
        _wout_kernel,
        grid=(bsz, n // tm),
        in_specs=[
            pl.BlockSpec((1, tm, rw), lambda b, i: (b, i, 0)),
            pl.BlockSpec((1, tm, d - rw), lambda b, i: (b, i, 0)),
            pl.BlockSpec((rw, d), lambda b, i: (0, 0)),
            pl.BlockSpec((rw, d), lambda b, i: (1, 0)),
            pl.BlockSpec((1, tm, d), lambda b, i: (b, i, 0)),
            pl.BlockSpec((1, 1, d), lambda b, i: (b, 0, 0)),
        ],
        out_specs=pl.BlockSpec((1, tm, d), lambda b, i: (b, i, 0)),
        out_shape=jax.ShapeDtypeStruct((bsz, n, d), F32),
        compiler_params=_params(("arbitrary", "arbitrary")),
        name="mixer_out",
    )(r, yh, w_out, w_out, x, gate)


def _pool_kernel(h_ref, mc_ref, icol_ref, w_ref, sc_ref, x_ref, gate_ref, o_ref, cp_scr, pm_scr,
                 *, n, tm, to):
    g = pl.program_id(1)
    i = pl.program_id(2)
    n_rows = n // GRID_W

    @pl.when(i == 0)
    def _():
        for t in range(n // tm):
            rows = slice(t * tm, (t + 1) * tm)
            cp_scr[rows, :] = jnp.dot(mc_ref[0], h_ref[0, rows, :], preferred_element_type=F32)

    def grid_row(r):
        return slice(r * GRID_W, (r + 1) * GRID_W)

    for gi, win in enumerate(POOL_WINDOWS):
        @pl.when(jnp.logical_and(i == 0, g == gi))
        def _(win=win):
            lo, hi = win // 2, win - 1 - win // 2
            icol = icol_ref[0]
            acc = cp_scr[grid_row(0), :]
            for r in range(1, hi + 1):
                acc = acc + cp_scr[grid_row(r), :]
            for r in range(n_rows):
                if r > 0 and r + hi < n_rows:
                    acc = acc + cp_scr[grid_row(r + hi), :]
                if r - lo - 1 >= 0:
                    acc = acc - cp_scr[grid_row(r - lo - 1), :]
                cnt_r = min(r + hi, n_rows - 1) + 1 - max(r - lo, 0)
                mean = acc * (icol * (1.0 / cnt_r))
                pm_scr[grid_row(r), :] = (mean - h_ref[0, grid_row(r), :].astype(F32)).astype(BF16)

    rows = pl.ds(pl.multiple_of(i * to, to), to)
    y = jnp.dot(pm_scr[rows, :], w_ref[0], preferred_element_type=F32) * sc_ref[...]
    o_ref[0] = x_ref[0] + gate_ref[0] * y


def _pool_tables(tm, pg):
    tok = np.arange(tm)
    r_t, c_t = tok // GRID_W, tok % GRID_W
    mats, inv = [], []
    for win in POOL_WINDOWS:
        lo, hi = -(win // 2), win - 1 - win // 2
        dc = c_t[None, :] - c_t[:, None]
        mats.append(((r_t[None, :] == r_t[:, None]) & (dc >= lo) & (dc <= hi)).astype(np.float32))
        idx_c = np.arange(GRID_W)
        cnt_c = np.minimum(idx_c + hi, GRID_W - 1) + 1 - np.maximum(idx_c + lo, 0)
        inv.append(np.broadcast_to((1.0 / cnt_c)[:, None], (GRID_W, pg)))
    return (jnp.asarray(np.stack(mats), dtype=BF16),
            jnp.asarray(np.stack(inv).astype(np.float32)))


def _pool_call(h, pool_w, pool_scale, x, gate):
    bsz, n, d = x.shape
    ng = len(POOL_WINDOWS)
    pg = d // ng
    tm = _tile(n, 512)
    to = _tile(n, 1024)
    assert tm % GRID_W == 0 and n % GRID_W == 0
    mats, icol = _pool_tables(tm, pg)
    return pl.pallas_call(
        functools.partial(_pool_kernel, n=n, tm=tm, to=to),
        grid=(bsz, ng, n // to),
        in_specs=[
            pl.BlockSpec((1, n, pg), lambda b, g, i: (b, 0, g)),
            pl.BlockSpec((1, tm, tm), lambda b, g, i: (g, 0, 0)),
            pl.BlockSpec((1, GRID_W, pg), lambda b, g, i: (g, 0, 0)),
            pl.BlockSpec((1, pg, pg), lambda b, g, i: (g, 0, 0)),
            pl.BlockSpec((1, pg), lambda b, g, i: (0, g)),
            pl.BlockSpec((1, to, pg), lambda b, g, i: (b, i, g)),
            pl.BlockSpec((1, 1, pg), lambda b, g, i: (b, 0, g)),
        ],
        out_specs=pl.BlockSpec((1, to, pg), lambda b, g, i: (b, i, g)),
        out_shape=jax.ShapeDtypeStruct((bsz, n, d), F32),
        scratch_shapes=[pltpu.VMEM((n, pg), F32), pltpu.VMEM((n, pg), BF16)],
        compiler_params=_params(("arbitrary", "arbitrary", "arbitrary")),
        name="pool_mixer",
    )(h, mats, icol, pool_w, pool_scale.reshape(1, d), x, gate)


def _rotary_tables(n):
    pos = np.arange(n)
    row = (pos // GRID_W).astype(np.float32)
    col = (pos % GRID_W).astype(np.float32)
    n_freq = HEAD_DIM // 4
    inv = (np.float32(ROPE_BASE) ** (-np.arange(n_freq, dtype=np.float32) / np.float32(n_freq))).astype(np.float32)
    ang = np.concatenate([row[:, None] * inv, col[:, None] * inv], axis=-1).astype(np.float32)
    cos, sin = np.cos(ang.astype(np.float64)), np.sin(ang.astype(np.float64))
    cos2 = np.concatenate([cos, cos], axis=-1).astype(np.float32)
    sin2 = np.concatenate([-sin, sin], axis=-1).astype(np.float32)
    return jnp.asarray(cos2), jnp.asarray(sin2)


def kernel(x, c, ctx, c_ctx, w_mod, b_mod, ffn1_w1, ffn1_w3, ffn1_w2, ffn2_w1, ffn2_w3, ffn2_w2, ab_w_in, ab_w_out, ret_log_decay, hy_conv_w, hy_conv_b, hy_f_w1, hy_f_b1, hy_f_w2, hy_f_b2, hy_f_w3, hy_f_b3, hy_f_freq, hy_f_w4, hy_bias, pool_w, pool_scale, final_gain):
    bsz, n, d = x.shape
    n_ctx = ctx.shape[1]
    depth = w_mod.shape[0]
    assert depth == 2, "layer 0 = retention/Hyena mixer, layer 1 = pooling mixer"
    ret_w = RET_HEADS * HEAD_DIM
    hw = d - ret_w
    assert hy_f_w4.shape[2] == HY_ORDER * 2 * hw and bsz % 2 == 0

    c_all = jnp.concatenate([c, c_ctx[None, :], jnp.zeros((8 - bsz - 1, d), F32)], axis=0)
    mod = _mod_call(c_all, w_mod, b_mod).reshape(depth, 8, N_MOD, d)

    def mod_rows(layer, base, count=3):
        return mod[layer, :bsz, base:base + count]

    def mod_ctx(base, count=3):
        return mod[0, bsz:bsz + 1, base:base + count]

    n_f = ffn1_w1.shape[2] // _tile(ffn1_w1.shape[2], 512)
    w13 = _cast_pairs_call([(ffn1_w1, ffn1_w3), (ffn2_w1, ffn2_w3)], n_f)
    w_in_shape, w_out_shape = ab_w_in.shape[1:], ab_w_out.shape[1:]
    w2a, w2b, w_in, w_out = _cast_call(
        [ffn1_w2, ffn2_w2, ab_w_in[0].reshape(depth, -1, w_in_shape[1]),
         ab_w_out[0].reshape(depth, -1, w_out_shape[1])], 16)
    w_in, w_out = w_in.reshape(w_in_shape), w_out.reshape(w_out_shape)
    ffn1_w, ffn2_w = (w13[0], w2a), (w13[1], w2b)

    x, h = _ffn_call(x, mod_rows(0, 0), *ffn1_w, 0, modn=mod_rows(0, 3))
    _, h_c = _ffn_call(ctx.reshape(1, bsz * n_ctx, d), mod_ctx(0), *ffn1_w, 0, modn=mod_ctx(3))

    p = _matmul_call(h.reshape(bsz * n, d), w_in, BF16, b_cols=(0, 4 * ret_w)).reshape(bsz, n, -1)
    kvc = _matmul_call(h_c.reshape(bsz * n_ctx, d), w_in, BF16, b_cols=(ret_w, 3 * ret_w))
    kvc = kvc.reshape(bsz, n_ctx, 2 * ret_w)

    cos2, sin2 = _rotary_tables(n)
    r = _ret_call(p, kvc, ret_log_decay[0], cos2, sin2, ret_w)

    mcls = n // HY_CLASSES
    h_cm = h.reshape(bsz, mcls, HY_CLASSES, d).transpose(0, 2, 1, 3).reshape(bsz * n, d)
    p_hy = _matmul_call(h_cm, w_in, BF16, b_cols=(4 * ret_w, w_in.shape[1])).reshape(bsz, n, -1)
    hf, hb = _filt_call(n, hy_f_w1[0], hy_f_b1[0], hy_f_w2[0], hy_f_b2[0], hy_f_w3[0],
                        hy_f_b3[0], hy_f_freq[0], hy_f_w4[0], hw)
    tables = _fft_tables(n)
    kre, kim = _spec_call(tables, hf, hb)
    u = _sconv_call(p_hy, hy_conv_w[0], hy_conv_b[0])
    z = _fftconv_call(tables, u, 0, u, hw, kre, kim, 0, hy_bias[0, 0])
    yh = _fftconv_call(tables, z, 0, u, 2 * hw, kre, kim, hw, hy_bias[0, 1])
    yh = yh.reshape(bsz, HY_CLASSES, mcls, hw).transpose(0, 2, 1, 3).reshape(bsz, n, hw)

    x = _wout_call(r, yh, w_out, x, mod_rows(0, 5, 1))
    x = _ffn_call(x, mod_rows(0, 6), *ffn2_w, 0)

    x, h = _ffn_call(x, mod_rows(1, 0), *ffn1_w, 1, modn=mod_rows(1, 3))
    x = _pool_call(h, pool_w[0].astype(BF16), pool_scale[0], x, mod_rows(1, 5, 1))
    return _ffn_call(x, mod_rows(1, 6), *ffn2_w, 1, gain=final_gain)
```

```python
import functools
import math

import numpy as np
import jax
import jax.numpy as jnp
from jax import lax
from jax.experimental import pallas as pl
from jax.experimental.pallas import tpu as pltpu

F32 = jnp.float32
BF16 = jnp.bfloat16
HIGHEST = lax.Precision.HIGHEST

EPS = 1e-6
GRID_W = 64
N_MOD = 9
RET_HEADS = 8
HEAD_DIM = 128
HY_ORDER = 2
HY_SHORT = 3
HY_EMB_DIM = 33
HY_DECAY_SHORT_PCT = 0.3
HY_DECAY_LONG_PCT = 1.5
HY_DECAY_TARGET = 1e-2
ROPE_BASE = 10000.0
POOL_WINDOWS = (2, 4, 8, 16)
RET_CHUNK = 256
RET_UNROLL = 16
HY_CLASSES = 16
LANE = 128
SUBLANE = 8
PACKED_SUBLANE = 16
ROW_TILE = 512
WIDE_TILE = 1024
FFN_TF = 512
FILTER_ROWS = 256
CAST_STEPS = 16
VMEM_LIMIT = 56 * 1024 * 1024


def _tile(dim, pref):
    if dim <= pref:
        return dim
    for t in range(pref - pref % LANE, 0, -LANE):
        if dim % t == 0:
            return t
    raise ValueError((dim, pref))


def _params(sem):
    return pltpu.CompilerParams(dimension_semantics=sem, vmem_limit_bytes=VMEM_LIMIT)


def _rms(x):
    return x * lax.rsqrt(jnp.mean(x * x, axis=-1, keepdims=True) + EPS)


def _silu(x):
    return x * jax.nn.sigmoid(x)


def _mod_kernel(c_ref, w_ref, b_ref, o_ref):
    sc = _silu(c_ref[...])
    w = w_ref[0]
    sc_hi = sc.astype(BF16)
    sc_lo = (sc - sc_hi.astype(F32)).astype(BF16)
    w_hi = w.astype(BF16)
    w_lo = (w - w_hi.astype(F32)).astype(BF16)
    rows = sc.shape[0]
    top = jnp.dot(jnp.concatenate([sc_hi, sc_lo], axis=0), w_hi, preferred_element_type=F32)
    o_ref[0] = (top[0:rows] + top[rows:2 * rows]
                + jnp.dot(sc_hi, w_lo, preferred_element_type=F32) + b_ref[0])


def _mod_call(c_all, w_mod, b_mod):
    depth, d, nd = w_mod.shape
    rows = c_all.shape[0]
    tn = _tile(nd, WIDE_TILE)
    return pl.pallas_call(
        _mod_kernel,
        grid=(depth, nd // tn),
        in_specs=[
            pl.BlockSpec((rows, d), lambda l, j: (0, 0)),
            pl.BlockSpec((1, d, tn), lambda l, j: (l, 0, j)),
            pl.BlockSpec((1, 1, tn), lambda l, j: (l, 0, j)),
        ],
        out_specs=pl.BlockSpec((1, rows, tn), lambda l, j: (l, 0, j)),
        out_shape=jax.ShapeDtypeStruct((depth, rows, nd), F32),
        compiler_params=_params(("arbitrary", "arbitrary")),
        name="mod",
    )(c_all, w_mod, b_mod.reshape(depth, 1, nd))


def _ffn_kernel(*refs, n_s, emit_h, final):
    x_ref, mod_ref, w13a_ref, w13b_ref, w2a_ref, w2b_ref = refs[:6]
    pos = 6
    modn_ref = gain_ref = hn_ref = None
    if emit_h:
        modn_ref = refs[pos]
        pos += 1
    if final:
        gain_ref = refs[pos]
        pos += 1
    o_ref = refs[pos]
    pos += 1
    if emit_h:
        hn_ref = refs[pos]
        pos += 1
    h_scr, acc_scr = refs[pos], refs[pos + 1]
    s = pl.program_id(2)

    assert n_s >= 3

    tf = w2a_ref.shape[0]

    def tile_out(h, w13_ref, w2_ref):
        ab = jnp.dot(h, w13_ref[...], preferred_element_type=F32)
        g = (_silu(ab[:, 0:tf]) * ab[:, tf:2 * tf]).astype(BF16)
        return jnp.dot(g, w2_ref[...], preferred_element_type=F32)

    def pair_out(h):
        return tile_out(h, w13a_ref, w2a_ref) + tile_out(h, w13b_ref, w2b_ref)

    @pl.when(s == 0)
    def _():
        m = mod_ref[0]
        h = (_rms(x_ref[0]) * (1.0 + m[1:2]) + m[0:1]).astype(BF16)
        h_scr[...] = h
        acc_scr[...] = tile_out(h, w13a_ref, w2a_ref)

    @pl.when(jnp.logical_and(s > 0, s < n_s - 1))
    def _():
        acc_scr[...] += pair_out(h_scr[...])

    @pl.when(s == n_s - 1)
    def _():
        m = mod_ref[0]
        xo = x_ref[0] + 0.5 * m[2:3] * (acc_scr[...] + pair_out(h_scr[...]))
        if emit_h:
            mn = modn_ref[0]
            hn_ref[0] = (_rms(xo) * (1.0 + mn[1:2]) + mn[0:1]).astype(BF16)
        if final:
            xo = _rms(xo) * gain_ref[...]
        o_ref[0] = xo


def _ffn_call(x, mod3, w13, w2, layer, modn=None, gain=None):
    bsz, s, d = x.shape
    n_f, tf = w13.shape[1], w13.shape[3] // 2
    assert n_f % 2 == 1, "step 0 takes one D_FF tile, later steps two"
    n_s = (n_f + 1) // 2
    tm = _tile(s, ROW_TILE)
    emit_h = modn is not None
    final = gain is not None
    in_specs = [
        pl.BlockSpec((1, tm, d), lambda b, i, f: (b, i, 0)),
        pl.BlockSpec((1, 3, d), lambda b, i, f: (b, 0, 0)),
        pl.BlockSpec((None, None, d, 2 * tf), lambda b, i, f: (layer, jnp.maximum(2 * f - 1, 0), 0, 0)),
        pl.BlockSpec((None, None, d, 2 * tf), lambda b, i, f: (layer, 2 * f, 0, 0)),
        pl.BlockSpec((None, tf, d), lambda b, i, f: (layer, jnp.maximum(2 * f - 1, 0), 0)),
        pl.BlockSpec((None, tf, d), lambda b, i, f: (layer, 2 * f, 0)),
    ]
    args = [x, mod3, w13, w13, w2, w2]
    if emit_h:
        in_specs.append(pl.BlockSpec((1, 3, d), lambda b, i, f: (b, 0, 0)))
        args.append(modn)
    if final:
        in_specs.append(pl.BlockSpec((1, d), lambda b, i, f: (0, 0)))
        args.append(gain.reshape(1, d))
    out_specs = [pl.BlockSpec((1, tm, d), lambda b, i, f: (b, i, 0))]
    out_shape = [jax.ShapeDtypeStruct((bsz, s, d), F32)]
    if emit_h:
        out_specs.append(pl.BlockSpec((1, tm, d), lambda b, i, f: (b, i, 0)))
        out_shape.append(jax.ShapeDtypeStruct((bsz, s, d), BF16))
    outs = pl.pallas_call(
        functools.partial(_ffn_kernel, n_s=n_s, emit_h=emit_h, final=final),
        grid=(bsz, s // tm, n_s),
        in_specs=in_specs,
        out_specs=out_specs,
        out_shape=out_shape,
        scratch_shapes=[pltpu.VMEM((tm, d), BF16), pltpu.VMEM((tm, d), F32)],
        compiler_params=_params(("arbitrary", "arbitrary", "arbitrary")),
        name="ffn",
    )(*args)
    return outs if emit_h else outs[0]


def _cast_kernel(*refs):
    half = len(refs) // 2
    for src, dst in zip(refs[:half], refs[half:]):
        dst[...] = src[...].astype(dst.dtype)


def _cast_call(arrays, steps):
    depth = arrays[0].shape[0]
    specs = []
    for a in arrays:
        assert a.shape[0] == depth and a.shape[1] % (PACKED_SUBLANE * steps) == 0
        specs.append(pl.BlockSpec((1, a.shape[1] // steps, a.shape[2]), lambda l, j: (l, j, 0)))
    return pl.pallas_call(
        _cast_kernel,
        grid=(depth, steps),
        in_specs=specs,
        out_specs=specs,
        out_shape=[jax.ShapeDtypeStruct(a.shape, BF16) for a in arrays],
        compiler_params=_params(("arbitrary", "arbitrary")),
        name="cast_weights",
    )(*arrays)


def _cast_pairs_kernel(*refs):
    n_out = len(refs) // 3
    for k in range(n_out):
        wa, wb, dst = refs[2 * k], refs[2 * k + 1], refs[2 * n_out + k]
        tf = wa.shape[2]
        dst[0, 0, :, 0:tf] = wa[0].astype(dst.dtype)
        dst[0, 0, :, tf:2 * tf] = wb[0].astype(dst.dtype)


def _cast_pairs_call(pairs, steps):
    depth, r, c = pairs[0][0].shape
    tf = c // steps
    in_spec = pl.BlockSpec((1, r, tf), lambda l, j: (l, 0, j))
    out_spec = pl.BlockSpec((1, 1, r, 2 * tf), lambda l, j: (l, j, 0, 0))
    return pl.pallas_call(
        _cast_pairs_kernel,
        grid=(depth, steps),
        in_specs=[in_spec] * (2 * len(pairs)),
        out_specs=[out_spec] * len(pairs),
        out_shape=[jax.ShapeDtypeStruct((depth, steps, r, 2 * tf), BF16)] * len(pairs),
        compiler_params=_params(("arbitrary", "arbitrary")),
        name="cast_weight_pairs",
    )(*[w for pair in pairs for w in pair])


def _matmul_kernel(a_ref, b_ref, o_ref):
    o_ref[...] = jnp.dot(a_ref[...], b_ref[...], preferred_element_type=F32).astype(o_ref.dtype)


def _matmul_call(a, b, out_dtype, tm_pref=WIDE_TILE, tn_pref=WIDE_TILE, b_cols=None):
    m, k = a.shape
    c0, c1 = b_cols if b_cols is not None else (0, b.shape[1])
    n = c1 - c0
    tm = _tile(m, tm_pref)
    tn = math.gcd(_tile(n, tn_pref), c0) if c0 else _tile(n, tn_pref)
    assert tn % LANE == 0
    joff = c0 // tn
    return pl.pallas_call(
        _matmul_kernel,
        grid=(m // tm, n // tn),
        in_specs=[
            pl.BlockSpec((tm, k), lambda i, j: (i, 0)),
            pl.BlockSpec((k, tn), lambda i, j: (0, joff + j)),
        ],
        out_specs=pl.BlockSpec((tm, tn), lambda i, j: (i, j)),
        out_shape=jax.ShapeDtypeStruct((m, n), out_dtype),
        compiler_params=_params(("arbitrary", "arbitrary")),
        name="matmul",
    )(a, b)


def _ret_kernel(lg_ref, q_ref, k_ref, v_ref, g_ref, cos_ref, sin_ref, kc_ref, vc_ref, o_ref,
                qr_scr, kr_scr, upd_scr, st_scr, *, n, n_ctx, chunk):
    hd = HEAD_DIM
    head = pl.program_id(1)
    lgf = lg_ref[0, head]
    lgb = lg_ref[1, head]
    nc = n // chunk
    k_scale = hd ** -0.5

    posc = lax.broadcasted_iota(jnp.int32, (n_ctx, hd), 0).astype(F32)
    kc = kc_ref[0].astype(F32) * k_scale
    vc = vc_ref[0]
    kcf = (kc * jnp.exp((n_ctx - 1.0 - posc) * lgf)).T.astype(BF16)
    kcb = (kc * jnp.exp(posc * lgb)).T.astype(BF16)
    s0_f = jnp.dot(kcf, vc, preferred_element_type=F32)
    s0_b = jnp.dot(kcb, vc, preferred_element_type=F32)

    pos = lax.broadcasted_iota(jnp.int32, (chunk, hd), 0).astype(F32)
    qdec_f = jnp.exp((pos + 1.0) * lgf)
    kdec_f = jnp.exp((chunk - 1.0 - pos) * lgf)
    qdec_b = jnp.exp((chunk - pos) * lgb)
    kdec_b = jnp.exp(pos * lgb)
    cdec_f = jnp.exp(jnp.full((1, hd), chunk, F32) * lgf)
    cdec_b = jnp.exp(jnp.full((1, hd), chunk, F32) * lgb)
    ri = lax.broadcasted_iota(jnp.int32, (chunk, chunk), 0)
    ci = lax.broadcasted_iota(jnp.int32, (chunk, chunk), 1)
    diff = (ri - ci).astype(F32)
    dmat = (jnp.where(diff >= 0, jnp.exp(jnp.maximum(diff, 0.0) * lgf), 0.0)
            + jnp.where(diff <= 0, jnp.exp(jnp.maximum(-diff, 0.0) * lgb), 0.0))

    def rotary(x, rows):
        return x * cos_ref[rows, :] + pltpu.roll(x, hd // 2, 1) * sin_ref[rows, :]

    def chunk_rows(c):
        return pl.ds(pl.multiple_of(c * chunk, chunk), chunk)

    def prep_body(c, carry):
        rows = chunk_rows(c)
        qr_scr[rows, :] = rotary(q_ref[0, rows, :].astype(F32), rows)
        k = rotary(k_ref[0, rows, :].astype(F32), rows) * k_scale
        kr_scr[rows, :] = k
        kk = jnp.concatenate([k * kdec_f, k * kdec_b], axis=1).T.astype(BF16)
        upd_scr[c] = jnp.dot(kk, v_ref[0, rows, :], preferred_element_type=F32)
        return carry

    lax.fori_loop(0, nc, prep_body, 0, unroll=math.gcd(nc, RET_UNROLL))

    def fwd_scan(c, sf):
        st_scr[c, 0:hd, :] = sf.astype(BF16)
        return cdec_f * sf + upd_scr[c, 0:hd, :]

    def bwd_scan(j, sb):
        c = nc - 1 - j
        st_scr[c, hd:2 * hd, :] = sb.astype(BF16)
        return cdec_b * sb + upd_scr[c, hd:2 * hd, :]

    lax.fori_loop(0, nc, fwd_scan, s0_f, unroll=math.gcd(nc, RET_UNROLL))
    lax.fori_loop(0, nc, bwd_scan, s0_b, unroll=math.gcd(nc, RET_UNROLL))

    def out_body(c, carry):
        rows = chunk_rows(c)
        q = qr_scr[rows, :]
        s = lax.dot_general(q.astype(BF16), kr_scr[rows, :].astype(BF16), (((1,), (1,)), ((), ())),
                            preferred_element_type=F32) * dmat
        qq = jnp.concatenate([q * qdec_f, q * qdec_b], axis=1).astype(BF16)
        o = (jnp.dot(s.astype(BF16), v_ref[0, rows, :], preferred_element_type=F32)
             + jnp.dot(qq, st_scr[c], preferred_element_type=F32))
        g = g_ref[0, rows, :].astype(F32)
        o_ref[0, rows, :] = (_rms(o) * _silu(g)).astype(o_ref.dtype)
        return carry

    lax.fori_loop(0, nc, out_body, 0, unroll=math.gcd(nc, RET_UNROLL))


def _ret_call(p, kvc, log_decay, cos2, sin2, ret_w):
    bsz, n, _ = p.shape
    n_ctx = kvc.shape[1]
    heads = ret_w // HEAD_DIM
    hd = HEAD_DIM
    chunk = _tile(n, RET_CHUNK)
    kern = functools.partial(_ret_kernel, n=n, n_ctx=n_ctx, chunk=chunk)
    col = lambda off: (lambda b, h, lg: (b, 0, off * heads + h))
    return pl.pallas_call(
        kern,
        grid_spec=pltpu.PrefetchScalarGridSpec(
            num_scalar_prefetch=1,
            grid=(bsz, heads),
            in_specs=[
                pl.BlockSpec((1, n, hd), col(0)),
                pl.BlockSpec((1, n, hd), col(1)),
                pl.BlockSpec((1, n, hd), col(2)),
                pl.BlockSpec((1, n, hd), col(3)),
                pl.BlockSpec((n, hd), lambda b, h, lg: (0, 0)),
                pl.BlockSpec((n, hd), lambda b, h, lg: (0, 0)),
                pl.BlockSpec((1, n_ctx, hd), col(0)),
                pl.BlockSpec((1, n_ctx, hd), col(1)),
            ],
            out_specs=pl.BlockSpec((1, n, hd), lambda b, h, lg: (b, 0, h)),
            scratch_shapes=[
                pltpu.VMEM((n, hd), F32),
                pltpu.VMEM((n, hd), F32),
                pltpu.VMEM((n // chunk, 2 * hd, hd), F32),
                pltpu.VMEM((n // chunk, 2 * hd, hd), BF16),
            ],
        ),
        out_shape=jax.ShapeDtypeStruct((bsz, n, ret_w), BF16),
        compiler_params=_params(("arbitrary", "arbitrary")),
        name="retention",
    )(log_decay, p, p, p, p, cos2, sin2, kvc, kvc)


def _cmul(a, b):
    return a[0] * b[0] - a[1] * b[1], a[0] * b[1] + a[1] * b[0]


def _cmul_const(a, c, s):
    eps = 1e-12
    if abs(s) < eps:
        return (a[0], a[1]) if c > 0 else (-a[0], -a[1])
    if abs(c) < eps:
        return (-a[1], a[0]) if s > 0 else (a[1], -a[0])
    return a[0] * c - a[1] * s, a[0] * s + a[1] * c


def _fft_list(xs, sign):
    size = len(xs)
    if size == 1:
        return xs
    even = _fft_list(xs[0::2], sign)
    odd = _fft_list(xs[1::2], sign)
    out = [None] * size
    for k in range(size // 2):
        ang = sign * 2.0 * math.pi * k / size
        tr, ti = _cmul_const(odd[k], math.cos(ang), math.sin(ang))
        out[k] = (even[k][0] + tr, even[k][1] + ti)
        out[k + size // 2] = (even[k][0] - tr, even[k][1] - ti)
    return out


def _fft_tables(n):
    j_cls = HY_CLASSES
    k0n, mcls = 2 * n // j_cls, n // j_cls
    t = j_cls * np.arange(mcls)[None, None, :] + np.arange(j_cls)[:, None, None]
    ang = 2.0 * math.pi * ((np.arange(k0n)[None, :, None] * t) % (2 * n)) / (2 * n)
    cs = np.concatenate([np.cos(ang), np.sin(ang)], axis=1)
    cst = np.concatenate([np.cos(ang).transpose(0, 2, 1), np.sin(ang).transpose(0, 2, 1)], axis=1)
    return jnp.asarray(cs, dtype=BF16), jnp.asarray(cst / (2 * n), dtype=BF16)


def _filt_kernel(z_ref, w1_ref, b1_ref, w2_ref, b2_ref, w3_ref, b3_ref, fr_ref, w4_ref, dl_ref,
                 hf_ref, hb_ref, *, n, tt, hw):
    i = pl.program_id(0)
    dot = functools.partial(jnp.dot, preferred_element_type=F32, precision=HIGHEST)
    fr = fr_ref[...]
    a = jnp.sin(fr[0:1] * (dot(z_ref[...], w1_ref[...]) + b1_ref[...]))
    a = jnp.sin(fr[1:2] * (dot(a, w2_ref[...]) + b2_ref[...]))
    a = jnp.sin(fr[2:3] * (dot(a, w3_ref[...]) + b3_ref[...]))
    hf = jnp.dot(a.astype(BF16), w4_ref[...], preferred_element_type=F32)
    mcls = n // HY_CLASSES
    rho = lax.broadcasted_iota(jnp.int32, (tt, hw), 0) + i * tt
    ti = HY_CLASSES * (rho & (mcls - 1)) + lax.shift_right_logical(rho, int(math.log2(mcls)))
    t = ti.astype(F32) * (1.0 / (n - 1.0))
    window = jnp.exp(-t * dl_ref[...])
    for o in range(HY_ORDER):
        hf_ref[:, o * hw:(o + 1) * hw] = (hf[:, (2 * o) * hw:(2 * o + 1) * hw] * window).astype(BF16)
        h_b = jnp.where(ti == 0, 0.0, hf[:, (2 * o + 1) * hw:(2 * o + 2) * hw] * window)
        hb_ref[:, o * hw:(o + 1) * hw] = h_b.astype(BF16)


def _filt_call(n, fw1, fb1, fw2, fb2, fw3, fb3, freq, fw4, hw):
    hid = fw1.shape[1]
    pk = LANE
    mcls = n // HY_CLASSES
    assert mcls & (mcls - 1) == 0
    bands = (HY_EMB_DIM - 1) // 2
    tpos = np.linspace(0.0, 1.0, n, dtype=np.float32)[:, None].astype(np.float64)
    fb = np.linspace(1e-4, bands - 1, bands, dtype=np.float32)[None, :].astype(np.float64)
    wv = (2.0 * math.pi * np.arange(n, dtype=np.float64)[:, None] / n)
    z = np.concatenate([tpos, np.cos(fb * wv), -np.sin(fb * wv)], axis=-1)
    rho = np.arange(n)
    t_of_row = HY_CLASSES * (rho % mcls) + rho // mcls
    z = np.pad(z, ((0, 0), (0, pk - HY_EMB_DIM))).astype(np.float32)[t_of_row]
    max_decay = math.log(HY_DECAY_TARGET) / HY_DECAY_SHORT_PCT
    min_decay = math.log(HY_DECAY_TARGET) / HY_DECAY_LONG_PCT
    deltas = np.abs(np.linspace(min_decay, max_decay, hw, dtype=np.float32)).reshape(1, hw)

    padc = lambda w: jnp.pad(w, ((0, 0), (0, pk - hid)))
    w1 = jnp.pad(fw1, ((0, pk - HY_EMB_DIM), (0, pk - hid)))
    w2 = jnp.pad(fw2, ((0, pk - hid), (0, pk - hid)))
    w3 = jnp.pad(fw3, ((0, pk - hid), (0, pk - hid)))
    w4 = jnp.pad(fw4, ((0, pk - hid), (0, 0))).astype(BF16)
    b1, b2, b3 = (padc(b.reshape(1, hid)) for b in (fb1, fb2, fb3))
    fr = jnp.pad(freq, ((0, SUBLANE - freq.shape[0]), (0, pk - hid)))
    tt = _tile(n, FILTER_ROWS)
    nw = HY_ORDER * hw
    full = lambda shape: pl.BlockSpec(shape, lambda i: (0, 0))
    return pl.pallas_call(
        functools.partial(_filt_kernel, n=n, tt=tt, hw=hw),
        grid=(n // tt,),
        in_specs=[
            pl.BlockSpec((tt, pk), lambda i: (i, 0)),
            full((pk, pk)), full((1, pk)), full((pk, pk)), full((1, pk)),
            full((pk, pk)), full((1, pk)), full((SUBLANE, pk)), full((pk, 2 * nw)), full((1, hw)),
        ],
        out_specs=[pl.BlockSpec((tt, nw), lambda i: (i, 0))] * 2,
        out_shape=[jax.ShapeDtypeStruct((n, nw), BF16)] * 2,
        compiler_params=_params(("arbitrary",)),
        name="hyena_filters",
    )(jnp.asarray(z), w1, b1, w2, b2, w3, b3, fr, w4, jnp.asarray(deltas))


def _sconv_kernel(p_ref, w_ref, b_ref, o_ref, *, n):
    w = w_ref[...]
    bias = b_ref[...]
    mcls = n // HY_CLASSES
    row = lax.broadcasted_iota(jnp.int32, (mcls, p_ref.shape[2]), 0)

    def cls(j):
        return p_ref[0, j * mcls:(j + 1) * mcls, :].astype(F32)

    for j in range(HY_CLASSES):
        cur = cls(j)
        if j > 0:
            prev = cls(j - 1)
        else:
            prev = jnp.where(row == 0, 0.0, pltpu.roll(cls(HY_CLASSES - 1), 1, 0))
        if j < HY_CLASSES - 1:
            nxt = cls(j + 1)
        else:
            nxt = jnp.where(row == mcls - 1, 0.0, pltpu.roll(cls(0), mcls - 1, 0))
        out = bias + prev * w[0:1] + cur * w[1:2] + nxt * w[2:3]
        o_ref[0, j * mcls:(j + 1) * mcls, :] = out.astype(o_ref.dtype)


def _sconv_call(p, conv_w, conv_b):
    bsz, n, width = p.shape
    tc = _tile(width, ROW_TILE)
    return pl.pallas_call(
        functools.partial(_sconv_kernel, n=n),
        grid=(bsz, width // tc),
        in_specs=[
            pl.BlockSpec((1, n, tc), lambda b, j: (b, 0, j)),
            pl.BlockSpec((HY_SHORT, tc), lambda b, j: (0, j)),
            pl.BlockSpec((1, tc), lambda b, j: (0, j)),
        ],
        out_specs=pl.BlockSpec((1, n, tc), lambda b, j: (b, 0, j)),
        out_shape=jax.ShapeDtypeStruct((bsz, n, width), BF16),
        compiler_params=_params(("arbitrary", "arbitrary")),
        name="short_conv",
    )(p, conv_w, conv_b.reshape(1, width))


def _spec_kernel(cs_ref, hf_ref, hb_ref, kre_ref, kim_ref, wf_scr, wb_scr,
                 *, k0n, mcls):
    j_cls = HY_CLASSES
    tn = hf_ref.shape[1]
    for j in range(j_cls):
        rows = slice(j * mcls, (j + 1) * mcls)
        prod = jnp.dot(cs_ref[j], jnp.concatenate([hf_ref[rows, :], hb_ref[rows, :]], axis=1),
                       preferred_element_type=F32)
        wf_scr[j, 0:k0n, :] = prod[0:k0n, 0:tn]
        wf_scr[j, k0n:2 * k0n, :] = -prod[k0n:2 * k0n, 0:tn]
        wb_scr[j, 0:k0n, :] = prod[0:k0n, tn:2 * tn]
        wb_scr[j, k0n:2 * k0n, :] = -prod[k0n:2 * k0n, tn:2 * tn]

    def body(c, carry):
        halves = []
        for half in range(PACKED_SUBLANE // SUBLANE):
            r8 = pl.ds(pl.multiple_of(c * PACKED_SUBLANE + half * SUBLANE, SUBLANE), SUBLANE)
            i8 = pl.ds(pl.multiple_of(k0n + c * PACKED_SUBLANE + half * SUBLANE, SUBLANE), SUBLANE)
            xs = [(wf_scr[j, r8, :], wf_scr[j, i8, :]) for j in range(j_cls)]
            ys = [(wb_scr[j, r8, :], wb_scr[j, i8, :]) for j in range(j_cls)]
            halves.append((_fft_list(xs, -1), _fft_list(ys, -1)))
        for q in range(j_cls):
            rows = pl.ds(pl.multiple_of(q * k0n + c * PACKED_SUBLANE, PACKED_SUBLANE), PACKED_SUBLANE)
            kre = jnp.concatenate([h[0][q][0] + h[1][q][0] for h in halves], axis=0)
            kim = jnp.concatenate([h[0][q][1] - h[1][q][1] for h in halves], axis=0)
            kre_ref[rows, :] = kre.astype(kre_ref.dtype)
            kim_ref[rows, :] = kim.astype(kim_ref.dtype)
        return carry

    lax.fori_loop(0, k0n // PACKED_SUBLANE, body, 0)


def _spec_call(tables, hf, hb):
    cs, _ = tables
    n, width = hf.shape
    tn = LANE
    k0n, mcls = cs.shape[1] // 2, cs.shape[2]
    const = lambda shape: pl.BlockSpec(shape, lambda j: (0,) * len(shape),
                                       pipeline_mode=pl.Buffered(1))
    return pl.pallas_call(
        functools.partial(_spec_kernel, k0n=k0n, mcls=mcls),
        grid=(width // tn,),
        in_specs=[
            const(cs.shape),
            pl.BlockSpec((n, tn), lambda j: (0, j)),
            pl.BlockSpec((n, tn), lambda j: (0, j)),
        ],
        out_specs=[pl.BlockSpec((2 * n, tn), lambda j: (0, j))] * 2,
        out_shape=[jax.ShapeDtypeStruct((2 * n, width), BF16)] * 2,
        scratch_shapes=[pltpu.VMEM((HY_CLASSES, 2 * k0n, tn), F32)] * 2,
        compiler_params=_params(("arbitrary",)),
        name="filter_spectrum",
    )(cs, hf, hb)


def _fftconv_kernel(cs_ref, cst_ref, z_ref, g_ref, kre_ref, kim_ref, bias_ref,
                    o_ref, w_scr, *, k0n, mcls):
    j_cls = HY_CLASSES
    tn = o_ref.shape[3]
    for j in range(j_cls):
        rows = slice(j * mcls, (j + 1) * mcls)
        prod = jnp.dot(cs_ref[j], jnp.concatenate([z_ref[0, 0, rows, :], z_ref[0, 1, rows, :]], axis=1),
                       preferred_element_type=F32)
        w_scr[j, 0:k0n, :] = prod[0:k0n, 0:tn] + prod[k0n:2 * k0n, tn:2 * tn]
        w_scr[j, k0n:2 * k0n, :] = prod[0:k0n, tn:2 * tn] - prod[k0n:2 * k0n, 0:tn]

    def body(c, carry):
        for half in range(PACKED_SUBLANE // SUBLANE):
            r8 = pl.ds(pl.multiple_of(c * PACKED_SUBLANE + half * SUBLANE, SUBLANE), SUBLANE)
            i8 = pl.ds(pl.multiple_of(k0n + c * PACKED_SUBLANE + half * SUBLANE, SUBLANE), SUBLANE)
            xs = [(w_scr[j, r8, :], w_scr[j, i8, :]) for j in range(j_cls)]
            spec = _fft_list(xs, -1)
            prod = []
            for q in range(j_cls):
                rows = pl.ds(pl.multiple_of(q * k0n + c * PACKED_SUBLANE, PACKED_SUBLANE), PACKED_SUBLANE)
                part = slice(half * SUBLANE, (half + 1) * SUBLANE)
                kq = (kre_ref[rows, :].astype(F32)[part], kim_ref[rows, :].astype(F32)[part])
                prod.append(_cmul(spec[q], kq))
            back = _fft_list(prod, 1)
            for j in range(j_cls):
                w_scr[j, r8, :] = back[j][0]
                w_scr[j, i8, :] = back[j][1]
        return carry

    lax.fori_loop(0, k0n // PACKED_SUBLANE, body, 0)

    bias = bias_ref[...]
    for j in range(j_cls):
        rows = slice(j * mcls, (j + 1) * mcls)
        v2 = jnp.concatenate([w_scr[j, 0:k0n, :], w_scr[j, k0n:2 * k0n, :]], axis=1).astype(BF16)
        prod = jnp.dot(cst_ref[j], v2, preferred_element_type=F32)
        ys = (prod[0:mcls, 0:tn] - prod[mcls:2 * mcls, tn:2 * tn],
              prod[mcls:2 * mcls, 0:tn] + prod[0:mcls, tn:2 * tn])
        for b2 in range(2):
            conv = ys[b2] + z_ref[0, b2, rows, :].astype(F32) * bias
            o_ref[0, b2, rows, :] = (g_ref[0, b2, rows, :].astype(F32) * conv).astype(o_ref.dtype)


def _fftconv_call(tables, z, z_col0, gate, gate_col0, kre, kim, k_col0, bias):
    cs, cst = tables
    bsz, n, _ = z.shape
    hw = bias.shape[0]
    tn = LANE
    k0n, mcls = cs.shape[1] // 2, cs.shape[2]
    zo, go, ko = z_col0 // tn, gate_col0 // tn, k_col0 // tn
    z4 = z.reshape(bsz // 2, 2, n, z.shape[2])
    g4 = gate.reshape(bsz // 2, 2, n, gate.shape[2])
    const = lambda shape: pl.BlockSpec(shape, lambda j, p: (0,) * len(shape),
                                       pipeline_mode=pl.Buffered(1))
    out = pl.pallas_call(
        functools.partial(_fftconv_kernel, k0n=k0n, mcls=mcls),
        grid=(hw // tn, bsz // 2),
        in_specs=[
            const(cs.shape), const(cst.shape),
            pl.BlockSpec((1, 2, n, tn), lambda j, p: (p, 0, 0, zo + j)),
            pl.BlockSpec((1, 2, n, tn), lambda j, p: (p, 0, 0, go + j)),
            pl.BlockSpec((2 * n, tn), lambda j, p: (0, ko + j)),
            pl.BlockSpec((2 * n, tn), lambda j, p: (0, ko + j)),
            pl.BlockSpec((1, tn), lambda j, p: (0, j)),
        ],
        out_specs=pl.BlockSpec((1, 2, n, tn), lambda j, p: (p, 0, 0, j)),
        out_shape=jax.ShapeDtypeStruct((bsz // 2, 2, n, hw), BF16),
        scratch_shapes=[pltpu.VMEM((HY_CLASSES, 2 * k0n, tn), F32)],
        compiler_params=_params(("arbitrary", "arbitrary")),
        name="fft_conv",
    )(cs, cst, z4, g4, kre, kim, bias.reshape(1, hw))
    return out.reshape(bsz, n, hw)


def _wout_kernel(r_ref, y_ref, wa_ref, wb_ref, x_ref, gate_ref, o_ref):
    y = (jnp.dot(r_ref[0], wa_ref[...], preferred_element_type=F32)
         + jnp.dot(y_ref[0], wb_ref[...], preferred_element_type=F32))
    o_ref[0] = x_ref[0] + gate_ref[0] * y


def _wout_call(r, yh, w_out, x, gate):
    bsz, n, d = x.shape
    rw = r.shape[2]
    assert d == 2 * rw and yh.shape[2] == rw
    tm = _tile(n, ROW_TILE)
    return pl.pallas_call(
        _wout_kernel,
        grid=(bsz, n // tm),
        in_specs=[
            pl.BlockSpec((1, tm, rw), lambda b, i: (b, i, 0)),
            pl.BlockSpec((1, tm, d - rw), lambda b, i: (b, i, 0)),
            pl.BlockSpec((rw, d), lambda b, i: (0, 0)),
            pl.BlockSpec((rw, d), lambda b, i: (1, 0)),
            pl.BlockSpec((1, tm, d), lambda b, i: (b, i, 0)),
            pl.BlockSpec((1, 1, d), lambda b, i: (b, 0, 0)),
        ],
        out_specs=pl.BlockSpec((1, tm, d), lambda b, i: (b, i, 0)),
        out_shape=jax.ShapeDtypeStruct((bsz, n, d), F32),
        compiler_params=_params(("arbitrary", "arbitrary")),
        name="mixer_out",
    )(r, yh, w_out, w_out, x, gate)


def _pool_kernel(h_ref, mc_ref, icol_ref, w_ref, sc_ref, x_ref, gate_ref, o_ref, cp_scr, pm_scr,
                 *, n, tm, to):
    g = pl.program_id(1)
    i = pl.program_id(2)
    n_rows = n // GRID_W

    @pl.when(i == 0)
    def _():
        for t in range(n // tm):
            rows = slice(t * tm, (t + 1) * tm)
            cp_scr[rows, :] = jnp.dot(mc_ref[0], h_ref[0, rows, :], preferred_element_type=F32)

    def grid_row(r):
        return slice(r * GRID_W, (r + 1) * GRID_W)

    for gi, win in enumerate(POOL_WINDOWS):
        @pl.when(jnp.logical_and(i == 0, g == gi))
        def _(win=win):
            lo, hi = win // 2, win - 1 - win // 2
            icol = icol_ref[0]
            acc = cp_scr[grid_row(0), :]
            for r in range(1, hi + 1):
                acc = acc + cp_scr[grid_row(r), :]
            for r in range(n_rows):
                if r > 0 and r + hi < n_rows:
                    acc = acc + cp_scr[grid_row(r + hi), :]
                if r - lo - 1 >= 0:
                    acc = acc - cp_scr[grid_row(r - lo - 1), :]
                cnt_r = min(r + hi, n_rows - 1) + 1 - max(r - lo, 0)
                mean = acc * (icol * (1.0 / cnt_r))
                pm_scr[grid_row(r), :] = (mean - h_ref[0, grid_row(r), :].astype(F32)).astype(BF16)

    rows = pl.ds(pl.multiple_of(i * to, to), to)
    y = jnp.dot(pm_scr[rows, :], w_ref[0], preferred_element_type=F32) * sc_ref[...]
    o_ref[0] = x_ref[0] + gate_ref[0] * y


def _pool_tables(tm, pg):
    tok = np.arange(tm)
    r_t, c_t = tok // GRID_W, tok % GRID_W
    mats, inv = [], []
    for win in POOL_WINDOWS:
        lo, hi = -(win // 2), win - 1 - win // 2
        dc = c_t[None, :] - c_t[:, None]
        mats.append(((r_t[None, :] == r_t[:, None]) & (dc >= lo) & (dc <= hi)).astype(np.float32))
        idx_c = np.arange(GRID_W)
        cnt_c = np.minimum(idx_c + hi, GRID_W - 1) + 1 - np.maximum(idx_c + lo, 0)
        inv.append(np.broadcast_to((1.0 / cnt_c)[:, None], (GRID_W, pg)))
    return (jnp.asarray(np.stack(mats), dtype=BF16),
            jnp.asarray(np.stack(inv).astype(np.float32)))


def _pool_call(h, pool_w, pool_scale, x, gate):
    bsz, n, d = x.shape
    ng = len(POOL_WINDOWS)
    pg = d // ng
    tm = _tile(n, ROW_TILE)
    to = _tile(n, WIDE_TILE)
    assert tm % GRID_W == 0 and n % GRID_W == 0
    mats, icol = _pool_tables(tm, pg)
    return pl.pallas_call(
        functools.partial(_pool_kernel, n=n, tm=tm, to=to),
        grid=(bsz, ng, n // to),
        in_specs=[
            pl.BlockSpec((1, n, pg), lambda b, g, i: (b, 0, g)),
            pl.BlockSpec((1, tm, tm), lambda b, g, i: (g, 0, 0)),
            pl.BlockSpec((1, GRID_W, pg), lambda b, g, i: (g, 0, 0)),
            pl.BlockSpec((1, pg, pg), lambda b, g, i: (g, 0, 0)),
            pl.BlockSpec((1, pg), lambda b, g, i: (0, g)),
            pl.BlockSpec((1, to, pg), lambda b, g, i: (b, i, g)),
            pl.BlockSpec((1, 1, pg), lambda b, g, i: (b, 0, g)),
        ],
        out_specs=pl.BlockSpec((1, to, pg), lambda b, g, i: (b, i, g)),
        out_shape=jax.ShapeDtypeStruct((bsz, n, d), F32),
        scratch_shapes=[pltpu.VMEM((n, pg), F32), pltpu.VMEM((n, pg), BF16)],
        compiler_params=_params(("arbitrary", "arbitrary", "arbitrary")),
        name="pool_mixer",
    )(h, mats, icol, pool_w, pool_scale.reshape(1, d), x, gate)


def _rotary_tables(n):
    pos = np.arange(n)
    row = (pos // GRID_W).astype(np.float32)
    col = (pos % GRID_W).astype(np.float32)
    n_freq = HEAD_DIM // 4
    inv = (np.float32(ROPE_BASE) ** (-np.arange(n_freq, dtype=np.float32) / np.float32(n_freq))).astype(np.float32)
    ang = np.concatenate([row[:, None] * inv, col[:, None] * inv], axis=-1).astype(np.float32)
    cos, sin = np.cos(ang.astype(np.float64)), np.sin(ang.astype(np.float64))
    cos2 = np.concatenate([cos, cos], axis=-1).astype(np.float32)
    sin2 = np.concatenate([-sin, sin], axis=-1).astype(np.float32)
    return jnp.asarray(cos2), jnp.asarray(sin2)


def kernel(x, c, ctx, c_ctx, w_mod, b_mod, ffn1_w1, ffn1_w3, ffn1_w2, ffn2_w1, ffn2_w3, ffn2_w2, ab_w_in, ab_w_out, ret_log_decay, hy_conv_w, hy_conv_b, hy_f_w1, hy_f_b1, hy_f_w2, hy_f_b2, hy_f_w3, hy_f_b3, hy_f_freq, hy_f_w4, hy_bias, pool_w, pool_scale, final_gain):
    bsz, n, d = x.shape
    n_ctx = ctx.shape[1]
    depth = w_mod.shape[0]
    assert depth == 2, "layer 0 = retention/Hyena mixer, layer 1 = pooling mixer"
    ret_w = RET_HEADS * HEAD_DIM
    hw = d - ret_w
    assert hy_f_w4.shape[2] == HY_ORDER * 2 * hw and bsz % 2 == 0

    c_all = jnp.concatenate([c, c_ctx[None, :], jnp.zeros((SUBLANE - bsz - 1, d), F32)], axis=0)
    mod = _mod_call(c_all, w_mod, b_mod).reshape(depth, SUBLANE, N_MOD, d)

    def mod_rows(layer, base, count=3):
        return mod[layer, :bsz, base:base + count]

    def mod_ctx(base, count=3):
        return mod[0, bsz:bsz + 1, base:base + count]

    n_f = ffn1_w1.shape[2] // _tile(ffn1_w1.shape[2], FFN_TF)
    w13 = _cast_pairs_call([(ffn1_w1, ffn1_w3), (ffn2_w1, ffn2_w3)], n_f)
    w_in_shape, w_out_shape = ab_w_in.shape[1:], ab_w_out.shape[1:]
    w2a, w2b, w_in, w_out = _cast_call(
        [ffn1_w2, ffn2_w2, ab_w_in[0].reshape(depth, -1, w_in_shape[1]),
         ab_w_out[0].reshape(depth, -1, w_out_shape[1])], CAST_STEPS)
    w_in, w_out = w_in.reshape(w_in_shape), w_out.reshape(w_out_shape)
    ffn1_w, ffn2_w = (w13[0], w2a), (w13[1], w2b)

    x, h = _ffn_call(x, mod_rows(0, 0), *ffn1_w, 0, modn=mod_rows(0, 3))
    _, h_c = _ffn_call(ctx.reshape(1, bsz * n_ctx, d), mod_ctx(0), *ffn1_w, 0, modn=mod_ctx(3))

    p = _matmul_call(h.reshape(bsz * n, d), w_in, BF16, b_cols=(0, 4 * ret_w)).reshape(bsz, n, -1)
    kvc = _matmul_call(h_c.reshape(bsz * n_ctx, d), w_in, BF16, b_cols=(ret_w, 3 * ret_w))
    kvc = kvc.reshape(bsz, n_ctx, 2 * ret_w)

    cos2, sin2 = _rotary_tables(n)
    r = _ret_call(p, kvc, ret_log_decay[0], cos2, sin2, ret_w)

    mcls = n // HY_CLASSES
    h_cm = h.reshape(bsz, mcls, HY_CLASSES, d).transpose(0, 2, 1, 3).reshape(bsz * n, d)
    p_hy = _matmul_call(h_cm, w_in, BF16, b_cols=(4 * ret_w, w_in.shape[1])).reshape(bsz, n, -1)
    hf, hb = _filt_call(n, hy_f_w1[0], hy_f_b1[0], hy_f_w2[0], hy_f_b2[0], hy_f_w3[0],
                        hy_f_b3[0], hy_f_freq[0], hy_f_w4[0], hw)
    tables = _fft_tables(n)
    kre, kim = _spec_call(tables, hf, hb)
    u = _sconv_call(p_hy, hy_conv_w[0], hy_conv_b[0])
    z = _fftconv_call(tables, u, 0, u, hw, kre, kim, 0, hy_bias[0, 0])
    yh = _fftconv_call(tables, z, 0, u, 2 * hw, kre, kim, hw, hy_bias[0, 1])
    yh = yh.reshape(bsz, HY_CLASSES, mcls, hw).transpose(0, 2, 1, 3).reshape(bsz, n, hw)

    x = _wout_call(r, yh, w_out, x, mod_rows(0, 5, 1))
    x = _ffn_call(x, mod_rows(0, 6), *ffn2_w, 0)

    x, h = _ffn_call(x, mod_rows(1, 0), *ffn1_w, 1, modn=mod_rows(1, 3))
    x = _pool_call(h, pool_w[0].astype(BF16), pool_scale[0], x, mod_rows(1, 5, 1))
    return _ffn_call(x, mod_rows(1, 6), *ffn2_w, 1, gain=final_gain)
```

```python
import functools
import math

import numpy as np
import jax
import jax.numpy as jnp
from jax import lax
from jax.experimental import pallas as pl
from jax.experimental.pallas import tpu as pltpu

F32 = jnp.float32
BF16 = jnp.bfloat16
HIGHEST = lax.Precision.HIGHEST

EPS = 1e-6
GRID_W = 64
N_MOD = 9
RET_HEADS = 8
HEAD_DIM = 128
HY_ORDER = 2
HY_SHORT = 3
HY_EMB_DIM = 33
HY_DECAY_SHORT_PCT = 0.3
HY_DECAY_LONG_PCT = 1.5
HY_DECAY_TARGET = 1e-2
ROPE_BASE = 10000.0
POOL_WINDOWS = (2, 4, 8, 16)
RET_CHUNK = 256
RET_UNROLL = 16
HY_CLASSES = 16
LANE = 128
SUBLANE = 8
PACKED_SUBLANE = 16
ROW_TILE = 512
WIDE_TILE = 1024
FFN_TF = 512
FILTER_ROWS = 256
CAST_STEPS = 16
VMEM_LIMIT = 56 * 1024 * 1024


def _tile(dim, pref):
    if dim <= pref:
        return dim
    for t in range(pref - pref % LANE, 0, -LANE):
        if dim % t == 0:
            return t
    raise ValueError((dim, pref))


def _params(sem):
    return pltpu.CompilerParams(dimension_semantics=sem, vmem_limit_bytes=VMEM_LIMIT)


def _rms(x):
    return x * lax.rsqrt(jnp.mean(x * x, axis=-1, keepdims=True) + EPS)


def _silu(x):
    return x * jax.nn.sigmoid(x)


def _mod_kernel(c_ref, w_ref, b_ref, o_ref):
    sc = _silu(c_ref[...])
    w = w_ref[0]
    sc_hi = sc.astype(BF16)
    sc_lo = (sc - sc_hi.astype(F32)).astype(BF16)
    w_hi = w.astype(BF16)
    w_lo = (w - w_hi.astype(F32)).astype(BF16)
    rows = sc.shape[0]
    top = jnp.dot(jnp.concatenate([sc_hi, sc_lo], axis=0), w_hi, preferred_element_type=F32)
    o_ref[0] = (top[0:rows] + top[rows:2 * rows]
                + jnp.dot(sc_hi, w_lo, preferred_element_type=F32) + b_ref[0])


def _mod_call(c_all, w_mod, b_mod):
    depth, d, nd = w_mod.shape
    rows = c_all.shape[0]
    tn = _tile(nd, WIDE_TILE)
    return pl.pallas_call(
        _mod_kernel,
        grid=(depth, nd // tn),
        in_specs=[
            pl.BlockSpec((rows, d), lambda l, j: (0, 0)),
            pl.BlockSpec((1, d, tn), lambda l, j: (l, 0, j)),
            pl.BlockSpec((1, 1, tn), lambda l, j: (l, 0, j)),
        ],
        out_specs=pl.BlockSpec((1, rows, tn), lambda l, j: (l, 0, j)),
        out_shape=jax.ShapeDtypeStruct((depth, rows, nd), F32),
        compiler_params=_params(("arbitrary", "arbitrary")),
        name="mod",
    )(c_all, w_mod, b_mod.reshape(depth, 1, nd))


def _ffn_kernel(*refs, n_f, emit_h, final):
    x_ref, mod_ref, w13_ref, w2_ref = refs[:4]
    pos = 4
    modn_ref = gain_ref = hn_ref = None
    if emit_h:
        modn_ref = refs[pos]
        pos += 1
    if final:
        gain_ref = refs[pos]
        pos += 1
    o_ref = refs[pos]
    pos += 1
    if emit_h:
        hn_ref = refs[pos]
        pos += 1
    h_scr, acc_scr = refs[pos], refs[pos + 1]
    f = pl.program_id(2)

    assert n_f >= 3

    tf = w2_ref.shape[0]

    def partial_out(h):
        ab = jnp.dot(h, w13_ref[...], preferred_element_type=F32)
        g = (_silu(ab[:, 0:tf]) * ab[:, tf:2 * tf]).astype(BF16)
        return jnp.dot(g, w2_ref[...], preferred_element_type=F32)

    @pl.when(f == 0)
    def _():
        m = mod_ref[0]
        h = (_rms(x_ref[0]) * (1.0 + m[1:2]) + m[0:1]).astype(BF16)
        h_scr[...] = h
        acc_scr[...] = partial_out(h)

    @pl.when(jnp.logical_and(f > 0, f < n_f - 1))
    def _():
        acc_scr[...] += partial_out(h_scr[...])

    @pl.when(f == n_f - 1)
    def _():
        m = mod_ref[0]
        xo = x_ref[0] + 0.5 * m[2:3] * (acc_scr[...] + partial_out(h_scr[...]))
        if emit_h:
            mn = modn_ref[0]
            hn_ref[0] = (_rms(xo) * (1.0 + mn[1:2]) + mn[0:1]).astype(BF16)
        if final:
            xo = _rms(xo) * gain_ref[...]
        o_ref[0] = xo


def _ffn_call(x, mod3, w13, w2, layer, modn=None, gain=None):
    bsz, s, d = x.shape
    n_f, tf = w13.shape[1], w13.shape[3] // 2
    tm = _tile(s, ROW_TILE)
    emit_h = modn is not None
    final = gain is not None
    in_specs = [
        pl.BlockSpec((1, tm, d), lambda b, i, f: (b, i, 0)),
        pl.BlockSpec((1, 3, d), lambda b, i, f: (b, 0, 0)),
        pl.BlockSpec((None, None, d, 2 * tf), lambda b, i, f: (layer, f, 0, 0)),
        pl.BlockSpec((None, tf, d), lambda b, i, f: (layer, f, 0)),
    ]
    args = [x, mod3, w13, w2]
    if emit_h:
        in_specs.append(pl.BlockSpec((1, 3, d), lambda b, i, f: (b, 0, 0)))
        args.append(modn)
    if final:
        in_specs.append(pl.BlockSpec((1, d), lambda b, i, f: (0, 0)))
        args.append(gain.reshape(1, d))
    out_specs = [pl.BlockSpec((1, tm, d), lambda b, i, f: (b, i, 0))]
    out_shape = [jax.ShapeDtypeStruct((bsz, s, d), F32)]
    if emit_h:
        out_specs.append(pl.BlockSpec((1, tm, d), lambda b, i, f: (b, i, 0)))
        out_shape.append(jax.ShapeDtypeStruct((bsz, s, d), BF16))
    outs = pl.pallas_call(
        functools.partial(_ffn_kernel, n_f=n_f, emit_h=emit_h, final=final),
        grid=(bsz, s // tm, n_f),
        in_specs=in_specs,
        out_specs=out_specs,
        out_shape=out_shape,
        scratch_shapes=[pltpu.VMEM((tm, d), BF16), pltpu.VMEM((tm, d), F32)],
        compiler_params=_params(("arbitrary", "arbitrary", "arbitrary")),
        name="ffn",
    )(*args)
    return outs if emit_h else outs[0]


def _cast_kernel(*refs):
    half = len(refs) // 2
    for src, dst in zip(refs[:half], refs[half:]):
        dst[...] = src[...].astype(dst.dtype)


def _cast_call(arrays, steps):
    depth = arrays[0].shape[0]
    specs = []
    for a in arrays:
        assert a.shape[0] == depth and a.shape[1] % (PACKED_SUBLANE * steps) == 0
        specs.append(pl.BlockSpec((1, a.shape[1] // steps, a.shape[2]), lambda l, j: (l, j, 0)))
    return pl.pallas_call(
        _cast_kernel,
        grid=(depth, steps),
        in_specs=specs,
        out_specs=specs,
        out_shape=[jax.ShapeDtypeStruct(a.shape, BF16) for a in arrays],
        compiler_params=_params(("arbitrary", "arbitrary")),
        name="cast_weights",
    )(*arrays)


def _cast_pairs_kernel(*refs):
    n_out = len(refs) // 3
    for k in range(n_out):
        wa, wb, dst = refs[2 * k], refs[2 * k + 1], refs[2 * n_out + k]
        tf = wa.shape[2]
        dst[0, 0, :, 0:tf] = wa[0].astype(dst.dtype)
        dst[0, 0, :, tf:2 * tf] = wb[0].astype(dst.dtype)


def _cast_pairs_call(pairs, steps):
    depth, r, c = pairs[0][0].shape
    tf = c // steps
    in_spec = pl.BlockSpec((1, r, tf), lambda l, j: (l, 0, j))
    out_spec = pl.BlockSpec((1, 1, r, 2 * tf), lambda l, j: (l, j, 0, 0))
    return pl.pallas_call(
        _cast_pairs_kernel,
        grid=(depth, steps),
        in_specs=[in_spec] * (2 * len(pairs)),
        out_specs=[out_spec] * len(pairs),
        out_shape=[jax.ShapeDtypeStruct((depth, steps, r, 2 * tf), BF16)] * len(pairs),
        compiler_params=_params(("arbitrary", "arbitrary")),
        name="cast_weight_pairs",
    )(*[w for pair in pairs for w in pair])


def _matmul_kernel(a_ref, b_ref, o_ref):
    o_ref[...] = jnp.dot(a_ref[...], b_ref[...], preferred_element_type=F32).astype(o_ref.dtype)


def _matmul_call(a, b, out_dtype, tm_pref=WIDE_TILE, tn_pref=WIDE_TILE, b_cols=None):
    m, k = a.shape
    c0, c1 = b_cols if b_cols is not None else (0, b.shape[1])
    n = c1 - c0
    tm = _tile(m, tm_pref)
    tn = math.gcd(_tile(n, tn_pref), c0) if c0 else _tile(n, tn_pref)
    assert tn % LANE == 0
    joff = c0 // tn
    return pl.pallas_call(
        _matmul_kernel,
        grid=(m // tm, n // tn),
        in_specs=[
            pl.BlockSpec((tm, k), lambda i, j: (i, 0)),
            pl.BlockSpec((k, tn), lambda i, j: (0, joff + j)),
        ],
        out_specs=pl.BlockSpec((tm, tn), lambda i, j: (i, j)),
        out_shape=jax.ShapeDtypeStruct((m, n), out_dtype),
        compiler_params=_params(("arbitrary", "arbitrary")),
        name="matmul",
    )(a, b)


def _ret_kernel(lg_ref, q_ref, k_ref, v_ref, g_ref, cos_ref, sin_ref, kc_ref, vc_ref, o_ref,
                qr_scr, kr_scr, upd_scr, st_scr, *, n, n_ctx, chunk):
    hd = HEAD_DIM
    head = pl.program_id(1)
    lgf = lg_ref[0, head]
    lgb = lg_ref[1, head]
    nc = n // chunk
    k_scale = hd ** -0.5

    posc = lax.broadcasted_iota(jnp.int32, (n_ctx, hd), 0).astype(F32)
    kc = kc_ref[0].astype(F32) * k_scale
    vc = vc_ref[0]
    kcf = (kc * jnp.exp((n_ctx - 1.0 - posc) * lgf)).T.astype(BF16)
    kcb = (kc * jnp.exp(posc * lgb)).T.astype(BF16)
    s0_f = jnp.dot(kcf, vc, preferred_element_type=F32)
    s0_b = jnp.dot(kcb, vc, preferred_element_type=F32)

    pos = lax.broadcasted_iota(jnp.int32, (chunk, hd), 0).astype(F32)
    qdec_f = jnp.exp((pos + 1.0) * lgf)
    kdec_f = jnp.exp((chunk - 1.0 - pos) * lgf)
    qdec_b = jnp.exp((chunk - pos) * lgb)
    kdec_b = jnp.exp(pos * lgb)
    cdec_f = jnp.exp(jnp.full((1, hd), chunk, F32) * lgf)
    cdec_b = jnp.exp(jnp.full((1, hd), chunk, F32) * lgb)
    ri = lax.broadcasted_iota(jnp.int32, (chunk, chunk), 0)
    ci = lax.broadcasted_iota(jnp.int32, (chunk, chunk), 1)
    diff = (ri - ci).astype(F32)
    dmat = (jnp.where(diff >= 0, jnp.exp(jnp.maximum(diff, 0.0) * lgf), 0.0)
            + jnp.where(diff <= 0, jnp.exp(jnp.maximum(-diff, 0.0) * lgb), 0.0))

    def rotary(x, rows):
        return x * cos_ref[rows, :] + pltpu.roll(x, hd // 2, 1) * sin_ref[rows, :]

    def chunk_rows(c):
        return pl.ds(pl.multiple_of(c * chunk, chunk), chunk)

    def prep_body(c, carry):
        rows = chunk_rows(c)
        qr_scr[rows, :] = rotary(q_ref[0, rows, :].astype(F32), rows)
        k = rotary(k_ref[0, rows, :].astype(F32), rows) * k_scale
        kr_scr[rows, :] = k
        kk = jnp.concatenate([k * kdec_f, k * kdec_b], axis=1).T.astype(BF16)
        upd_scr[c] = jnp.dot(kk, v_ref[0, rows, :], preferred_element_type=F32)
        return carry

    lax.fori_loop(0, nc, prep_body, 0, unroll=math.gcd(nc, RET_UNROLL))

    def fwd_scan(c, sf):
        st_scr[c, 0:hd, :] = sf.astype(BF16)
        return cdec_f * sf + upd_scr[c, 0:hd, :]

    def bwd_scan(j, sb):
        c = nc - 1 - j
        st_scr[c, hd:2 * hd, :] = sb.astype(BF16)
        return cdec_b * sb + upd_scr[c, hd:2 * hd, :]

    lax.fori_loop(0, nc, fwd_scan, s0_f, unroll=math.gcd(nc, RET_UNROLL))
    lax.fori_loop(0, nc, bwd_scan, s0_b, unroll=math.gcd(nc, RET_UNROLL))

    def out_body(c, carry):
        rows = chunk_rows(c)
        q = qr_scr[rows, :]
        s = lax.dot_general(q.astype(BF16), kr_scr[rows, :].astype(BF16), (((1,), (1,)), ((), ())),
                            preferred_element_type=F32) * dmat
        qq = jnp.concatenate([q * qdec_f, q * qdec_b], axis=1).astype(BF16)
        o = (jnp.dot(s.astype(BF16), v_ref[0, rows, :], preferred_element_type=F32)
             + jnp.dot(qq, st_scr[c], preferred_element_type=F32))
        g = g_ref[0, rows, :].astype(F32)
        o_ref[0, rows, :] = (_rms(o) * _silu(g)).astype(o_ref.dtype)
        return carry

    lax.fori_loop(0, nc, out_body, 0, unroll=math.gcd(nc, RET_UNROLL))


def _ret_call(p, kvc, log_decay, cos2, sin2, ret_w):
    bsz, n, _ = p.shape
    n_ctx = kvc.shape[1]
    heads = ret_w // HEAD_DIM
    hd = HEAD_DIM
    chunk = _tile(n, RET_CHUNK)
    kern = functools.partial(_ret_kernel, n=n, n_ctx=n_ctx, chunk=chunk)
    col = lambda off: (lambda b, h, lg: (b, 0, off * heads + h))
    return pl.pallas_call(
        kern,
        grid_spec=pltpu.PrefetchScalarGridSpec(
            num_scalar_prefetch=1,
            grid=(bsz, heads),
            in_specs=[
                pl.BlockSpec((1, n, hd), col(0)),
                pl.BlockSpec((1, n, hd), col(1)),
                pl.BlockSpec((1, n, hd), col(2)),
                pl.BlockSpec((1, n, hd), col(3)),
                pl.BlockSpec((n, hd), lambda b, h, lg: (0, 0)),
                pl.BlockSpec((n, hd), lambda b, h, lg: (0, 0)),
                pl.BlockSpec((1, n_ctx, hd), col(0)),
                pl.BlockSpec((1, n_ctx, hd), col(1)),
            ],
            out_specs=pl.BlockSpec((1, n, hd), lambda b, h, lg: (b, 0, h)),
            scratch_shapes=[
                pltpu.VMEM((n, hd), F32),
                pltpu.VMEM((n, hd), F32),
                pltpu.VMEM((n // chunk, 2 * hd, hd), F32),
                pltpu.VMEM((n // chunk, 2 * hd, hd), BF16),
            ],
        ),
        out_shape=jax.ShapeDtypeStruct((bsz, n, ret_w), BF16),
        compiler_params=_params(("arbitrary", "arbitrary")),
        name="retention",
    )(log_decay, p, p, p, p, cos2, sin2, kvc, kvc)


def _cmul(a, b):
    return a[0] * b[0] - a[1] * b[1], a[0] * b[1] + a[1] * b[0]


def _cmul_const(a, c, s):
    eps = 1e-12
    if abs(s) < eps:
        return (a[0], a[1]) if c > 0 else (-a[0], -a[1])
    if abs(c) < eps:
        return (-a[1], a[0]) if s > 0 else (a[1], -a[0])
    return a[0] * c - a[1] * s, a[0] * s + a[1] * c


def _fft_list(xs, sign):
    size = len(xs)
    if size == 1:
        return xs
    even = _fft_list(xs[0::2], sign)
    odd = _fft_list(xs[1::2], sign)
    out = [None] * size
    for k in range(size // 2):
        ang = sign * 2.0 * math.pi * k / size
        tr, ti = _cmul_const(odd[k], math.cos(ang), math.sin(ang))
        out[k] = (even[k][0] + tr, even[k][1] + ti)
        out[k + size // 2] = (even[k][0] - tr, even[k][1] - ti)
    return out


def _fft_tables(n):
    j_cls = HY_CLASSES
    k0n, mcls = 2 * n // j_cls, n // j_cls
    t = j_cls * np.arange(mcls)[None, None, :] + np.arange(j_cls)[:, None, None]
    ang = 2.0 * math.pi * ((np.arange(k0n)[None, :, None] * t) % (2 * n)) / (2 * n)
    cs = np.concatenate([np.cos(ang), np.sin(ang)], axis=1)
    cst = np.concatenate([np.cos(ang).transpose(0, 2, 1), np.sin(ang).transpose(0, 2, 1)], axis=1)
    return jnp.asarray(cs, dtype=BF16), jnp.asarray(cst / (2 * n), dtype=BF16)


def _filt_kernel(z_ref, w1_ref, b1_ref, w2_ref, b2_ref, w3_ref, b3_ref, fr_ref, w4_ref, dl_ref,
                 hf_ref, hb_ref, *, n, tt, hw):
    i = pl.program_id(0)
    dot = functools.partial(jnp.dot, preferred_element_type=F32, precision=HIGHEST)
    fr = fr_ref[...]
    a = jnp.sin(fr[0:1] * (dot(z_ref[...], w1_ref[...]) + b1_ref[...]))
    a = jnp.sin(fr[1:2] * (dot(a, w2_ref[...]) + b2_ref[...]))
    a = jnp.sin(fr[2:3] * (dot(a, w3_ref[...]) + b3_ref[...]))
    hf = jnp.dot(a.astype(BF16), w4_ref[...], preferred_element_type=F32)
    mcls = n // HY_CLASSES
    rho = lax.broadcasted_iota(jnp.int32, (tt, hw), 0) + i * tt
    ti = HY_CLASSES * (rho & (mcls - 1)) + lax.shift_right_logical(rho, int(math.log2(mcls)))
    t = ti.astype(F32) * (1.0 / (n - 1.0))
    window = jnp.exp(-t * dl_ref[...])
    for o in range(HY_ORDER):
        hf_ref[:, o * hw:(o + 1) * hw] = (hf[:, (2 * o) * hw:(2 * o + 1) * hw] * window).astype(BF16)
        h_b = jnp.where(ti == 0, 0.0, hf[:, (2 * o + 1) * hw:(2 * o + 2) * hw] * window)
        hb_ref[:, o * hw:(o + 1) * hw] = h_b.astype(BF16)


def _filt_call(n, fw1, fb1, fw2, fb2, fw3, fb3, freq, fw4, hw):
    hid = fw1.shape[1]
    pk = LANE
    mcls = n // HY_CLASSES
    assert mcls & (mcls - 1) == 0
    bands = (HY_EMB_DIM - 1) // 2
    tpos = np.linspace(0.0, 1.0, n, dtype=np.float32)[:, None].astype(np.float64)
    fb = np.linspace(1e-4, bands - 1, bands, dtype=np.float32)[None, :].astype(np.float64)
    wv = (2.0 * math.pi * np.arange(n, dtype=np.float64)[:, None] / n)
    z = np.concatenate([tpos, np.cos(fb * wv), -np.sin(fb * wv)], axis=-1)
    rho = np.arange(n)
    t_of_row = HY_CLASSES * (rho % mcls) + rho // mcls
    z = np.pad(z, ((0, 0), (0, pk - HY_EMB_DIM))).astype(np.float32)[t_of_row]
    max_decay = math.log(HY_DECAY_TARGET) / HY_DECAY_SHORT_PCT
    min_decay = math.log(HY_DECAY_TARGET) / HY_DECAY_LONG_PCT
    deltas = np.abs(np.linspace(min_decay, max_decay, hw, dtype=np.float32)).reshape(1, hw)

    padc = lambda w: jnp.pad(w, ((0, 0), (0, pk - hid)))
    w1 = jnp.pad(fw1, ((0, pk - HY_EMB_DIM), (0, pk - hid)))
    w2 = jnp.pad(fw2, ((0, pk - hid), (0, pk - hid)))
    w3 = jnp.pad(fw3, ((0, pk - hid), (0, pk - hid)))
    w4 = jnp.pad(fw4, ((0, pk - hid), (0, 0))).astype(BF16)
    b1, b2, b3 = (padc(b.reshape(1, hid)) for b in (fb1, fb2, fb3))
    fr = jnp.pad(freq, ((0, SUBLANE - freq.shape[0]), (0, pk - hid)))
    tt = _tile(n, FILTER_ROWS)
    nw = HY_ORDER * hw
    full = lambda shape: pl.BlockSpec(shape, lambda i: (0, 0))
    return pl.pallas_call(
        functools.partial(_filt_kernel, n=n, tt=tt, hw=hw),
        grid=(n // tt,),
        in_specs=[
            pl.BlockSpec((tt, pk), lambda i: (i, 0)),
            full((pk, pk)), full((1, pk)), full((pk, pk)), full((1, pk)),
            full((pk, pk)), full((1, pk)), full((SUBLANE, pk)), full((pk, 2 * nw)), full((1, hw)),
        ],
        out_specs=[pl.BlockSpec((tt, nw), lambda i: (i, 0))] * 2,
        out_shape=[jax.ShapeDtypeStruct((n, nw), BF16)] * 2,
        compiler_params=_params(("arbitrary",)),
        name="hyena_filters",
    )(jnp.asarray(z), w1, b1, w2, b2, w3, b3, fr, w4, jnp.asarray(deltas))


def _sconv_kernel(p_ref, w_ref, b_ref, o_ref, *, n):
    w = w_ref[...]
    bias = b_ref[...]
    mcls = n // HY_CLASSES
    row = lax.broadcasted_iota(jnp.int32, (mcls, p_ref.shape[2]), 0)

    def cls(j):
        return p_ref[0, j * mcls:(j + 1) * mcls, :].astype(F32)

    for j in range(HY_CLASSES):
        cur = cls(j)
        if j > 0:
            prev = cls(j - 1)
        else:
            prev = jnp.where(row == 0, 0.0, pltpu.roll(cls(HY_CLASSES - 1), 1, 0))
        if j < HY_CLASSES - 1:
            nxt = cls(j + 1)
        else:
            nxt = jnp.where(row == mcls - 1, 0.0, pltpu.roll(cls(0), mcls - 1, 0))
        out = bias + prev * w[0:1] + cur * w[1:2] + nxt * w[2:3]
        o_ref[0, j * mcls:(j + 1) * mcls, :] = out.astype(o_ref.dtype)


def _sconv_call(p, conv_w, conv_b):
    bsz, n, width = p.shape
    tc = _tile(width, ROW_TILE)
    return pl.pallas_call(
        functools.partial(_sconv_kernel, n=n),
        grid=(bsz, width // tc),
        in_specs=[
            pl.BlockSpec((1, n, tc), lambda b, j: (b, 0, j)),
            pl.BlockSpec((HY_SHORT, tc), lambda b, j: (0, j)),
            pl.BlockSpec((1, tc), lambda b, j: (0, j)),
        ],
        out_specs=pl.BlockSpec((1, n, tc), lambda b, j: (b, 0, j)),
        out_shape=jax.ShapeDtypeStruct((bsz, n, width), BF16),
        compiler_params=_params(("arbitrary", "arbitrary")),
        name="short_conv",
    )(p, conv_w, conv_b.reshape(1, width))


def _spec_kernel(cs_ref, hf_ref, hb_ref, kre_ref, kim_ref, wf_scr, wb_scr,
                 *, k0n, mcls):
    j_cls = HY_CLASSES
    tn = hf_ref.shape[1]
    for j in range(j_cls):
        rows = slice(j * mcls, (j + 1) * mcls)
        prod = jnp.dot(cs_ref[j], jnp.concatenate([hf_ref[rows, :], hb_ref[rows, :]], axis=1),
                       preferred_element_type=F32)
        wf_scr[j, 0:k0n, :] = prod[0:k0n, 0:tn]
        wf_scr[j, k0n:2 * k0n, :] = -prod[k0n:2 * k0n, 0:tn]
        wb_scr[j, 0:k0n, :] = prod[0:k0n, tn:2 * tn]
        wb_scr[j, k0n:2 * k0n, :] = -prod[k0n:2 * k0n, tn:2 * tn]

    def body(c, carry):
        halves = []
        for half in range(PACKED_SUBLANE // SUBLANE):
            r8 = pl.ds(pl.multiple_of(c * PACKED_SUBLANE + half * SUBLANE, SUBLANE), SUBLANE)
            i8 = pl.ds(pl.multiple_of(k0n + c * PACKED_SUBLANE + half * SUBLANE, SUBLANE), SUBLANE)
            xs = [(wf_scr[j, r8, :], wf_scr[j, i8, :]) for j in range(j_cls)]
            ys = [(wb_scr[j, r8, :], wb_scr[j, i8, :]) for j in range(j_cls)]
            halves.append((_fft_list(xs, -1), _fft_list(ys, -1)))
        for q in range(j_cls):
            rows = pl.ds(pl.multiple_of(q * k0n + c * PACKED_SUBLANE, PACKED_SUBLANE), PACKED_SUBLANE)
            kre = jnp.concatenate([h[0][q][0] + h[1][q][0] for h in halves], axis=0)
            kim = jnp.concatenate([h[0][q][1] - h[1][q][1] for h in halves], axis=0)
            kre_ref[rows, :] = kre.astype(kre_ref.dtype)
            kim_ref[rows, :] = kim.astype(kim_ref.dtype)
        return carry

    lax.fori_loop(0, k0n // PACKED_SUBLANE, body, 0)


def _spec_call(tables, hf, hb):
    cs, _ = tables
    n, width = hf.shape
    tn = LANE
    k0n, mcls = cs.shape[1] // 2, cs.shape[2]
    const = lambda shape: pl.BlockSpec(shape, lambda j: (0,) * len(shape),
                                       pipeline_mode=pl.Buffered(1))
    return pl.pallas_call(
        functools.partial(_spec_kernel, k0n=k0n, mcls=mcls),
        grid=(width // tn,),
        in_specs=[
            const(cs.shape),
            pl.BlockSpec((n, tn), lambda j: (0, j)),
            pl.BlockSpec((n, tn), lambda j: (0, j)),
        ],
        out_specs=[pl.BlockSpec((2 * n, tn), lambda j: (0, j))] * 2,
        out_shape=[jax.ShapeDtypeStruct((2 * n, width), BF16)] * 2,
        scratch_shapes=[pltpu.VMEM((HY_CLASSES, 2 * k0n, tn), F32)] * 2,
        compiler_params=_params(("arbitrary",)),
        name="filter_spectrum",
    )(cs, hf, hb)


def _fftconv_kernel(cs_ref, cst_ref, z_ref, g_ref, kre_ref, kim_ref, bias_ref,
                    o_ref, w_scr, *, k0n, mcls):
    j_cls = HY_CLASSES
    tn = o_ref.shape[3]
    for j in range(j_cls):
        rows = slice(j * mcls, (j + 1) * mcls)
        prod = jnp.dot(cs_ref[j], jnp.concatenate([z_ref[0, 0, rows, :], z_ref[0, 1, rows, :]], axis=1),
                       preferred_element_type=F32)
        w_scr[j, 0:k0n, :] = prod[0:k0n, 0:tn] + prod[k0n:2 * k0n, tn:2 * tn]
        w_scr[j, k0n:2 * k0n, :] = prod[0:k0n, tn:2 * tn] - prod[k0n:2 * k0n, 0:tn]

    def body(c, carry):
        for half in range(PACKED_SUBLANE // SUBLANE):
            r8 = pl.ds(pl.multiple_of(c * PACKED_SUBLANE + half * SUBLANE, SUBLANE), SUBLANE)
            i8 = pl.ds(pl.multiple_of(k0n + c * PACKED_SUBLANE + half * SUBLANE, SUBLANE), SUBLANE)
            xs = [(w_scr[j, r8, :], w_scr[j, i8, :]) for j in range(j_cls)]
            spec = _fft_list(xs, -1)
            prod = []
            for q in range(j_cls):
                rows = pl.ds(pl.multiple_of(q * k0n + c * PACKED_SUBLANE, PACKED_SUBLANE), PACKED_SUBLANE)
                part = slice(half * SUBLANE, (half + 1) * SUBLANE)
                kq = (kre_ref[rows, :].astype(F32)[part], kim_ref[rows, :].astype(F32)[part])
                prod.append(_cmul(spec[q], kq))
            back = _fft_list(prod, 1)
            for j in range(j_cls):
                w_scr[j, r8, :] = back[j][0]
                w_scr[j, i8, :] = back[j][1]
        return carry

    lax.fori_loop(0, k0n // PACKED_SUBLANE, body, 0)

    bias = bias_ref[...]
    for j in range(j_cls):
        rows = slice(j * mcls, (j + 1) * mcls)
        v2 = jnp.concatenate([w_scr[j, 0:k0n, :], w_scr[j, k0n:2 * k0n, :]], axis=1).astype(BF16)
        prod = jnp.dot(cst_ref[j], v2, preferred_element_type=F32)
        ys = (prod[0:mcls, 0:tn] - prod[mcls:2 * mcls, tn:2 * tn],
              prod[mcls:2 * mcls, 0:tn] + prod[0:mcls, tn:2 * tn])
        for b2 in range(2):
            conv = ys[b2] + z_ref[0, b2, rows, :].astype(F32) * bias
            o_ref[0, b2, rows, :] = (g_ref[0, b2, rows, :].astype(F32) * conv).astype(o_ref.dtype)


def _fftconv_call(tables, z, z_col0, gate, gate_col0, kre, kim, k_col0, bias):
    cs, cst = tables
    bsz, n, _ = z.shape
    hw = bias.shape[0]
    tn = LANE
    k0n, mcls = cs.shape[1] // 2, cs.shape[2]
    zo, go, ko = z_col0 // tn, gate_col0 // tn, k_col0 // tn
    z4 = z.reshape(bsz // 2, 2, n, z.shape[2])
    g4 = gate.reshape(bsz // 2, 2, n, gate.shape[2])
    const = lambda shape: pl.BlockSpec(shape, lambda j, p: (0,) * len(shape),
                                       pipeline_mode=pl.Buffered(1))
    out = pl.pallas_call(
        functools.partial(_fftconv_kernel, k0n=k0n, mcls=mcls),
        grid=(hw // tn, bsz // 2),
        in_specs=[
            const(cs.shape), const(cst.shape),
            pl.BlockSpec((1, 2, n, tn), lambda j, p: (p, 0, 0, zo + j)),
            pl.BlockSpec((1, 2, n, tn), lambda j, p: (p, 0, 0, go + j)),
            pl.BlockSpec((2 * n, tn), lambda j, p: (0, ko + j)),
            pl.BlockSpec((2 * n, tn), lambda j, p: (0, ko + j)),
            pl.BlockSpec((1, tn), lambda j, p: (0, j)),
        ],
        out_specs=pl.BlockSpec((1, 2, n, tn), lambda j, p: (p, 0, 0, j)),
        out_shape=jax.ShapeDtypeStruct((bsz // 2, 2, n, hw), BF16),
        scratch_shapes=[pltpu.VMEM((HY_CLASSES, 2 * k0n, tn), F32)],
        compiler_params=_params(("arbitrary", "arbitrary")),
        name="fft_conv",
    )(cs, cst, z4, g4, kre, kim, bias.reshape(1, hw))
    return out.reshape(bsz, n, hw)


def _wout_kernel(r_ref, y_ref, wa_ref, wb_ref, x_ref, gate_ref, o_ref):
    y = (jnp.dot(r_ref[0], wa_ref[...], preferred_element_type=F32)
         + jnp.dot(y_ref[0], wb_ref[...], preferred_element_type=F32))
    o_ref[0] = x_ref[0] + gate_ref[0] * y


def _wout_call(r, yh, w_out, x, gate):
    bsz, n, d = x.shape
    rw = r.shape[2]
    assert d == 2 * rw and yh.shape[2] == rw
    tm = _tile(n, ROW_TILE)
    return pl.pallas_call(
        _wout_kernel,
        grid=(bsz, n // tm),
        in_specs=[
            pl.BlockSpec((1, tm, rw), lambda b, i: (b, i, 0)),
            pl.BlockSpec((1, tm, d - rw), lambda b, i: (b, i, 0)),
            pl.BlockSpec((rw, d), lambda b, i: (0, 0)),
            pl.BlockSpec((rw, d), lambda b, i: (1, 0)),
            pl.BlockSpec((1, tm, d), lambda b, i: (b, i, 0)),
            pl.BlockSpec((1, 1, d), lambda b, i: (b, 0, 0)),
        ],
        out_specs=pl.BlockSpec((1, tm, d), lambda b, i: (b, i, 0)),
        out_shape=jax.ShapeDtypeStruct((bsz, n, d), F32),
        compiler_params=_params(("arbitrary", "arbitrary")),
        name="mixer_out",
    )(r, yh, w_out, w_out, x, gate)


def _pool_kernel(h_ref, mc_ref, icol_ref, w_ref, sc_ref, x_ref, gate_ref, o_ref, cp_scr, pm_scr,
                 *, n, tm, to):
    g = pl.program_id(1)
    i = pl.program_id(2)
    n_rows = n // GRID_W

    @pl.when(i == 0)
    def _():
        for t in range(n // tm):
            rows = slice(t * tm, (t + 1) * tm)
            cp_scr[rows, :] = jnp.dot(mc_ref[0], h_ref[0, rows, :], preferred_element_type=F32)

    def grid_row(r):
        return slice(r * GRID_W, (r + 1) * GRID_W)

    for gi, win in enumerate(POOL_WINDOWS):
        @pl.when(jnp.logical_and(i == 0, g == gi))
        def _(win=win):
            lo, hi = win // 2, win - 1 - win // 2
            icol = icol_ref[0]
            acc = cp_scr[grid_row(0), :]
            for r in range(1, hi + 1):
                acc = acc + cp_scr[grid_row(r), :]
            for r in range(n_rows):
                if r > 0 and r + hi < n_rows:
                    acc = acc + cp_scr[grid_row(r + hi), :]
                if r - lo - 1 >= 0:
                    acc = acc - cp_scr[grid_row(r - lo - 1), :]
                cnt_r = min(r + hi, n_rows - 1) + 1 - max(r - lo, 0)
                mean = acc * (icol * (1.0 / cnt_r))
                pm_scr[grid_row(r), :] = (mean - h_ref[0, grid_row(r), :].astype(F32)).astype(BF16)

    rows = pl.ds(pl.multiple_of(i * to, to), to)
    y = jnp.dot(pm_scr[rows, :], w_ref[0], preferred_element_type=F32) * sc_ref[...]
    o_ref[0] = x_ref[0] + gate_ref[0] * y


def _pool_tables(tm, pg):
    tok = np.arange(tm)
    r_t, c_t = tok // GRID_W, tok % GRID_W
    mats, inv = [], []
    for win in POOL_WINDOWS:
        lo, hi = -(win // 2), win - 1 - win // 2
        dc = c_t[None, :] - c_t[:, None]
        mats.append(((r_t[None, :] == r_t[:, None]) & (dc >= lo) & (dc <= hi)).astype(np.float32))
        idx_c = np.arange(GRID_W)
        cnt_c = np.minimum(idx_c + hi, GRID_W - 1) + 1 - np.maximum(idx_c + lo, 0)
        inv.append(np.broadcast_to((1.0 / cnt_c)[:, None], (GRID_W, pg)))
    return (jnp.asarray(np.stack(mats), dtype=BF16),
            jnp.asarray(np.stack(inv).astype(np.float32)))


def _pool_call(h, pool_w, pool_scale, x, gate):
    bsz, n, d = x.shape
    ng = len(POOL_WINDOWS)
    pg = d // ng
    tm = _tile(n, ROW_TILE)
    to = _tile(n, WIDE_TILE)
    assert tm % GRID_W == 0 and n % GRID_W == 0
    mats, icol = _pool_tables(tm, pg)
    return pl.pallas_call(
        functools.partial(_pool_kernel, n=n, tm=tm, to=to),
        grid=(bsz, ng, n // to),
        in_specs=[
            pl.BlockSpec((1, n, pg), lambda b, g, i: (b, 0, g)),
            pl.BlockSpec((1, tm, tm), lambda b, g, i: (g, 0, 0)),
            pl.BlockSpec((1, GRID_W, pg), lambda b, g, i: (g, 0, 0)),
            pl.BlockSpec((1, pg, pg), lambda b, g, i: (g, 0, 0)),
            pl.BlockSpec((1, pg), lambda b, g, i: (0, g)),
            pl.BlockSpec((1, to, pg), lambda b, g, i: (b, i, g)),
            pl.BlockSpec((1, 1, pg), lambda b, g, i: (b, 0, g)),
        ],
        out_specs=pl.BlockSpec((1, to, pg), lambda b, g, i: (b, i, g)),
        out_shape=jax.ShapeDtypeStruct((bsz, n, d), F32),
        scratch_shapes=[pltpu.VMEM((n, pg), F32), pltpu.VMEM((n, pg), BF16)],
        compiler_params=_params(("arbitrary", "arbitrary", "arbitrary")),
        name="pool_mixer",
    )(h, mats, icol, pool_w, pool_scale.reshape(1, d), x, gate)


def _rotary_tables(n):
    pos = np.arange(n)
    row = (pos // GRID_W).astype(np.float32)
    col = (pos % GRID_W).astype(np.float32)
    n_freq = HEAD_DIM // 4
    inv = (np.float32(ROPE_BASE) ** (-np.arange(n_freq, dtype=np.float32) / np.float32(n_freq))).astype(np.float32)
    ang = np.concatenate([row[:, None] * inv, col[:, None] * inv], axis=-1).astype(np.float32)
    cos, sin = np.cos(ang.astype(np.float64)), np.sin(ang.astype(np.float64))
    cos2 = np.concatenate([cos, cos], axis=-1).astype(np.float32)
    sin2 = np.concatenate([-sin, sin], axis=-1).astype(np.float32)
    return jnp.asarray(cos2), jnp.asarray(sin2)


def kernel(x, c, ctx, c_ctx, w_mod, b_mod, ffn1_w1, ffn1_w3, ffn1_w2, ffn2_w1, ffn2_w3, ffn2_w2, ab_w_in, ab_w_out, ret_log_decay, hy_conv_w, hy_conv_b, hy_f_w1, hy_f_b1, hy_f_w2, hy_f_b2, hy_f_w3, hy_f_b3, hy_f_freq, hy_f_w4, hy_bias, pool_w, pool_scale, final_gain):
    bsz, n, d = x.shape
    n_ctx = ctx.shape[1]
    depth = w_mod.shape[0]
    assert depth == 2, "layer 0 = retention/Hyena mixer, layer 1 = pooling mixer"
    ret_w = RET_HEADS * HEAD_DIM
    hw = d - ret_w
    assert hy_f_w4.shape[2] == HY_ORDER * 2 * hw and bsz % 2 == 0

    c_all = jnp.concatenate([c, c_ctx[None, :], jnp.zeros((SUBLANE - bsz - 1, d), F32)], axis=0)
    mod = _mod_call(c_all, w_mod, b_mod).reshape(depth, SUBLANE, N_MOD, d)

    def mod_rows(layer, base, count=3):
        return mod[layer, :bsz, base:base + count]

    def mod_ctx(base, count=3):
        return mod[0, bsz:bsz + 1, base:base + count]

    n_f = ffn1_w1.shape[2] // _tile(ffn1_w1.shape[2], FFN_TF)
    w13 = _cast_pairs_call([(ffn1_w1, ffn1_w3), (ffn2_w1, ffn2_w3)], n_f)
    w_in_shape, w_out_shape = ab_w_in.shape[1:], ab_w_out.shape[1:]
    w2a, w2b, w_in, w_out = _cast_call(
        [ffn1_w2, ffn2_w2, ab_w_in[0].reshape(depth, -1, w_in_shape[1]),
         ab_w_out[0].reshape(depth, -1, w_out_shape[1])], CAST_STEPS)
    w_in, w_out = w_in.reshape(w_in_shape), w_out.reshape(w_out_shape)
    ffn1_w, ffn2_w = (w13[0], w2a), (w13[1], w2b)

    x, h = _ffn_call(x, mod_rows(0, 0), *ffn1_w, 0, modn=mod_rows(0, 3))
    _, h_c = _ffn_call(ctx.reshape(1, bsz * n_ctx, d), mod_ctx(0), *ffn1_w, 0, modn=mod_ctx(3))

    p = _matmul_call(h.reshape(bsz * n, d), w_in, BF16, b_cols=(0, 4 * ret_w)).reshape(bsz, n, -1)
    kvc = _matmul_call(h_c.reshape(bsz * n_ctx, d), w_in, BF16, b_cols=(ret_w, 3 * ret_w))
    kvc = kvc.reshape(bsz, n_ctx, 2 * ret_w)

    cos2, sin2 = _rotary_tables(n)
    r = _ret_call(p, kvc, ret_log_decay[0], cos2, sin2, ret_w)

    mcls = n // HY_CLASSES
    h_cm = h.reshape(bsz, mcls, HY_CLASSES, d).transpose(0, 2, 1, 3).reshape(bsz * n, d)
    p_hy = _matmul_call(h_cm, w_in, BF16, b_cols=(4 * ret_w, w_in.shape[1])).reshape(bsz, n, -1)
    hf, hb = _filt_call(n, hy_f_w1[0], hy_f_b1[0], hy_f_w2[0], hy_f_b2[0], hy_f_w3[0],
                        hy_f_b3[0], hy_f_freq[0], hy_f_w4[0], hw)
    tables = _fft_tables(n)
    kre, kim = _spec_call(tables, hf, hb)
    u = _sconv_call(p_hy, hy_conv_w[0], hy_conv_b[0])
    z = _fftconv_call(tables, u, 0, u, hw, kre, kim, 0, hy_bias[0, 0])
    yh = _fftconv_call(tables, z, 0, u, 2 * hw, kre, kim, hw, hy_bias[0, 1])
    yh = yh.reshape(bsz, HY_CLASSES, mcls, hw).transpose(0, 2, 1, 3).reshape(bsz, n, hw)

    x = _wout_call(r, yh, w_out, x, mod_rows(0, 5, 1))
    x = _ffn_call(x, mod_rows(0, 6), *ffn2_w, 0)

    x, h = _ffn_call(x, mod_rows(1, 0), *ffn1_w, 1, modn=mod_rows(1, 3))
    x = _pool_call(h, pool_w[0].astype(BF16), pool_scale[0], x, mod_rows(1, 5, 1))
    return _ffn_call(x, mod_rows(1, 6), *ffn2_w, 1, gain=final_gain)
```

```python
import functools
import math

import numpy as np
import jax
import jax.numpy as jnp
from jax import lax
from jax.experimental import pallas as pl
from jax.experimental.pallas import tpu as pltpu

F32 = jnp.float32
BF16 = jnp.bfloat16
HIGHEST = lax.Precision.HIGHEST

EPS = 1e-6
GRID_W = 64
N_MOD = 9
RET_HEADS = 8
HEAD_DIM = 128
HY_ORDER = 2
HY_SHORT = 3
HY_EMB_DIM = 33
HY_DECAY_SHORT_PCT = 0.3
HY_DECAY_LONG_PCT = 1.5
HY_DECAY_TARGET = 1e-2
ROPE_BASE = 10000.0
POOL_WINDOWS = (2, 4, 8, 16)
RET_CHUNK = 256
RET_UNROLL = 16
HY_CLASSES = 16
LANE = 128
SUBLANE = 8
PACKED_SUBLANE = 16
ROW_TILE = 512
WIDE_TILE = 1024
FFN_TF = 512
FILTER_ROWS = 256
CAST_STEPS = 16
VMEM_LIMIT = 56 * 1024 * 1024


def _tile(dim, pref):
    if dim <= pref:
        return dim
    for t in range(pref - pref % LANE, 0, -LANE):
        if dim % t == 0:
            return t
    raise ValueError((dim, pref))


def _params(sem):
    return pltpu.CompilerParams(dimension_semantics=sem, vmem_limit_bytes=VMEM_LIMIT)


def _rms(x):
    return x * lax.rsqrt(jnp.mean(x * x, axis=-1, keepdims=True) + EPS)


def _silu(x):
    return x * jax.nn.sigmoid(x)


def _mod_kernel(c_ref, w_ref, b_ref, o_ref):
    sc = _silu(c_ref[...])
    w = w_ref[0]
    sc_hi = sc.astype(BF16)
    sc_lo = (sc - sc_hi.astype(F32)).astype(BF16)
    w_hi = w.astype(BF16)
    w_lo = (w - w_hi.astype(F32)).astype(BF16)
    rows = sc.shape[0]
    top = jnp.dot(jnp.concatenate([sc_hi, sc_lo], axis=0), w_hi, preferred_element_type=F32)
    o_ref[0] = (top[0:rows] + top[rows:2 * rows]
                + jnp.dot(sc_hi, w_lo, preferred_element_type=F32) + b_ref[0])


def _mod_call(c_all, w_mod, b_mod):
    depth, d, nd = w_mod.shape
    rows = c_all.shape[0]
    tn = _tile(nd, WIDE_TILE)
    return pl.pallas_call(
        _mod_kernel,
        grid=(depth, nd // tn),
        in_specs=[
            pl.BlockSpec((rows, d), lambda l, j: (0, 0)),
            pl.BlockSpec((1, d, tn), lambda l, j: (l, 0, j)),
            pl.BlockSpec((1, 1, tn), lambda l, j: (l, 0, j)),
        ],
        out_specs=pl.BlockSpec((1, rows, tn), lambda l, j: (l, 0, j)),
        out_shape=jax.ShapeDtypeStruct((depth, rows, nd), F32),
        compiler_params=_params(("arbitrary", "arbitrary")),
        name="mod",
    )(c_all, w_mod, b_mod.reshape(depth, 1, nd))


def _ffn_kernel(*refs, n_f, emit_h, final, row_chunks):
    x_ref, mod_ref, w13_ref, w2_ref = refs[:4]
    pos = 4
    modn_ref = gain_ref = hn_ref = None
    if emit_h:
        modn_ref = refs[pos]
        pos += 1
    if final:
        gain_ref = refs[pos]
        pos += 1
    o_ref = refs[pos]
    pos += 1
    if emit_h:
        hn_ref = refs[pos]
        pos += 1
    h_scr = refs[pos]
    acc = o_ref.at[0] if row_chunks > 1 else refs[pos + 1]
    f = pl.program_id(2)
    tm = h_scr.shape[0]
    chunks = [slice(r * (tm // row_chunks), (r + 1) * (tm // row_chunks)) for r in range(row_chunks)]

    assert n_f >= 3

    tf = w2_ref.shape[0]

    def partial_out(h):
        ab = jnp.dot(h, w13_ref[...], preferred_element_type=F32)
        g = (_silu(ab[:, 0:tf]) * ab[:, tf:2 * tf]).astype(BF16)
        return jnp.dot(g, w2_ref[...], preferred_element_type=F32)

    @pl.when(f == 0)
    def _():
        m = mod_ref[0]
        for rows in chunks:
            h = (_rms(x_ref[0, rows, :]) * (1.0 + m[1:2]) + m[0:1]).astype(BF16)
            h_scr[rows, :] = h
            acc[rows, :] = partial_out(h)

    @pl.when(jnp.logical_and(f > 0, f < n_f - 1))
    def _():
        for rows in chunks:
            acc[rows, :] += partial_out(h_scr[rows, :])

    @pl.when(f == n_f - 1)
    def _():
        m = mod_ref[0]
        for rows in chunks:
            xo = x_ref[0, rows, :] + 0.5 * m[2:3] * (acc[rows, :] + partial_out(h_scr[rows, :]))
            if emit_h:
                mn = modn_ref[0]
                hn_ref[0, rows, :] = (_rms(xo) * (1.0 + mn[1:2]) + mn[0:1]).astype(BF16)
            if final:
                xo = _rms(xo) * gain_ref[...]
            o_ref[0, rows, :] = xo


def _ffn_call(x, mod3, w13, w2, layer, modn=None, gain=None):
    bsz, s, d = x.shape
    n_f, tf = w13.shape[1], w13.shape[3] // 2
    emit_h = modn is not None
    final = gain is not None
    tm = _tile(s, ROW_TILE if emit_h else WIDE_TILE)
    row_chunks = tm // _tile(tm, ROW_TILE)
    in_specs = [
        pl.BlockSpec((1, tm, d), lambda b, i, f: (b, i, 0)),
        pl.BlockSpec((1, 3, d), lambda b, i, f: (b, 0, 0)),
        pl.BlockSpec((None, None, d, 2 * tf), lambda b, i, f: (layer, f, 0, 0)),
        pl.BlockSpec((None, tf, d), lambda b, i, f: (layer, f, 0)),
    ]
    args = [x, mod3, w13, w2]
    if emit_h:
        in_specs.append(pl.BlockSpec((1, 3, d), lambda b, i, f: (b, 0, 0)))
        args.append(modn)
    if final:
        in_specs.append(pl.BlockSpec((1, d), lambda b, i, f: (0, 0)))
        args.append(gain.reshape(1, d))
    out_specs = [pl.BlockSpec((1, tm, d), lambda b, i, f: (b, i, 0))]
    out_shape = [jax.ShapeDtypeStruct((bsz, s, d), F32)]
    if emit_h:
        out_specs.append(pl.BlockSpec((1, tm, d), lambda b, i, f: (b, i, 0)))
        out_shape.append(jax.ShapeDtypeStruct((bsz, s, d), BF16))
    outs = pl.pallas_call(
        functools.partial(_ffn_kernel, n_f=n_f, emit_h=emit_h, final=final, row_chunks=row_chunks),
        grid=(bsz, s // tm, n_f),
        in_specs=in_specs,
        out_specs=out_specs,
        out_shape=out_shape,
        scratch_shapes=[pltpu.VMEM((tm, d), BF16)] + ([] if row_chunks > 1 else [pltpu.VMEM((tm, d), F32)]),
        compiler_params=_params(("arbitrary", "arbitrary", "arbitrary")),
        name="ffn",
    )(*args)
    return outs if emit_h else outs[0]


def _cast_kernel(*refs):
    half = len(refs) // 2
    for src, dst in zip(refs[:half], refs[half:]):
        dst[...] = src[...].astype(dst.dtype)


def _cast_call(arrays, steps):
    depth = arrays[0].shape[0]
    specs = []
    for a in arrays:
        assert a.shape[0] == depth and a.shape[1] % (PACKED_SUBLANE * steps) == 0
        specs.append(pl.BlockSpec((1, a.shape[1] // steps, a.shape[2]), lambda l, j: (l, j, 0)))
    return pl.pallas_call(
        _cast_kernel,
        grid=(depth, steps),
        in_specs=specs,
        out_specs=specs,
        out_shape=[jax.ShapeDtypeStruct(a.shape, BF16) for a in arrays],
        compiler_params=_params(("arbitrary", "arbitrary")),
        name="cast_weights",
    )(*arrays)


def _cast_pairs_kernel(*refs):
    n_out = len(refs) // 3
    for k in range(n_out):
        wa, wb, dst = refs[2 * k], refs[2 * k + 1], refs[2 * n_out + k]
        tf = wa.shape[2]
        dst[0, 0, :, 0:tf] = wa[0].astype(dst.dtype)
        dst[0, 0, :, tf:2 * tf] = wb[0].astype(dst.dtype)


def _cast_pairs_call(pairs, steps):
    depth, r, c = pairs[0][0].shape
    tf = c // steps
    in_spec = pl.BlockSpec((1, r, tf), lambda l, j: (l, 0, j))
    out_spec = pl.BlockSpec((1, 1, r, 2 * tf), lambda l, j: (l, j, 0, 0))
    return pl.pallas_call(
        _cast_pairs_kernel,
        grid=(depth, steps),
        in_specs=[in_spec] * (2 * len(pairs)),
        out_specs=[out_spec] * len(pairs),
        out_shape=[jax.ShapeDtypeStruct((depth, steps, r, 2 * tf), BF16)] * len(pairs),
        compiler_params=_params(("arbitrary", "arbitrary")),
        name="cast_weight_pairs",
    )(*[w for pair in pairs for w in pair])


def _matmul_kernel(a_ref, b_ref, o_ref):
    o_ref[...] = jnp.dot(a_ref[...], b_ref[...], preferred_element_type=F32).astype(o_ref.dtype)


def _matmul_call(a, b, out_dtype, tm_pref=WIDE_TILE, tn_pref=WIDE_TILE, b_cols=None):
    m, k = a.shape
    c0, c1 = b_cols if b_cols is not None else (0, b.shape[1])
    n = c1 - c0
    tm = _tile(m, tm_pref)
    tn = math.gcd(_tile(n, tn_pref), c0) if c0 else _tile(n, tn_pref)
    assert tn % LANE == 0
    joff = c0 // tn
    return pl.pallas_call(
        _matmul_kernel,
        grid=(m // tm, n // tn),
        in_specs=[
            pl.BlockSpec((tm, k), lambda i, j: (i, 0)),
            pl.BlockSpec((k, tn), lambda i, j: (0, joff + j)),
        ],
        out_specs=pl.BlockSpec((tm, tn), lambda i, j: (i, j)),
        out_shape=jax.ShapeDtypeStruct((m, n), out_dtype),
        compiler_params=_params(("arbitrary", "arbitrary")),
        name="matmul",
    )(a, b)


def _ret_kernel(lg_ref, q_ref, k_ref, v_ref, g_ref, cos_ref, sin_ref, kc_ref, vc_ref, o_ref,
                qr_scr, kr_scr, upd_scr, st_scr, *, n, n_ctx, chunk):
    hd = HEAD_DIM
    head = pl.program_id(1)
    lgf = lg_ref[0, head]
    lgb = lg_ref[1, head]
    nc = n // chunk
    k_scale = hd ** -0.5

    posc = lax.broadcasted_iota(jnp.int32, (n_ctx, hd), 0).astype(F32)
    kc = kc_ref[0].astype(F32) * k_scale
    vc = vc_ref[0]
    kcf = (kc * jnp.exp((n_ctx - 1.0 - posc) * lgf)).T.astype(BF16)
    kcb = (kc * jnp.exp(posc * lgb)).T.astype(BF16)
    s0_f = jnp.dot(kcf, vc, preferred_element_type=F32)
    s0_b = jnp.dot(kcb, vc, preferred_element_type=F32)

    pos = lax.broadcasted_iota(jnp.int32, (chunk, hd), 0).astype(F32)
    qdec_f = jnp.exp((pos + 1.0) * lgf)
    kdec_f = jnp.exp((chunk - 1.0 - pos) * lgf)
    qdec_b = jnp.exp((chunk - pos) * lgb)
    kdec_b = jnp.exp(pos * lgb)
    cdec_f = jnp.exp(jnp.full((1, hd), chunk, F32) * lgf)
    cdec_b = jnp.exp(jnp.full((1, hd), chunk, F32) * lgb)
    ri = lax.broadcasted_iota(jnp.int32, (chunk, chunk), 0)
    ci = lax.broadcasted_iota(jnp.int32, (chunk, chunk), 1)
    diff = (ri - ci).astype(F32)
    dmat = (jnp.where(diff >= 0, jnp.exp(jnp.maximum(diff, 0.0) * lgf), 0.0)
            + jnp.where(diff <= 0, jnp.exp(jnp.maximum(-diff, 0.0) * lgb), 0.0))

    def rotary(x, rows):
        return x * cos_ref[rows, :] + pltpu.roll(x, hd // 2, 1) * sin_ref[rows, :]

    def chunk_rows(c):
        return pl.ds(pl.multiple_of(c * chunk, chunk), chunk)

    def prep_body(c, carry):
        rows = chunk_rows(c)
        qr_scr[rows, :] = rotary(q_ref[0, rows, :].astype(F32), rows)
        k = rotary(k_ref[0, rows, :].astype(F32), rows) * k_scale
        kr_scr[rows, :] = k
        kk = jnp.concatenate([k * kdec_f, k * kdec_b], axis=1).T.astype(BF16)
        upd_scr[c] = jnp.dot(kk, v_ref[0, rows, :], preferred_element_type=F32)
        return carry

    lax.fori_loop(0, nc, prep_body, 0, unroll=math.gcd(nc, RET_UNROLL))

    def fwd_scan(c, sf):
        st_scr[c, 0:hd, :] = sf.astype(BF16)
        return cdec_f * sf + upd_scr[c, 0:hd, :]

    def bwd_scan(j, sb):
        c = nc - 1 - j
        st_scr[c, hd:2 * hd, :] = sb.astype(BF16)
        return cdec_b * sb + upd_scr[c, hd:2 * hd, :]

    lax.fori_loop(0, nc, fwd_scan, s0_f, unroll=math.gcd(nc, RET_UNROLL))
    lax.fori_loop(0, nc, bwd_scan, s0_b, unroll=math.gcd(nc, RET_UNROLL))

    def out_body(c, carry):
        rows = chunk_rows(c)
        q = qr_scr[rows, :]
        s = lax.dot_general(q.astype(BF16), kr_scr[rows, :].astype(BF16), (((1,), (1,)), ((), ())),
                            preferred_element_type=F32) * dmat
        qq = jnp.concatenate([q * qdec_f, q * qdec_b], axis=1).astype(BF16)
        o = (jnp.dot(s.astype(BF16), v_ref[0, rows, :], preferred_element_type=F32)
             + jnp.dot(qq, st_scr[c], preferred_element_type=F32))
        g = g_ref[0, rows, :].astype(F32)
        o_ref[0, rows, :] = (_rms(o) * _silu(g)).astype(o_ref.dtype)
        return carry

    lax.fori_loop(0, nc, out_body, 0, unroll=math.gcd(nc, RET_UNROLL))


def _ret_call(p, kvc, log_decay, cos2, sin2, ret_w):
    bsz, n, _ = p.shape
    n_ctx = kvc.shape[1]
    heads = ret_w // HEAD_DIM
    hd = HEAD_DIM
    chunk = _tile(n, RET_CHUNK)
    kern = functools.partial(_ret_kernel, n=n, n_ctx=n_ctx, chunk=chunk)
    col = lambda off: (lambda b, h, lg: (b, 0, off * heads + h))
    return pl.pallas_call(
        kern,
        grid_spec=pltpu.PrefetchScalarGridSpec(
            num_scalar_prefetch=1,
            grid=(bsz, heads),
            in_specs=[
                pl.BlockSpec((1, n, hd), col(0)),
                pl.BlockSpec((1, n, hd), col(1)),
                pl.BlockSpec((1, n, hd), col(2)),
                pl.BlockSpec((1, n, hd), col(3)),
                pl.BlockSpec((n, hd), lambda b, h, lg: (0, 0)),
                pl.BlockSpec((n, hd), lambda b, h, lg: (0, 0)),
                pl.BlockSpec((1, n_ctx, hd), col(0)),
                pl.BlockSpec((1, n_ctx, hd), col(1)),
            ],
            out_specs=pl.BlockSpec((1, n, hd), lambda b, h, lg: (b, 0, h)),
            scratch_shapes=[
                pltpu.VMEM((n, hd), F32),
                pltpu.VMEM((n, hd), F32),
                pltpu.VMEM((n // chunk, 2 * hd, hd), F32),
                pltpu.VMEM((n // chunk, 2 * hd, hd), BF16),
            ],
        ),
        out_shape=jax.ShapeDtypeStruct((bsz, n, ret_w), BF16),
        compiler_params=_params(("arbitrary", "arbitrary")),
        name="retention",
    )(log_decay, p, p, p, p, cos2, sin2, kvc, kvc)


def _cmul(a, b):
    return a[0] * b[0] - a[1] * b[1], a[0] * b[1] + a[1] * b[0]


def _cmul_const(a, c, s):
    eps = 1e-12
    if abs(s) < eps:
        return (a[0], a[1]) if c > 0 else (-a[0], -a[1])
    if abs(c) < eps:
        return (-a[1], a[0]) if s > 0 else (a[1], -a[0])
    return a[0] * c - a[1] * s, a[0] * s + a[1] * c


def _fft_list(xs, sign):
    size = len(xs)
    if size == 1:
        return xs
    even = _fft_list(xs[0::2], sign)
    odd = _fft_list(xs[1::2], sign)
    out = [None] * size
    for k in range(size // 2):
        ang = sign * 2.0 * math.pi * k / size
        tr, ti = _cmul_const(odd[k], math.cos(ang), math.sin(ang))
        out[k] = (even[k][0] + tr, even[k][1] + ti)
        out[k + size // 2] = (even[k][0] - tr, even[k][1] - ti)
    return out


def _fft_tables(n):
    j_cls = HY_CLASSES
    k0n, mcls = 2 * n // j_cls, n // j_cls
    t = j_cls * np.arange(mcls)[None, None, :] + np.arange(j_cls)[:, None, None]
    ang = 2.0 * math.pi * ((np.arange(k0n)[None, :, None] * t) % (2 * n)) / (2 * n)
    cs = np.concatenate([np.cos(ang), np.sin(ang)], axis=1)
    cst = np.concatenate([np.cos(ang).transpose(0, 2, 1), np.sin(ang).transpose(0, 2, 1)], axis=1)
    return jnp.asarray(cs, dtype=BF16), jnp.asarray(cst / (2 * n), dtype=BF16)


def _filt_kernel(z_ref, w1_ref, b1_ref, w2_ref, b2_ref, w3_ref, b3_ref, fr_ref, w4_ref, dl_ref,
                 hf_ref, hb_ref, *, n, tt, hw):
    i = pl.program_id(0)
    dot = functools.partial(jnp.dot, preferred_element_type=F32, precision=HIGHEST)
    fr = fr_ref[...]
    a = jnp.sin(fr[0:1] * (dot(z_ref[...], w1_ref[...]) + b1_ref[...]))
    a = jnp.sin(fr[1:2] * (dot(a, w2_ref[...]) + b2_ref[...]))
    a = jnp.sin(fr[2:3] * (dot(a, w3_ref[...]) + b3_ref[...]))
    hf = jnp.dot(a.astype(BF16), w4_ref[...], preferred_element_type=F32)
    mcls = n // HY_CLASSES
    rho = lax.broadcasted_iota(jnp.int32, (tt, hw), 0) + i * tt
    ti = HY_CLASSES * (rho & (mcls - 1)) + lax.shift_right_logical(rho, int(math.log2(mcls)))
    t = ti.astype(F32) * (1.0 / (n - 1.0))
    window = jnp.exp(-t * dl_ref[...])
    for o in range(HY_ORDER):
        hf_ref[:, o * hw:(o + 1) * hw] = (hf[:, (2 * o) * hw:(2 * o + 1) * hw] * window).astype(BF16)
        h_b = jnp.where(ti == 0, 0.0, hf[:, (2 * o + 1) * hw:(2 * o + 2) * hw] * window)
        hb_ref[:, o * hw:(o + 1) * hw] = h_b.astype(BF16)


def _filt_call(n, fw1, fb1, fw2, fb2, fw3, fb3, freq, fw4, hw):
    hid = fw1.shape[1]
    pk = LANE
    mcls = n // HY_CLASSES
    assert mcls & (mcls - 1) == 0
    bands = (HY_EMB_DIM - 1) // 2
    tpos = np.linspace(0.0, 1.0, n, dtype=np.float32)[:, None].astype(np.float64)
    fb = np.linspace(1e-4, bands - 1, bands, dtype=np.float32)[None, :].astype(np.float64)
    wv = (2.0 * math.pi * np.arange(n, dtype=np.float64)[:, None] / n)
    z = np.concatenate([tpos, np.cos(fb * wv), -np.sin(fb * wv)], axis=-1)
    rho = np.arange(n)
    t_of_row = HY_CLASSES * (rho % mcls) + rho // mcls
    z = np.pad(z, ((0, 0), (0, pk - HY_EMB_DIM))).astype(np.float32)[t_of_row]
    max_decay = math.log(HY_DECAY_TARGET) / HY_DECAY_SHORT_PCT
    min_decay = math.log(HY_DECAY_TARGET) / HY_DECAY_LONG_PCT
    deltas = np.abs(np.linspace(min_decay, max_decay, hw, dtype=np.float32)).reshape(1, hw)

    padc = lambda w: jnp.pad(w, ((0, 0), (0, pk - hid)))
    w1 = jnp.pad(fw1, ((0, pk - HY_EMB_DIM), (0, pk - hid)))
    w2 = jnp.pad(fw2, ((0, pk - hid), (0, pk - hid)))
    w3 = jnp.pad(fw3, ((0, pk - hid), (0, pk - hid)))
    w4 = jnp.pad(fw4, ((0, pk - hid), (0, 0))).astype(BF16)
    b1, b2, b3 = (padc(b.reshape(1, hid)) for b in (fb1, fb2, fb3))
    fr = jnp.pad(freq, ((0, SUBLANE - freq.shape[0]), (0, pk - hid)))
    tt = _tile(n, FILTER_ROWS)
    nw = HY_ORDER * hw
    full = lambda shape: pl.BlockSpec(shape, lambda i: (0, 0))
    return pl.pallas_call(
        functools.partial(_filt_kernel, n=n, tt=tt, hw=hw),
        grid=(n // tt,),
        in_specs=[
            pl.BlockSpec((tt, pk), lambda i: (i, 0)),
            full((pk, pk)), full((1, pk)), full((pk, pk)), full((1, pk)),
            full((pk, pk)), full((1, pk)), full((SUBLANE, pk)), full((pk, 2 * nw)), full((1, hw)),
        ],
        out_specs=[pl.BlockSpec((tt, nw), lambda i: (i, 0))] * 2,
        out_shape=[jax.ShapeDtypeStruct((n, nw), BF16)] * 2,
        compiler_params=_params(("arbitrary",)),
        name="hyena_filters",
    )(jnp.asarray(z), w1, b1, w2, b2, w3, b3, fr, w4, jnp.asarray(deltas))


def _sconv_kernel(p_ref, w_ref, b_ref, o_ref, *, n):
    w = w_ref[...]
    bias = b_ref[...]
    mcls = n // HY_CLASSES
    row = lax.broadcasted_iota(jnp.int32, (mcls, p_ref.shape[2]), 0)

    def cls(j):
        return p_ref[0, j * mcls:(j + 1) * mcls, :].astype(F32)

    for j in range(HY_CLASSES):
        cur = cls(j)
        if j > 0:
            prev = cls(j - 1)
        else:
            prev = jnp.where(row == 0, 0.0, pltpu.roll(cls(HY_CLASSES - 1), 1, 0))
        if j < HY_CLASSES - 1:
            nxt = cls(j + 1)
        else:
            nxt = jnp.where(row == mcls - 1, 0.0, pltpu.roll(cls(0), mcls - 1, 0))
        out = bias + prev * w[0:1] + cur * w[1:2] + nxt * w[2:3]
        o_ref[0, j * mcls:(j + 1) * mcls, :] = out.astype(o_ref.dtype)


def _sconv_call(p, conv_w, conv_b):
    bsz, n, width = p.shape
    tc = _tile(width, ROW_TILE)
    return pl.pallas_call(
        functools.partial(_sconv_kernel, n=n),
        grid=(bsz, width // tc),
        in_specs=[
            pl.BlockSpec((1, n, tc), lambda b, j: (b, 0, j)),
            pl.BlockSpec((HY_SHORT, tc), lambda b, j: (0, j)),
            pl.BlockSpec((1, tc), lambda b, j: (0, j)),
        ],
        out_specs=pl.BlockSpec((1, n, tc), lambda b, j: (b, 0, j)),
        out_shape=jax.ShapeDtypeStruct((bsz, n, width), BF16),
        compiler_params=_params(("arbitrary", "arbitrary")),
        name="short_conv",
    )(p, conv_w, conv_b.reshape(1, width))


def _spec_kernel(cs_ref, hf_ref, hb_ref, kre_ref, kim_ref, wf_scr, wb_scr,
                 *, k0n, mcls):
    j_cls = HY_CLASSES
    tn = hf_ref.shape[1]
    for j in range(j_cls):
        rows = slice(j * mcls, (j + 1) * mcls)
        prod = jnp.dot(cs_ref[j], jnp.concatenate([hf_ref[rows, :], hb_ref[rows, :]], axis=1),
                       preferred_element_type=F32)
        wf_scr[j, 0:k0n, :] = prod[0:k0n, 0:tn]
        wf_scr[j, k0n:2 * k0n, :] = -prod[k0n:2 * k0n, 0:tn]
        wb_scr[j, 0:k0n, :] = prod[0:k0n, tn:2 * tn]
        wb_scr[j, k0n:2 * k0n, :] = -prod[k0n:2 * k0n, tn:2 * tn]

    def body(c, carry):
        halves = []
        for half in range(PACKED_SUBLANE // SUBLANE):
            r8 = pl.ds(pl.multiple_of(c * PACKED_SUBLANE + half * SUBLANE, SUBLANE), SUBLANE)
            i8 = pl.ds(pl.multiple_of(k0n + c * PACKED_SUBLANE + half * SUBLANE, SUBLANE), SUBLANE)
            xs = [(wf_scr[j, r8, :], wf_scr[j, i8, :]) for j in range(j_cls)]
            ys = [(wb_scr[j, r8, :], wb_scr[j, i8, :]) for j in range(j_cls)]
            halves.append((_fft_list(xs, -1), _fft_list(ys, -1)))
        for q in range(j_cls):
            rows = pl.ds(pl.multiple_of(q * k0n + c * PACKED_SUBLANE, PACKED_SUBLANE), PACKED_SUBLANE)
            kre = jnp.concatenate([h[0][q][0] + h[1][q][0] for h in halves], axis=0)
            kim = jnp.concatenate([h[0][q][1] - h[1][q][1] for h in halves], axis=0)
            kre_ref[rows, :] = kre.astype(kre_ref.dtype)
            kim_ref[rows, :] = kim.astype(kim_ref.dtype)
        return carry

    lax.fori_loop(0, k0n // PACKED_SUBLANE, body, 0)


def _spec_call(tables, hf, hb):
    cs, _ = tables
    n, width = hf.shape
    tn = LANE
    k0n, mcls = cs.shape[1] // 2, cs.shape[2]
    const = lambda shape: pl.BlockSpec(shape, lambda j: (0,) * len(shape),
                                       pipeline_mode=pl.Buffered(1))
    return pl.pallas_call(
        functools.partial(_spec_kernel, k0n=k0n, mcls=mcls),
        grid=(width // tn,),
        in_specs=[
            const(cs.shape),
            pl.BlockSpec((n, tn), lambda j: (0, j)),
            pl.BlockSpec((n, tn), lambda j: (0, j)),
        ],
        out_specs=[pl.BlockSpec((2 * n, tn), lambda j: (0, j))] * 2,
        out_shape=[jax.ShapeDtypeStruct((2 * n, width), BF16)] * 2,
        scratch_shapes=[pltpu.VMEM((HY_CLASSES, 2 * k0n, tn), F32)] * 2,
        compiler_params=_params(("arbitrary",)),
        name="filter_spectrum",
    )(cs, hf, hb)


def _fftconv_kernel(cs_ref, cst_ref, z_ref, g_ref, kre_ref, kim_ref, bias_ref,
                    o_ref, w_scr, *, k0n, mcls):
    j_cls = HY_CLASSES
    tn = o_ref.shape[3]
    for j in range(j_cls):
        rows = slice(j * mcls, (j + 1) * mcls)
        prod = jnp.dot(cs_ref[j], jnp.concatenate([z_ref[0, 0, rows, :], z_ref[0, 1, rows, :]], axis=1),
                       preferred_element_type=F32)
        w_scr[j, 0:k0n, :] = prod[0:k0n, 0:tn] + prod[k0n:2 * k0n, tn:2 * tn]
        w_scr[j, k0n:2 * k0n, :] = prod[0:k0n, tn:2 * tn] - prod[k0n:2 * k0n, 0:tn]

    def body(c, carry):
        for half in range(PACKED_SUBLANE // SUBLANE):
            r8 = pl.ds(pl.multiple_of(c * PACKED_SUBLANE + half * SUBLANE, SUBLANE), SUBLANE)
            i8 = pl.ds(pl.multiple_of(k0n + c * PACKED_SUBLANE + half * SUBLANE, SUBLANE), SUBLANE)
            xs = [(w_scr[j, r8, :], w_scr[j, i8, :]) for j in range(j_cls)]
            spec = _fft_list(xs, -1)
            prod = []
            for q in range(j_cls):
                rows = pl.ds(pl.multiple_of(q * k0n + c * PACKED_SUBLANE, PACKED_SUBLANE), PACKED_SUBLANE)
                part = slice(half * SUBLANE, (half + 1) * SUBLANE)
                kq = (kre_ref[rows, :].astype(F32)[part], kim_ref[rows, :].astype(F32)[part])
                prod.append(_cmul(spec[q], kq))
            back = _fft_list(prod, 1)
            for j in range(j_cls):
                w_scr[j, r8, :] = back[j][0]
                w_scr[j, i8, :] = back[j][1]
        return carry

    lax.fori_loop(0, k0n // PACKED_SUBLANE, body, 0)

    bias = bias_ref[...]
    for j in range(j_cls):
        rows = slice(j * mcls, (j + 1) * mcls)
        v2 = jnp.concatenate([w_scr[j, 0:k0n, :], w_scr[j, k0n:2 * k0n, :]], axis=1).astype(BF16)
        prod = jnp.dot(cst_ref[j], v2, preferred_element_type=F32)
        ys = (prod[0:mcls, 0:tn] - prod[mcls:2 * mcls, tn:2 * tn],
              prod[mcls:2 * mcls, 0:tn] + prod[0:mcls, tn:2 * tn])
        for b2 in range(2):
            conv = ys[b2] + z_ref[0, b2, rows, :].astype(F32) * bias
            o_ref[0, b2, rows, :] = (g_ref[0, b2, rows, :].astype(F32) * conv).astype(o_ref.dtype)


def _fftconv_call(tables, z, z_col0, gate, gate_col0, kre, kim, k_col0, bias):
    cs, cst = tables
    bsz, n, _ = z.shape
    hw = bias.shape[0]
    tn = LANE
    k0n, mcls = cs.shape[1] // 2, cs.shape[2]
    zo, go, ko = z_col0 // tn, gate_col0 // tn, k_col0 // tn
    z4 = z.reshape(bsz // 2, 2, n, z.shape[2])
    g4 = gate.reshape(bsz // 2, 2, n, gate.shape[2])
    const = lambda shape: pl.BlockSpec(shape, lambda j, p: (0,) * len(shape),
                                       pipeline_mode=pl.Buffered(1))
    out = pl.pallas_call(
        functools.partial(_fftconv_kernel, k0n=k0n, mcls=mcls),
        grid=(hw // tn, bsz // 2),
        in_specs=[
            const(cs.shape), const(cst.shape),
            pl.BlockSpec((1, 2, n, tn), lambda j, p: (p, 0, 0, zo + j)),
            pl.BlockSpec((1, 2, n, tn), lambda j, p: (p, 0, 0, go + j)),
            pl.BlockSpec((2 * n, tn), lambda j, p: (0, ko + j)),
            pl.BlockSpec((2 * n, tn), lambda j, p: (0, ko + j)),
            pl.BlockSpec((1, tn), lambda j, p: (0, j)),
        ],
        out_specs=pl.BlockSpec((1, 2, n, tn), lambda j, p: (p, 0, 0, j)),
        out_shape=jax.ShapeDtypeStruct((bsz // 2, 2, n, hw), BF16),
        scratch_shapes=[pltpu.VMEM((HY_CLASSES, 2 * k0n, tn), F32)],
        compiler_params=_params(("arbitrary", "arbitrary")),
        name="fft_conv",
    )(cs, cst, z4, g4, kre, kim, bias.reshape(1, hw))
    return out.reshape(bsz, n, hw)


def _wout_kernel(r_ref, y_ref, wa_ref, wb_ref, x_ref, gate_ref, o_ref):
    y = (jnp.dot(r_ref[0], wa_ref[...], preferred_element_type=F32)
         + jnp.dot(y_ref[0], wb_ref[...], preferred_element_type=F32))
    o_ref[0] = x_ref[0] + gate_ref[0] * y


def _wout_call(r, yh, w_out, x, gate):
    bsz, n, d = x.shape
    rw = r.shape[2]
    assert d == 2 * rw and yh.shape[2] == rw
    tm = _tile(n, ROW_TILE)
    return pl.pallas_call(
        _wout_kernel,
        grid=(bsz, n // tm),
        in_specs=[
            pl.BlockSpec((1, tm, rw), lambda b, i: (b, i, 0)),
            pl.BlockSpec((1, tm, d - rw), lambda b, i: (b, i, 0)),
            pl.BlockSpec((rw, d), lambda b, i: (0, 0)),
            pl.BlockSpec((rw, d), lambda b, i: (1, 0)),
            pl.BlockSpec((1, tm, d), lambda b, i: (b, i, 0)),
            pl.BlockSpec((1, 1, d), lambda b, i: (b, 0, 0)),
        ],
        out_specs=pl.BlockSpec((1, tm, d), lambda b, i: (b, i, 0)),
        out_shape=jax.ShapeDtypeStruct((bsz, n, d), F32),
        compiler_params=_params(("arbitrary", "arbitrary")),
        name="mixer_out",
    )(r, yh, w_out, w_out, x, gate)


def _pool_kernel(h_ref, mc_ref, icol_ref, w_ref, sc_ref, x_ref, gate_ref, o_ref, cp_scr, pm_scr,
                 *, n, tm, to):
    g = pl.program_id(1)
    i = pl.program_id(2)
    n_rows = n // GRID_W

    @pl.when(i == 0)
    def _():
        for t in range(n // tm):
            rows = slice(t * tm, (t + 1) * tm)
            cp_scr[rows, :] = jnp.dot(mc_ref[0], h_ref[0, rows, :], preferred_element_type=F32)

    def grid_row(r):
        return slice(r * GRID_W, (r + 1) * GRID_W)

    for gi, win in enumerate(POOL_WINDOWS):
        @pl.when(jnp.logical_and(i == 0, g == gi))
        def _(win=win):
            lo, hi = win // 2, win - 1 - win // 2
            icol = icol_ref[0]
            acc = cp_scr[grid_row(0), :]
            for r in range(1, hi + 1):
                acc = acc + cp_scr[grid_row(r), :]
            for r in range(n_rows):
                if r > 0 and r + hi < n_rows:
                    acc = acc + cp_scr[grid_row(r + hi), :]
                if r - lo - 1 >= 0:
                    acc = acc - cp_scr[grid_row(r - lo - 1), :]
                cnt_r = min(r + hi, n_rows - 1) + 1 - max(r - lo, 0)
                mean = acc * (icol * (1.0 / cnt_r))
                pm_scr[grid_row(r), :] = (mean - h_ref[0, grid_row(r), :].astype(F32)).astype(BF16)

    rows = pl.ds(pl.multiple_of(i * to, to), to)
    y = jnp.dot(pm_scr[rows, :], w_ref[0], preferred_element_type=F32) * sc_ref[...]
    o_ref[0] = x_ref[0] + gate_ref[0] * y


def _pool_tables(tm, pg):
    tok = np.arange(tm)
    r_t, c_t = tok // GRID_W, tok % GRID_W
    mats, inv = [], []
    for win in POOL_WINDOWS:
        lo, hi = -(win // 2), win - 1 - win // 2
        dc = c_t[None, :] - c_t[:, None]
        mats.append(((r_t[None, :] == r_t[:, None]) & (dc >= lo) & (dc <= hi)).astype(np.float32))
        idx_c = np.arange(GRID_W)
        cnt_c = np.minimum(idx_c + hi, GRID_W - 1) + 1 - np.maximum(idx_c + lo, 0)
        inv.append(np.broadcast_to((1.0 / cnt_c)[:, None], (GRID_W, pg)))
    return (jnp.asarray(np.stack(mats), dtype=BF16),
            jnp.asarray(np.stack(inv).astype(np.float32)))


def _pool_call(h, pool_w, pool_scale, x, gate):
    bsz, n, d = x.shape
    ng = len(POOL_WINDOWS)
    pg = d // ng
    tm = _tile(n, ROW_TILE)
    to = _tile(n, WIDE_TILE)
    assert tm % GRID_W == 0 and n % GRID_W == 0
    mats, icol = _pool_tables(tm, pg)
    return pl.pallas_call(
        functools.partial(_pool_kernel, n=n, tm=tm, to=to),
        grid=(bsz, ng, n // to),
        in_specs=[
            pl.BlockSpec((1, n, pg), lambda b, g, i: (b, 0, g)),
            pl.BlockSpec((1, tm, tm), lambda b, g, i: (g, 0, 0)),
            pl.BlockSpec((1, GRID_W, pg), lambda b, g, i: (g, 0, 0)),
            pl.BlockSpec((1, pg, pg), lambda b, g, i: (g, 0, 0)),
            pl.BlockSpec((1, pg), lambda b, g, i: (0, g)),
            pl.BlockSpec((1, to, pg), lambda b, g, i: (b, i, g)),
            pl.BlockSpec((1, 1, pg), lambda b, g, i: (b, 0, g)),
        ],
        out_specs=pl.BlockSpec((1, to, pg), lambda b, g, i: (b, i, g)),
        out_shape=jax.ShapeDtypeStruct((bsz, n, d), F32),
        scratch_shapes=[pltpu.VMEM((n, pg), F32), pltpu.VMEM((n, pg), BF16)],
        compiler_params=_params(("arbitrary", "arbitrary", "arbitrary")),
        name="pool_mixer",
    )(h, mats, icol, pool_w, pool_scale.reshape(1, d), x, gate)


def _rotary_tables(n):
    pos = np.arange(n)
    row = (pos // GRID_W).astype(np.float32)
    col = (pos % GRID_W).astype(np.float32)
    n_freq = HEAD_DIM // 4
    inv = (np.float32(ROPE_BASE) ** (-np.arange(n_freq, dtype=np.float32) / np.float32(n_freq))).astype(np.float32)
    ang = np.concatenate([row[:, None] * inv, col[:, None] * inv], axis=-1).astype(np.float32)
    cos, sin = np.cos(ang.astype(np.float64)), np.sin(ang.astype(np.float64))
    cos2 = np.concatenate([cos, cos], axis=-1).astype(np.float32)
    sin2 = np.concatenate([-sin, sin], axis=-1).astype(np.float32)
    return jnp.asarray(cos2), jnp.asarray(sin2)


def kernel(x, c, ctx, c_ctx, w_mod, b_mod, ffn1_w1, ffn1_w3, ffn1_w2, ffn2_w1, ffn2_w3, ffn2_w2, ab_w_in, ab_w_out, ret_log_decay, hy_conv_w, hy_conv_b, hy_f_w1, hy_f_b1, hy_f_w2, hy_f_b2, hy_f_w3, hy_f_b3, hy_f_freq, hy_f_w4, hy_bias, pool_w, pool_scale, final_gain):
    bsz, n, d = x.shape
    n_ctx = ctx.shape[1]
    depth = w_mod.shape[0]
    assert depth == 2, "layer 0 = retention/Hyena mixer, layer 1 = pooling mixer"
    ret_w = RET_HEADS * HEAD_DIM
    hw = d - ret_w
    assert hy_f_w4.shape[2] == HY_ORDER * 2 * hw and bsz % 2 == 0

    c_all = jnp.concatenate([c, c_ctx[None, :], jnp.zeros((SUBLANE - bsz - 1, d), F32)], axis=0)
    mod = _mod_call(c_all, w_mod, b_mod).reshape(depth, SUBLANE, N_MOD, d)

    def mod_rows(layer, base, count=3):
        return mod[layer, :bsz, base:base + count]

    def mod_ctx(base, count=3):
        return mod[0, bsz:bsz + 1, base:base + count]

    n_f = ffn1_w1.shape[2] // _tile(ffn1_w1.shape[2], FFN_TF)
    w13 = _cast_pairs_call([(ffn1_w1, ffn1_w3), (ffn2_w1, ffn2_w3)], n_f)
    w_in_shape, w_out_shape = ab_w_in.shape[1:], ab_w_out.shape[1:]
    w2a, w2b, w_in, w_out = _cast_call(
        [ffn1_w2, ffn2_w2, ab_w_in[0].reshape(depth, -1, w_in_shape[1]),
         ab_w_out[0].reshape(depth, -1, w_out_shape[1])], CAST_STEPS)
    w_in, w_out = w_in.reshape(w_in_shape), w_out.reshape(w_out_shape)
    ffn1_w, ffn2_w = (w13[0], w2a), (w13[1], w2b)

    x, h = _ffn_call(x, mod_rows(0, 0), *ffn1_w, 0, modn=mod_rows(0, 3))
    _, h_c = _ffn_call(ctx.reshape(1, bsz * n_ctx, d), mod_ctx(0), *ffn1_w, 0, modn=mod_ctx(3))

    p = _matmul_call(h.reshape(bsz * n, d), w_in, BF16, b_cols=(0, 4 * ret_w)).reshape(bsz, n, -1)
    kvc = _matmul_call(h_c.reshape(bsz * n_ctx, d), w_in, BF16, b_cols=(ret_w, 3 * ret_w))
    kvc = kvc.reshape(bsz, n_ctx, 2 * ret_w)

    cos2, sin2 = _rotary_tables(n)
    r = _ret_call(p, kvc, ret_log_decay[0], cos2, sin2, ret_w)

    mcls = n // HY_CLASSES
    h_cm = h.reshape(bsz, mcls, HY_CLASSES, d).transpose(0, 2, 1, 3).reshape(bsz * n, d)
    p_hy = _matmul_call(h_cm, w_in, BF16, b_cols=(4 * ret_w, w_in.shape[1])).reshape(bsz, n, -1)
    hf, hb = _filt_call(n, hy_f_w1[0], hy_f_b1[0], hy_f_w2[0], hy_f_b2[0], hy_f_w3[0],
                        hy_f_b3[0], hy_f_freq[0], hy_f_w4[0], hw)
    tables = _fft_tables(n)
    kre, kim = _spec_call(tables, hf, hb)
    u = _sconv_call(p_hy, hy_conv_w[0], hy_conv_b[0])
    z = _fftconv_call(tables, u, 0, u, hw, kre, kim, 0, hy_bias[0, 0])
    yh = _fftconv_call(tables, z, 0, u, 2 * hw, kre, kim, hw, hy_bias[0, 1])
    yh = yh.reshape(bsz, HY_CLASSES, mcls, hw).transpose(0, 2, 1, 3).reshape(bsz, n, hw)

    x = _wout_call(r, yh, w_out, x, mod_rows(0, 5, 1))
    x = _ffn_call(x, mod_rows(0, 6), *ffn2_w, 0)

    x, h = _ffn_call(x, mod_rows(1, 0), *ffn1_w, 1, modn=mod_rows(1, 3))
    x = _pool_call(h, pool_w[0].astype(BF16), pool_scale[0], x, mod_rows(1, 5, 1))
    return _ffn_call(x, mod_rows(1, 6), *ffn2_w, 1, gain=final_gain)
```

```python
import functools
import math

import numpy as np
import jax
import jax.numpy as jnp
from jax import lax
from jax.experimental import pallas as pl
from jax.experimental.pallas import tpu as pltpu

F32 = jnp.float32
BF16 = jnp.bfloat16
HIGHEST = lax.Precision.HIGHEST

EPS = 1e-6
GRID_W = 64
N_MOD = 9
RET_HEADS = 8
HEAD_DIM = 128
HY_ORDER = 2
HY_SHORT = 3
HY_EMB_DIM = 33
HY_DECAY_SHORT_PCT = 0.3
HY_DECAY_LONG_PCT = 1.5
HY_DECAY_TARGET = 1e-2
ROPE_BASE = 10000.0
POOL_WINDOWS = (2, 4, 8, 16)
RET_CHUNK = 256
RET_UNROLL = 16
HY_CLASSES = 16
LANE = 128
SUBLANE = 8
PACKED_SUBLANE = 16
ROW_TILE = 512
WIDE_TILE = 1024
FFN_TF = 512
FILTER_ROWS = 256
CAST_STEPS = 16
VMEM_LIMIT = 56 * 1024 * 1024


def _tile(dim, pref):
    if dim <= pref:
        return dim
    for t in range(pref - pref % LANE, 0, -LANE):
        if dim % t == 0:
            return t
    raise ValueError((dim, pref))


def _params(sem):
    return pltpu.CompilerParams(dimension_semantics=sem, vmem_limit_bytes=VMEM_LIMIT)


def _rms(x):
    return x * lax.rsqrt(jnp.mean(x * x, axis=-1, keepdims=True) + EPS)


def _silu(x):
    return x * jax.nn.sigmoid(x)


def _mod_kernel(c_ref, w_ref, b_ref, o_ref):
    sc = _silu(c_ref[...])
    w = w_ref[0]
    sc_hi = sc.astype(BF16)
    sc_lo = (sc - sc_hi.astype(F32)).astype(BF16)
    w_hi = w.astype(BF16)
    w_lo = (w - w_hi.astype(F32)).astype(BF16)
    rows = sc.shape[0]
    top = jnp.dot(jnp.concatenate([sc_hi, sc_lo], axis=0), w_hi, preferred_element_type=F32)
    o_ref[0] = (top[0:rows] + top[rows:2 * rows]
                + jnp.dot(sc_hi, w_lo, preferred_element_type=F32) + b_ref[0])


def _mod_call(c_all, w_mod, b_mod):
    depth, d, nd = w_mod.shape
    rows = c_all.shape[0]
    tn = _tile(nd, WIDE_TILE)
    return pl.pallas_call(
        _mod_kernel,
        grid=(depth, nd // tn),
        in_specs=[
            pl.BlockSpec((rows, d), lambda l, j: (0, 0)),
            pl.BlockSpec((1, d, tn), lambda l, j: (l, 0, j)),
            pl.BlockSpec((1, 1, tn), lambda l, j: (l, 0, j)),
        ],
        out_specs=pl.BlockSpec((1, rows, tn), lambda l, j: (l, 0, j)),
        out_shape=jax.ShapeDtypeStruct((depth, rows, nd), F32),
        compiler_params=_params(("arbitrary", "arbitrary")),
        name="mod",
    )(c_all, w_mod, b_mod.reshape(depth, 1, nd))


def _ffn_kernel(*refs, n_f, emit_h, final, row_chunks):
    x_ref, mod_ref, w13_ref, w2_ref = refs[:4]
    pos = 4
    modn_ref = gain_ref = hn_ref = None
    if emit_h:
        modn_ref = refs[pos]
        pos += 1
    if final:
        gain_ref = refs[pos]
        pos += 1
    o_ref = refs[pos]
    pos += 1
    if emit_h:
        hn_ref = refs[pos]
        pos += 1
    h_scr = refs[pos]
    acc = o_ref.at[0] if row_chunks > 1 else refs[pos + 1]
    f = pl.program_id(2)
    tm = h_scr.shape[0]
    chunks = [slice(r * (tm // row_chunks), (r + 1) * (tm // row_chunks)) for r in range(row_chunks)]

    assert n_f >= 3

    tf = w2_ref.shape[0]

    def partial_out(h):
        ab = jnp.dot(h, w13_ref[...], preferred_element_type=F32)
        g = (_silu(ab[:, 0:tf]) * ab[:, tf:2 * tf]).astype(BF16)
        return jnp.dot(g, w2_ref[...], preferred_element_type=F32)

    @pl.when(f == 0)
    def _():
        m = mod_ref[0]
        for rows in chunks:
            h = (_rms(x_ref[0, rows, :]) * (1.0 + m[1:2]) + m[0:1]).astype(BF16)
            h_scr[rows, :] = h
            acc[rows, :] = partial_out(h)

    @pl.when(jnp.logical_and(f > 0, f < n_f - 1))
    def _():
        for rows in chunks:
            acc[rows, :] += partial_out(h_scr[rows, :])

    @pl.when(f == n_f - 1)
    def _():
        m = mod_ref[0]
        for rows in chunks:
            xo = x_ref[0, rows, :] + 0.5 * m[2:3] * (acc[rows, :] + partial_out(h_scr[rows, :]))
            if emit_h:
                mn = modn_ref[0]
                hn_ref[0, rows, :] = (_rms(xo) * (1.0 + mn[1:2]) + mn[0:1]).astype(BF16)
            if final:
                xo = _rms(xo) * gain_ref[...]
            o_ref[0, rows, :] = xo


def _ffn_call(x, mod3, w13, w2, layer, modn=None, gain=None):
    bsz, s, d = x.shape
    n_f, tf = w13.shape[1], w13.shape[3] // 2
    emit_h = modn is not None
    final = gain is not None
    tm = _tile(s, WIDE_TILE)
    row_chunks = tm // _tile(tm, ROW_TILE)
    in_specs = [
        pl.BlockSpec((1, tm, d), lambda b, i, f: (b, i, 0)),
        pl.BlockSpec((1, 3, d), lambda b, i, f: (b, 0, 0)),
        pl.BlockSpec((None, None, d, 2 * tf), lambda b, i, f: (layer, f, 0, 0)),
        pl.BlockSpec((None, tf, d), lambda b, i, f: (layer, f, 0)),
    ]
    args = [x, mod3, w13, w2]
    if emit_h:
        in_specs.append(pl.BlockSpec((1, 3, d), lambda b, i, f: (b, 0, 0)))
        args.append(modn)
    if final:
        in_specs.append(pl.BlockSpec((1, d), lambda b, i, f: (0, 0)))
        args.append(gain.reshape(1, d))
    out_specs = [pl.BlockSpec((1, tm, d), lambda b, i, f: (b, i, 0))]
    out_shape = [jax.ShapeDtypeStruct((bsz, s, d), F32)]
    if emit_h:
        out_specs.append(pl.BlockSpec((1, tm, d), lambda b, i, f: (b, i, 0),
                                      pipeline_mode=pl.Buffered(1)))
        out_shape.append(jax.ShapeDtypeStruct((bsz, s, d), BF16))
    outs = pl.pallas_call(
        functools.partial(_ffn_kernel, n_f=n_f, emit_h=emit_h, final=final, row_chunks=row_chunks),
        grid=(bsz, s // tm, n_f),
        in_specs=in_specs,
        out_specs=out_specs,
        out_shape=out_shape,
        scratch_shapes=[pltpu.VMEM((tm, d), BF16)] + ([] if row_chunks > 1 else [pltpu.VMEM((tm, d), F32)]),
        compiler_params=_params(("arbitrary", "arbitrary", "arbitrary")),
        name="ffn",
    )(*args)
    return outs if emit_h else outs[0]


def _cast_kernel(*refs):
    half = len(refs) // 2
    for src, dst in zip(refs[:half], refs[half:]):
        dst[...] = src[...].astype(dst.dtype)


def _cast_call(arrays, steps):
    depth = arrays[0].shape[0]
    specs = []
    for a in arrays:
        assert a.shape[0] == depth and a.shape[1] % (PACKED_SUBLANE * steps) == 0
        specs.append(pl.BlockSpec((1, a.shape[1] // steps, a.shape[2]), lambda l, j: (l, j, 0)))
    return pl.pallas_call(
        _cast_kernel,
        grid=(depth, steps),
        in_specs=specs,
        out_specs=specs,
        out_shape=[jax.ShapeDtypeStruct(a.shape, BF16) for a in arrays],
        compiler_params=_params(("arbitrary", "arbitrary")),
        name="cast_weights",
    )(*arrays)


def _cast_pairs_kernel(*refs):
    n_out = len(refs) // 3
    for k in range(n_out):
        wa, wb, dst = refs[2 * k], refs[2 * k + 1], refs[2 * n_out + k]
        tf = wa.shape[2]
        dst[0, 0, :, 0:tf] = wa[0].astype(dst.dtype)
        dst[0, 0, :, tf:2 * tf] = wb[0].astype(dst.dtype)


def _cast_pairs_call(pairs, steps):
    depth, r, c = pairs[0][0].shape
    tf = c // steps
    in_spec = pl.BlockSpec((1, r, tf), lambda l, j: (l, 0, j))
    out_spec = pl.BlockSpec((1, 1, r, 2 * tf), lambda l, j: (l, j, 0, 0))
    return pl.pallas_call(
        _cast_pairs_kernel,
        grid=(depth, steps),
        in_specs=[in_spec] * (2 * len(pairs)),
        out_specs=[out_spec] * len(pairs),
        out_shape=[jax.ShapeDtypeStruct((depth, steps, r, 2 * tf), BF16)] * len(pairs),
        compiler_params=_params(("arbitrary", "arbitrary")),
        name="cast_weight_pairs",
    )(*[w for pair in pairs for w in pair])


def _matmul_kernel(a_ref, b_ref, o_ref):
    o_ref[...] = jnp.dot(a_ref[...], b_ref[...], preferred_element_type=F32).astype(o_ref.dtype)


def _matmul_call(a, b, out_dtype, tm_pref=WIDE_TILE, tn_pref=WIDE_TILE, b_cols=None):
    m, k = a.shape
    c0, c1 = b_cols if b_cols is not None else (0, b.shape[1])
    n = c1 - c0
    tm = _tile(m, tm_pref)
    tn = math.gcd(_tile(n, tn_pref), c0) if c0 else _tile(n, tn_pref)
    assert tn % LANE == 0
    joff = c0 // tn
    return pl.pallas_call(
        _matmul_kernel,
        grid=(m // tm, n // tn),
        in_specs=[
            pl.BlockSpec((tm, k), lambda i, j: (i, 0)),
            pl.BlockSpec((k, tn), lambda i, j: (0, joff + j)),
        ],
        out_specs=pl.BlockSpec((tm, tn), lambda i, j: (i, j)),
        out_shape=jax.ShapeDtypeStruct((m, n), out_dtype),
        compiler_params=_params(("arbitrary", "arbitrary")),
        name="matmul",
    )(a, b)


def _ret_kernel(lg_ref, q_ref, k_ref, v_ref, g_ref, cos_ref, sin_ref, kc_ref, vc_ref, o_ref,
                qr_scr, kr_scr, upd_scr, st_scr, *, n, n_ctx, chunk):
    hd = HEAD_DIM
    head = pl.program_id(1)
    lgf = lg_ref[0, head]
    lgb = lg_ref[1, head]
    nc = n // chunk
    k_scale = hd ** -0.5

    posc = lax.broadcasted_iota(jnp.int32, (n_ctx, hd), 0).astype(F32)
    kc = kc_ref[0].astype(F32) * k_scale
    vc = vc_ref[0]
    kcf = (kc * jnp.exp((n_ctx - 1.0 - posc) * lgf)).T.astype(BF16)
    kcb = (kc * jnp.exp(posc * lgb)).T.astype(BF16)
    s0_f = jnp.dot(kcf, vc, preferred_element_type=F32)
    s0_b = jnp.dot(kcb, vc, preferred_element_type=F32)

    pos = lax.broadcasted_iota(jnp.int32, (chunk, hd), 0).astype(F32)
    qdec_f = jnp.exp((pos + 1.0) * lgf)
    kdec_f = jnp.exp((chunk - 1.0 - pos) * lgf)
    qdec_b = jnp.exp((chunk - pos) * lgb)
    kdec_b = jnp.exp(pos * lgb)
    cdec_f = jnp.exp(jnp.full((1, hd), chunk, F32) * lgf)
    cdec_b = jnp.exp(jnp.full((1, hd), chunk, F32) * lgb)
    ri = lax.broadcasted_iota(jnp.int32, (chunk, chunk), 0)
    ci = lax.broadcasted_iota(jnp.int32, (chunk, chunk), 1)
    diff = (ri - ci).astype(F32)
    dmat = (jnp.where(diff >= 0, jnp.exp(jnp.maximum(diff, 0.0) * lgf), 0.0)
            + jnp.where(diff <= 0, jnp.exp(jnp.maximum(-diff, 0.0) * lgb), 0.0))

    def rotary(x, rows):
        return x * cos_ref[rows, :] + pltpu.roll(x, hd // 2, 1) * sin_ref[rows, :]

    def chunk_rows(c):
        return pl.ds(pl.multiple_of(c * chunk, chunk), chunk)

    def prep_body(c, carry):
        rows = chunk_rows(c)
        qr_scr[rows, :] = rotary(q_ref[0, rows, :].astype(F32), rows)
        k = rotary(k_ref[0, rows, :].astype(F32), rows) * k_scale
        kr_scr[rows, :] = k
        kk = jnp.concatenate([k * kdec_f, k * kdec_b], axis=1).T.astype(BF16)
        upd_scr[c] = jnp.dot(kk, v_ref[0, rows, :], preferred_element_type=F32)
        return carry

    lax.fori_loop(0, nc, prep_body, 0, unroll=math.gcd(nc, RET_UNROLL))

    def fwd_scan(c, sf):
        st_scr[c, 0:hd, :] = sf.astype(BF16)
        return cdec_f * sf + upd_scr[c, 0:hd, :]

    def bwd_scan(j, sb):
        c = nc - 1 - j
        st_scr[c, hd:2 * hd, :] = sb.astype(BF16)
        return cdec_b * sb + upd_scr[c, hd:2 * hd, :]

    lax.fori_loop(0, nc, fwd_scan, s0_f, unroll=math.gcd(nc, RET_UNROLL))
    lax.fori_loop(0, nc, bwd_scan, s0_b, unroll=math.gcd(nc, RET_UNROLL))

    def out_body(c, carry):
        rows = chunk_rows(c)
        q = qr_scr[rows, :]
        s = lax.dot_general(q.astype(BF16), kr_scr[rows, :].astype(BF16), (((1,), (1,)), ((), ())),
                            preferred_element_type=F32) * dmat
        qq = jnp.concatenate([q * qdec_f, q * qdec_b], axis=1).astype(BF16)
        o = (jnp.dot(s.astype(BF16), v_ref[0, rows, :], preferred_element_type=F32)
             + jnp.dot(qq, st_scr[c], preferred_element_type=F32))
        g = g_ref[0, rows, :].astype(F32)
        o_ref[0, rows, :] = (_rms(o) * _silu(g)).astype(o_ref.dtype)
        return carry

    lax.fori_loop(0, nc, out_body, 0, unroll=math.gcd(nc, RET_UNROLL))


def _ret_call(p, kvc, log_decay, cos2, sin2, ret_w):
    bsz, n, _ = p.shape
    n_ctx = kvc.shape[1]
    heads = ret_w // HEAD_DIM
    hd = HEAD_DIM
    chunk = _tile(n, RET_CHUNK)
    kern = functools.partial(_ret_kernel, n=n, n_ctx=n_ctx, chunk=chunk)
    col = lambda off: (lambda b, h, lg: (b, 0, off * heads + h))
    return pl.pallas_call(
        kern,
        grid_spec=pltpu.PrefetchScalarGridSpec(
            num_scalar_prefetch=1,
            grid=(bsz, heads),
            in_specs=[
                pl.BlockSpec((1, n, hd), col(0)),
                pl.BlockSpec((1, n, hd), col(1)),
                pl.BlockSpec((1, n, hd), col(2)),
                pl.BlockSpec((1, n, hd), col(3)),
                pl.BlockSpec((n, hd), lambda b, h, lg: (0, 0)),
                pl.BlockSpec((n, hd), lambda b, h, lg: (0, 0)),
                pl.BlockSpec((1, n_ctx, hd), col(0)),
                pl.BlockSpec((1, n_ctx, hd), col(1)),
            ],
            out_specs=pl.BlockSpec((1, n, hd), lambda b, h, lg: (b, 0, h)),
            scratch_shapes=[
                pltpu.VMEM((n, hd), F32),
                pltpu.VMEM((n, hd), F32),
                pltpu.VMEM((n // chunk, 2 * hd, hd), F32),
                pltpu.VMEM((n // chunk, 2 * hd, hd), BF16),
            ],
        ),
        out_shape=jax.ShapeDtypeStruct((bsz, n, ret_w), BF16),
        compiler_params=_params(("arbitrary", "arbitrary")),
        name="retention",
    )(log_decay, p, p, p, p, cos2, sin2, kvc, kvc)


def _cmul(a, b):
    return a[0] * b[0] - a[1] * b[1], a[0] * b[1] + a[1] * b[0]


def _cmul_const(a, c, s):
    eps = 1e-12
    if abs(s) < eps:
        return (a[0], a[1]) if c > 0 else (-a[0], -a[1])
    if abs(c) < eps:
        return (-a[1], a[0]) if s > 0 else (a[1], -a[0])
    return a[0] * c - a[1] * s, a[0] * s + a[1] * c


def _fft_list(xs, sign):
    size = len(xs)
    if size == 1:
        return xs
    even = _fft_list(xs[0::2], sign)
    odd = _fft_list(xs[1::2], sign)
    out = [None] * size
    for k in range(size // 2):
        ang = sign * 2.0 * math.pi * k / size
        tr, ti = _cmul_const(odd[k], math.cos(ang), math.sin(ang))
        out[k] = (even[k][0] + tr, even[k][1] + ti)
        out[k + size // 2] = (even[k][0] - tr, even[k][1] - ti)
    return out


def _fft_tables(n):
    j_cls = HY_CLASSES
    k0n, mcls = 2 * n // j_cls, n // j_cls
    t = j_cls * np.arange(mcls)[None, None, :] + np.arange(j_cls)[:, None, None]
    ang = 2.0 * math.pi * ((np.arange(k0n)[None, :, None] * t) % (2 * n)) / (2 * n)
    cs = np.concatenate([np.cos(ang), np.sin(ang)], axis=1)
    cst = np.concatenate([np.cos(ang).transpose(0, 2, 1), np.sin(ang).transpose(0, 2, 1)], axis=1)
    return jnp.asarray(cs, dtype=BF16), jnp.asarray(cst / (2 * n), dtype=BF16)


def _filt_kernel(z_ref, w1_ref, b1_ref, w2_ref, b2_ref, w3_ref, b3_ref, fr_ref, w4_ref, dl_ref,
                 hf_ref, hb_ref, *, n, tt, hw):
    i = pl.program_id(0)
    dot = functools.partial(jnp.dot, preferred_element_type=F32, precision=HIGHEST)
    fr = fr_ref[...]
    a = jnp.sin(fr[0:1] * (dot(z_ref[...], w1_ref[...]) + b1_ref[...]))
    a = jnp.sin(fr[1:2] * (dot(a, w2_ref[...]) + b2_ref[...]))
    a = jnp.sin(fr[2:3] * (dot(a, w3_ref[...]) + b3_ref[...]))
    hf = jnp.dot(a.astype(BF16), w4_ref[...], preferred_element_type=F32)
    mcls = n // HY_CLASSES
    rho = lax.broadcasted_iota(jnp.int32, (tt, hw), 0) + i * tt
    ti = HY_CLASSES * (rho & (mcls - 1)) + lax.shift_right_logical(rho, int(math.log2(mcls)))
    t = ti.astype(F32) * (1.0 / (n - 1.0))
    window = jnp.exp(-t * dl_ref[...])
    for o in range(HY_ORDER):
        hf_ref[:, o * hw:(o + 1) * hw] = (hf[:, (2 * o) * hw:(2 * o + 1) * hw] * window).astype(BF16)
        h_b = jnp.where(ti == 0, 0.0, hf[:, (2 * o + 1) * hw:(2 * o + 2) * hw] * window)
        hb_ref[:, o * hw:(o + 1) * hw] = h_b.astype(BF16)


def _filt_call(n, fw1, fb1, fw2, fb2, fw3, fb3, freq, fw4, hw):
    hid = fw1.shape[1]
    pk = LANE
    mcls = n // HY_CLASSES
    assert mcls & (mcls - 1) == 0
    bands = (HY_EMB_DIM - 1) // 2
    tpos = np.linspace(0.0, 1.0, n, dtype=np.float32)[:, None].astype(np.float64)
    fb = np.linspace(1e-4, bands - 1, bands, dtype=np.float32)[None, :].astype(np.float64)
    wv = (2.0 * math.pi * np.arange(n, dtype=np.float64)[:, None] / n)
    z = np.concatenate([tpos, np.cos(fb * wv), -np.sin(fb * wv)], axis=-1)
    rho = np.arange(n)
    t_of_row = HY_CLASSES * (rho % mcls) + rho // mcls
    z = np.pad(z, ((0, 0), (0, pk - HY_EMB_DIM))).astype(np.float32)[t_of_row]
    max_decay = math.log(HY_DECAY_TARGET) / HY_DECAY_SHORT_PCT
    min_decay = math.log(HY_DECAY_TARGET) / HY_DECAY_LONG_PCT
    deltas = np.abs(np.linspace(min_decay, max_decay, hw, dtype=np.float32)).reshape(1, hw)

    padc = lambda w: jnp.pad(w, ((0, 0), (0, pk - hid)))
    w1 = jnp.pad(fw1, ((0, pk - HY_EMB_DIM), (0, pk - hid)))
    w2 = jnp.pad(fw2, ((0, pk - hid), (0, pk - hid)))
    w3 = jnp.pad(fw3, ((0, pk - hid), (0, pk - hid)))
    w4 = jnp.pad(fw4, ((0, pk - hid), (0, 0))).astype(BF16)
    b1, b2, b3 = (padc(b.reshape(1, hid)) for b in (fb1, fb2, fb3))
    fr = jnp.pad(freq, ((0, SUBLANE - freq.shape[0]), (0, pk - hid)))
    tt = _tile(n, FILTER_ROWS)
    nw = HY_ORDER * hw
    full = lambda shape: pl.BlockSpec(shape, lambda i: (0, 0))
    return pl.pallas_call(
        functools.partial(_filt_kernel, n=n, tt=tt, hw=hw),
        grid=(n // tt,),
        in_specs=[
            pl.BlockSpec((tt, pk), lambda i: (i, 0)),
            full((pk, pk)), full((1, pk)), full((pk, pk)), full((1, pk)),
            full((pk, pk)), full((1, pk)), full((SUBLANE, pk)), full((pk, 2 * nw)), full((1, hw)),
        ],
        out_specs=[pl.BlockSpec((tt, nw), lambda i: (i, 0))] * 2,
        out_shape=[jax.ShapeDtypeStruct((n, nw), BF16)] * 2,
        compiler_params=_params(("arbitrary",)),
        name="hyena_filters",
    )(jnp.asarray(z), w1, b1, w2, b2, w3, b3, fr, w4, jnp.asarray(deltas))


def _sconv_kernel(p_ref, w_ref, b_ref, o_ref, *, n):
    w = w_ref[...]
    bias = b_ref[...]
    mcls = n // HY_CLASSES
    row = lax.broadcasted_iota(jnp.int32, (mcls, p_ref.shape[2]), 0)

    def cls(j):
        return p_ref[0, j * mcls:(j + 1) * mcls, :].astype(F32)

    for j in range(HY_CLASSES):
        cur = cls(j)
        if j > 0:
            prev = cls(j - 1)
        else:
            prev = jnp.where(row == 0, 0.0, pltpu.roll(cls(HY_CLASSES - 1), 1, 0))
        if j < HY_CLASSES - 1:
            nxt = cls(j + 1)
        else:
            nxt = jnp.where(row == mcls - 1, 0.0, pltpu.roll(cls(0), mcls - 1, 0))
        out = bias + prev * w[0:1] + cur * w[1:2] + nxt * w[2:3]
        o_ref[0, j * mcls:(j + 1) * mcls, :] = out.astype(o_ref.dtype)


def _sconv_call(p, conv_w, conv_b):
    bsz, n, width = p.shape
    tc = _tile(width, ROW_TILE)
    return pl.pallas_call(
        functools.partial(_sconv_kernel, n=n),
        grid=(bsz, width // tc),
        in_specs=[
            pl.BlockSpec((1, n, tc), lambda b, j: (b, 0, j)),
            pl.BlockSpec((HY_SHORT, tc), lambda b, j: (0, j)),
            pl.BlockSpec((1, tc), lambda b, j: (0, j)),
        ],
        out_specs=pl.BlockSpec((1, n, tc), lambda b, j: (b, 0, j)),
        out_shape=jax.ShapeDtypeStruct((bsz, n, width), BF16),
        compiler_params=_params(("arbitrary", "arbitrary")),
        name="short_conv",
    )(p, conv_w, conv_b.reshape(1, width))


def _spec_kernel(cs_ref, hf_ref, hb_ref, kre_ref, kim_ref, wf_scr, wb_scr,
                 *, k0n, mcls):
    j_cls = HY_CLASSES
    tn = hf_ref.shape[1]
    for j in range(j_cls):
        rows = slice(j * mcls, (j + 1) * mcls)
        prod = jnp.dot(cs_ref[j], jnp.concatenate([hf_ref[rows, :], hb_ref[rows, :]], axis=1),
                       preferred_element_type=F32)
        wf_scr[j, 0:k0n, :] = prod[0:k0n, 0:tn]
        wf_scr[j, k0n:2 * k0n, :] = -prod[k0n:2 * k0n, 0:tn]
        wb_scr[j, 0:k0n, :] = prod[0:k0n, tn:2 * tn]
        wb_scr[j, k0n:2 * k0n, :] = -prod[k0n:2 * k0n, tn:2 * tn]

    def body(c, carry):
        halves = []
        for half in range(PACKED_SUBLANE // SUBLANE):
            r8 = pl.ds(pl.multiple_of(c * PACKED_SUBLANE + half * SUBLANE, SUBLANE), SUBLANE)
            i8 = pl.ds(pl.multiple_of(k0n + c * PACKED_SUBLANE + half * SUBLANE, SUBLANE), SUBLANE)
            xs = [(wf_scr[j, r8, :], wf_scr[j, i8, :]) for j in range(j_cls)]
            ys = [(wb_scr[j, r8, :], wb_scr[j, i8, :]) for j in range(j_cls)]
            halves.append((_fft_list(xs, -1), _fft_list(ys, -1)))
        for q in range(j_cls):
            rows = pl.ds(pl.multiple_of(q * k0n + c * PACKED_SUBLANE, PACKED_SUBLANE), PACKED_SUBLANE)
            kre = jnp.concatenate([h[0][q][0] + h[1][q][0] for h in halves], axis=0)
            kim = jnp.concatenate([h[0][q][1] - h[1][q][1] for h in halves], axis=0)
            kre_ref[rows, :] = kre.astype(kre_ref.dtype)
            kim_ref[rows, :] = kim.astype(kim_ref.dtype)
        return carry

    lax.fori_loop(0, k0n // PACKED_SUBLANE, body, 0)


def _spec_call(tables, hf, hb):
    cs, _ = tables
    n, width = hf.shape
    tn = LANE
    k0n, mcls = cs.shape[1] // 2, cs.shape[2]
    const = lambda shape: pl.BlockSpec(shape, lambda j: (0,) * len(shape),
                                       pipeline_mode=pl.Buffered(1))
    return pl.pallas_call(
        functools.partial(_spec_kernel, k0n=k0n, mcls=mcls),
        grid=(width // tn,),
        in_specs=[
            const(cs.shape),
            pl.BlockSpec((n, tn), lambda j: (0, j)),
            pl.BlockSpec((n, tn), lambda j: (0, j)),
        ],
        out_specs=[pl.BlockSpec((2 * n, tn), lambda j: (0, j))] * 2,
        out_shape=[jax.ShapeDtypeStruct((2 * n, width), BF16)] * 2,
        scratch_shapes=[pltpu.VMEM((HY_CLASSES, 2 * k0n, tn), F32)] * 2,
        compiler_params=_params(("arbitrary",)),
        name="filter_spectrum",
    )(cs, hf, hb)


def _fftconv_kernel(cs_ref, cst_ref, z_ref, g_ref, kre_ref, kim_ref, bias_ref,
                    o_ref, w_scr, *, k0n, mcls):
    j_cls = HY_CLASSES
    tn = o_ref.shape[3]
    for j in range(j_cls):
        rows = slice(j * mcls, (j + 1) * mcls)
        prod = jnp.dot(cs_ref[j], jnp.concatenate([z_ref[0, 0, rows, :], z_ref[0, 1, rows, :]], axis=1),
                       preferred_element_type=F32)
        w_scr[j, 0:k0n, :] = prod[0:k0n, 0:tn] + prod[k0n:2 * k0n, tn:2 * tn]
        w_scr[j, k0n:2 * k0n, :] = prod[0:k0n, tn:2 * tn] - prod[k0n:2 * k0n, 0:tn]

    def body(c, carry):
        for half in range(PACKED_SUBLANE // SUBLANE):
            r8 = pl.ds(pl.multiple_of(c * PACKED_SUBLANE + half * SUBLANE, SUBLANE), SUBLANE)
            i8 = pl.ds(pl.multiple_of(k0n + c * PACKED_SUBLANE + half * SUBLANE, SUBLANE), SUBLANE)
            xs = [(w_scr[j, r8, :], w_scr[j, i8, :]) for j in range(j_cls)]
            spec = _fft_list(xs, -1)
            prod = []
            for q in range(j_cls):
                rows = pl.ds(pl.multiple_of(q * k0n + c * PACKED_SUBLANE, PACKED_SUBLANE), PACKED_SUBLANE)
                part = slice(half * SUBLANE, (half + 1) * SUBLANE)
                kq = (kre_ref[rows, :].astype(F32)[part], kim_ref[rows, :].astype(F32)[part])
                prod.append(_cmul(spec[q], kq))
            back = _fft_list(prod, 1)
            for j in range(j_cls):
                w_scr[j, r8, :] = back[j][0]
                w_scr[j, i8, :] = back[j][1]
        return carry

    lax.fori_loop(0, k0n // PACKED_SUBLANE, body, 0)

    bias = bias_ref[...]
    for j in range(j_cls):
        rows = slice(j * mcls, (j + 1) * mcls)
        v2 = jnp.concatenate([w_scr[j, 0:k0n, :], w_scr[j, k0n:2 * k0n, :]], axis=1).astype(BF16)
        prod = jnp.dot(cst_ref[j], v2, preferred_element_type=F32)
        ys = (prod[0:mcls, 0:tn] - prod[mcls:2 * mcls, tn:2 * tn],
              prod[mcls:2 * mcls, 0:tn] + prod[0:mcls, tn:2 * tn])
        for b2 in range(2):
            conv = ys[b2] + z_ref[0, b2, rows, :].astype(F32) * bias
            o_ref[0, b2, rows, :] = (g_ref[0, b2, rows, :].astype(F32) * conv).astype(o_ref.dtype)


def _fftconv_call(tables, z, z_col0, gate, gate_col0, kre, kim, k_col0, bias):
    cs, cst = tables
    bsz, n, _ = z.shape
    hw = bias.shape[0]
    tn = LANE
    k0n, mcls = cs.shape[1] // 2, cs.shape[2]
    zo, go, ko = z_col0 // tn, gate_col0 // tn, k_col0 // tn
    z4 = z.reshape(bsz // 2, 2, n, z.shape[2])
    g4 = gate.reshape(bsz // 2, 2, n, gate.shape[2])
    const = lambda shape: pl.BlockSpec(shape, lambda j, p: (0,) * len(shape),
                                       pipeline_mode=pl.Buffered(1))
    out = pl.pallas_call(
        functools.partial(_fftconv_kernel, k0n=k0n, mcls=mcls),
        grid=(hw // tn, bsz // 2),
        in_specs=[
            const(cs.shape), const(cst.shape),
            pl.BlockSpec((1, 2, n, tn), lambda j, p: (p, 0, 0, zo + j)),
            pl.BlockSpec((1, 2, n, tn), lambda j, p: (p, 0, 0, go + j)),
            pl.BlockSpec((2 * n, tn), lambda j, p: (0, ko + j)),
            pl.BlockSpec((2 * n, tn), lambda j, p: (0, ko + j)),
            pl.BlockSpec((1, tn), lambda j, p: (0, j)),
        ],
        out_specs=pl.BlockSpec((1, 2, n, tn), lambda j, p: (p, 0, 0, j)),
        out_shape=jax.ShapeDtypeStruct((bsz // 2, 2, n, hw), BF16),
        scratch_shapes=[pltpu.VMEM((HY_CLASSES, 2 * k0n, tn), F32)],
        compiler_params=_params(("arbitrary", "arbitrary")),
        name="fft_conv",
    )(cs, cst, z4, g4, kre, kim, bias.reshape(1, hw))
    return out.reshape(bsz, n, hw)


def _wout_kernel(r_ref, y_ref, wa_ref, wb_ref, x_ref, gate_ref, o_ref):
    y = (jnp.dot(r_ref[0], wa_ref[...], preferred_element_type=F32)
         + jnp.dot(y_ref[0], wb_ref[...], preferred_element_type=F32))
    o_ref[0] = x_ref[0] + gate_ref[0] * y


def _wout_call(r, yh, w_out, x, gate):
    bsz, n, d = x.shape
    rw = r.shape[2]
    assert d == 2 * rw and yh.shape[2] == rw
    tm = _tile(n, ROW_TILE)
    return pl.pallas_call(
        _wout_kernel,
        grid=(bsz, n // tm),
        in_specs=[
            pl.BlockSpec((1, tm, rw), lambda b, i: (b, i, 0)),
            pl.BlockSpec((1, tm, d - rw), lambda b, i: (b, i, 0)),
            pl.BlockSpec((rw, d), lambda b, i: (0, 0)),
            pl.BlockSpec((rw, d), lambda b, i: (1, 0)),
            pl.BlockSpec((1, tm, d), lambda b, i: (b, i, 0)),
            pl.BlockSpec((1, 1, d), lambda b, i: (b, 0, 0)),
        ],
        out_specs=pl.BlockSpec((1, tm, d), lambda b, i: (b, i, 0)),
        out_shape=jax.ShapeDtypeStruct((bsz, n, d), F32),
        compiler_params=_params(("arbitrary", "arbitrary")),
        name="mixer_out",
    )(r, yh, w_out, w_out, x, gate)


def _pool_kernel(h_ref, mc_ref, icol_ref, w_ref, sc_ref, x_ref, gate_ref, o_ref, cp_scr, pm_scr,
                 *, n, tm, to):
    g = pl.program_id(1)
    i = pl.program_id(2)
    n_rows = n // GRID_W

    @pl.when(i == 0)
    def _():
        for t in range(n // tm):
            rows = slice(t * tm, (t + 1) * tm)
            cp_scr[rows, :] = jnp.dot(mc_ref[0], h_ref[0, rows, :], preferred_element_type=F32)

    def grid_row(r):
        return slice(r * GRID_W, (r + 1) * GRID_W)

    for gi, win in enumerate(POOL_WINDOWS):
        @pl.when(jnp.logical_and(i == 0, g == gi))
        def _(win=win):
            lo, hi = win // 2, win - 1 - win // 2
            icol = icol_ref[0]
            acc = cp_scr[grid_row(0), :]
            for r in range(1, hi + 1):
                acc = acc + cp_scr[grid_row(r), :]
            for r in range(n_rows):
                if r > 0 and r + hi < n_rows:
                    acc = acc + cp_scr[grid_row(r + hi), :]
                if r - lo - 1 >= 0:
                    acc = acc - cp_scr[grid_row(r - lo - 1), :]
                cnt_r = min(r + hi, n_rows - 1) + 1 - max(r - lo, 0)
                mean = acc * (icol * (1.0 / cnt_r))
                pm_scr[grid_row(r), :] = (mean - h_ref[0, grid_row(r), :].astype(F32)).astype(BF16)

    rows = pl.ds(pl.multiple_of(i * to, to), to)
    y = jnp.dot(pm_scr[rows, :], w_ref[0], preferred_element_type=F32) * sc_ref[...]
    o_ref[0] = x_ref[0] + gate_ref[0] * y


def _pool_tables(tm, pg):
    tok = np.arange(tm)
    r_t, c_t = tok // GRID_W, tok % GRID_W
    mats, inv = [], []
    for win in POOL_WINDOWS:
        lo, hi = -(win // 2), win - 1 - win // 2
        dc = c_t[None, :] - c_t[:, None]
        mats.append(((r_t[None, :] == r_t[:, None]) & (dc >= lo) & (dc <= hi)).astype(np.float32))
        idx_c = np.arange(GRID_W)
        cnt_c = np.minimum(idx_c + hi, GRID_W - 1) + 1 - np.maximum(idx_c + lo, 0)
        inv.append(np.broadcast_to((1.0 / cnt_c)[:, None], (GRID_W, pg)))
    return (jnp.asarray(np.stack(mats), dtype=BF16),
            jnp.asarray(np.stack(inv).astype(np.float32)))


def _pool_call(h, pool_w, pool_scale, x, gate):
    bsz, n, d = x.shape
    ng = len(POOL_WINDOWS)
    pg = d // ng
    tm = _tile(n, ROW_TILE)
    to = _tile(n, WIDE_TILE)
    assert tm % GRID_W == 0 and n % GRID_W == 0
    mats, icol = _pool_tables(tm, pg)
    return pl.pallas_call(
        functools.partial(_pool_kernel, n=n, tm=tm, to=to),
        grid=(bsz, ng, n // to),
        in_specs=[
            pl.BlockSpec((1, n, pg), lambda b, g, i: (b, 0, g)),
            pl.BlockSpec((1, tm, tm), lambda b, g, i: (g, 0, 0)),
            pl.BlockSpec((1, GRID_W, pg), lambda b, g, i: (g, 0, 0)),
            pl.BlockSpec((1, pg, pg), lambda b, g, i: (g, 0, 0)),
            pl.BlockSpec((1, pg), lambda b, g, i: (0, g)),
            pl.BlockSpec((1, to, pg), lambda b, g, i: (b, i, g)),
            pl.BlockSpec((1, 1, pg), lambda b, g, i: (b, 0, g)),
        ],
        out_specs=pl.BlockSpec((1, to, pg), lambda b, g, i: (b, i, g)),
        out_shape=jax.ShapeDtypeStruct((bsz, n, d), F32),
        scratch_shapes=[pltpu.VMEM((n, pg), F32), pltpu.VMEM((n, pg), BF16)],
        compiler_params=_params(("arbitrary", "arbitrary", "arbitrary")),
        name="pool_mixer",
    )(h, mats, icol, pool_w, pool_scale.reshape(1, d), x, gate)


def _rotary_tables(n):
    pos = np.arange(n)
    row = (pos // GRID_W).astype(np.float32)
    col = (pos % GRID_W).astype(np.float32)
    n_freq = HEAD_DIM // 4
    inv = (np.float32(ROPE_BASE) ** (-np.arange(n_freq, dtype=np.float32) / np.float32(n_freq))).astype(np.float32)
    ang = np.concatenate([row[:, None] * inv, col[:, None] * inv], axis=-1).astype(np.float32)
    cos, sin = np.cos(ang.astype(np.float64)), np.sin(ang.astype(np.float64))
    cos2 = np.concatenate([cos, cos], axis=-1).astype(np.float32)
    sin2 = np.concatenate([-sin, sin], axis=-1).astype(np.float32)
    return jnp.asarray(cos2), jnp.asarray(sin2)


def kernel(x, c, ctx, c_ctx, w_mod, b_mod, ffn1_w1, ffn1_w3, ffn1_w2, ffn2_w1, ffn2_w3, ffn2_w2, ab_w_in, ab_w_out, ret_log_decay, hy_conv_w, hy_conv_b, hy_f_w1, hy_f_b1, hy_f_w2, hy_f_b2, hy_f_w3, hy_f_b3, hy_f_freq, hy_f_w4, hy_bias, pool_w, pool_scale, final_gain):
    bsz, n, d = x.shape
    n_ctx = ctx.shape[1]
    depth = w_mod.shape[0]
    assert depth == 2, "layer 0 = retention/Hyena mixer, layer 1 = pooling mixer"
    ret_w = RET_HEADS * HEAD_DIM
    hw = d - ret_w
    assert hy_f_w4.shape[2] == HY_ORDER * 2 * hw and bsz % 2 == 0

    c_all = jnp.concatenate([c, c_ctx[None, :], jnp.zeros((SUBLANE - bsz - 1, d), F32)], axis=0)
    mod = _mod_call(c_all, w_mod, b_mod).reshape(depth, SUBLANE, N_MOD, d)

    def mod_rows(layer, base, count=3):
        return mod[layer, :bsz, base:base + count]

    def mod_ctx(base, count=3):
        return mod[0, bsz:bsz + 1, base:base + count]

    n_f = ffn1_w1.shape[2] // _tile(ffn1_w1.shape[2], FFN_TF)
    w13 = _cast_pairs_call([(ffn1_w1, ffn1_w3), (ffn2_w1, ffn2_w3)], n_f)
    w_in_shape, w_out_shape = ab_w_in.shape[1:], ab_w_out.shape[1:]
    w2a, w2b, w_in, w_out = _cast_call(
        [ffn1_w2, ffn2_w2, ab_w_in[0].reshape(depth, -1, w_in_shape[1]),
         ab_w_out[0].reshape(depth, -1, w_out_shape[1])], CAST_STEPS)
    w_in, w_out = w_in.reshape(w_in_shape), w_out.reshape(w_out_shape)
    ffn1_w, ffn2_w = (w13[0], w2a), (w13[1], w2b)

    x, h = _ffn_call(x, mod_rows(0, 0), *ffn1_w, 0, modn=mod_rows(0, 3))
    _, h_c = _ffn_call(ctx.reshape(1, bsz * n_ctx, d), mod_ctx(0), *ffn1_w, 0, modn=mod_ctx(3))

    p = _matmul_call(h.reshape(bsz * n, d), w_in, BF16, b_cols=(0, 4 * ret_w)).reshape(bsz, n, -1)
    kvc = _matmul_call(h_c.reshape(bsz * n_ctx, d), w_in, BF16, b_cols=(ret_w, 3 * ret_w))
    kvc = kvc.reshape(bsz, n_ctx, 2 * ret_w)

    cos2, sin2 = _rotary_tables(n)
    r = _ret_call(p, kvc, ret_log_decay[0], cos2, sin2, ret_w)

    mcls = n // HY_CLASSES
    h_cm = h.reshape(bsz, mcls, HY_CLASSES, d).transpose(0, 2, 1, 3).reshape(bsz * n, d)
    p_hy = _matmul_call(h_cm, w_in, BF16, b_cols=(4 * ret_w, w_in.shape[1])).reshape(bsz, n, -1)
    hf, hb = _filt_call(n, hy_f_w1[0], hy_f_b1[0], hy_f_w2[0], hy_f_b2[0], hy_f_w3[0],
                        hy_f_b3[0], hy_f_freq[0], hy_f_w4[0], hw)
    tables = _fft_tables(n)
    kre, kim = _spec_call(tables, hf, hb)
    u = _sconv_call(p_hy, hy_conv_w[0], hy_conv_b[0])
    z = _fftconv_call(tables, u, 0, u, hw, kre, kim, 0, hy_bias[0, 0])
    yh = _fftconv_call(tables, z, 0, u, 2 * hw, kre, kim, hw, hy_bias[0, 1])
    yh = yh.reshape(bsz, HY_CLASSES, mcls, hw).transpose(0, 2, 1, 3).reshape(bsz, n, hw)

    x = _wout_call(r, yh, w_out, x, mod_rows(0, 5, 1))
    x = _ffn_call(x, mod_rows(0, 6), *ffn2_w, 0)

    x, h = _ffn_call(x, mod_rows(1, 0), *ffn1_w, 1, modn=mod_rows(1, 3))
    x = _pool_call(h, pool_w[0].astype(BF16), pool_scale[0], x, mod_rows(1, 5, 1))
    return _ffn_call(x, mod_rows(1, 6), *ffn2_w, 1, gain=final_gain)
```

```python
import functools
import math

import numpy as np
import jax
import jax.numpy as jnp
from jax import lax
from jax.experimental import pallas as pl
from jax.experimental.pallas import tpu as pltpu

F32 = jnp.float32
BF16 = jnp.bfloat16
HIGHEST = lax.Precision.HIGHEST

EPS = 1e-6
GRID_W = 64
N_MOD = 9
RET_HEADS = 8
HEAD_DIM = 128
HY_ORDER = 2
HY_SHORT = 3
HY_EMB_DIM = 33
HY_DECAY_SHORT_PCT = 0.3
HY_DECAY_LONG_PCT = 1.5
HY_DECAY_TARGET = 1e-2
ROPE_BASE = 10000.0
POOL_WINDOWS = (2, 4, 8, 16)
RET_CHUNK = 256
RET_UNROLL = 16
HY_CLASSES = 16
LANE = 128
SUBLANE = 8
PACKED_SUBLANE = 16
ROW_TILE = 512
WIDE_TILE = 1024
FFN_TF = 512
FILTER_ROWS = 256
CAST_STEPS = 16
VMEM_LIMIT = 56 * 1024 * 1024
FFN_VMEM_LIMIT = 60 * 1024 * 1024


def _tile(dim, pref):
    if dim <= pref:
        return dim
    for t in range(pref - pref % LANE, 0, -LANE):
        if dim % t == 0:
            return t
    raise ValueError((dim, pref))


def _params(sem, vmem_limit=VMEM_LIMIT):
    return pltpu.CompilerParams(dimension_semantics=sem, vmem_limit_bytes=vmem_limit)


def _rms(x):
    return x * lax.rsqrt(jnp.mean(x * x, axis=-1, keepdims=True) + EPS)


def _silu(x):
    return x * jax.nn.sigmoid(x)


def _mod_kernel(c_ref, w_ref, b_ref, o_ref):
    sc = _silu(c_ref[...])
    w = w_ref[0]
    sc_hi = sc.astype(BF16)
    sc_lo = (sc - sc_hi.astype(F32)).astype(BF16)
    w_hi = w.astype(BF16)
    w_lo = (w - w_hi.astype(F32)).astype(BF16)
    rows = sc.shape[0]
    top = jnp.dot(jnp.concatenate([sc_hi, sc_lo], axis=0), w_hi, preferred_element_type=F32)
    o_ref[0] = (top[0:rows] + top[rows:2 * rows]
                + jnp.dot(sc_hi, w_lo, preferred_element_type=F32) + b_ref[0])


def _mod_call(c_all, w_mod, b_mod):
    depth, d, nd = w_mod.shape
    rows = c_all.shape[0]
    tn = _tile(nd, WIDE_TILE)
    return pl.pallas_call(
        _mod_kernel,
        grid=(depth, nd // tn),
        in_specs=[
            pl.BlockSpec((rows, d), lambda l, j: (0, 0)),
            pl.BlockSpec((1, d, tn), lambda l, j: (l, 0, j)),
            pl.BlockSpec((1, 1, tn), lambda l, j: (l, 0, j)),
        ],
        out_specs=pl.BlockSpec((1, rows, tn), lambda l, j: (l, 0, j)),
        out_shape=jax.ShapeDtypeStruct((depth, rows, nd), F32),
        compiler_params=_params(("arbitrary", "arbitrary")),
        name="mod",
    )(c_all, w_mod, b_mod.reshape(depth, 1, nd))


def _ffn_kernel(*refs, n_f, emit_h, final, row_chunks):
    x_ref, mod_ref, w13_ref, w2_ref = refs[:4]
    pos = 4
    modn_ref = gain_ref = hn_ref = None
    if emit_h:
        modn_ref = refs[pos]
        pos += 1
    if final:
        gain_ref = refs[pos]
        pos += 1
    o_ref = refs[pos]
    pos += 1
    if emit_h:
        hn_ref = refs[pos]
        pos += 1
    h_scr = refs[pos]
    acc = o_ref.at[0] if row_chunks > 1 else refs[pos + 1]
    f = pl.program_id(2)
    tm = h_scr.shape[0]
    chunks = [slice(r * (tm // row_chunks), (r + 1) * (tm // row_chunks)) for r in range(row_chunks)]

    assert n_f >= 3

    tf = w2_ref.shape[0]

    def partial_out(h):
        ab = jnp.dot(h, w13_ref[...], preferred_element_type=F32)
        g = (_silu(ab[:, 0:tf]) * ab[:, tf:2 * tf]).astype(BF16)
        return jnp.dot(g, w2_ref[...], preferred_element_type=F32)

    @pl.when(f == 0)
    def _():
        m = mod_ref[0]
        for rows in chunks:
            h = (_rms(x_ref[0, rows, :]) * (1.0 + m[1:2]) + m[0:1]).astype(BF16)
            h_scr[rows, :] = h
            acc[rows, :] = partial_out(h)

    @pl.when(jnp.logical_and(f > 0, f < n_f - 1))
    def _():
        for rows in chunks:
            acc[rows, :] += partial_out(h_scr[rows, :])

    @pl.when(f == n_f - 1)
    def _():
        m = mod_ref[0]
        for rows in chunks:
            xo = x_ref[0, rows, :] + 0.5 * m[2:3] * (acc[rows, :] + partial_out(h_scr[rows, :]))
            if emit_h:
                mn = modn_ref[0]
                hn_ref[0, rows, :] = (_rms(xo) * (1.0 + mn[1:2]) + mn[0:1]).astype(BF16)
            if final:
                xo = _rms(xo) * gain_ref[...]
            o_ref[0, rows, :] = xo


def _ffn_call(x, mod3, w13, w2, layer, modn=None, gain=None):
    bsz, s, d = x.shape
    n_f, tf = w13.shape[1], w13.shape[3] // 2
    emit_h = modn is not None
    final = gain is not None
    tm = _tile(s, WIDE_TILE)
    row_chunks = tm // _tile(tm, ROW_TILE)
    in_specs = [
        pl.BlockSpec((1, tm, d), lambda b, i, f: (b, i, 0)),
        pl.BlockSpec((1, 3, d), lambda b, i, f: (b, 0, 0)),
        pl.BlockSpec((None, None, d, 2 * tf), lambda b, i, f: (layer, f, 0, 0)),
        pl.BlockSpec((None, tf, d), lambda b, i, f: (layer, f, 0)),
    ]
    args = [x, mod3, w13, w2]
    if emit_h:
        in_specs.append(pl.BlockSpec((1, 3, d), lambda b, i, f: (b, 0, 0)))
        args.append(modn)
    if final:
        in_specs.append(pl.BlockSpec((1, d), lambda b, i, f: (0, 0)))
        args.append(gain.reshape(1, d))
    out_specs = [pl.BlockSpec((1, tm, d), lambda b, i, f: (b, i, 0))]
    out_shape = [jax.ShapeDtypeStruct((bsz, s, d), F32)]
    if emit_h:
        out_specs.append(pl.BlockSpec((1, tm, d), lambda b, i, f: (b, i, 0)))
        out_shape.append(jax.ShapeDtypeStruct((bsz, s, d), BF16))
    outs = pl.pallas_call(
        functools.partial(_ffn_kernel, n_f=n_f, emit_h=emit_h, final=final, row_chunks=row_chunks),
        grid=(bsz, s // tm, n_f),
        in_specs=in_specs,
        out_specs=out_specs,
        out_shape=out_shape,
        scratch_shapes=[pltpu.VMEM((tm, d), BF16)] + ([] if row_chunks > 1 else [pltpu.VMEM((tm, d), F32)]),
        compiler_params=_params(("arbitrary", "arbitrary", "arbitrary"), FFN_VMEM_LIMIT),
        name="ffn",
    )(*args)
    return outs if emit_h else outs[0]


def _cast_kernel(*refs):
    half = len(refs) // 2
    for src, dst in zip(refs[:half], refs[half:]):
        dst[...] = src[...].astype(dst.dtype)


def _cast_call(arrays, steps):
    depth = arrays[0].shape[0]
    specs = []
    for a in arrays:
        assert a.shape[0] == depth and a.shape[1] % (PACKED_SUBLANE * steps) == 0
        specs.append(pl.BlockSpec((1, a.shape[1] // steps, a.shape[2]), lambda l, j: (l, j, 0)))
    return pl.pallas_call(
        _cast_kernel,
        grid=(depth, steps),
        in_specs=specs,
        out_specs=specs,
        out_shape=[jax.ShapeDtypeStruct(a.shape, BF16) for a in arrays],
        compiler_params=_params(("arbitrary", "arbitrary")),
        name="cast_weights",
    )(*arrays)


def _cast_pairs_kernel(*refs):
    n_out = len(refs) // 3
    for k in range(n_out):
        wa, wb, dst = refs[2 * k], refs[2 * k + 1], refs[2 * n_out + k]
        tf = wa.shape[2]
        dst[0, 0, :, 0:tf] = wa[0].astype(dst.dtype)
        dst[0, 0, :, tf:2 * tf] = wb[0].astype(dst.dtype)


def _cast_pairs_call(pairs, steps):
    depth, r, c = pairs[0][0].shape
    tf = c // steps
    in_spec = pl.BlockSpec((1, r, tf), lambda l, j: (l, 0, j))
    out_spec = pl.BlockSpec((1, 1, r, 2 * tf), lambda l, j: (l, j, 0, 0))
    return pl.pallas_call(
        _cast_pairs_kernel,
        grid=(depth, steps),
        in_specs=[in_spec] * (2 * len(pairs)),
        out_specs=[out_spec] * len(pairs),
        out_shape=[jax.ShapeDtypeStruct((depth, steps, r, 2 * tf), BF16)] * len(pairs),
        compiler_params=_params(("arbitrary", "arbitrary")),
        name="cast_weight_pairs",
    )(*[w for pair in pairs for w in pair])


def _matmul_kernel(a_ref, b_ref, o_ref):
    o_ref[...] = jnp.dot(a_ref[...], b_ref[...], preferred_element_type=F32).astype(o_ref.dtype)


def _matmul_call(a, b, out_dtype, tm_pref=WIDE_TILE, tn_pref=WIDE_TILE, b_cols=None):
    m, k = a.shape
    c0, c1 = b_cols if b_cols is not None else (0, b.shape[1])
    n = c1 - c0
    tm = _tile(m, tm_pref)
    tn = math.gcd(_tile(n, tn_pref), c0) if c0 else _tile(n, tn_pref)
    assert tn % LANE == 0
    joff = c0 // tn
    return pl.pallas_call(
        _matmul_kernel,
        grid=(m // tm, n // tn),
        in_specs=[
            pl.BlockSpec((tm, k), lambda i, j: (i, 0)),
            pl.BlockSpec((k, tn), lambda i, j: (0, joff + j)),
        ],
        out_specs=pl.BlockSpec((tm, tn), lambda i, j: (i, j)),
        out_shape=jax.ShapeDtypeStruct((m, n), out_dtype),
        compiler_params=_params(("arbitrary", "arbitrary")),
        name="matmul",
    )(a, b)


def _ret_kernel(lg_ref, q_ref, k_ref, v_ref, g_ref, cos_ref, sin_ref, kc_ref, vc_ref, o_ref,
                qr_scr, kr_scr, upd_scr, st_scr, *, n, n_ctx, chunk):
    hd = HEAD_DIM
    head = pl.program_id(1)
    lgf = lg_ref[0, head]
    lgb = lg_ref[1, head]
    nc = n // chunk
    k_scale = hd ** -0.5

    posc = lax.broadcasted_iota(jnp.int32, (n_ctx, hd), 0).astype(F32)
    kc = kc_ref[0].astype(F32) * k_scale
    vc = vc_ref[0]
    kcf = (kc * jnp.exp((n_ctx - 1.0 - posc) * lgf)).T.astype(BF16)
    kcb = (kc * jnp.exp(posc * lgb)).T.astype(BF16)
    s0_f = jnp.dot(kcf, vc, preferred_element_type=F32)
    s0_b = jnp.dot(kcb, vc, preferred_element_type=F32)

    pos = lax.broadcasted_iota(jnp.int32, (chunk, hd), 0).astype(F32)
    qdec_f = jnp.exp((pos + 1.0) * lgf)
    kdec_f = jnp.exp((chunk - 1.0 - pos) * lgf)
    qdec_b = jnp.exp((chunk - pos) * lgb)
    kdec_b = jnp.exp(pos * lgb)
    cdec_f = jnp.exp(jnp.full((1, hd), chunk, F32) * lgf)
    cdec_b = jnp.exp(jnp.full((1, hd), chunk, F32) * lgb)
    ri = lax.broadcasted_iota(jnp.int32, (chunk, chunk), 0)
    ci = lax.broadcasted_iota(jnp.int32, (chunk, chunk), 1)
    diff = (ri - ci).astype(F32)
    dmat = (jnp.where(diff >= 0, jnp.exp(jnp.maximum(diff, 0.0) * lgf), 0.0)
            + jnp.where(diff <= 0, jnp.exp(jnp.maximum(-diff, 0.0) * lgb), 0.0))

    def rotary(x, rows):
        return x * cos_ref[rows, :] + pltpu.roll(x, hd // 2, 1) * sin_ref[rows, :]

    def chunk_rows(c):
        return pl.ds(pl.multiple_of(c * chunk, chunk), chunk)

    def prep_body(c, carry):
        rows = chunk_rows(c)
        qr_scr[rows, :] = rotary(q_ref[0, rows, :].astype(F32), rows)
        k = rotary(k_ref[0, rows, :].astype(F32), rows) * k_scale
        kr_scr[rows, :] = k
        kk = jnp.concatenate([k * kdec_f, k * kdec_b], axis=1).T.astype(BF16)
        upd_scr[c] = jnp.dot(kk, v_ref[0, rows, :], preferred_element_type=F32)
        return carry

    lax.fori_loop(0, nc, prep_body, 0, unroll=math.gcd(nc, RET_UNROLL))

    def fwd_scan(c, sf):
        st_scr[c, 0:hd, :] = sf.astype(BF16)
        return cdec_f * sf + upd_scr[c, 0:hd, :]

    def bwd_scan(j, sb):
        c = nc - 1 - j
        st_scr[c, hd:2 * hd, :] = sb.astype(BF16)
        return cdec_b * sb + upd_scr[c, hd:2 * hd, :]

    lax.fori_loop(0, nc, fwd_scan, s0_f, unroll=math.gcd(nc, RET_UNROLL))
    lax.fori_loop(0, nc, bwd_scan, s0_b, unroll=math.gcd(nc, RET_UNROLL))

    def out_body(c, carry):
        rows = chunk_rows(c)
        q = qr_scr[rows, :]
        s = lax.dot_general(q.astype(BF16), kr_scr[rows, :].astype(BF16), (((1,), (1,)), ((), ())),
                            preferred_element_type=F32) * dmat
        qq = jnp.concatenate([q * qdec_f, q * qdec_b], axis=1).astype(BF16)
        o = (jnp.dot(s.astype(BF16), v_ref[0, rows, :], preferred_element_type=F32)
             + jnp.dot(qq, st_scr[c], preferred_element_type=F32))
        g = g_ref[0, rows, :].astype(F32)
        o_ref[0, rows, :] = (_rms(o) * _silu(g)).astype(o_ref.dtype)
        return carry

    lax.fori_loop(0, nc, out_body, 0, unroll=math.gcd(nc, RET_UNROLL))


def _ret_call(p, kvc, log_decay, cos2, sin2, ret_w):
    bsz, n, _ = p.shape
    n_ctx = kvc.shape[1]
    heads = ret_w // HEAD_DIM
    hd = HEAD_DIM
    chunk = _tile(n, RET_CHUNK)
    kern = functools.partial(_ret_kernel, n=n, n_ctx=n_ctx, chunk=chunk)
    col = lambda off: (lambda b, h, lg: (b, 0, off * heads + h))
    return pl.pallas_call(
        kern,
        grid_spec=pltpu.PrefetchScalarGridSpec(
            num_scalar_prefetch=1,
            grid=(bsz, heads),
            in_specs=[
                pl.BlockSpec((1, n, hd), col(0)),
                pl.BlockSpec((1, n, hd), col(1)),
                pl.BlockSpec((1, n, hd), col(2)),
                pl.BlockSpec((1, n, hd), col(3)),
                pl.BlockSpec((n, hd), lambda b, h, lg: (0, 0)),
                pl.BlockSpec((n, hd), lambda b, h, lg: (0, 0)),
                pl.BlockSpec((1, n_ctx, hd), col(0)),
                pl.BlockSpec((1, n_ctx, hd), col(1)),
            ],
            out_specs=pl.BlockSpec((1, n, hd), lambda b, h, lg: (b, 0, h)),
            scratch_shapes=[
                pltpu.VMEM((n, hd), F32),
                pltpu.VMEM((n, hd), F32),
                pltpu.VMEM((n // chunk, 2 * hd, hd), F32),
                pltpu.VMEM((n // chunk, 2 * hd, hd), BF16),
            ],
        ),
        out_shape=jax.ShapeDtypeStruct((bsz, n, ret_w), BF16),
        compiler_params=_params(("arbitrary", "arbitrary")),
        name="retention",
    )(log_decay, p, p, p, p, cos2, sin2, kvc, kvc)


def _cmul(a, b):
    return a[0] * b[0] - a[1] * b[1], a[0] * b[1] + a[1] * b[0]


def _cmul_const(a, c, s):
    eps = 1e-12
    if abs(s) < eps:
        return (a[0], a[1]) if c > 0 else (-a[0], -a[1])
    if abs(c) < eps:
        return (-a[1], a[0]) if s > 0 else (a[1], -a[0])
    return a[0] * c - a[1] * s, a[0] * s + a[1] * c


def _fft_list(xs, sign):
    size = len(xs)
    if size == 1:
        return xs
    even = _fft_list(xs[0::2], sign)
    odd = _fft_list(xs[1::2], sign)
    out = [None] * size
    for k in range(size // 2):
        ang = sign * 2.0 * math.pi * k / size
        tr, ti = _cmul_const(odd[k], math.cos(ang), math.sin(ang))
        out[k] = (even[k][0] + tr, even[k][1] + ti)
        out[k + size // 2] = (even[k][0] - tr, even[k][1] - ti)
    return out


def _fft_tables(n):
    j_cls = HY_CLASSES
    k0n, mcls = 2 * n // j_cls, n // j_cls
    t = j_cls * np.arange(mcls)[None, None, :] + np.arange(j_cls)[:, None, None]
    ang = 2.0 * math.pi * ((np.arange(k0n)[None, :, None] * t) % (2 * n)) / (2 * n)
    cs = np.concatenate([np.cos(ang), np.sin(ang)], axis=1)
    cst = np.concatenate([np.cos(ang).transpose(0, 2, 1), np.sin(ang).transpose(0, 2, 1)], axis=1)
    return jnp.asarray(cs, dtype=BF16), jnp.asarray(cst / (2 * n), dtype=BF16)


def _filt_kernel(z_ref, w1_ref, b1_ref, w2_ref, b2_ref, w3_ref, b3_ref, fr_ref, w4_ref, dl_ref,
                 hf_ref, hb_ref, *, n, tt, hw):
    i = pl.program_id(0)
    dot = functools.partial(jnp.dot, preferred_element_type=F32, precision=HIGHEST)
    fr = fr_ref[...]
    a = jnp.sin(fr[0:1] * (dot(z_ref[...], w1_ref[...]) + b1_ref[...]))
    a = jnp.sin(fr[1:2] * (dot(a, w2_ref[...]) + b2_ref[...]))
    a = jnp.sin(fr[2:3] * (dot(a, w3_ref[...]) + b3_ref[...]))
    hf = jnp.dot(a.astype(BF16), w4_ref[...], preferred_element_type=F32)
    mcls = n // HY_CLASSES
    rho = lax.broadcasted_iota(jnp.int32, (tt, hw), 0) + i * tt
    ti = HY_CLASSES * (rho & (mcls - 1)) + lax.shift_right_logical(rho, int(math.log2(mcls)))
    t = ti.astype(F32) * (1.0 / (n - 1.0))
    window = jnp.exp(-t * dl_ref[...])
    for o in range(HY_ORDER):
        hf_ref[:, o * hw:(o + 1) * hw] = (hf[:, (2 * o) * hw:(2 * o + 1) * hw] * window).astype(BF16)
        h_b = jnp.where(ti == 0, 0.0, hf[:, (2 * o + 1) * hw:(2 * o + 2) * hw] * window)
        hb_ref[:, o * hw:(o + 1) * hw] = h_b.astype(BF16)


def _filt_call(n, fw1, fb1, fw2, fb2, fw3, fb3, freq, fw4, hw):
    hid = fw1.shape[1]
    pk = LANE
    mcls = n // HY_CLASSES
    assert mcls & (mcls - 1) == 0
    bands = (HY_EMB_DIM - 1) // 2
    tpos = np.linspace(0.0, 1.0, n, dtype=np.float32)[:, None].astype(np.float64)
    fb = np.linspace(1e-4, bands - 1, bands, dtype=np.float32)[None, :].astype(np.float64)
    wv = (2.0 * math.pi * np.arange(n, dtype=np.float64)[:, None] / n)
    z = np.concatenate([tpos, np.cos(fb * wv), -np.sin(fb * wv)], axis=-1)
    rho = np.arange(n)
    t_of_row = HY_CLASSES * (rho % mcls) + rho // mcls
    z = np.pad(z, ((0, 0), (0, pk - HY_EMB_DIM))).astype(np.float32)[t_of_row]
    max_decay = math.log(HY_DECAY_TARGET) / HY_DECAY_SHORT_PCT
    min_decay = math.log(HY_DECAY_TARGET) / HY_DECAY_LONG_PCT
    deltas = np.abs(np.linspace(min_decay, max_decay, hw, dtype=np.float32)).reshape(1, hw)

    padc = lambda w: jnp.pad(w, ((0, 0), (0, pk - hid)))
    w1 = jnp.pad(fw1, ((0, pk - HY_EMB_DIM), (0, pk - hid)))
    w2 = jnp.pad(fw2, ((0, pk - hid), (0, pk - hid)))
    w3 = jnp.pad(fw3, ((0, pk - hid), (0, pk - hid)))
    w4 = jnp.pad(fw4, ((0, pk - hid), (0, 0))).astype(BF16)
    b1, b2, b3 = (padc(b.reshape(1, hid)) for b in (fb1, fb2, fb3))
    fr = jnp.pad(freq, ((0, SUBLANE - freq.shape[0]), (0, pk - hid)))
    tt = _tile(n, FILTER_ROWS)
    nw = HY_ORDER * hw
    full = lambda shape: pl.BlockSpec(shape, lambda i: (0, 0))
    return pl.pallas_call(
        functools.partial(_filt_kernel, n=n, tt=tt, hw=hw),
        grid=(n // tt,),
        in_specs=[
            pl.BlockSpec((tt, pk), lambda i: (i, 0)),
            full((pk, pk)), full((1, pk)), full((pk, pk)), full((1, pk)),
            full((pk, pk)), full((1, pk)), full((SUBLANE, pk)), full((pk, 2 * nw)), full((1, hw)),
        ],
        out_specs=[pl.BlockSpec((tt, nw), lambda i: (i, 0))] * 2,
        out_shape=[jax.ShapeDtypeStruct((n, nw), BF16)] * 2,
        compiler_params=_params(("arbitrary",)),
        name="hyena_filters",
    )(jnp.asarray(z), w1, b1, w2, b2, w3, b3, fr, w4, jnp.asarray(deltas))


def _sconv_kernel(p_ref, w_ref, b_ref, o_ref, *, n):
    w = w_ref[...]
    bias = b_ref[...]
    mcls = n // HY_CLASSES
    row = lax.broadcasted_iota(jnp.int32, (mcls, p_ref.shape[2]), 0)

    def cls(j):
        return p_ref[0, j * mcls:(j + 1) * mcls, :].astype(F32)

    for j in range(HY_CLASSES):
        cur = cls(j)
        if j > 0:
            prev = cls(j - 1)
        else:
            prev = jnp.where(row == 0, 0.0, pltpu.roll(cls(HY_CLASSES - 1), 1, 0))
        if j < HY_CLASSES - 1:
            nxt = cls(j + 1)
        else:
            nxt = jnp.where(row == mcls - 1, 0.0, pltpu.roll(cls(0), mcls - 1, 0))
        out = bias + prev * w[0:1] + cur * w[1:2] + nxt * w[2:3]
        o_ref[0, j * mcls:(j + 1) * mcls, :] = out.astype(o_ref.dtype)


def _sconv_call(p, conv_w, conv_b):
    bsz, n, width = p.shape
    tc = _tile(width, ROW_TILE)
    return pl.pallas_call(
        functools.partial(_sconv_kernel, n=n),
        grid=(bsz, width // tc),
        in_specs=[
            pl.BlockSpec((1, n, tc), lambda b, j: (b, 0, j)),
            pl.BlockSpec((HY_SHORT, tc), lambda b, j: (0, j)),
            pl.BlockSpec((1, tc), lambda b, j: (0, j)),
        ],
        out_specs=pl.BlockSpec((1, n, tc), lambda b, j: (b, 0, j)),
        out_shape=jax.ShapeDtypeStruct((bsz, n, width), BF16),
        compiler_params=_params(("arbitrary", "arbitrary")),
        name="short_conv",
    )(p, conv_w, conv_b.reshape(1, width))


def _spec_kernel(cs_ref, hf_ref, hb_ref, kre_ref, kim_ref, wf_scr, wb_scr,
                 *, k0n, mcls):
    j_cls = HY_CLASSES
    tn = hf_ref.shape[1]
    for j in range(j_cls):
        rows = slice(j * mcls, (j + 1) * mcls)
        prod = jnp.dot(cs_ref[j], jnp.concatenate([hf_ref[rows, :], hb_ref[rows, :]], axis=1),
                       preferred_element_type=F32)
        wf_scr[j, 0:k0n, :] = prod[0:k0n, 0:tn]
        wf_scr[j, k0n:2 * k0n, :] = -prod[k0n:2 * k0n, 0:tn]
        wb_scr[j, 0:k0n, :] = prod[0:k0n, tn:2 * tn]
        wb_scr[j, k0n:2 * k0n, :] = -prod[k0n:2 * k0n, tn:2 * tn]

    def body(c, carry):
        halves = []
        for half in range(PACKED_SUBLANE // SUBLANE):
            r8 = pl.ds(pl.multiple_of(c * PACKED_SUBLANE + half * SUBLANE, SUBLANE), SUBLANE)
            i8 = pl.ds(pl.multiple_of(k0n + c * PACKED_SUBLANE + half * SUBLANE, SUBLANE), SUBLANE)
            xs = [(wf_scr[j, r8, :], wf_scr[j, i8, :]) for j in range(j_cls)]
            ys = [(wb_scr[j, r8, :], wb_scr[j, i8, :]) for j in range(j_cls)]
            halves.append((_fft_list(xs, -1), _fft_list(ys, -1)))
        for q in range(j_cls):
            rows = pl.ds(pl.multiple_of(q * k0n + c * PACKED_SUBLANE, PACKED_SUBLANE), PACKED_SUBLANE)
            kre = jnp.concatenate([h[0][q][0] + h[1][q][0] for h in halves], axis=0)
            kim = jnp.concatenate([h[0][q][1] - h[1][q][1] for h in halves], axis=0)
            kre_ref[rows, :] = kre.astype(kre_ref.dtype)
            kim_ref[rows, :] = kim.astype(kim_ref.dtype)
        return carry

    lax.fori_loop(0, k0n // PACKED_SUBLANE, body, 0)


def _spec_call(tables, hf, hb):
    cs, _ = tables
    n, width = hf.shape
    tn = LANE
    k0n, mcls = cs.shape[1] // 2, cs.shape[2]
    const = lambda shape: pl.BlockSpec(shape, lambda j: (0,) * len(shape),
                                       pipeline_mode=pl.Buffered(1))
    return pl.pallas_call(
        functools.partial(_spec_kernel, k0n=k0n, mcls=mcls),
        grid=(width // tn,),
        in_specs=[
            const(cs.shape),
            pl.BlockSpec((n, tn), lambda j: (0, j)),
            pl.BlockSpec((n, tn), lambda j: (0, j)),
        ],
        out_specs=[pl.BlockSpec((2 * n, tn), lambda j: (0, j))] * 2,
        out_shape=[jax.ShapeDtypeStruct((2 * n, width), BF16)] * 2,
        scratch_shapes=[pltpu.VMEM((HY_CLASSES, 2 * k0n, tn), F32)] * 2,
        compiler_params=_params(("arbitrary",)),
        name="filter_spectrum",
    )(cs, hf, hb)


def _fftconv_kernel(cs_ref, cst_ref, z_ref, g_ref, kre_ref, kim_ref, bias_ref,
                    o_ref, w_scr, *, k0n, mcls):
    j_cls = HY_CLASSES
    tn = o_ref.shape[3]
    for j in range(j_cls):
        rows = slice(j * mcls, (j + 1) * mcls)
        prod = jnp.dot(cs_ref[j], jnp.concatenate([z_ref[0, 0, rows, :], z_ref[0, 1, rows, :]], axis=1),
                       preferred_element_type=F32)
        w_scr[j, 0:k0n, :] = prod[0:k0n, 0:tn] + prod[k0n:2 * k0n, tn:2 * tn]
        w_scr[j, k0n:2 * k0n, :] = prod[0:k0n, tn:2 * tn] - prod[k0n:2 * k0n, 0:tn]

    def body(c, carry):
        for half in range(PACKED_SUBLANE // SUBLANE):
            r8 = pl.ds(pl.multiple_of(c * PACKED_SUBLANE + half * SUBLANE, SUBLANE), SUBLANE)
            i8 = pl.ds(pl.multiple_of(k0n + c * PACKED_SUBLANE + half * SUBLANE, SUBLANE), SUBLANE)
            xs = [(w_scr[j, r8, :], w_scr[j, i8, :]) for j in range(j_cls)]
            spec = _fft_list(xs, -1)
            prod = []
            for q in range(j_cls):
                rows = pl.ds(pl.multiple_of(q * k0n + c * PACKED_SUBLANE, PACKED_SUBLANE), PACKED_SUBLANE)
                part = slice(half * SUBLANE, (half + 1) * SUBLANE)
                kq = (kre_ref[rows, :].astype(F32)[part], kim_ref[rows, :].astype(F32)[part])
                prod.append(_cmul(spec[q], kq))
            back = _fft_list(prod, 1)
            for j in range(j_cls):
                w_scr[j, r8, :] = back[j][0]
                w_scr[j, i8, :] = back[j][1]
        return carry

    lax.fori_loop(0, k0n // PACKED_SUBLANE, body, 0)

    bias = bias_ref[...]
    for j in range(j_cls):
        rows = slice(j * mcls, (j + 1) * mcls)
        v2 = jnp.concatenate([w_scr[j, 0:k0n, :], w_scr[j, k0n:2 * k0n, :]], axis=1).astype(BF16)
        prod = jnp.dot(cst_ref[j], v2, preferred_element_type=F32)
        ys = (prod[0:mcls, 0:tn] - prod[mcls:2 * mcls, tn:2 * tn],
              prod[mcls:2 * mcls, 0:tn] + prod[0:mcls, tn:2 * tn])
        for b2 in range(2):
            conv = ys[b2] + z_ref[0, b2, rows, :].astype(F32) * bias
            o_ref[0, b2, rows, :] = (g_ref[0, b2, rows, :].astype(F32) * conv).astype(o_ref.dtype)


def _fftconv_call(tables, z, z_col0, gate, gate_col0, kre, kim, k_col0, bias):
    cs, cst = tables
    bsz, n, _ = z.shape
    hw = bias.shape[0]
    tn = LANE
    k0n, mcls = cs.shape[1] // 2, cs.shape[2]
    zo, go, ko = z_col0 // tn, gate_col0 // tn, k_col0 // tn
    z4 = z.reshape(bsz // 2, 2, n, z.shape[2])
    g4 = gate.reshape(bsz // 2, 2, n, gate.shape[2])
    const = lambda shape: pl.BlockSpec(shape, lambda j, p: (0,) * len(shape),
                                       pipeline_mode=pl.Buffered(1))
    out = pl.pallas_call(
        functools.partial(_fftconv_kernel, k0n=k0n, mcls=mcls),
        grid=(hw // tn, bsz // 2),
        in_specs=[
            const(cs.shape), const(cst.shape),
            pl.BlockSpec((1, 2, n, tn), lambda j, p: (p, 0, 0, zo + j)),
            pl.BlockSpec((1, 2, n, tn), lambda j, p: (p, 0, 0, go + j)),
            pl.BlockSpec((2 * n, tn), lambda j, p: (0, ko + j)),
            pl.BlockSpec((2 * n, tn), lambda j, p: (0, ko + j)),
            pl.BlockSpec((1, tn), lambda j, p: (0, j)),
        ],
        out_specs=pl.BlockSpec((1, 2, n, tn), lambda j, p: (p, 0, 0, j)),
        out_shape=jax.ShapeDtypeStruct((bsz // 2, 2, n, hw), BF16),
        scratch_shapes=[pltpu.VMEM((HY_CLASSES, 2 * k0n, tn), F32)],
        compiler_params=_params(("arbitrary", "arbitrary")),
        name="fft_conv",
    )(cs, cst, z4, g4, kre, kim, bias.reshape(1, hw))
    return out.reshape(bsz, n, hw)


def _wout_kernel(r_ref, y_ref, wa_ref, wb_ref, x_ref, gate_ref, o_ref):
    y = (jnp.dot(r_ref[0], wa_ref[...], preferred_element_type=F32)
         + jnp.dot(y_ref[0], wb_ref[...], preferred_element_type=F32))
    o_ref[0] = x_ref[0] + gate_ref[0] * y


def _wout_call(r, yh, w_out, x, gate):
    bsz, n, d = x.shape
    rw = r.shape[2]
    assert d == 2 * rw and yh.shape[2] == rw
    tm = _tile(n, ROW_TILE)
    return pl.pallas_call(
        _wout_kernel,
        grid=(bsz, n // tm),
        in_specs=[
            pl.BlockSpec((1, tm, rw), lambda b, i: (b, i, 0)),
            pl.BlockSpec((1, tm, d - rw), lambda b, i: (b, i, 0)),
            pl.BlockSpec((rw, d), lambda b, i: (0, 0)),
            pl.BlockSpec((rw, d), lambda b, i: (1, 0)),
            pl.BlockSpec((1, tm, d), lambda b, i: (b, i, 0)),
            pl.BlockSpec((1, 1, d), lambda b, i: (b, 0, 0)),
        ],
        out_specs=pl.BlockSpec((1, tm, d), lambda b, i: (b, i, 0)),
        out_shape=jax.ShapeDtypeStruct((bsz, n, d), F32),
        compiler_params=_params(("arbitrary", "arbitrary")),
        name="mixer_out",
    )(r, yh, w_out, w_out, x, gate)


def _pool_kernel(h_ref, mc_ref, icol_ref, w_ref, sc_ref, x_ref, gate_ref, o_ref, cp_scr, pm_scr,
                 *, n, tm, to):
    g = pl.program_id(1)
    i = pl.program_id(2)
    n_rows = n // GRID_W

    @pl.when(i == 0)
    def _():
        for t in range(n // tm):
            rows = slice(t * tm, (t + 1) * tm)
            cp_scr[rows, :] = jnp.dot(mc_ref[0], h_ref[0, rows, :], preferred_element_type=F32)

    def grid_row(r):
        return slice(r * GRID_W, (r + 1) * GRID_W)

    for gi, win in enumerate(POOL_WINDOWS):
        @pl.when(jnp.logical_and(i == 0, g == gi))
        def _(win=win):
            lo, hi = win // 2, win - 1 - win // 2
            icol = icol_ref[0]
            acc = cp_scr[grid_row(0), :]
            for r in range(1, hi + 1):
                acc = acc + cp_scr[grid_row(r), :]
            for r in range(n_rows):
                if r > 0 and r + hi < n_rows:
                    acc = acc + cp_scr[grid_row(r + hi), :]
                if r - lo - 1 >= 0:
                    acc = acc - cp_scr[grid_row(r - lo - 1), :]
                cnt_r = min(r + hi, n_rows - 1) + 1 - max(r - lo, 0)
                mean = acc * (icol * (1.0 / cnt_r))
                pm_scr[grid_row(r), :] = (mean - h_ref[0, grid_row(r), :].astype(F32)).astype(BF16)

    rows = pl.ds(pl.multiple_of(i * to, to), to)
    y = jnp.dot(pm_scr[rows, :], w_ref[0], preferred_element_type=F32) * sc_ref[...]
    o_ref[0] = x_ref[0] + gate_ref[0] * y


def _pool_tables(tm, pg):
    tok = np.arange(tm)
    r_t, c_t = tok // GRID_W, tok % GRID_W
    mats, inv = [], []
    for win in POOL_WINDOWS:
        lo, hi = -(win // 2), win - 1 - win // 2
        dc = c_t[None, :] - c_t[:, None]
        mats.append(((r_t[None, :] == r_t[:, None]) & (dc >= lo) & (dc <= hi)).astype(np.float32))
        idx_c = np.arange(GRID_W)
        cnt_c = np.minimum(idx_c + hi, GRID_W - 1) + 1 - np.maximum(idx_c + lo, 0)
        inv.append(np.broadcast_to((1.0 / cnt_c)[:, None], (GRID_W, pg)))
    return (jnp.asarray(np.stack(mats), dtype=BF16),
            jnp.asarray(np.stack(inv).astype(np.float32)))


def _pool_call(h, pool_w, pool_scale, x, gate):
    bsz, n, d = x.shape
    ng = len(POOL_WINDOWS)
    pg = d // ng
    tm = _tile(n, ROW_TILE)
    to = _tile(n, WIDE_TILE)
    assert tm % GRID_W == 0 and n % GRID_W == 0
    mats, icol = _pool_tables(tm, pg)
    return pl.pallas_call(
        functools.partial(_pool_kernel, n=n, tm=tm, to=to),
        grid=(bsz, ng, n // to),
        in_specs=[
            pl.BlockSpec((1, n, pg), lambda b, g, i: (b, 0, g)),
            pl.BlockSpec((1, tm, tm), lambda b, g, i: (g, 0, 0)),
            pl.BlockSpec((1, GRID_W, pg), lambda b, g, i: (g, 0, 0)),
            pl.BlockSpec((1, pg, pg), lambda b, g, i: (g, 0, 0)),
            pl.BlockSpec((1, pg), lambda b, g, i: (0, g)),
            pl.BlockSpec((1, to, pg), lambda b, g, i: (b, i, g)),
            pl.BlockSpec((1, 1, pg), lambda b, g, i: (b, 0, g)),
        ],
        out_specs=pl.BlockSpec((1, to, pg), lambda b, g, i: (b, i, g)),
        out_shape=jax.ShapeDtypeStruct((bsz, n, d), F32),
        scratch_shapes=[pltpu.VMEM((n, pg), F32), pltpu.VMEM((n, pg), BF16)],
        compiler_params=_params(("arbitrary", "arbitrary", "arbitrary")),
        name="pool_mixer",
    )(h, mats, icol, pool_w, pool_scale.reshape(1, d), x, gate)


def _rotary_tables(n):
    pos = np.arange(n)
    row = (pos // GRID_W).astype(np.float32)
    col = (pos % GRID_W).astype(np.float32)
    n_freq = HEAD_DIM // 4
    inv = (np.float32(ROPE_BASE) ** (-np.arange(n_freq, dtype=np.float32) / np.float32(n_freq))).astype(np.float32)
    ang = np.concatenate([row[:, None] * inv, col[:, None] * inv], axis=-1).astype(np.float32)
    cos, sin = np.cos(ang.astype(np.float64)), np.sin(ang.astype(np.float64))
    cos2 = np.concatenate([cos, cos], axis=-1).astype(np.float32)
    sin2 = np.concatenate([-sin, sin], axis=-1).astype(np.float32)
    return jnp.asarray(cos2), jnp.asarray(sin2)


def kernel(x, c, ctx, c_ctx, w_mod, b_mod, ffn1_w1, ffn1_w3, ffn1_w2, ffn2_w1, ffn2_w3, ffn2_w2, ab_w_in, ab_w_out, ret_log_decay, hy_conv_w, hy_conv_b, hy_f_w1, hy_f_b1, hy_f_w2, hy_f_b2, hy_f_w3, hy_f_b3, hy_f_freq, hy_f_w4, hy_bias, pool_w, pool_scale, final_gain):
    bsz, n, d = x.shape
    n_ctx = ctx.shape[1]
    depth = w_mod.shape[0]
    assert depth == 2, "layer 0 = retention/Hyena mixer, layer 1 = pooling mixer"
    ret_w = RET_HEADS * HEAD_DIM
    hw = d - ret_w
    assert hy_f_w4.shape[2] == HY_ORDER * 2 * hw and bsz % 2 == 0

    c_all = jnp.concatenate([c, c_ctx[None, :], jnp.zeros((SUBLANE - bsz - 1, d), F32)], axis=0)
    mod = _mod_call(c_all, w_mod, b_mod).reshape(depth, SUBLANE, N_MOD, d)

    def mod_rows(layer, base, count=3):
        return mod[layer, :bsz, base:base + count]

    def mod_ctx(base, count=3):
        return mod[0, bsz:bsz + 1, base:base + count]

    n_f = ffn1_w1.shape[2] // _tile(ffn1_w1.shape[2], FFN_TF)
    w13 = _cast_pairs_call([(ffn1_w1, ffn1_w3), (ffn2_w1, ffn2_w3)], n_f)
    w_in_shape, w_out_shape = ab_w_in.shape[1:], ab_w_out.shape[1:]
    w2a, w2b, w_in, w_out = _cast_call(
        [ffn1_w2, ffn2_w2, ab_w_in[0].reshape(depth, -1, w_in_shape[1]),
         ab_w_out[0].reshape(depth, -1, w_out_shape[1])], CAST_STEPS)
    w_in, w_out = w_in.reshape(w_in_shape), w_out.reshape(w_out_shape)
    ffn1_w, ffn2_w = (w13[0], w2a), (w13[1], w2b)

    x, h = _ffn_call(x, mod_rows(0, 0), *ffn1_w, 0, modn=mod_rows(0, 3))
    _, h_c = _ffn_call(ctx.reshape(1, bsz * n_ctx, d), mod_ctx(0), *ffn1_w, 0, modn=mod_ctx(3))

    p = _matmul_call(h.reshape(bsz * n, d), w_in, BF16, b_cols=(0, 4 * ret_w)).reshape(bsz, n, -1)
    kvc = _matmul_call(h_c.reshape(bsz * n_ctx, d), w_in, BF16, b_cols=(ret_w, 3 * ret_w))
    kvc = kvc.reshape(bsz, n_ctx, 2 * ret_w)

    cos2, sin2 = _rotary_tables(n)
    r = _ret_call(p, kvc, ret_log_decay[0], cos2, sin2, ret_w)

    mcls = n // HY_CLASSES
    h_cm = h.reshape(bsz, mcls, HY_CLASSES, d).transpose(0, 2, 1, 3).reshape(bsz * n, d)
    p_hy = _matmul_call(h_cm, w_in, BF16, b_cols=(4 * ret_w, w_in.shape[1])).reshape(bsz, n, -1)
    hf, hb = _filt_call(n, hy_f_w1[0], hy_f_b1[0], hy_f_w2[0], hy_f_b2[0], hy_f_w3[0],
                        hy_f_b3[0], hy_f_freq[0], hy_f_w4[0], hw)
    tables = _fft_tables(n)
    kre, kim = _spec_call(tables, hf, hb)
    u = _sconv_call(p_hy, hy_conv_w[0], hy_conv_b[0])
    z = _fftconv_call(tables, u, 0, u, hw, kre, kim, 0, hy_bias[0, 0])
    yh = _fftconv_call(tables, z, 0, u, 2 * hw, kre, kim, hw, hy_bias[0, 1])
    yh = yh.reshape(bsz, HY_CLASSES, mcls, hw).transpose(0, 2, 1, 3).reshape(bsz, n, hw)

    x = _wout_call(r, yh, w_out, x, mod_rows(0, 5, 1))
    x = _ffn_call(x, mod_rows(0, 6), *ffn2_w, 0)

    x, h = _ffn_call(x, mod_rows(1, 0), *ffn1_w, 1, modn=mod_rows(1, 3))
    x = _pool_call(h, pool_w[0].astype(BF16), pool_scale[0], x, mod_rows(1, 5, 1))
    return _ffn_call(x, mod_rows(1, 6), *ffn2_w, 1, gain=final_gain)
```

```python
import functools
import math

import numpy as np
import jax
import jax.numpy as jnp
from jax import lax
from jax.experimental import pallas as pl
from jax.experimental.pallas import tpu as pltpu

F32 = jnp.float32
BF16 = jnp.bfloat16
HIGHEST = lax.Precision.HIGHEST

EPS = 1e-6
GRID_W = 64
N_MOD = 9
RET_HEADS = 8
HEAD_DIM = 128
HY_ORDER = 2
HY_SHORT = 3
HY_EMB_DIM = 33
HY_DECAY_SHORT_PCT = 0.3
HY_DECAY_LONG_PCT = 1.5
HY_DECAY_TARGET = 1e-2
ROPE_BASE = 10000.0
POOL_WINDOWS = (2, 4, 8, 16)
RET_CHUNK = 256
RET_UNROLL = 16
HY_CLASSES = 16
LANE = 128
SUBLANE = 8
PACKED_SUBLANE = 16
ROW_TILE = 512
WIDE_TILE = 1024
FFN_TF = 512
FILTER_ROWS = 256
CAST_STEPS = 16
VMEM_LIMIT = 56 * 1024 * 1024
FFN_VMEM_LIMIT = 60 * 1024 * 1024


def _tile(dim, pref):
    if dim <= pref:
        return dim
    for t in range(pref - pref % LANE, 0, -LANE):
        if dim % t == 0:
            return t
    raise ValueError((dim, pref))


def _params(sem, vmem_limit=VMEM_LIMIT):
    return pltpu.CompilerParams(dimension_semantics=sem, vmem_limit_bytes=vmem_limit)


def _rms(x):
    return x * lax.rsqrt(jnp.mean(x * x, axis=-1, keepdims=True) + EPS)


def _silu(x):
    return x * jax.nn.sigmoid(x)


def _mod_kernel(c_ref, w_ref, b_ref, o_ref):
    sc = _silu(c_ref[...])
    w = w_ref[0]
    sc_hi = sc.astype(BF16)
    sc_lo = (sc - sc_hi.astype(F32)).astype(BF16)
    w_hi = w.astype(BF16)
    w_lo = (w - w_hi.astype(F32)).astype(BF16)
    rows = sc.shape[0]
    top = jnp.dot(jnp.concatenate([sc_hi, sc_lo], axis=0), w_hi, preferred_element_type=F32)
    o_ref[0] = (top[0:rows] + top[rows:2 * rows]
                + jnp.dot(sc_hi, w_lo, preferred_element_type=F32) + b_ref[0])


def _mod_call(c_all, w_mod, b_mod):
    depth, d, nd = w_mod.shape
    rows = c_all.shape[0]
    tn = _tile(nd, WIDE_TILE)
    return pl.pallas_call(
        _mod_kernel,
        grid=(depth, nd // tn),
        in_specs=[
            pl.BlockSpec((rows, d), lambda l, j: (0, 0)),
            pl.BlockSpec((1, d, tn), lambda l, j: (l, 0, j)),
            pl.BlockSpec((1, 1, tn), lambda l, j: (l, 0, j)),
        ],
        out_specs=pl.BlockSpec((1, rows, tn), lambda l, j: (l, 0, j)),
        out_shape=jax.ShapeDtypeStruct((depth, rows, nd), F32),
        compiler_params=_params(("arbitrary", "arbitrary")),
        name="mod",
    )(c_all, w_mod, b_mod.reshape(depth, 1, nd))


def _ffn_kernel(*refs, n_f, emit_h, final, row_chunks):
    x_ref, mod_ref, w13_ref, w2_ref = refs[:4]
    pos = 4
    modn_ref = gain_ref = hn_ref = None
    if emit_h:
        modn_ref = refs[pos]
        pos += 1
    if final:
        gain_ref = refs[pos]
        pos += 1
    o_ref = refs[pos]
    pos += 1
    if emit_h:
        hn_ref = refs[pos]
        pos += 1
    h_scr = refs[pos]
    acc = o_ref.at[0] if row_chunks > 1 else refs[pos + 1]
    f = pl.program_id(2)
    tm = h_scr.shape[0]
    chunks = [slice(r * (tm // row_chunks), (r + 1) * (tm // row_chunks)) for r in range(row_chunks)]

    assert n_f >= 3

    tf = w2_ref.shape[0]

    def partial_out(h):
        ab = jnp.dot(h, w13_ref[...], preferred_element_type=F32)
        g = (_silu(ab[:, 0:tf]) * ab[:, tf:2 * tf]).astype(BF16)
        return jnp.dot(g, w2_ref[...], preferred_element_type=F32)

    @pl.when(f == 0)
    def _():
        m = mod_ref[0]
        for rows in chunks:
            h = (_rms(x_ref[0, rows, :]) * (1.0 + m[1:2]) + m[0:1]).astype(BF16)
            h_scr[rows, :] = h
            acc[rows, :] = partial_out(h)

    @pl.when(jnp.logical_and(f > 0, f < n_f - 1))
    def _():
        for rows in chunks:
            acc[rows, :] += partial_out(h_scr[rows, :])

    @pl.when(f == n_f - 1)
    def _():
        m = mod_ref[0]
        for rows in chunks:
            xo = x_ref[0, rows, :] + 0.5 * m[2:3] * (acc[rows, :] + partial_out(h_scr[rows, :]))
            if emit_h:
                mn = modn_ref[0]
                hn_ref[0, rows, :] = (_rms(xo) * (1.0 + mn[1:2]) + mn[0:1]).astype(BF16)
            if final:
                xo = _rms(xo) * gain_ref[...]
            o_ref[0, rows, :] = xo


def _ffn_call(x, mod3, w13, w2, layer, modn=None, gain=None):
    bsz, s, d = x.shape
    n_f, tf = w13.shape[1], w13.shape[3] // 2
    emit_h = modn is not None
    final = gain is not None
    tm = _tile(s, WIDE_TILE)
    row_chunks = tm // _tile(tm, ROW_TILE)
    in_specs = [
        pl.BlockSpec((1, tm, d), lambda b, i, f: (b, i, 0)),
        pl.BlockSpec((1, 3, d), lambda b, i, f: (b, 0, 0)),
        pl.BlockSpec((None, None, d, 2 * tf), lambda b, i, f: (layer, f, 0, 0)),
        pl.BlockSpec((None, tf, d), lambda b, i, f: (layer, f, 0)),
    ]
    args = [x, mod3, w13, w2]
    if emit_h:
        in_specs.append(pl.BlockSpec((1, 3, d), lambda b, i, f: (b, 0, 0)))
        args.append(modn)
    if final:
        in_specs.append(pl.BlockSpec((1, d), lambda b, i, f: (0, 0)))
        args.append(gain.reshape(1, d))
    out_specs = [pl.BlockSpec((1, tm, d), lambda b, i, f: (b, i, 0))]
    out_shape = [jax.ShapeDtypeStruct((bsz, s, d), F32)]
    if emit_h:
        out_specs.append(pl.BlockSpec((1, tm, d), lambda b, i, f: (b, i, 0)))
        out_shape.append(jax.ShapeDtypeStruct((bsz, s, d), BF16))
    outs = pl.pallas_call(
        functools.partial(_ffn_kernel, n_f=n_f, emit_h=emit_h, final=final, row_chunks=row_chunks),
        grid=(bsz, s // tm, n_f),
        in_specs=in_specs,
        out_specs=out_specs,
        out_shape=out_shape,
        scratch_shapes=[pltpu.VMEM((tm, d), BF16)] + ([] if row_chunks > 1 else [pltpu.VMEM((tm, d), F32)]),
        compiler_params=_params(("arbitrary", "arbitrary", "arbitrary"), FFN_VMEM_LIMIT),
        name="ffn",
    )(*args)
    return outs if emit_h else outs[0]


def _cast_kernel(*refs):
    half = len(refs) // 2
    for src, dst in zip(refs[:half], refs[half:]):
        dst[...] = src[...].astype(dst.dtype)


def _cast_call(arrays, steps):
    depth = arrays[0].shape[0]
    specs = []
    for a in arrays:
        assert a.shape[0] == depth and a.shape[1] % (PACKED_SUBLANE * steps) == 0
        specs.append(pl.BlockSpec((1, a.shape[1] // steps, a.shape[2]), lambda l, j: (l, j, 0)))
    return pl.pallas_call(
        _cast_kernel,
        grid=(depth, steps),
        in_specs=specs,
        out_specs=specs,
        out_shape=[jax.ShapeDtypeStruct(a.shape, BF16) for a in arrays],
        compiler_params=_params(("arbitrary", "arbitrary")),
        name="cast_weights",
    )(*arrays)


def _cast_pairs_kernel(*refs):
    n_out = len(refs) // 3
    for k in range(n_out):
        wa, wb, dst = refs[2 * k], refs[2 * k + 1], refs[2 * n_out + k]
        tf = wa.shape[2]
        dst[0, 0, :, 0:tf] = wa[0].astype(dst.dtype)
        dst[0, 0, :, tf:2 * tf] = wb[0].astype(dst.dtype)


def _cast_pairs_call(pairs, steps):
    depth, r, c = pairs[0][0].shape
    tf = c // steps
    in_spec = pl.BlockSpec((1, r, tf), lambda l, j: (l, 0, j))
    out_spec = pl.BlockSpec((1, 1, r, 2 * tf), lambda l, j: (l, j, 0, 0))
    return pl.pallas_call(
        _cast_pairs_kernel,
        grid=(depth, steps),
        in_specs=[in_spec] * (2 * len(pairs)),
        out_specs=[out_spec] * len(pairs),
        out_shape=[jax.ShapeDtypeStruct((depth, steps, r, 2 * tf), BF16)] * len(pairs),
        compiler_params=_params(("arbitrary", "arbitrary")),
        name="cast_weight_pairs",
    )(*[w for pair in pairs for w in pair])


def _matmul_kernel(a_ref, b_ref, o_ref):
    o_ref[...] = jnp.dot(a_ref[...], b_ref[...], preferred_element_type=F32).astype(o_ref.dtype)


def _matmul_call(a, b, out_dtype, tm_pref=WIDE_TILE, tn_pref=WIDE_TILE, b_cols=None):
    m, k = a.shape
    c0, c1 = b_cols if b_cols is not None else (0, b.shape[1])
    n = c1 - c0
    tm = _tile(m, tm_pref)
    tn = math.gcd(_tile(n, tn_pref), c0) if c0 else _tile(n, tn_pref)
    assert tn % LANE == 0
    joff = c0 // tn
    return pl.pallas_call(
        _matmul_kernel,
        grid=(m // tm, n // tn),
        in_specs=[
            pl.BlockSpec((tm, k), lambda i, j: (i, 0)),
            pl.BlockSpec((k, tn), lambda i, j: (0, joff + j)),
        ],
        out_specs=pl.BlockSpec((tm, tn), lambda i, j: (i, j)),
        out_shape=jax.ShapeDtypeStruct((m, n), out_dtype),
        compiler_params=_params(("arbitrary", "arbitrary")),
        name="matmul",
    )(a, b)


def _ret_kernel(lg_ref, q_ref, k_ref, v_ref, g_ref, cos_ref, sin_ref, kc_ref, vc_ref, o_ref,
                qr_scr, kr_scr, upd_scr, st_scr, *, n, n_ctx, chunk):
    hd = HEAD_DIM
    head = pl.program_id(1)
    lgf = lg_ref[0, head]
    lgb = lg_ref[1, head]
    nc = n // chunk
    k_scale = hd ** -0.5

    posc = lax.broadcasted_iota(jnp.int32, (n_ctx, hd), 0).astype(F32)
    kc = kc_ref[0].astype(F32) * k_scale
    vc = vc_ref[0]
    kcf = (kc * jnp.exp((n_ctx - 1.0 - posc) * lgf)).T.astype(BF16)
    kcb = (kc * jnp.exp(posc * lgb)).T.astype(BF16)
    s0_f = jnp.dot(kcf, vc, preferred_element_type=F32)
    s0_b = jnp.dot(kcb, vc, preferred_element_type=F32)

    pos = lax.broadcasted_iota(jnp.int32, (chunk, hd), 0).astype(F32)
    qdec_f = jnp.exp((pos + 1.0) * lgf)
    kdec_f = jnp.exp((chunk - 1.0 - pos) * lgf)
    qdec_b = jnp.exp((chunk - pos) * lgb)
    kdec_b = jnp.exp(pos * lgb)
    cdec_f = jnp.exp(jnp.full((1, hd), chunk, F32) * lgf)
    cdec_b = jnp.exp(jnp.full((1, hd), chunk, F32) * lgb)
    ri = lax.broadcasted_iota(jnp.int32, (chunk, chunk), 0)
    ci = lax.broadcasted_iota(jnp.int32, (chunk, chunk), 1)
    diff = (ri - ci).astype(F32)
    dmat = (jnp.where(diff >= 0, jnp.exp(jnp.maximum(diff, 0.0) * lgf), 0.0)
            + jnp.where(diff <= 0, jnp.exp(jnp.maximum(-diff, 0.0) * lgb), 0.0))

    def rotary(x, rows):
        return x * cos_ref[rows, :] + pltpu.roll(x, hd // 2, 1) * sin_ref[rows, :]

    def chunk_rows(c):
        return pl.ds(pl.multiple_of(c * chunk, chunk), chunk)

    def prep_body(c, carry):
        rows = chunk_rows(c)
        qr_scr[rows, :] = rotary(q_ref[0, rows, :].astype(F32), rows)
        k = rotary(k_ref[0, rows, :].astype(F32), rows) * k_scale
        kr_scr[rows, :] = k
        kk = jnp.concatenate([k * kdec_f, k * kdec_b], axis=1).T.astype(BF16)
        upd_scr[c] = jnp.dot(kk, v_ref[0, rows, :], preferred_element_type=F32)
        return carry

    lax.fori_loop(0, nc, prep_body, 0, unroll=math.gcd(nc, RET_UNROLL))

    def fwd_scan(c, sf):
        st_scr[c, 0:hd, :] = sf.astype(BF16)
        return cdec_f * sf + upd_scr[c, 0:hd, :]

    def bwd_scan(j, sb):
        c = nc - 1 - j
        st_scr[c, hd:2 * hd, :] = sb.astype(BF16)
        return cdec_b * sb + upd_scr[c, hd:2 * hd, :]

    lax.fori_loop(0, nc, fwd_scan, s0_f, unroll=math.gcd(nc, RET_UNROLL))
    lax.fori_loop(0, nc, bwd_scan, s0_b, unroll=math.gcd(nc, RET_UNROLL))

    def out_body(c, carry):
        rows = chunk_rows(c)
        q = qr_scr[rows, :]
        s = lax.dot_general(q.astype(BF16), kr_scr[rows, :].astype(BF16), (((1,), (1,)), ((), ())),
                            preferred_element_type=F32) * dmat
        qq = jnp.concatenate([q * qdec_f, q * qdec_b], axis=1).astype(BF16)
        o = (jnp.dot(s.astype(BF16), v_ref[0, rows, :], preferred_element_type=F32)
             + jnp.dot(qq, st_scr[c], preferred_element_type=F32))
        g = g_ref[0, rows, :].astype(F32)
        o_ref[0, rows, :] = (_rms(o) * _silu(g)).astype(o_ref.dtype)
        return carry

    lax.fori_loop(0, nc, out_body, 0, unroll=math.gcd(nc, RET_UNROLL))


def _ret_call(p, kvc, log_decay, cos2, sin2, ret_w):
    bsz, n, _ = p.shape
    n_ctx = kvc.shape[1]
    heads = ret_w // HEAD_DIM
    hd = HEAD_DIM
    chunk = _tile(n, RET_CHUNK)
    kern = functools.partial(_ret_kernel, n=n, n_ctx=n_ctx, chunk=chunk)
    col = lambda off: (lambda b, h, lg: (b, 0, off * heads + h))
    return pl.pallas_call(
        kern,
        grid_spec=pltpu.PrefetchScalarGridSpec(
            num_scalar_prefetch=1,
            grid=(bsz, heads),
            in_specs=[
                pl.BlockSpec((1, n, hd), col(0)),
                pl.BlockSpec((1, n, hd), col(1)),
                pl.BlockSpec((1, n, hd), col(2)),
                pl.BlockSpec((1, n, hd), col(3)),
                pl.BlockSpec((n, hd), lambda b, h, lg: (0, 0)),
                pl.BlockSpec((n, hd), lambda b, h, lg: (0, 0)),
                pl.BlockSpec((1, n_ctx, hd), col(0)),
                pl.BlockSpec((1, n_ctx, hd), col(1)),
            ],
            out_specs=pl.BlockSpec((1, n, hd), lambda b, h, lg: (b, 0, h)),
            scratch_shapes=[
                pltpu.VMEM((n, hd), F32),
                pltpu.VMEM((n, hd), F32),
                pltpu.VMEM((n // chunk, 2 * hd, hd), F32),
                pltpu.VMEM((n // chunk, 2 * hd, hd), BF16),
            ],
        ),
        out_shape=jax.ShapeDtypeStruct((bsz, n, ret_w), BF16),
        compiler_params=_params(("arbitrary", "arbitrary")),
        name="retention",
    )(log_decay, p, p, p, p, cos2, sin2, kvc, kvc)


def _cmul(a, b):
    return a[0] * b[0] - a[1] * b[1], a[0] * b[1] + a[1] * b[0]


def _cmul_const(a, c, s):
    eps = 1e-12
    if abs(s) < eps:
        return (a[0], a[1]) if c > 0 else (-a[0], -a[1])
    if abs(c) < eps:
        return (-a[1], a[0]) if s > 0 else (a[1], -a[0])
    return a[0] * c - a[1] * s, a[0] * s + a[1] * c


def _fft_list(xs, sign):
    size = len(xs)
    if size == 1:
        return xs
    even = _fft_list(xs[0::2], sign)
    odd = _fft_list(xs[1::2], sign)
    out = [None] * size
    for k in range(size // 2):
        ang = sign * 2.0 * math.pi * k / size
        tr, ti = _cmul_const(odd[k], math.cos(ang), math.sin(ang))
        out[k] = (even[k][0] + tr, even[k][1] + ti)
        out[k + size // 2] = (even[k][0] - tr, even[k][1] - ti)
    return out


def _fft_tables(n):
    j_cls = HY_CLASSES
    k0n, mcls = 2 * n // j_cls, n // j_cls
    t = j_cls * np.arange(mcls)[None, None, :] + np.arange(j_cls)[:, None, None]
    ang = 2.0 * math.pi * ((np.arange(k0n)[None, :, None] * t) % (2 * n)) / (2 * n)
    cs = np.concatenate([np.cos(ang), np.sin(ang)], axis=1)
    cst = np.concatenate([np.cos(ang).transpose(0, 2, 1), np.sin(ang).transpose(0, 2, 1)], axis=1)
    return jnp.asarray(cs, dtype=BF16), jnp.asarray(cst / (2 * n), dtype=BF16)


def _filt_kernel(z_ref, w1_ref, b1_ref, w2_ref, b2_ref, w3_ref, b3_ref, fr_ref, w4_ref, dl_ref,
                 hf_ref, hb_ref, *, n, tt, hw):
    i = pl.program_id(0)
    dot = functools.partial(jnp.dot, preferred_element_type=F32, precision=HIGHEST)
    fr = fr_ref[...]
    a = jnp.sin(fr[0:1] * (dot(z_ref[...], w1_ref[...]) + b1_ref[...]))
    a = jnp.sin(fr[1:2] * (dot(a, w2_ref[...]) + b2_ref[...]))
    a = jnp.sin(fr[2:3] * (dot(a, w3_ref[...]) + b3_ref[...]))
    hf = jnp.dot(a.astype(BF16), w4_ref[...], preferred_element_type=F32)
    mcls = n // HY_CLASSES
    rho = lax.broadcasted_iota(jnp.int32, (tt, hw), 0) + i * tt
    ti = HY_CLASSES * (rho & (mcls - 1)) + lax.shift_right_logical(rho, int(math.log2(mcls)))
    t = ti.astype(F32) * (1.0 / (n - 1.0))
    window = jnp.exp(-t * dl_ref[...])
    for o in range(HY_ORDER):
        hf_ref[:, o * hw:(o + 1) * hw] = (hf[:, (2 * o) * hw:(2 * o + 1) * hw] * window).astype(BF16)
        h_b = jnp.where(ti == 0, 0.0, hf[:, (2 * o + 1) * hw:(2 * o + 2) * hw] * window)
        hb_ref[:, o * hw:(o + 1) * hw] = h_b.astype(BF16)


def _filt_call(n, fw1, fb1, fw2, fb2, fw3, fb3, freq, fw4, hw):
    hid = fw1.shape[1]
    pk = LANE
    mcls = n // HY_CLASSES
    assert mcls & (mcls - 1) == 0
    bands = (HY_EMB_DIM - 1) // 2
    tpos = np.linspace(0.0, 1.0, n, dtype=np.float32)[:, None].astype(np.float64)
    fb = np.linspace(1e-4, bands - 1, bands, dtype=np.float32)[None, :].astype(np.float64)
    wv = (2.0 * math.pi * np.arange(n, dtype=np.float64)[:, None] / n)
    z = np.concatenate([tpos, np.cos(fb * wv), -np.sin(fb * wv)], axis=-1)
    rho = np.arange(n)
    t_of_row = HY_CLASSES * (rho % mcls) + rho // mcls
    z = np.pad(z, ((0, 0), (0, pk - HY_EMB_DIM))).astype(np.float32)[t_of_row]
    max_decay = math.log(HY_DECAY_TARGET) / HY_DECAY_SHORT_PCT
    min_decay = math.log(HY_DECAY_TARGET) / HY_DECAY_LONG_PCT
    deltas = np.abs(np.linspace(min_decay, max_decay, hw, dtype=np.float32)).reshape(1, hw)

    padc = lambda w: jnp.pad(w, ((0, 0), (0, pk - hid)))
    w1 = jnp.pad(fw1, ((0, pk - HY_EMB_DIM), (0, pk - hid)))
    w2 = jnp.pad(fw2, ((0, pk - hid), (0, pk - hid)))
    w3 = jnp.pad(fw3, ((0, pk - hid), (0, pk - hid)))
    w4 = jnp.pad(fw4, ((0, pk - hid), (0, 0))).astype(BF16)
    b1, b2, b3 = (padc(b.reshape(1, hid)) for b in (fb1, fb2, fb3))
    fr = jnp.pad(freq, ((0, SUBLANE - freq.shape[0]), (0, pk - hid)))
    tt = _tile(n, FILTER_ROWS)
    nw = HY_ORDER * hw
    full = lambda shape: pl.BlockSpec(shape, lambda i: (0, 0))
    return pl.pallas_call(
        functools.partial(_filt_kernel, n=n, tt=tt, hw=hw),
        grid=(n // tt,),
        in_specs=[
            pl.BlockSpec((tt, pk), lambda i: (i, 0)),
            full((pk, pk)), full((1, pk)), full((pk, pk)), full((1, pk)),
            full((pk, pk)), full((1, pk)), full((SUBLANE, pk)), full((pk, 2 * nw)), full((1, hw)),
        ],
        out_specs=[pl.BlockSpec((tt, nw), lambda i: (i, 0))] * 2,
        out_shape=[jax.ShapeDtypeStruct((n, nw), BF16)] * 2,
        compiler_params=_params(("arbitrary",)),
        name="hyena_filters",
    )(jnp.asarray(z), w1, b1, w2, b2, w3, b3, fr, w4, jnp.asarray(deltas))


def _sconv_kernel(p_ref, w_ref, b_ref, o_ref, *, n):
    w = w_ref[...]
    bias = b_ref[...]
    mcls = n // HY_CLASSES
    row = lax.broadcasted_iota(jnp.int32, (mcls, p_ref.shape[2]), 0)

    def cls(j):
        return p_ref[0, j * mcls:(j + 1) * mcls, :].astype(F32)

    for j in range(HY_CLASSES):
        cur = cls(j)
        if j > 0:
            prev = cls(j - 1)
        else:
            prev = jnp.where(row == 0, 0.0, pltpu.roll(cls(HY_CLASSES - 1), 1, 0))
        if j < HY_CLASSES - 1:
            nxt = cls(j + 1)
        else:
            nxt = jnp.where(row == mcls - 1, 0.0, pltpu.roll(cls(0), mcls - 1, 0))
        out = bias + prev * w[0:1] + cur * w[1:2] + nxt * w[2:3]
        o_ref[0, j * mcls:(j + 1) * mcls, :] = out.astype(o_ref.dtype)


def _sconv_call(p, conv_w, conv_b):
    bsz, n, width = p.shape
    tc = _tile(width, ROW_TILE)
    return pl.pallas_call(
        functools.partial(_sconv_kernel, n=n),
        grid=(bsz, width // tc),
        in_specs=[
            pl.BlockSpec((1, n, tc), lambda b, j: (b, 0, j)),
            pl.BlockSpec((HY_SHORT, tc), lambda b, j: (0, j)),
            pl.BlockSpec((1, tc), lambda b, j: (0, j)),
        ],
        out_specs=pl.BlockSpec((1, n, tc), lambda b, j: (b, 0, j)),
        out_shape=jax.ShapeDtypeStruct((bsz, n, width), BF16),
        compiler_params=_params(("arbitrary", "arbitrary")),
        name="short_conv",
    )(p, conv_w, conv_b.reshape(1, width))


def _spec_kernel(cs_ref, hf_ref, hb_ref, kre_ref, kim_ref, wf_scr, wb_scr,
                 *, k0n, mcls):
    j_cls = HY_CLASSES
    tn = hf_ref.shape[1]
    for j in range(j_cls):
        rows = slice(j * mcls, (j + 1) * mcls)
        prod = jnp.dot(cs_ref[j], jnp.concatenate([hf_ref[rows, :], hb_ref[rows, :]], axis=1),
                       preferred_element_type=F32)
        wf_scr[j, 0:k0n, :] = prod[0:k0n, 0:tn]
        wf_scr[j, k0n:2 * k0n, :] = -prod[k0n:2 * k0n, 0:tn]
        wb_scr[j, 0:k0n, :] = prod[0:k0n, tn:2 * tn]
        wb_scr[j, k0n:2 * k0n, :] = -prod[k0n:2 * k0n, tn:2 * tn]

    def body(c, carry):
        halves = []
        for half in range(PACKED_SUBLANE // SUBLANE):
            r8 = pl.ds(pl.multiple_of(c * PACKED_SUBLANE + half * SUBLANE, SUBLANE), SUBLANE)
            i8 = pl.ds(pl.multiple_of(k0n + c * PACKED_SUBLANE + half * SUBLANE, SUBLANE), SUBLANE)
            xs = [(wf_scr[j, r8, :], wf_scr[j, i8, :]) for j in range(j_cls)]
            ys = [(wb_scr[j, r8, :], wb_scr[j, i8, :]) for j in range(j_cls)]
            halves.append((_fft_list(xs, -1), _fft_list(ys, -1)))
        for q in range(j_cls):
            rows = pl.ds(pl.multiple_of(q * k0n + c * PACKED_SUBLANE, PACKED_SUBLANE), PACKED_SUBLANE)
            kre = jnp.concatenate([h[0][q][0] + h[1][q][0] for h in halves], axis=0)
            kim = jnp.concatenate([h[0][q][1] - h[1][q][1] for h in halves], axis=0)
            kre_ref[rows, :] = kre.astype(kre_ref.dtype)
            kim_ref[rows, :] = kim.astype(kim_ref.dtype)
        return carry

    lax.fori_loop(0, k0n // PACKED_SUBLANE, body, 0)


def _spec_call(tables, hf, hb):
    cs, _ = tables
    n, width = hf.shape
    tn = LANE
    k0n, mcls = cs.shape[1] // 2, cs.shape[2]
    const = lambda shape: pl.BlockSpec(shape, lambda j: (0,) * len(shape),
                                       pipeline_mode=pl.Buffered(1))
    return pl.pallas_call(
        functools.partial(_spec_kernel, k0n=k0n, mcls=mcls),
        grid=(width // tn,),
        in_specs=[
            const(cs.shape),
            pl.BlockSpec((n, tn), lambda j: (0, j)),
            pl.BlockSpec((n, tn), lambda j: (0, j)),
        ],
        out_specs=[pl.BlockSpec((2 * n, tn), lambda j: (0, j))] * 2,
        out_shape=[jax.ShapeDtypeStruct((2 * n, width), BF16)] * 2,
        scratch_shapes=[pltpu.VMEM((HY_CLASSES, 2 * k0n, tn), F32)] * 2,
        compiler_params=_params(("arbitrary",)),
        name="filter_spectrum",
    )(cs, hf, hb)


def _fftconv_kernel(cs_ref, cst_ref, z_ref, g_ref, kre_ref, kim_ref, bias_ref,
                    o_ref, w_scr, *, k0n, mcls):
    j_cls = HY_CLASSES
    tn = o_ref.shape[3]
    for j in range(j_cls):
        rows = slice(j * mcls, (j + 1) * mcls)
        prod = jnp.dot(cs_ref[j], jnp.concatenate([z_ref[0, 0, rows, :], z_ref[0, 1, rows, :]], axis=1),
                       preferred_element_type=F32)
        w_scr[j, 0:k0n, :] = prod[0:k0n, 0:tn] + prod[k0n:2 * k0n, tn:2 * tn]
        w_scr[j, k0n:2 * k0n, :] = prod[0:k0n, tn:2 * tn] - prod[k0n:2 * k0n, 0:tn]

    def body(c, carry):
        for half in range(PACKED_SUBLANE // SUBLANE):
            r8 = pl.ds(pl.multiple_of(c * PACKED_SUBLANE + half * SUBLANE, SUBLANE), SUBLANE)
            i8 = pl.ds(pl.multiple_of(k0n + c * PACKED_SUBLANE + half * SUBLANE, SUBLANE), SUBLANE)
            xs = [(w_scr[j, r8, :], w_scr[j, i8, :]) for j in range(j_cls)]
            spec = _fft_list(xs, -1)
            prod = []
            for q in range(j_cls):
                rows = pl.ds(pl.multiple_of(q * k0n + c * PACKED_SUBLANE, PACKED_SUBLANE), PACKED_SUBLANE)
                part = slice(half * SUBLANE, (half + 1) * SUBLANE)
                kq = (kre_ref[rows, :].astype(F32)[part], kim_ref[rows, :].astype(F32)[part])
                prod.append(_cmul(spec[q], kq))
            back = _fft_list(prod, 1)
            for j in range(j_cls):
                w_scr[j, r8, :] = back[j][0]
                w_scr[j, i8, :] = back[j][1]
        return carry

    lax.fori_loop(0, k0n // PACKED_SUBLANE, body, 0)

    bias = bias_ref[...]
    for j in range(j_cls):
        rows = slice(j * mcls, (j + 1) * mcls)
        v2 = jnp.concatenate([w_scr[j, 0:k0n, :], w_scr[j, k0n:2 * k0n, :]], axis=1).astype(BF16)
        prod = jnp.dot(cst_ref[j], v2, preferred_element_type=F32)
        ys = (prod[0:mcls, 0:tn] - prod[mcls:2 * mcls, tn:2 * tn],
              prod[mcls:2 * mcls, 0:tn] + prod[0:mcls, tn:2 * tn])
        for b2 in range(2):
            conv = ys[b2] + z_ref[0, b2, rows, :].astype(F32) * bias
            o_ref[0, b2, rows, :] = (g_ref[0, b2, rows, :].astype(F32) * conv).astype(o_ref.dtype)


def _fftconv_call(tables, z, z_col0, gate, gate_col0, kre, kim, k_col0, bias):
    cs, cst = tables
    bsz, n, _ = z.shape
    hw = bias.shape[0]
    tn = LANE
    k0n, mcls = cs.shape[1] // 2, cs.shape[2]
    zo, go, ko = z_col0 // tn, gate_col0 // tn, k_col0 // tn
    z4 = z.reshape(bsz // 2, 2, n, z.shape[2])
    g4 = gate.reshape(bsz // 2, 2, n, gate.shape[2])
    const = lambda shape: pl.BlockSpec(shape, lambda j, p: (0,) * len(shape),
                                       pipeline_mode=pl.Buffered(1))
    out = pl.pallas_call(
        functools.partial(_fftconv_kernel, k0n=k0n, mcls=mcls),
        grid=(hw // tn, bsz // 2),
        in_specs=[
            const(cs.shape), const(cst.shape),
            pl.BlockSpec((1, 2, n, tn), lambda j, p: (p, 0, 0, zo + j)),
            pl.BlockSpec((1, 2, n, tn), lambda j, p: (p, 0, 0, go + j)),
            pl.BlockSpec((2 * n, tn), lambda j, p: (0, ko + j)),
            pl.BlockSpec((2 * n, tn), lambda j, p: (0, ko + j)),
            pl.BlockSpec((1, tn), lambda j, p: (0, j)),
        ],
        out_specs=pl.BlockSpec((1, 2, n, tn), lambda j, p: (p, 0, 0, j)),
        out_shape=jax.ShapeDtypeStruct((bsz // 2, 2, n, hw), BF16),
        scratch_shapes=[pltpu.VMEM((HY_CLASSES, 2 * k0n, tn), F32)],
        compiler_params=_params(("arbitrary", "arbitrary")),
        name="fft_conv",
    )(cs, cst, z4, g4, kre, kim, bias.reshape(1, hw))
    return out.reshape(bsz, n, hw)


def _wout_kernel(r_ref, y_ref, wa_ref, wb_ref, x_ref, gate_ref, o_ref):
    y = (jnp.dot(r_ref[0], wa_ref[...], preferred_element_type=F32)
         + jnp.dot(y_ref[0], wb_ref[...], preferred_element_type=F32))
    o_ref[0] = x_ref[0] + gate_ref[0] * y


def _wout_call(r, yh, w_out, x, gate):
    bsz, n, d = x.shape
    rw = r.shape[2]
    assert d == 2 * rw and yh.shape[2] == rw
    tm = _tile(n, ROW_TILE)
    return pl.pallas_call(
        _wout_kernel,
        grid=(bsz, n // tm),
        in_specs=[
            pl.BlockSpec((1, tm, rw), lambda b, i: (b, i, 0)),
            pl.BlockSpec((1, tm, d - rw), lambda b, i: (b, i, 0)),
            pl.BlockSpec((rw, d), lambda b, i: (0, 0)),
            pl.BlockSpec((rw, d), lambda b, i: (1, 0)),
            pl.BlockSpec((1, tm, d), lambda b, i: (b, i, 0)),
            pl.BlockSpec((1, 1, d), lambda b, i: (b, 0, 0)),
        ],
        out_specs=pl.BlockSpec((1, tm, d), lambda b, i: (b, i, 0)),
        out_shape=jax.ShapeDtypeStruct((bsz, n, d), F32),
        compiler_params=_params(("arbitrary", "arbitrary")),
        name="mixer_out",
    )(r, yh, w_out, w_out, x, gate)


def _pool_kernel(h_ref, mc_ref, icol_ref, w_ref, sc_ref, x_ref, gate_ref, o_ref, cp_scr, pm_scr,
                 *, n, tm, to):
    g = pl.program_id(1)
    i = pl.program_id(2)
    n_rows = n // GRID_W

    @pl.when(i == 0)
    def _():
        for t in range(n // tm):
            rows = slice(t * tm, (t + 1) * tm)
            cp_scr[rows, :] = jnp.dot(mc_ref[0], h_ref[0, rows, :], preferred_element_type=F32)

    def grid_row(r):
        return slice(r * GRID_W, (r + 1) * GRID_W)

    for gi, win in enumerate(POOL_WINDOWS):
        @pl.when(jnp.logical_and(i == 0, g == gi))
        def _(win=win):
            lo, hi = win // 2, win - 1 - win // 2
            icol = icol_ref[0]
            acc = cp_scr[grid_row(0), :]
            for r in range(1, hi + 1):
                acc = acc + cp_scr[grid_row(r), :]
            for r in range(n_rows):
                if r > 0 and r + hi < n_rows:
                    acc = acc + cp_scr[grid_row(r + hi), :]
                if r - lo - 1 >= 0:
                    acc = acc - cp_scr[grid_row(r - lo - 1), :]
                cnt_r = min(r + hi, n_rows - 1) + 1 - max(r - lo, 0)
                mean = acc * (icol * (1.0 / cnt_r))
                pm_scr[grid_row(r), :] = (mean - h_ref[0, grid_row(r), :].astype(F32)).astype(BF16)

    rows = pl.ds(pl.multiple_of(i * to, to), to)
    y = jnp.dot(pm_scr[rows, :], w_ref[0], preferred_element_type=F32) * sc_ref[...]
    o_ref[0] = x_ref[0] + gate_ref[0] * y


def _pool_tables(tm, pg):
    tok = np.arange(tm)
    r_t, c_t = tok // GRID_W, tok % GRID_W
    mats, inv = [], []
    for win in POOL_WINDOWS:
        lo, hi = -(win // 2), win - 1 - win // 2
        dc = c_t[None, :] - c_t[:, None]
        mats.append(((r_t[None, :] == r_t[:, None]) & (dc >= lo) & (dc <= hi)).astype(np.float32))
        idx_c = np.arange(GRID_W)
        cnt_c = np.minimum(idx_c + hi, GRID_W - 1) + 1 - np.maximum(idx_c + lo, 0)
        inv.append(np.broadcast_to((1.0 / cnt_c)[:, None], (GRID_W, pg)))
    return (jnp.asarray(np.stack(mats), dtype=BF16),
            jnp.asarray(np.stack(inv).astype(np.float32)))


def _pool_call(h, pool_w, pool_scale, x, gate):
    bsz, n, d = x.shape
    ng = len(POOL_WINDOWS)
    pg = d // ng
    tm = _tile(n, ROW_TILE)
    to = _tile(n, WIDE_TILE)
    assert tm % GRID_W == 0 and n % GRID_W == 0
    mats, icol = _pool_tables(tm, pg)
    return pl.pallas_call(
        functools.partial(_pool_kernel, n=n, tm=tm, to=to),
        grid=(bsz, ng, n // to),
        in_specs=[
            pl.BlockSpec((1, n, pg), lambda b, g, i: (b, 0, g)),
            pl.BlockSpec((1, tm, tm), lambda b, g, i: (g, 0, 0)),
            pl.BlockSpec((1, GRID_W, pg), lambda b, g, i: (g, 0, 0)),
            pl.BlockSpec((1, pg, pg), lambda b, g, i: (g, 0, 0)),
            pl.BlockSpec((1, pg), lambda b, g, i: (0, g)),
            pl.BlockSpec((1, to, pg), lambda b, g, i: (b, i, g)),
            pl.BlockSpec((1, 1, pg), lambda b, g, i: (b, 0, g)),
        ],
        out_specs=pl.BlockSpec((1, to, pg), lambda b, g, i: (b, i, g)),
        out_shape=jax.ShapeDtypeStruct((bsz, n, d), F32),
        scratch_shapes=[pltpu.VMEM((n, pg), F32), pltpu.VMEM((n, pg), BF16)],
        compiler_params=_params(("arbitrary", "arbitrary", "arbitrary")),
        name="pool_mixer",
    )(h, mats, icol, pool_w, pool_scale.reshape(1, d), x, gate)


def _rotary_tables(n):
    pos = np.arange(n)
    row = (pos // GRID_W).astype(np.float32)
    col = (pos % GRID_W).astype(np.float32)
    n_freq = HEAD_DIM // 4
    inv = (np.float32(ROPE_BASE) ** (-np.arange(n_freq, dtype=np.float32) / np.float32(n_freq))).astype(np.float32)
    ang = np.concatenate([row[:, None] * inv, col[:, None] * inv], axis=-1).astype(np.float32)
    cos, sin = np.cos(ang.astype(np.float64)), np.sin(ang.astype(np.float64))
    cos2 = np.concatenate([cos, cos], axis=-1).astype(np.float32)
    sin2 = np.concatenate([-sin, sin], axis=-1).astype(np.float32)
    return jnp.asarray(cos2), jnp.asarray(sin2)


def kernel(x, c, ctx, c_ctx, w_mod, b_mod, ffn1_w1, ffn1_w3, ffn1_w2, ffn2_w1, ffn2_w3, ffn2_w2, ab_w_in, ab_w_out, ret_log_decay, hy_conv_w, hy_conv_b, hy_f_w1, hy_f_b1, hy_f_w2, hy_f_b2, hy_f_w3, hy_f_b3, hy_f_freq, hy_f_w4, hy_bias, pool_w, pool_scale, final_gain):
    bsz, n, d = x.shape
    n_ctx = ctx.shape[1]
    depth = w_mod.shape[0]
    assert depth == 2, "layer 0 = retention/Hyena mixer, layer 1 = pooling mixer"
    ret_w = RET_HEADS * HEAD_DIM
    hw = d - ret_w
    assert hy_f_w4.shape[2] == HY_ORDER * 2 * hw and bsz % 2 == 0

    c_all = jnp.concatenate([c, c_ctx[None, :], jnp.zeros((SUBLANE - bsz - 1, d), F32)], axis=0)
    mod = _mod_call(c_all, w_mod, b_mod).reshape(depth, SUBLANE, N_MOD, d)

    def mod_rows(layer, base, count=3):
        return mod[layer, :bsz, base:base + count]

    def mod_ctx(base, count=3):
        return mod[0, bsz:bsz + 1, base:base + count]

    n_f = ffn1_w1.shape[2] // _tile(ffn1_w1.shape[2], FFN_TF)
    w13 = _cast_pairs_call([(ffn1_w1, ffn1_w3), (ffn2_w1, ffn2_w3)], n_f)
    w_in_shape, w_out_shape = ab_w_in.shape[1:], ab_w_out.shape[1:]
    w2a, w2b, w_in, w_out = _cast_call(
        [ffn1_w2, ffn2_w2, ab_w_in[0].reshape(depth, -1, w_in_shape[1]),
         ab_w_out[0].reshape(depth, -1, w_out_shape[1])], CAST_STEPS)
    w_in, w_out = w_in.reshape(w_in_shape), w_out.reshape(w_out_shape)
    ffn1_w, ffn2_w = (w13[0], w2a), (w13[1], w2b)

    x, h = _ffn_call(x, mod_rows(0, 0), *ffn1_w, 0, modn=mod_rows(0, 3))
    _, h_c = _ffn_call(ctx.reshape(1, bsz * n_ctx, d), mod_ctx(0), *ffn1_w, 0, modn=mod_ctx(3))

    p = _matmul_call(h.reshape(bsz * n, d), w_in, BF16, tm_pref=2 * WIDE_TILE,
                     b_cols=(0, 4 * ret_w)).reshape(bsz, n, -1)
    kvc = _matmul_call(h_c.reshape(bsz * n_ctx, d), w_in, BF16, b_cols=(ret_w, 3 * ret_w))
    kvc = kvc.reshape(bsz, n_ctx, 2 * ret_w)

    cos2, sin2 = _rotary_tables(n)
    r = _ret_call(p, kvc, ret_log_decay[0], cos2, sin2, ret_w)

    mcls = n // HY_CLASSES
    h_cm = h.reshape(bsz, mcls, HY_CLASSES, d).transpose(0, 2, 1, 3).reshape(bsz * n, d)
    p_hy = _matmul_call(h_cm, w_in, BF16, tm_pref=2 * WIDE_TILE,
                        b_cols=(4 * ret_w, w_in.shape[1])).reshape(bsz, n, -1)
    hf, hb = _filt_call(n, hy_f_w1[0], hy_f_b1[0], hy_f_w2[0], hy_f_b2[0], hy_f_w3[0],
                        hy_f_b3[0], hy_f_freq[0], hy_f_w4[0], hw)
    tables = _fft_tables(n)
    kre, kim = _spec_call(tables, hf, hb)
    u = _sconv_call(p_hy, hy_conv_w[0], hy_conv_b[0])
    z = _fftconv_call(tables, u, 0, u, hw, kre, kim, 0, hy_bias[0, 0])
    yh = _fftconv_call(tables, z, 0, u, 2 * hw, kre, kim, hw, hy_bias[0, 1])
    yh = yh.reshape(bsz, HY_CLASSES, mcls, hw).transpose(0, 2, 1, 3).reshape(bsz, n, hw)

    x = _wout_call(r, yh, w_out, x, mod_rows(0, 5, 1))
    x = _ffn_call(x, mod_rows(0, 6), *ffn2_w, 0)

    x, h = _ffn_call(x, mod_rows(1, 0), *ffn1_w, 1, modn=mod_rows(1, 3))
    x = _pool_call(h, pool_w[0].astype(BF16), pool_scale[0], x, mod_rows(1, 5, 1))
    return _ffn_call(x, mod_rows(1, 6), *ffn2_w, 1, gain=final_gain)
```

```python
import functools
import math

import numpy as np
import jax
import jax.numpy as jnp
from jax import lax
from jax.experimental import pallas as pl
from jax.experimental.pallas import tpu as pltpu

F32 = jnp.float32
BF16 = jnp.bfloat16
HIGHEST = lax.Precision.HIGHEST

EPS = 1e-6
GRID_W = 64
N_MOD = 9
RET_HEADS = 8
HEAD_DIM = 128
HY_ORDER = 2
HY_SHORT = 3
HY_EMB_DIM = 33
HY_DECAY_SHORT_PCT = 0.3
HY_DECAY_LONG_PCT = 1.5
HY_DECAY_TARGET = 1e-2
ROPE_BASE = 10000.0
POOL_WINDOWS = (2, 4, 8, 16)
RET_CHUNK = 256
RET_UNROLL = 16
HY_CLASSES = 16
LANE = 128
SUBLANE = 8
PACKED_SUBLANE = 16
ROW_TILE = 512
WIDE_TILE = 1024
FFN_TF = 512
FILTER_ROWS = 256
CAST_STEPS = 16
VMEM_LIMIT = 56 * 1024 * 1024
FFN_VMEM_LIMIT = 60 * 1024 * 1024


def _tile(dim, pref):
    if dim <= pref:
        return dim
    for t in range(pref - pref % LANE, 0, -LANE):
        if dim % t == 0:
            return t
    raise ValueError((dim, pref))


def _params(sem, vmem_limit=VMEM_LIMIT):
    return pltpu.CompilerParams(dimension_semantics=sem, vmem_limit_bytes=vmem_limit)


def _rms(x):
    return x * lax.rsqrt(jnp.mean(x * x, axis=-1, keepdims=True) + EPS)


def _silu(x):
    return x * jax.nn.sigmoid(x)


def _mod_kernel(c_ref, w_ref, b_ref, o_ref):
    sc = _silu(c_ref[...])
    w = w_ref[0]
    sc_hi = sc.astype(BF16)
    sc_lo = (sc - sc_hi.astype(F32)).astype(BF16)
    w_hi = w.astype(BF16)
    w_lo = (w - w_hi.astype(F32)).astype(BF16)
    rows = sc.shape[0]
    top = jnp.dot(jnp.concatenate([sc_hi, sc_lo], axis=0), w_hi, preferred_element_type=F32)
    o_ref[0] = (top[0:rows] + top[rows:2 * rows]
                + jnp.dot(sc_hi, w_lo, preferred_element_type=F32) + b_ref[0])


def _mod_call(c_all, w_mod, b_mod):
    depth, d, nd = w_mod.shape
    rows = c_all.shape[0]
    tn = _tile(nd, WIDE_TILE)
    return pl.pallas_call(
        _mod_kernel,
        grid=(depth, nd // tn),
        in_specs=[
            pl.BlockSpec((rows, d), lambda l, j: (0, 0)),
            pl.BlockSpec((1, d, tn), lambda l, j: (l, 0, j)),
            pl.BlockSpec((1, 1, tn), lambda l, j: (l, 0, j)),
        ],
        out_specs=pl.BlockSpec((1, rows, tn), lambda l, j: (l, 0, j)),
        out_shape=jax.ShapeDtypeStruct((depth, rows, nd), F32),
        compiler_params=_params(("arbitrary", "arbitrary")),
        name="mod",
    )(c_all, w_mod, b_mod.reshape(depth, 1, nd))


def _ffn_kernel(*refs, n_f, emit_h, final, row_chunks):
    x_ref, mod_ref, w13_ref, w2_ref = refs[:4]
    pos = 4
    modn_ref = gain_ref = hn_ref = None
    if emit_h:
        modn_ref = refs[pos]
        pos += 1
    if final:
        gain_ref = refs[pos]
        pos += 1
    o_ref = refs[pos]
    pos += 1
    if emit_h:
        hn_ref = refs[pos]
        pos += 1
    h_scr = refs[pos]
    acc = o_ref.at[0] if row_chunks > 1 else refs[pos + 1]
    f = pl.program_id(2)
    tm = h_scr.shape[0]
    chunks = [slice(r * (tm // row_chunks), (r + 1) * (tm // row_chunks)) for r in range(row_chunks)]

    assert n_f >= 3

    tf = w2_ref.shape[0]

    def partial_out(h):
        ab = jnp.dot(h, w13_ref[...], preferred_element_type=F32)
        g = (_silu(ab[:, 0:tf]) * ab[:, tf:2 * tf]).astype(BF16)
        return jnp.dot(g, w2_ref[...], preferred_element_type=F32)

    @pl.when(f == 0)
    def _():
        m = mod_ref[0]
        for rows in chunks:
            h = (_rms(x_ref[0, rows, :]) * (1.0 + m[1:2]) + m[0:1]).astype(BF16)
            h_scr[rows, :] = h
            acc[rows, :] = partial_out(h)

    @pl.when(jnp.logical_and(f > 0, f < n_f - 1))
    def _():
        for rows in chunks:
            acc[rows, :] += partial_out(h_scr[rows, :])

    @pl.when(f == n_f - 1)
    def _():
        m = mod_ref[0]
        for rows in chunks:
            xo = x_ref[0, rows, :] + 0.5 * m[2:3] * (acc[rows, :] + partial_out(h_scr[rows, :]))
            if emit_h:
                mn = modn_ref[0]
                hn_ref[0, rows, :] = (_rms(xo) * (1.0 + mn[1:2]) + mn[0:1]).astype(BF16)
            if final:
                xo = _rms(xo) * gain_ref[...]
            o_ref[0, rows, :] = xo


def _ffn_call(x, mod3, w13, w2, layer, modn=None, gain=None):
    bsz, s, d = x.shape
    n_f, tf = w13.shape[1], w13.shape[3] // 2
    emit_h = modn is not None
    final = gain is not None
    tm = _tile(s, WIDE_TILE)
    row_chunks = tm // _tile(tm, ROW_TILE)
    in_specs = [
        pl.BlockSpec((1, tm, d), lambda b, i, f: (b, i, 0)),
        pl.BlockSpec((1, 3, d), lambda b, i, f: (b, 0, 0)),
        pl.BlockSpec((None, None, d, 2 * tf), lambda b, i, f: (layer, f, 0, 0)),
        pl.BlockSpec((None, tf, d), lambda b, i, f: (layer, f, 0)),
    ]
    args = [x, mod3, w13, w2]
    if emit_h:
        in_specs.append(pl.BlockSpec((1, 3, d), lambda b, i, f: (b, 0, 0)))
        args.append(modn)
    if final:
        in_specs.append(pl.BlockSpec((1, d), lambda b, i, f: (0, 0)))
        args.append(gain.reshape(1, d))
    out_specs = [pl.BlockSpec((1, tm, d), lambda b, i, f: (b, i, 0))]
    out_shape = [jax.ShapeDtypeStruct((bsz, s, d), F32)]
    if emit_h:
        out_specs.append(pl.BlockSpec((1, tm, d), lambda b, i, f: (b, i, 0)))
        out_shape.append(jax.ShapeDtypeStruct((bsz, s, d), BF16))
    outs = pl.pallas_call(
        functools.partial(_ffn_kernel, n_f=n_f, emit_h=emit_h, final=final, row_chunks=row_chunks),
        grid=(bsz, s // tm, n_f),
        in_specs=in_specs,
        out_specs=out_specs,
        out_shape=out_shape,
        scratch_shapes=[pltpu.VMEM((tm, d), BF16)] + ([] if row_chunks > 1 else [pltpu.VMEM((tm, d), F32)]),
        compiler_params=_params(("arbitrary", "arbitrary", "arbitrary"), FFN_VMEM_LIMIT),
        name="ffn",
    )(*args)
    return outs if emit_h else outs[0]


def _cast_kernel(*refs):
    half = len(refs) // 2
    for src, dst in zip(refs[:half], refs[half:]):
        dst[...] = src[...].astype(dst.dtype)


def _cast_call(arrays, steps):
    depth = arrays[0].shape[0]
    specs = []
    for a in arrays:
        assert a.shape[0] == depth and a.shape[1] % (PACKED_SUBLANE * steps) == 0
        specs.append(pl.BlockSpec((1, a.shape[1] // steps, a.shape[2]), lambda l, j: (l, j, 0)))
    return pl.pallas_call(
        _cast_kernel,
        grid=(depth, steps),
        in_specs=specs,
        out_specs=specs,
        out_shape=[jax.ShapeDtypeStruct(a.shape, BF16) for a in arrays],
        compiler_params=_params(("arbitrary", "arbitrary")),
        name="cast_weights",
    )(*arrays)


def _cast_pairs_kernel(*refs):
    n_out = len(refs) // 3
    for k in range(n_out):
        wa, wb, dst = refs[2 * k], refs[2 * k + 1], refs[2 * n_out + k]
        tf = wa.shape[2]
        dst[0, 0, :, 0:tf] = wa[0].astype(dst.dtype)
        dst[0, 0, :, tf:2 * tf] = wb[0].astype(dst.dtype)


def _cast_pairs_call(pairs, steps):
    depth, r, c = pairs[0][0].shape
    tf = c // steps
    in_spec = pl.BlockSpec((1, r, tf), lambda l, j: (l, 0, j))
    out_spec = pl.BlockSpec((1, 1, r, 2 * tf), lambda l, j: (l, j, 0, 0))
    return pl.pallas_call(
        _cast_pairs_kernel,
        grid=(depth, steps),
        in_specs=[in_spec] * (2 * len(pairs)),
        out_specs=[out_spec] * len(pairs),
        out_shape=[jax.ShapeDtypeStruct((depth, steps, r, 2 * tf), BF16)] * len(pairs),
        compiler_params=_params(("arbitrary", "arbitrary")),
        name="cast_weight_pairs",
    )(*[w for pair in pairs for w in pair])


def _matmul_kernel(a_ref, b_ref, o_ref):
    o_ref[...] = jnp.dot(a_ref[...], b_ref[...], preferred_element_type=F32).astype(o_ref.dtype)


def _matmul_call(a, b, out_dtype, tm_pref=WIDE_TILE, tn_pref=WIDE_TILE, b_cols=None):
    m, k = a.shape
    c0, c1 = b_cols if b_cols is not None else (0, b.shape[1])
    n = c1 - c0
    tm = _tile(m, tm_pref)
    tn = math.gcd(_tile(n, tn_pref), c0) if c0 else _tile(n, tn_pref)
    assert tn % LANE == 0
    joff = c0 // tn
    return pl.pallas_call(
        _matmul_kernel,
        grid=(m // tm, n // tn),
        in_specs=[
            pl.BlockSpec((tm, k), lambda i, j: (i, 0)),
            pl.BlockSpec((k, tn), lambda i, j: (0, joff + j)),
        ],
        out_specs=pl.BlockSpec((tm, tn), lambda i, j: (i, j)),
        out_shape=jax.ShapeDtypeStruct((m, n), out_dtype),
        compiler_params=_params(("arbitrary", "arbitrary")),
        name="matmul",
    )(a, b)


def _ret_kernel(lg_ref, q_ref, k_ref, v_ref, g_ref, cos_ref, sin_ref, kc_ref, vc_ref, o_ref,
                qr_scr, kr_scr, upd_scr, st_scr, *, n, n_ctx, chunk):
    hd = HEAD_DIM
    head = pl.program_id(1)
    lgf = lg_ref[0, head]
    lgb = lg_ref[1, head]
    nc = n // chunk
    k_scale = hd ** -0.5

    posc = lax.broadcasted_iota(jnp.int32, (n_ctx, hd), 0).astype(F32)
    kc = kc_ref[0].astype(F32) * k_scale
    vc = vc_ref[0]
    kcf = (kc * jnp.exp((n_ctx - 1.0 - posc) * lgf)).T.astype(BF16)
    kcb = (kc * jnp.exp(posc * lgb)).T.astype(BF16)
    s0_f = jnp.dot(kcf, vc, preferred_element_type=F32)
    s0_b = jnp.dot(kcb, vc, preferred_element_type=F32)

    pos = lax.broadcasted_iota(jnp.int32, (chunk, hd), 0).astype(F32)
    qdec_f = jnp.exp((pos + 1.0) * lgf)
    kdec_f = jnp.exp((chunk - 1.0 - pos) * lgf)
    qdec_b = jnp.exp((chunk - pos) * lgb)
    kdec_b = jnp.exp(pos * lgb)
    cdec_f = jnp.exp(jnp.full((1, hd), chunk, F32) * lgf)
    cdec_b = jnp.exp(jnp.full((1, hd), chunk, F32) * lgb)
    ri = lax.broadcasted_iota(jnp.int32, (chunk, chunk), 0)
    ci = lax.broadcasted_iota(jnp.int32, (chunk, chunk), 1)
    diff = (ri - ci).astype(F32)
    dmat = (jnp.where(diff >= 0, jnp.exp(jnp.maximum(diff, 0.0) * lgf), 0.0)
            + jnp.where(diff <= 0, jnp.exp(jnp.maximum(-diff, 0.0) * lgb), 0.0))

    def rotary(x, rows):
        return x * cos_ref[rows, :] + pltpu.roll(x, hd // 2, 1) * sin_ref[rows, :]

    def chunk_rows(c):
        return pl.ds(pl.multiple_of(c * chunk, chunk), chunk)

    def prep_body(c, carry):
        rows = chunk_rows(c)
        qr_scr[rows, :] = rotary(q_ref[0, rows, :].astype(F32), rows)
        k = rotary(k_ref[0, rows, :].astype(F32), rows) * k_scale
        kr_scr[rows, :] = k
        kk = jnp.concatenate([k * kdec_f, k * kdec_b], axis=1).T.astype(BF16)
        upd_scr[c] = jnp.dot(kk, v_ref[0, rows, :], preferred_element_type=F32)
        return carry

    lax.fori_loop(0, nc, prep_body, 0, unroll=math.gcd(nc, RET_UNROLL))

    def fwd_scan(c, sf):
        st_scr[c, 0:hd, :] = sf.astype(BF16)
        return cdec_f * sf + upd_scr[c, 0:hd, :]

    def bwd_scan(j, sb):
        c = nc - 1 - j
        st_scr[c, hd:2 * hd, :] = sb.astype(BF16)
        return cdec_b * sb + upd_scr[c, hd:2 * hd, :]

    lax.fori_loop(0, nc, fwd_scan, s0_f, unroll=math.gcd(nc, RET_UNROLL))
    lax.fori_loop(0, nc, bwd_scan, s0_b, unroll=math.gcd(nc, RET_UNROLL))

    def out_body(c, carry):
        rows = chunk_rows(c)
        q = qr_scr[rows, :]
        s = lax.dot_general(q.astype(BF16), kr_scr[rows, :].astype(BF16), (((1,), (1,)), ((), ())),
                            preferred_element_type=F32) * dmat
        qq = jnp.concatenate([q * qdec_f, q * qdec_b], axis=1).astype(BF16)
        o = (jnp.dot(s.astype(BF16), v_ref[0, rows, :], preferred_element_type=F32)
             + jnp.dot(qq, st_scr[c], preferred_element_type=F32))
        g = g_ref[0, rows, :].astype(F32)
        o_ref[0, rows, :] = (_rms(o) * _silu(g)).astype(o_ref.dtype)
        return carry

    lax.fori_loop(0, nc, out_body, 0, unroll=math.gcd(nc, RET_UNROLL))


def _ret_call(p, kvc, log_decay, cos2, sin2, ret_w):
    bsz, n, _ = p.shape
    n_ctx = kvc.shape[1]
    heads = ret_w // HEAD_DIM
    hd = HEAD_DIM
    chunk = _tile(n, RET_CHUNK)
    kern = functools.partial(_ret_kernel, n=n, n_ctx=n_ctx, chunk=chunk)
    col = lambda off: (lambda b, h, lg: (b, 0, off * heads + h))
    return pl.pallas_call(
        kern,
        grid_spec=pltpu.PrefetchScalarGridSpec(
            num_scalar_prefetch=1,
            grid=(bsz, heads),
            in_specs=[
                pl.BlockSpec((1, n, hd), col(0)),
                pl.BlockSpec((1, n, hd), col(1)),
                pl.BlockSpec((1, n, hd), col(2)),
                pl.BlockSpec((1, n, hd), col(3)),
                pl.BlockSpec((n, hd), lambda b, h, lg: (0, 0)),
                pl.BlockSpec((n, hd), lambda b, h, lg: (0, 0)),
                pl.BlockSpec((1, n_ctx, hd), col(0)),
                pl.BlockSpec((1, n_ctx, hd), col(1)),
            ],
            out_specs=pl.BlockSpec((1, n, hd), lambda b, h, lg: (b, 0, h)),
            scratch_shapes=[
                pltpu.VMEM((n, hd), F32),
                pltpu.VMEM((n, hd), F32),
                pltpu.VMEM((n // chunk, 2 * hd, hd), F32),
                pltpu.VMEM((n // chunk, 2 * hd, hd), BF16),
            ],
        ),
        out_shape=jax.ShapeDtypeStruct((bsz, n, ret_w), BF16),
        compiler_params=_params(("arbitrary", "arbitrary")),
        name="retention",
    )(log_decay, p, p, p, p, cos2, sin2, kvc, kvc)


def _cmul(a, b):
    return a[0] * b[0] - a[1] * b[1], a[0] * b[1] + a[1] * b[0]


def _cmul_const(a, c, s):
    eps = 1e-12
    if abs(s) < eps:
        return (a[0], a[1]) if c > 0 else (-a[0], -a[1])
    if abs(c) < eps:
        return (-a[1], a[0]) if s > 0 else (a[1], -a[0])
    return a[0] * c - a[1] * s, a[0] * s + a[1] * c


def _fft_list(xs, sign):
    size = len(xs)
    if size == 1:
        return xs
    even = _fft_list(xs[0::2], sign)
    odd = _fft_list(xs[1::2], sign)
    out = [None] * size
    for k in range(size // 2):
        ang = sign * 2.0 * math.pi * k / size
        tr, ti = _cmul_const(odd[k], math.cos(ang), math.sin(ang))
        out[k] = (even[k][0] + tr, even[k][1] + ti)
        out[k + size // 2] = (even[k][0] - tr, even[k][1] - ti)
    return out


def _fft_tables(n):
    j_cls = HY_CLASSES
    k0n, mcls = 2 * n // j_cls, n // j_cls
    t = j_cls * np.arange(mcls)[None, None, :] + np.arange(j_cls)[:, None, None]
    ang = 2.0 * math.pi * ((np.arange(k0n)[None, :, None] * t) % (2 * n)) / (2 * n)
    cs = np.concatenate([np.cos(ang), np.sin(ang)], axis=1)
    cst = np.concatenate([np.cos(ang).transpose(0, 2, 1), np.sin(ang).transpose(0, 2, 1)], axis=1)
    return jnp.asarray(cs, dtype=BF16), jnp.asarray(cst / (2 * n), dtype=BF16)


def _filt_kernel(z_ref, w1_ref, b1_ref, w2_ref, b2_ref, w3_ref, b3_ref, fr_ref, w4_ref, dl_ref,
                 hf_ref, hb_ref, *, n, tt, hw):
    i = pl.program_id(0)
    dot = functools.partial(jnp.dot, preferred_element_type=F32, precision=HIGHEST)
    fr = fr_ref[...]
    a = jnp.sin(fr[0:1] * (dot(z_ref[...], w1_ref[...]) + b1_ref[...]))
    a = jnp.sin(fr[1:2] * (dot(a, w2_ref[...]) + b2_ref[...]))
    a = jnp.sin(fr[2:3] * (dot(a, w3_ref[...]) + b3_ref[...]))
    hf = jnp.dot(a.astype(BF16), w4_ref[...], preferred_element_type=F32)
    mcls = n // HY_CLASSES
    rho = lax.broadcasted_iota(jnp.int32, (tt, hw), 0) + i * tt
    ti = HY_CLASSES * (rho & (mcls - 1)) + lax.shift_right_logical(rho, int(math.log2(mcls)))
    t = ti.astype(F32) * (1.0 / (n - 1.0))
    window = jnp.exp(-t * dl_ref[...])
    for o in range(HY_ORDER):
        hf_ref[:, o * hw:(o + 1) * hw] = (hf[:, (2 * o) * hw:(2 * o + 1) * hw] * window).astype(BF16)
        h_b = jnp.where(ti == 0, 0.0, hf[:, (2 * o + 1) * hw:(2 * o + 2) * hw] * window)
        hb_ref[:, o * hw:(o + 1) * hw] = h_b.astype(BF16)


def _filt_call(n, fw1, fb1, fw2, fb2, fw3, fb3, freq, fw4, hw):
    hid = fw1.shape[1]
    pk = LANE
    mcls = n // HY_CLASSES
    assert mcls & (mcls - 1) == 0
    bands = (HY_EMB_DIM - 1) // 2
    tpos = np.linspace(0.0, 1.0, n, dtype=np.float32)[:, None].astype(np.float64)
    fb = np.linspace(1e-4, bands - 1, bands, dtype=np.float32)[None, :].astype(np.float64)
    wv = (2.0 * math.pi * np.arange(n, dtype=np.float64)[:, None] / n)
    z = np.concatenate([tpos, np.cos(fb * wv), -np.sin(fb * wv)], axis=-1)
    rho = np.arange(n)
    t_of_row = HY_CLASSES * (rho % mcls) + rho // mcls
    z = np.pad(z, ((0, 0), (0, pk - HY_EMB_DIM))).astype(np.float32)[t_of_row]
    max_decay = math.log(HY_DECAY_TARGET) / HY_DECAY_SHORT_PCT
    min_decay = math.log(HY_DECAY_TARGET) / HY_DECAY_LONG_PCT
    deltas = np.abs(np.linspace(min_decay, max_decay, hw, dtype=np.float32)).reshape(1, hw)

    padc = lambda w: jnp.pad(w, ((0, 0), (0, pk - hid)))
    w1 = jnp.pad(fw1, ((0, pk - HY_EMB_DIM), (0, pk - hid)))
    w2 = jnp.pad(fw2, ((0, pk - hid), (0, pk - hid)))
    w3 = jnp.pad(fw3, ((0, pk - hid), (0, pk - hid)))
    w4 = jnp.pad(fw4, ((0, pk - hid), (0, 0))).astype(BF16)
    b1, b2, b3 = (padc(b.reshape(1, hid)) for b in (fb1, fb2, fb3))
    fr = jnp.pad(freq, ((0, SUBLANE - freq.shape[0]), (0, pk - hid)))
    tt = _tile(n, FILTER_ROWS)
    nw = HY_ORDER * hw
    full = lambda shape: pl.BlockSpec(shape, lambda i: (0, 0))
    return pl.pallas_call(
        functools.partial(_filt_kernel, n=n, tt=tt, hw=hw),
        grid=(n // tt,),
        in_specs=[
            pl.BlockSpec((tt, pk), lambda i: (i, 0)),
            full((pk, pk)), full((1, pk)), full((pk, pk)), full((1, pk)),
            full((pk, pk)), full((1, pk)), full((SUBLANE, pk)), full((pk, 2 * nw)), full((1, hw)),
        ],
        out_specs=[pl.BlockSpec((tt, nw), lambda i: (i, 0))] * 2,
        out_shape=[jax.ShapeDtypeStruct((n, nw), BF16)] * 2,
        compiler_params=_params(("arbitrary",)),
        name="hyena_filters",
    )(jnp.asarray(z), w1, b1, w2, b2, w3, b3, fr, w4, jnp.asarray(deltas))


def _sconv_kernel(p_ref, w_ref, b_ref, o_ref, *, n):
    w = w_ref[...]
    bias = b_ref[...]
    mcls = n // HY_CLASSES
    row = lax.broadcasted_iota(jnp.int32, (mcls, p_ref.shape[2]), 0)

    def cls(j):
        return p_ref[0, j * mcls:(j + 1) * mcls, :].astype(F32)

    for j in range(HY_CLASSES):
        cur = cls(j)
        if j > 0:
            prev = cls(j - 1)
        else:
            prev = jnp.where(row == 0, 0.0, pltpu.roll(cls(HY_CLASSES - 1), 1, 0))
        if j < HY_CLASSES - 1:
            nxt = cls(j + 1)
        else:
            nxt = jnp.where(row == mcls - 1, 0.0, pltpu.roll(cls(0), mcls - 1, 0))
        out = bias + prev * w[0:1] + cur * w[1:2] + nxt * w[2:3]
        o_ref[0, j * mcls:(j + 1) * mcls, :] = out.astype(o_ref.dtype)


def _sconv_call(p, conv_w, conv_b):
    bsz, n, width = p.shape
    tc = _tile(width, ROW_TILE)
    return pl.pallas_call(
        functools.partial(_sconv_kernel, n=n),
        grid=(bsz, width // tc),
        in_specs=[
            pl.BlockSpec((1, n, tc), lambda b, j: (b, 0, j)),
            pl.BlockSpec((HY_SHORT, tc), lambda b, j: (0, j)),
            pl.BlockSpec((1, tc), lambda b, j: (0, j)),
        ],
        out_specs=pl.BlockSpec((1, n, tc), lambda b, j: (b, 0, j)),
        out_shape=jax.ShapeDtypeStruct((bsz, n, width), BF16),
        compiler_params=_params(("arbitrary", "arbitrary")),
        name="short_conv",
    )(p, conv_w, conv_b.reshape(1, width))


def _spec_kernel(cs_ref, hf_ref, hb_ref, kre_ref, kim_ref, wf_scr, wb_scr,
                 *, k0n, mcls):
    j_cls = HY_CLASSES
    tn = hf_ref.shape[1]
    for j in range(j_cls):
        rows = slice(j * mcls, (j + 1) * mcls)
        prod = jnp.dot(cs_ref[j], jnp.concatenate([hf_ref[rows, :], hb_ref[rows, :]], axis=1),
                       preferred_element_type=F32)
        wf_scr[j, 0:k0n, :] = prod[0:k0n, 0:tn]
        wf_scr[j, k0n:2 * k0n, :] = -prod[k0n:2 * k0n, 0:tn]
        wb_scr[j, 0:k0n, :] = prod[0:k0n, tn:2 * tn]
        wb_scr[j, k0n:2 * k0n, :] = -prod[k0n:2 * k0n, tn:2 * tn]

    def body(c, carry):
        halves = []
        for half in range(PACKED_SUBLANE // SUBLANE):
            r8 = pl.ds(pl.multiple_of(c * PACKED_SUBLANE + half * SUBLANE, SUBLANE), SUBLANE)
            i8 = pl.ds(pl.multiple_of(k0n + c * PACKED_SUBLANE + half * SUBLANE, SUBLANE), SUBLANE)
            xs = [(wf_scr[j, r8, :], wf_scr[j, i8, :]) for j in range(j_cls)]
            ys = [(wb_scr[j, r8, :], wb_scr[j, i8, :]) for j in range(j_cls)]
            halves.append((_fft_list(xs, -1), _fft_list(ys, -1)))
        for q in range(j_cls):
            rows = pl.ds(pl.multiple_of(q * k0n + c * PACKED_SUBLANE, PACKED_SUBLANE), PACKED_SUBLANE)
            kre = jnp.concatenate([h[0][q][0] + h[1][q][0] for h in halves], axis=0)
            kim = jnp.concatenate([h[0][q][1] - h[1][q][1] for h in halves], axis=0)
            kre_ref[rows, :] = kre.astype(kre_ref.dtype)
            kim_ref[rows, :] = kim.astype(kim_ref.dtype)
        return carry

    lax.fori_loop(0, k0n // PACKED_SUBLANE, body, 0)


def _spec_call(tables, hf, hb):
    cs, _ = tables
    n, width = hf.shape
    tn = LANE
    k0n, mcls = cs.shape[1] // 2, cs.shape[2]
    const = lambda shape: pl.BlockSpec(shape, lambda j: (0,) * len(shape),
                                       pipeline_mode=pl.Buffered(1))
    return pl.pallas_call(
        functools.partial(_spec_kernel, k0n=k0n, mcls=mcls),
        grid=(width // tn,),
        in_specs=[
            const(cs.shape),
            pl.BlockSpec((n, tn), lambda j: (0, j)),
            pl.BlockSpec((n, tn), lambda j: (0, j)),
        ],
        out_specs=[pl.BlockSpec((2 * n, tn), lambda j: (0, j))] * 2,
        out_shape=[jax.ShapeDtypeStruct((2 * n, width), BF16)] * 2,
        scratch_shapes=[pltpu.VMEM((HY_CLASSES, 2 * k0n, tn), F32)] * 2,
        compiler_params=_params(("arbitrary",)),
        name="filter_spectrum",
    )(cs, hf, hb)


def _fftconv_kernel(cs_ref, cst_ref, z_ref, g_ref, kre_ref, kim_ref, bias_ref,
                    o_ref, w_scr, *, k0n, mcls):
    j_cls = HY_CLASSES
    tn = o_ref.shape[3]
    for j in range(j_cls):
        rows = slice(j * mcls, (j + 1) * mcls)
        prod = jnp.dot(cs_ref[j], jnp.concatenate([z_ref[0, 0, rows, :], z_ref[0, 1, rows, :]], axis=1),
                       preferred_element_type=F32)
        w_scr[j, 0:k0n, :] = prod[0:k0n, 0:tn] + prod[k0n:2 * k0n, tn:2 * tn]
        w_scr[j, k0n:2 * k0n, :] = prod[0:k0n, tn:2 * tn] - prod[k0n:2 * k0n, 0:tn]

    def body(c, carry):
        for half in range(PACKED_SUBLANE // SUBLANE):
            r8 = pl.ds(pl.multiple_of(c * PACKED_SUBLANE + half * SUBLANE, SUBLANE), SUBLANE)
            i8 = pl.ds(pl.multiple_of(k0n + c * PACKED_SUBLANE + half * SUBLANE, SUBLANE), SUBLANE)
            xs = [(w_scr[j, r8, :], w_scr[j, i8, :]) for j in range(j_cls)]
            spec = _fft_list(xs, -1)
            prod = []
            for q in range(j_cls):
                rows = pl.ds(pl.multiple_of(q * k0n + c * PACKED_SUBLANE, PACKED_SUBLANE), PACKED_SUBLANE)
                part = slice(half * SUBLANE, (half + 1) * SUBLANE)
                kq = (kre_ref[rows, :].astype(F32)[part], kim_ref[rows, :].astype(F32)[part])
                prod.append(_cmul(spec[q], kq))
            back = _fft_list(prod, 1)
            for j in range(j_cls):
                w_scr[j, r8, :] = back[j][0]
                w_scr[j, i8, :] = back[j][1]
        return carry

    lax.fori_loop(0, k0n // PACKED_SUBLANE, body, 0)

    bias = bias_ref[...]
    for j in range(j_cls):
        rows = slice(j * mcls, (j + 1) * mcls)
        v2 = jnp.concatenate([w_scr[j, 0:k0n, :], w_scr[j, k0n:2 * k0n, :]], axis=1).astype(BF16)
        prod = jnp.dot(cst_ref[j], v2, preferred_element_type=F32)
        ys = (prod[0:mcls, 0:tn] - prod[mcls:2 * mcls, tn:2 * tn],
              prod[mcls:2 * mcls, 0:tn] + prod[0:mcls, tn:2 * tn])
        for b2 in range(2):
            conv = ys[b2] + z_ref[0, b2, rows, :].astype(F32) * bias
            o_ref[0, b2, rows, :] = (g_ref[0, b2, rows, :].astype(F32) * conv).astype(o_ref.dtype)


def _fftconv_call(tables, z, z_col0, gate, gate_col0, kre, kim, k_col0, bias):
    cs, cst = tables
    bsz, n, _ = z.shape
    hw = bias.shape[0]
    tn = LANE
    k0n, mcls = cs.shape[1] // 2, cs.shape[2]
    zo, go, ko = z_col0 // tn, gate_col0 // tn, k_col0 // tn
    z4 = z.reshape(bsz // 2, 2, n, z.shape[2])
    g4 = gate.reshape(bsz // 2, 2, n, gate.shape[2])
    const = lambda shape: pl.BlockSpec(shape, lambda j, p: (0,) * len(shape),
                                       pipeline_mode=pl.Buffered(1))
    out = pl.pallas_call(
        functools.partial(_fftconv_kernel, k0n=k0n, mcls=mcls),
        grid=(hw // tn, bsz // 2),
        in_specs=[
            const(cs.shape), const(cst.shape),
            pl.BlockSpec((1, 2, n, tn), lambda j, p: (p, 0, 0, zo + j)),
            pl.BlockSpec((1, 2, n, tn), lambda j, p: (p, 0, 0, go + j)),
            pl.BlockSpec((2 * n, tn), lambda j, p: (0, ko + j)),
            pl.BlockSpec((2 * n, tn), lambda j, p: (0, ko + j)),
            pl.BlockSpec((1, tn), lambda j, p: (0, j)),
        ],
        out_specs=pl.BlockSpec((1, 2, n, tn), lambda j, p: (p, 0, 0, j)),
        out_shape=jax.ShapeDtypeStruct((bsz // 2, 2, n, hw), BF16),
        scratch_shapes=[pltpu.VMEM((HY_CLASSES, 2 * k0n, tn), F32)],
        compiler_params=_params(("arbitrary", "arbitrary")),
        name="fft_conv",
    )(cs, cst, z4, g4, kre, kim, bias.reshape(1, hw))
    return out.reshape(bsz, n, hw)


def _wout_kernel(r_ref, y_ref, wa_ref, wb_ref, x_ref, gate_ref, o_ref):
    y = (jnp.dot(r_ref[0], wa_ref[...], preferred_element_type=F32)
         + jnp.dot(y_ref[0], wb_ref[...], preferred_element_type=F32))
    o_ref[0] = x_ref[0] + gate_ref[0] * y


def _wout_call(r, yh, w_out, x, gate):
    bsz, n, d = x.shape
    rw = r.shape[2]
    assert d == 2 * rw and yh.shape[2] == rw
    tm = _tile(n, ROW_TILE)
    return pl.pallas_call(
        _wout_kernel,
        grid=(bsz, n // tm),
        in_specs=[
            pl.BlockSpec((1, tm, rw), lambda b, i: (b, i, 0)),
            pl.BlockSpec((1, tm, d - rw), lambda b, i: (b, i, 0)),
            pl.BlockSpec((rw, d), lambda b, i: (0, 0)),
            pl.BlockSpec((rw, d), lambda b, i: (1, 0)),
            pl.BlockSpec((1, tm, d), lambda b, i: (b, i, 0)),
            pl.BlockSpec((1, 1, d), lambda b, i: (b, 0, 0)),
        ],
        out_specs=pl.BlockSpec((1, tm, d), lambda b, i: (b, i, 0)),
        out_shape=jax.ShapeDtypeStruct((bsz, n, d), F32),
        compiler_params=_params(("arbitrary", "arbitrary")),
        name="mixer_out",
    )(r, yh, w_out, w_out, x, gate)


def _pool_kernel(h_ref, mc_ref, icol_ref, w_ref, sc_ref, x_ref, gate_ref, o_ref, cp_scr, pm_scr,
                 *, n, tm, to):
    g = pl.program_id(1)
    i = pl.program_id(2)
    n_rows = n // GRID_W

    @pl.when(i == 0)
    def _():
        for t in range(n // tm):
            rows = slice(t * tm, (t + 1) * tm)
            cp_scr[rows, :] = jnp.dot(mc_ref[0], h_ref[0, rows, :], preferred_element_type=F32)

    def grid_row(r):
        return slice(r * GRID_W, (r + 1) * GRID_W)

    for gi, win in enumerate(POOL_WINDOWS):
        @pl.when(jnp.logical_and(i == 0, g == gi))
        def _(win=win):
            lo, hi = win // 2, win - 1 - win // 2
            icol = icol_ref[0]
            acc = cp_scr[grid_row(0), :]
            for r in range(1, hi + 1):
                acc = acc + cp_scr[grid_row(r), :]
            for r in range(n_rows):
                if r > 0 and r + hi < n_rows:
                    acc = acc + cp_scr[grid_row(r + hi), :]
                if r - lo - 1 >= 0:
                    acc = acc - cp_scr[grid_row(r - lo - 1), :]
                cnt_r = min(r + hi, n_rows - 1) + 1 - max(r - lo, 0)
                mean = acc * (icol * (1.0 / cnt_r))
                pm_scr[grid_row(r), :] = (mean - h_ref[0, grid_row(r), :].astype(F32)).astype(BF16)

    rows = pl.ds(pl.multiple_of(i * to, to), to)
    y = jnp.dot(pm_scr[rows, :], w_ref[0], preferred_element_type=F32) * sc_ref[...]
    o_ref[0] = x_ref[0] + gate_ref[0] * y


def _pool_tables(tm, pg):
    tok = np.arange(tm)
    r_t, c_t = tok // GRID_W, tok % GRID_W
    mats, inv = [], []
    for win in POOL_WINDOWS:
        lo, hi = -(win // 2), win - 1 - win // 2
        dc = c_t[None, :] - c_t[:, None]
        mats.append(((r_t[None, :] == r_t[:, None]) & (dc >= lo) & (dc <= hi)).astype(np.float32))
        idx_c = np.arange(GRID_W)
        cnt_c = np.minimum(idx_c + hi, GRID_W - 1) + 1 - np.maximum(idx_c + lo, 0)
        inv.append(np.broadcast_to((1.0 / cnt_c)[:, None], (GRID_W, pg)))
    return (jnp.asarray(np.stack(mats), dtype=BF16),
            jnp.asarray(np.stack(inv).astype(np.float32)))


def _pool_call(h, pool_w, pool_scale, x, gate):
    bsz, n, d = x.shape
    ng = len(POOL_WINDOWS)
    pg = d // ng
    tm = _tile(n, ROW_TILE)
    to = _tile(n, 2 * WIDE_TILE)
    assert tm % GRID_W == 0 and n % GRID_W == 0
    mats, icol = _pool_tables(tm, pg)
    return pl.pallas_call(
        functools.partial(_pool_kernel, n=n, tm=tm, to=to),
        grid=(bsz, ng, n // to),
        in_specs=[
            pl.BlockSpec((1, n, pg), lambda b, g, i: (b, 0, g)),
            pl.BlockSpec((1, tm, tm), lambda b, g, i: (g, 0, 0)),
            pl.BlockSpec((1, GRID_W, pg), lambda b, g, i: (g, 0, 0)),
            pl.BlockSpec((1, pg, pg), lambda b, g, i: (g, 0, 0)),
            pl.BlockSpec((1, pg), lambda b, g, i: (0, g)),
            pl.BlockSpec((1, to, pg), lambda b, g, i: (b, i, g)),
            pl.BlockSpec((1, 1, pg), lambda b, g, i: (b, 0, g)),
        ],
        out_specs=pl.BlockSpec((1, to, pg), lambda b, g, i: (b, i, g)),
        out_shape=jax.ShapeDtypeStruct((bsz, n, d), F32),
        scratch_shapes=[pltpu.VMEM((n, pg), F32), pltpu.VMEM((n, pg), BF16)],
        compiler_params=_params(("arbitrary", "arbitrary", "arbitrary")),
        name="pool_mixer",
    )(h, mats, icol, pool_w, pool_scale.reshape(1, d), x, gate)


def _rotary_tables(n):
    pos = np.arange(n)
    row = (pos // GRID_W).astype(np.float32)
    col = (pos % GRID_W).astype(np.float32)
    n_freq = HEAD_DIM // 4
    inv = (np.float32(ROPE_BASE) ** (-np.arange(n_freq, dtype=np.float32) / np.float32(n_freq))).astype(np.float32)
    ang = np.concatenate([row[:, None] * inv, col[:, None] * inv], axis=-1).astype(np.float32)
    cos, sin = np.cos(ang.astype(np.float64)), np.sin(ang.astype(np.float64))
    cos2 = np.concatenate([cos, cos], axis=-1).astype(np.float32)
    sin2 = np.concatenate([-sin, sin], axis=-1).astype(np.float32)
    return jnp.asarray(cos2), jnp.asarray(sin2)


def kernel(x, c, ctx, c_ctx, w_mod, b_mod, ffn1_w1, ffn1_w3, ffn1_w2, ffn2_w1, ffn2_w3, ffn2_w2, ab_w_in, ab_w_out, ret_log_decay, hy_conv_w, hy_conv_b, hy_f_w1, hy_f_b1, hy_f_w2, hy_f_b2, hy_f_w3, hy_f_b3, hy_f_freq, hy_f_w4, hy_bias, pool_w, pool_scale, final_gain):
    bsz, n, d = x.shape
    n_ctx = ctx.shape[1]
    depth = w_mod.shape[0]
    assert depth == 2, "layer 0 = retention/Hyena mixer, layer 1 = pooling mixer"
    ret_w = RET_HEADS * HEAD_DIM
    hw = d - ret_w
    assert hy_f_w4.shape[2] == HY_ORDER * 2 * hw and bsz % 2 == 0

    c_all = jnp.concatenate([c, c_ctx[None, :], jnp.zeros((SUBLANE - bsz - 1, d), F32)], axis=0)
    mod = _mod_call(c_all, w_mod, b_mod).reshape(depth, SUBLANE, N_MOD, d)

    def mod_rows(layer, base, count=3):
        return mod[layer, :bsz, base:base + count]

    def mod_ctx(base, count=3):
        return mod[0, bsz:bsz + 1, base:base + count]

    n_f = ffn1_w1.shape[2] // _tile(ffn1_w1.shape[2], FFN_TF)
    w13 = _cast_pairs_call([(ffn1_w1, ffn1_w3), (ffn2_w1, ffn2_w3)], n_f)
    w_in_shape, w_out_shape = ab_w_in.shape[1:], ab_w_out.shape[1:]
    w2a, w2b, w_in, w_out = _cast_call(
        [ffn1_w2, ffn2_w2, ab_w_in[0].reshape(depth, -1, w_in_shape[1]),
         ab_w_out[0].reshape(depth, -1, w_out_shape[1])], CAST_STEPS)
    w_in, w_out = w_in.reshape(w_in_shape), w_out.reshape(w_out_shape)
    ffn1_w, ffn2_w = (w13[0], w2a), (w13[1], w2b)

    x, h = _ffn_call(x, mod_rows(0, 0), *ffn1_w, 0, modn=mod_rows(0, 3))
    _, h_c = _ffn_call(ctx.reshape(1, bsz * n_ctx, d), mod_ctx(0), *ffn1_w, 0, modn=mod_ctx(3))

    p = _matmul_call(h.reshape(bsz * n, d), w_in, BF16, tm_pref=2 * WIDE_TILE,
                     b_cols=(0, 4 * ret_w)).reshape(bsz, n, -1)
    kvc = _matmul_call(h_c.reshape(bsz * n_ctx, d), w_in, BF16, b_cols=(ret_w, 3 * ret_w))
    kvc = kvc.reshape(bsz, n_ctx, 2 * ret_w)

    cos2, sin2 = _rotary_tables(n)
    r = _ret_call(p, kvc, ret_log_decay[0], cos2, sin2, ret_w)

    mcls = n // HY_CLASSES
    h_cm = h.reshape(bsz, mcls, HY_CLASSES, d).transpose(0, 2, 1, 3).reshape(bsz * n, d)
    p_hy = _matmul_call(h_cm, w_in, BF16, tm_pref=2 * WIDE_TILE,
                        b_cols=(4 * ret_w, w_in.shape[1])).reshape(bsz, n, -1)
    hf, hb = _filt_call(n, hy_f_w1[0], hy_f_b1[0], hy_f_w2[0], hy_f_b2[0], hy_f_w3[0],
                        hy_f_b3[0], hy_f_freq[0], hy_f_w4[0], hw)
    tables = _fft_tables(n)
    kre, kim = _spec_call(tables, hf, hb)
    u = _sconv_call(p_hy, hy_conv_w[0], hy_conv_b[0])
    z = _fftconv_call(tables, u, 0, u, hw, kre, kim, 0, hy_bias[0, 0])
    yh = _fftconv_call(tables, z, 0, u, 2 * hw, kre, kim, hw, hy_bias[0, 1])
    yh = yh.reshape(bsz, HY_CLASSES, mcls, hw).transpose(0, 2, 1, 3).reshape(bsz, n, hw)

    x = _wout_call(r, yh, w_out, x, mod_rows(0, 5, 1))
    x = _ffn_call(x, mod_rows(0, 6), *ffn2_w, 0)

    x, h = _ffn_call(x, mod_rows(1, 0), *ffn1_w, 1, modn=mod_rows(1, 3))
    x = _pool_call(h, pool_w[0].astype(BF16), pool_scale[0], x, mod_rows(1, 5, 1))
    return _ffn_call(x, mod_rows(1, 6), *ffn2_w, 1, gain=final_gain)
```

```python
import functools
import math

import numpy as np
import jax
import jax.numpy as jnp
from jax import lax
from jax.experimental import pallas as pl
from jax.experimental.pallas import tpu as pltpu

F32 = jnp.float32
BF16 = jnp.bfloat16
HIGHEST = lax.Precision.HIGHEST

EPS = 1e-6
GRID_W = 64
N_MOD = 9
RET_HEADS = 8
HEAD_DIM = 128
HY_ORDER = 2
HY_SHORT = 3
HY_EMB_DIM = 33
HY_DECAY_SHORT_PCT = 0.3
HY_DECAY_LONG_PCT = 1.5
HY_DECAY_TARGET = 1e-2
ROPE_BASE = 10000.0
POOL_WINDOWS = (2, 4, 8, 16)
RET_CHUNK = 256
RET_UNROLL = 16
HY_CLASSES = 16
LANE = 128
SUBLANE = 8
PACKED_SUBLANE = 16
ROW_TILE = 512
WIDE_TILE = 1024
FFN_TF = 512
FILTER_ROWS = 256
CAST_STEPS = 16
VMEM_LIMIT = 56 * 1024 * 1024
FFN_VMEM_LIMIT = 60 * 1024 * 1024


def _tile(dim, pref):
    if dim <= pref:
        return dim
    for t in range(pref - pref % LANE, 0, -LANE):
        if dim % t == 0:
            return t
    raise ValueError((dim, pref))


def _params(sem, vmem_limit=VMEM_LIMIT):
    return pltpu.CompilerParams(dimension_semantics=sem, vmem_limit_bytes=vmem_limit)


def _rms(x):
    return x * lax.rsqrt(jnp.mean(x * x, axis=-1, keepdims=True) + EPS)


def _silu(x):
    return x * jax.nn.sigmoid(x)


def _mod_kernel(c_ref, w_ref, b_ref, o_ref):
    sc = _silu(c_ref[...])
    w = w_ref[0]
    sc_hi = sc.astype(BF16)
    sc_lo = (sc - sc_hi.astype(F32)).astype(BF16)
    w_hi = w.astype(BF16)
    w_lo = (w - w_hi.astype(F32)).astype(BF16)
    rows = sc.shape[0]
    top = jnp.dot(jnp.concatenate([sc_hi, sc_lo], axis=0), w_hi, preferred_element_type=F32)
    o_ref[0] = (top[0:rows] + top[rows:2 * rows]
                + jnp.dot(sc_hi, w_lo, preferred_element_type=F32) + b_ref[0])


def _mod_call(c_all, w_mod, b_mod):
    depth, d, nd = w_mod.shape
    rows = c_all.shape[0]
    tn = _tile(nd, WIDE_TILE)
    return pl.pallas_call(
        _mod_kernel,
        grid=(depth, nd // tn),
        in_specs=[
            pl.BlockSpec((rows, d), lambda l, j: (0, 0)),
            pl.BlockSpec((1, d, tn), lambda l, j: (l, 0, j)),
            pl.BlockSpec((1, 1, tn), lambda l, j: (l, 0, j)),
        ],
        out_specs=pl.BlockSpec((1, rows, tn), lambda l, j: (l, 0, j)),
        out_shape=jax.ShapeDtypeStruct((depth, rows, nd), F32),
        compiler_params=_params(("arbitrary", "arbitrary")),
        name="mod",
    )(c_all, w_mod, b_mod.reshape(depth, 1, nd))


def _ffn_kernel(*refs, n_f, emit_h, final, row_chunks):
    x_ref, mod_ref, w13_ref, w2_ref = refs[:4]
    pos = 4
    modn_ref = gain_ref = hn_ref = None
    if emit_h:
        modn_ref = refs[pos]
        pos += 1
    if final:
        gain_ref = refs[pos]
        pos += 1
    o_ref = refs[pos]
    pos += 1
    if emit_h:
        hn_ref = refs[pos]
        pos += 1
    h_scr = refs[pos]
    acc = o_ref.at[0] if row_chunks > 1 else refs[pos + 1]
    f = pl.program_id(2)
    tm = h_scr.shape[0]
    chunks = [slice(r * (tm // row_chunks), (r + 1) * (tm // row_chunks)) for r in range(row_chunks)]

    assert n_f >= 3

    tf = w2_ref.shape[0]

    def partial_out(h):
        ab = jnp.dot(h, w13_ref[...], preferred_element_type=F32)
        g = (_silu(ab[:, 0:tf]) * ab[:, tf:2 * tf]).astype(BF16)
        return jnp.dot(g, w2_ref[...], preferred_element_type=F32)

    @pl.when(f == 0)
    def _():
        m = mod_ref[0]
        for rows in chunks:
            h = (_rms(x_ref[0, rows, :]) * (1.0 + m[1:2]) + m[0:1]).astype(BF16)
            h_scr[rows, :] = h
            acc[rows, :] = partial_out(h)

    @pl.when(jnp.logical_and(f > 0, f < n_f - 1))
    def _():
        for rows in chunks:
            acc[rows, :] += partial_out(h_scr[rows, :])

    @pl.when(f == n_f - 1)
    def _():
        m = mod_ref[0]
        for rows in chunks:
            xo = x_ref[0, rows, :] + 0.5 * m[2:3] * (acc[rows, :] + partial_out(h_scr[rows, :]))
            if emit_h:
                mn = modn_ref[0]
                hn_ref[0, rows, :] = (_rms(xo) * (1.0 + mn[1:2]) + mn[0:1]).astype(BF16)
            if final:
                xo = _rms(xo) * gain_ref[...]
            o_ref[0, rows, :] = xo


def _ffn_call(x, mod3, w13, w2, layer, modn=None, gain=None):
    bsz, s, d = x.shape
    n_f, tf = w13.shape[1], w13.shape[3] // 2
    emit_h = modn is not None
    final = gain is not None
    tm = _tile(s, WIDE_TILE)
    row_chunks = tm // _tile(tm, ROW_TILE)
    in_specs = [
        pl.BlockSpec((1, tm, d), lambda b, i, f: (b, i, 0)),
        pl.BlockSpec((1, 3, d), lambda b, i, f: (b, 0, 0)),
        pl.BlockSpec((None, None, d, 2 * tf), lambda b, i, f: (layer, f, 0, 0)),
        pl.BlockSpec((None, tf, d), lambda b, i, f: (layer, f, 0)),
    ]
    args = [x, mod3, w13, w2]
    if emit_h:
        in_specs.append(pl.BlockSpec((1, 3, d), lambda b, i, f: (b, 0, 0)))
        args.append(modn)
    if final:
        in_specs.append(pl.BlockSpec((1, d), lambda b, i, f: (0, 0)))
        args.append(gain.reshape(1, d))
    out_specs = [pl.BlockSpec((1, tm, d), lambda b, i, f: (b, i, 0))]
    out_shape = [jax.ShapeDtypeStruct((bsz, s, d), F32)]
    if emit_h:
        out_specs.append(pl.BlockSpec((1, tm, d), lambda b, i, f: (b, i, 0)))
        out_shape.append(jax.ShapeDtypeStruct((bsz, s, d), BF16))
    outs = pl.pallas_call(
        functools.partial(_ffn_kernel, n_f=n_f, emit_h=emit_h, final=final, row_chunks=row_chunks),
        grid=(bsz, s // tm, n_f),
        in_specs=in_specs,
        out_specs=out_specs,
        out_shape=out_shape,
        scratch_shapes=[pltpu.VMEM((tm, d), BF16)] + ([] if row_chunks > 1 else [pltpu.VMEM((tm, d), F32)]),
        compiler_params=_params(("arbitrary", "arbitrary", "arbitrary"), FFN_VMEM_LIMIT),
        name="ffn",
    )(*args)
    return outs if emit_h else outs[0]


def _cast_kernel(*refs):
    half = len(refs) // 2
    for src, dst in zip(refs[:half], refs[half:]):
        dst[...] = src[...].astype(dst.dtype)


def _cast_call(arrays, steps):
    depth = arrays[0].shape[0]
    specs = []
    for a in arrays:
        assert a.shape[0] == depth and a.shape[1] % (PACKED_SUBLANE * steps) == 0
        specs.append(pl.BlockSpec((1, a.shape[1] // steps, a.shape[2]), lambda l, j: (l, j, 0)))
    return pl.pallas_call(
        _cast_kernel,
        grid=(depth, steps),
        in_specs=specs,
        out_specs=specs,
        out_shape=[jax.ShapeDtypeStruct(a.shape, BF16) for a in arrays],
        compiler_params=_params(("arbitrary", "arbitrary")),
        name="cast_weights",
    )(*arrays)


def _cast_pairs_kernel(*refs):
    n_out = len(refs) // 3
    for k in range(n_out):
        wa, wb, dst = refs[2 * k], refs[2 * k + 1], refs[2 * n_out + k]
        tf = wa.shape[2]
        dst[0, 0, :, 0:tf] = wa[0].astype(dst.dtype)
        dst[0, 0, :, tf:2 * tf] = wb[0].astype(dst.dtype)


def _cast_pairs_call(pairs, steps):
    depth, r, c = pairs[0][0].shape
    tf = c // steps
    in_spec = pl.BlockSpec((1, r, tf), lambda l, j: (l, 0, j))
    out_spec = pl.BlockSpec((1, 1, r, 2 * tf), lambda l, j: (l, j, 0, 0))
    return pl.pallas_call(
        _cast_pairs_kernel,
        grid=(depth, steps),
        in_specs=[in_spec] * (2 * len(pairs)),
        out_specs=[out_spec] * len(pairs),
        out_shape=[jax.ShapeDtypeStruct((depth, steps, r, 2 * tf), BF16)] * len(pairs),
        compiler_params=_params(("arbitrary", "arbitrary")),
        name="cast_weight_pairs",
    )(*[w for pair in pairs for w in pair])


def _matmul_kernel(a_ref, b_ref, o_ref):
    o_ref[...] = jnp.dot(a_ref[...], b_ref[...], preferred_element_type=F32).astype(o_ref.dtype)


def _matmul_call(a, b, out_dtype, tm_pref=WIDE_TILE, tn_pref=WIDE_TILE, b_cols=None):
    m, k = a.shape
    c0, c1 = b_cols if b_cols is not None else (0, b.shape[1])
    n = c1 - c0
    tm = _tile(m, tm_pref)
    tn = math.gcd(_tile(n, tn_pref), c0) if c0 else _tile(n, tn_pref)
    assert tn % LANE == 0
    joff = c0 // tn
    return pl.pallas_call(
        _matmul_kernel,
        grid=(m // tm, n // tn),
        in_specs=[
            pl.BlockSpec((tm, k), lambda i, j: (i, 0)),
            pl.BlockSpec((k, tn), lambda i, j: (0, joff + j)),
        ],
        out_specs=pl.BlockSpec((tm, tn), lambda i, j: (i, j)),
        out_shape=jax.ShapeDtypeStruct((m, n), out_dtype),
        compiler_params=_params(("arbitrary", "arbitrary")),
        name="matmul",
    )(a, b)


def _ret_kernel(lg_ref, q_ref, k_ref, v_ref, g_ref, cos_ref, sin_ref, kc_ref, vc_ref, o_ref,
                qr_scr, kr_scr, upd_scr, st_scr, *, n, n_ctx, chunk):
    hd = HEAD_DIM
    head = pl.program_id(1)
    lgf = lg_ref[0, head]
    lgb = lg_ref[1, head]
    nc = n // chunk
    k_scale = hd ** -0.5

    posc = lax.broadcasted_iota(jnp.int32, (n_ctx, hd), 0).astype(F32)
    kc = kc_ref[0].astype(F32) * k_scale
    vc = vc_ref[0]
    kcf = (kc * jnp.exp((n_ctx - 1.0 - posc) * lgf)).T.astype(BF16)
    kcb = (kc * jnp.exp(posc * lgb)).T.astype(BF16)
    s0_f = jnp.dot(kcf, vc, preferred_element_type=F32)
    s0_b = jnp.dot(kcb, vc, preferred_element_type=F32)

    pos = lax.broadcasted_iota(jnp.int32, (chunk, hd), 0).astype(F32)
    qdec_f = jnp.exp((pos + 1.0) * lgf)
    kdec_f = jnp.exp((chunk - 1.0 - pos) * lgf)
    qdec_b = jnp.exp((chunk - pos) * lgb)
    kdec_b = jnp.exp(pos * lgb)
    cdec_f = jnp.exp(jnp.full((1, hd), chunk, F32) * lgf)
    cdec_b = jnp.exp(jnp.full((1, hd), chunk, F32) * lgb)
    ri = lax.broadcasted_iota(jnp.int32, (chunk, chunk), 0)
    ci = lax.broadcasted_iota(jnp.int32, (chunk, chunk), 1)
    diff = (ri - ci).astype(F32)
    dmat = (jnp.where(diff >= 0, jnp.exp(jnp.maximum(diff, 0.0) * lgf), 0.0)
            + jnp.where(diff <= 0, jnp.exp(jnp.maximum(-diff, 0.0) * lgb), 0.0))

    def rotary(x, rows):
        return x * cos_ref[rows, :] + pltpu.roll(x, hd // 2, 1) * sin_ref[rows, :]

    def chunk_rows(c):
        return pl.ds(pl.multiple_of(c * chunk, chunk), chunk)

    def prep_body(c, carry):
        rows = chunk_rows(c)
        qr_scr[rows, :] = rotary(q_ref[0, rows, :].astype(F32), rows)
        k = rotary(k_ref[0, rows, :].astype(F32), rows) * k_scale
        kr_scr[rows, :] = k
        kk = jnp.concatenate([k * kdec_f, k * kdec_b], axis=1).T.astype(BF16)
        upd_scr[c] = jnp.dot(kk, v_ref[0, rows, :], preferred_element_type=F32)
        return carry

    lax.fori_loop(0, nc, prep_body, 0, unroll=math.gcd(nc, RET_UNROLL))

    def fwd_scan(c, sf):
        st_scr[c, 0:hd, :] = sf.astype(BF16)
        return cdec_f * sf + upd_scr[c, 0:hd, :]

    def bwd_scan(j, sb):
        c = nc - 1 - j
        st_scr[c, hd:2 * hd, :] = sb.astype(BF16)
        return cdec_b * sb + upd_scr[c, hd:2 * hd, :]

    lax.fori_loop(0, nc, fwd_scan, s0_f, unroll=math.gcd(nc, RET_UNROLL))
    lax.fori_loop(0, nc, bwd_scan, s0_b, unroll=math.gcd(nc, RET_UNROLL))

    def out_body(c, carry):
        rows = chunk_rows(c)
        q = qr_scr[rows, :]
        s = lax.dot_general(q.astype(BF16), kr_scr[rows, :].astype(BF16), (((1,), (1,)), ((), ())),
                            preferred_element_type=F32) * dmat
        qq = jnp.concatenate([q * qdec_f, q * qdec_b], axis=1).astype(BF16)
        o = (jnp.dot(s.astype(BF16), v_ref[0, rows, :], preferred_element_type=F32)
             + jnp.dot(qq, st_scr[c], preferred_element_type=F32))
        g = g_ref[0, rows, :].astype(F32)
        o_ref[0, rows, :] = (_rms(o) * _silu(g)).astype(o_ref.dtype)
        return carry

    lax.fori_loop(0, nc, out_body, 0, unroll=math.gcd(nc, RET_UNROLL))


def _ret_call(p, kvc, log_decay, cos2, sin2, ret_w):
    bsz, n, _ = p.shape
    n_ctx = kvc.shape[1]
    heads = ret_w // HEAD_DIM
    hd = HEAD_DIM
    chunk = _tile(n, RET_CHUNK)
    kern = functools.partial(_ret_kernel, n=n, n_ctx=n_ctx, chunk=chunk)
    col = lambda off: (lambda b, h, lg: (b, 0, off * heads + h))
    return pl.pallas_call(
        kern,
        grid_spec=pltpu.PrefetchScalarGridSpec(
            num_scalar_prefetch=1,
            grid=(bsz, heads),
            in_specs=[
                pl.BlockSpec((1, n, hd), col(0)),
                pl.BlockSpec((1, n, hd), col(1)),
                pl.BlockSpec((1, n, hd), col(2)),
                pl.BlockSpec((1, n, hd), col(3)),
                pl.BlockSpec((n, hd), lambda b, h, lg: (0, 0)),
                pl.BlockSpec((n, hd), lambda b, h, lg: (0, 0)),
                pl.BlockSpec((1, n_ctx, hd), col(0)),
                pl.BlockSpec((1, n_ctx, hd), col(1)),
            ],
            out_specs=pl.BlockSpec((1, n, hd), lambda b, h, lg: (b, 0, h)),
            scratch_shapes=[
                pltpu.VMEM((n, hd), F32),
                pltpu.VMEM((n, hd), F32),
                pltpu.VMEM((n // chunk, 2 * hd, hd), F32),
                pltpu.VMEM((n // chunk, 2 * hd, hd), BF16),
            ],
        ),
        out_shape=jax.ShapeDtypeStruct((bsz, n, ret_w), BF16),
        compiler_params=_params(("arbitrary", "arbitrary")),
        name="retention",
    )(log_decay, p, p, p, p, cos2, sin2, kvc, kvc)


def _cmul(a, b):
    return a[0] * b[0] - a[1] * b[1], a[0] * b[1] + a[1] * b[0]


def _cmul_const(a, c, s):
    eps = 1e-12
    if abs(s) < eps:
        return (a[0], a[1]) if c > 0 else (-a[0], -a[1])
    if abs(c) < eps:
        return (-a[1], a[0]) if s > 0 else (a[1], -a[0])
    return a[0] * c - a[1] * s, a[0] * s + a[1] * c


def _fft_list(xs, sign):
    size = len(xs)
    if size == 1:
        return xs
    even = _fft_list(xs[0::2], sign)
    odd = _fft_list(xs[1::2], sign)
    out = [None] * size
    for k in range(size // 2):
        ang = sign * 2.0 * math.pi * k / size
        tr, ti = _cmul_const(odd[k], math.cos(ang), math.sin(ang))
        out[k] = (even[k][0] + tr, even[k][1] + ti)
        out[k + size // 2] = (even[k][0] - tr, even[k][1] - ti)
    return out


def _fft_tables(n):
    j_cls = HY_CLASSES
    k0n, mcls = 2 * n // j_cls, n // j_cls
    t = j_cls * np.arange(mcls)[None, None, :] + np.arange(j_cls)[:, None, None]
    ang = 2.0 * math.pi * ((np.arange(k0n)[None, :, None] * t) % (2 * n)) / (2 * n)
    cs = np.concatenate([np.cos(ang), np.sin(ang)], axis=1)
    cst = np.concatenate([np.cos(ang).transpose(0, 2, 1), np.sin(ang).transpose(0, 2, 1)], axis=1)
    return jnp.asarray(cs, dtype=BF16), jnp.asarray(cst / (2 * n), dtype=BF16)


def _filt_kernel(z_ref, w1_ref, b1_ref, w2_ref, b2_ref, w3_ref, b3_ref, fr_ref, w4_ref, dl_ref,
                 hf_ref, hb_ref, *, n, tt, hw):
    i = pl.program_id(0)
    dot = functools.partial(jnp.dot, preferred_element_type=F32, precision=HIGHEST)
    fr = fr_ref[...]
    a = jnp.sin(fr[0:1] * (dot(z_ref[...], w1_ref[...]) + b1_ref[...]))
    a = jnp.sin(fr[1:2] * (dot(a, w2_ref[...]) + b2_ref[...]))
    a = jnp.sin(fr[2:3] * (dot(a, w3_ref[...]) + b3_ref[...]))
    hf = jnp.dot(a.astype(BF16), w4_ref[...], preferred_element_type=F32)
    mcls = n // HY_CLASSES
    rho = lax.broadcasted_iota(jnp.int32, (tt, hw), 0) + i * tt
    ti = HY_CLASSES * (rho & (mcls - 1)) + lax.shift_right_logical(rho, int(math.log2(mcls)))
    t = ti.astype(F32) * (1.0 / (n - 1.0))
    window = jnp.exp(-t * dl_ref[...])
    for o in range(HY_ORDER):
        hf_ref[:, o * hw:(o + 1) * hw] = (hf[:, (2 * o) * hw:(2 * o + 1) * hw] * window).astype(BF16)
        h_b = jnp.where(ti == 0, 0.0, hf[:, (2 * o + 1) * hw:(2 * o + 2) * hw] * window)
        hb_ref[:, o * hw:(o + 1) * hw] = h_b.astype(BF16)


def _filt_call(n, fw1, fb1, fw2, fb2, fw3, fb3, freq, fw4, hw):
    hid = fw1.shape[1]
    pk = LANE
    mcls = n // HY_CLASSES
    assert mcls & (mcls - 1) == 0
    bands = (HY_EMB_DIM - 1) // 2
    tpos = np.linspace(0.0, 1.0, n, dtype=np.float32)[:, None].astype(np.float64)
    fb = np.linspace(1e-4, bands - 1, bands, dtype=np.float32)[None, :].astype(np.float64)
    wv = (2.0 * math.pi * np.arange(n, dtype=np.float64)[:, None] / n)
    z = np.concatenate([tpos, np.cos(fb * wv), -np.sin(fb * wv)], axis=-1)
    rho = np.arange(n)
    t_of_row = HY_CLASSES * (rho % mcls) + rho // mcls
    z = np.pad(z, ((0, 0), (0, pk - HY_EMB_DIM))).astype(np.float32)[t_of_row]
    max_decay = math.log(HY_DECAY_TARGET) / HY_DECAY_SHORT_PCT
    min_decay = math.log(HY_DECAY_TARGET) / HY_DECAY_LONG_PCT
    deltas = np.abs(np.linspace(min_decay, max_decay, hw, dtype=np.float32)).reshape(1, hw)

    padc = lambda w: jnp.pad(w, ((0, 0), (0, pk - hid)))
    w1 = jnp.pad(fw1, ((0, pk - HY_EMB_DIM), (0, pk - hid)))
    w2 = jnp.pad(fw2, ((0, pk - hid), (0, pk - hid)))
    w3 = jnp.pad(fw3, ((0, pk - hid), (0, pk - hid)))
    w4 = jnp.pad(fw4, ((0, pk - hid), (0, 0))).astype(BF16)
    b1, b2, b3 = (padc(b.reshape(1, hid)) for b in (fb1, fb2, fb3))
    fr = jnp.pad(freq, ((0, SUBLANE - freq.shape[0]), (0, pk - hid)))
    tt = _tile(n, FILTER_ROWS)
    nw = HY_ORDER * hw
    full = lambda shape: pl.BlockSpec(shape, lambda i: (0, 0))
    return pl.pallas_call(
        functools.partial(_filt_kernel, n=n, tt=tt, hw=hw),
        grid=(n // tt,),
        in_specs=[
            pl.BlockSpec((tt, pk), lambda i: (i, 0)),
            full((pk, pk)), full((1, pk)), full((pk, pk)), full((1, pk)),
            full((pk, pk)), full((1, pk)), full((SUBLANE, pk)), full((pk, 2 * nw)), full((1, hw)),
        ],
        out_specs=[pl.BlockSpec((tt, nw), lambda i: (i, 0))] * 2,
        out_shape=[jax.ShapeDtypeStruct((n, nw), BF16)] * 2,
        compiler_params=_params(("arbitrary",)),
        name="hyena_filters",
    )(jnp.asarray(z), w1, b1, w2, b2, w3, b3, fr, w4, jnp.asarray(deltas))


def _sconv_kernel(p_ref, w_ref, b_ref, o_ref, *, n):
    w = w_ref[...]
    bias = b_ref[...]
    mcls = n // HY_CLASSES
    row = lax.broadcasted_iota(jnp.int32, (mcls, p_ref.shape[2]), 0)

    def cls(j):
        return p_ref[0, j * mcls:(j + 1) * mcls, :].astype(F32)

    for j in range(HY_CLASSES):
        cur = cls(j)
        if j > 0:
            prev = cls(j - 1)
        else:
            prev = jnp.where(row == 0, 0.0, pltpu.roll(cls(HY_CLASSES - 1), 1, 0))
        if j < HY_CLASSES - 1:
            nxt = cls(j + 1)
        else:
            nxt = jnp.where(row == mcls - 1, 0.0, pltpu.roll(cls(0), mcls - 1, 0))
        out = bias + prev * w[0:1] + cur * w[1:2] + nxt * w[2:3]
        o_ref[0, j * mcls:(j + 1) * mcls, :] = out.astype(o_ref.dtype)


def _sconv_call(p, conv_w, conv_b):
    bsz, n, width = p.shape
    tc = _tile(width, ROW_TILE)
    return pl.pallas_call(
        functools.partial(_sconv_kernel, n=n),
        grid=(bsz, width // tc),
        in_specs=[
            pl.BlockSpec((1, n, tc), lambda b, j: (b, 0, j)),
            pl.BlockSpec((HY_SHORT, tc), lambda b, j: (0, j)),
            pl.BlockSpec((1, tc), lambda b, j: (0, j)),
        ],
        out_specs=pl.BlockSpec((1, n, tc), lambda b, j: (b, 0, j)),
        out_shape=jax.ShapeDtypeStruct((bsz, n, width), BF16),
        compiler_params=_params(("arbitrary", "arbitrary")),
        name="short_conv",
    )(p, conv_w, conv_b.reshape(1, width))


def _spec_kernel(cs_ref, hf_ref, hb_ref, kre_ref, kim_ref, wf_scr, wb_scr,
                 *, k0n, mcls):
    j_cls = HY_CLASSES
    tn = hf_ref.shape[1]
    for j in range(j_cls):
        rows = slice(j * mcls, (j + 1) * mcls)
        prod = jnp.dot(cs_ref[j], jnp.concatenate([hf_ref[rows, :], hb_ref[rows, :]], axis=1),
                       preferred_element_type=F32)
        wf_scr[j, 0:k0n, :] = prod[0:k0n, 0:tn]
        wf_scr[j, k0n:2 * k0n, :] = -prod[k0n:2 * k0n, 0:tn]
        wb_scr[j, 0:k0n, :] = prod[0:k0n, tn:2 * tn]
        wb_scr[j, k0n:2 * k0n, :] = -prod[k0n:2 * k0n, tn:2 * tn]

    def body(c, carry):
        halves = []
        for half in range(PACKED_SUBLANE // SUBLANE):
            r8 = pl.ds(pl.multiple_of(c * PACKED_SUBLANE + half * SUBLANE, SUBLANE), SUBLANE)
            i8 = pl.ds(pl.multiple_of(k0n + c * PACKED_SUBLANE + half * SUBLANE, SUBLANE), SUBLANE)
            xs = [(wf_scr[j, r8, :], wf_scr[j, i8, :]) for j in range(j_cls)]
            ys = [(wb_scr[j, r8, :], wb_scr[j, i8, :]) for j in range(j_cls)]
            halves.append((_fft_list(xs, -1), _fft_list(ys, -1)))
        for q in range(j_cls):
            rows = pl.ds(pl.multiple_of(q * k0n + c * PACKED_SUBLANE, PACKED_SUBLANE), PACKED_SUBLANE)
            kre = jnp.concatenate([h[0][q][0] + h[1][q][0] for h in halves], axis=0)
            kim = jnp.concatenate([h[0][q][1] - h[1][q][1] for h in halves], axis=0)
            kre_ref[rows, :] = kre.astype(kre_ref.dtype)
            kim_ref[rows, :] = kim.astype(kim_ref.dtype)
        return carry

    lax.fori_loop(0, k0n // PACKED_SUBLANE, body, 0, unroll=2)


def _spec_call(tables, hf, hb):
    cs, _ = tables
    n, width = hf.shape
    tn = LANE
    k0n, mcls = cs.shape[1] // 2, cs.shape[2]
    const = lambda shape: pl.BlockSpec(shape, lambda j: (0,) * len(shape),
                                       pipeline_mode=pl.Buffered(1))
    return pl.pallas_call(
        functools.partial(_spec_kernel, k0n=k0n, mcls=mcls),
        grid=(width // tn,),
        in_specs=[
            const(cs.shape),
            pl.BlockSpec((n, tn), lambda j: (0, j)),
            pl.BlockSpec((n, tn), lambda j: (0, j)),
        ],
        out_specs=[pl.BlockSpec((2 * n, tn), lambda j: (0, j))] * 2,
        out_shape=[jax.ShapeDtypeStruct((2 * n, width), BF16)] * 2,
        scratch_shapes=[pltpu.VMEM((HY_CLASSES, 2 * k0n, tn), F32)] * 2,
        compiler_params=_params(("arbitrary",)),
        name="filter_spectrum",
    )(cs, hf, hb)


def _fftconv_kernel(cs_ref, cst_ref, z_ref, g_ref, kre_ref, kim_ref, bias_ref,
                    o_ref, w_scr, *, k0n, mcls):
    j_cls = HY_CLASSES
    tn = o_ref.shape[3]
    for j in range(j_cls):
        rows = slice(j * mcls, (j + 1) * mcls)
        prod = jnp.dot(cs_ref[j], jnp.concatenate([z_ref[0, 0, rows, :], z_ref[0, 1, rows, :]], axis=1),
                       preferred_element_type=F32)
        w_scr[j, 0:k0n, :] = prod[0:k0n, 0:tn] + prod[k0n:2 * k0n, tn:2 * tn]
        w_scr[j, k0n:2 * k0n, :] = prod[0:k0n, tn:2 * tn] - prod[k0n:2 * k0n, 0:tn]

    def body(c, carry):
        for half in range(PACKED_SUBLANE // SUBLANE):
            r8 = pl.ds(pl.multiple_of(c * PACKED_SUBLANE + half * SUBLANE, SUBLANE), SUBLANE)
            i8 = pl.ds(pl.multiple_of(k0n + c * PACKED_SUBLANE + half * SUBLANE, SUBLANE), SUBLANE)
            xs = [(w_scr[j, r8, :], w_scr[j, i8, :]) for j in range(j_cls)]
            spec = _fft_list(xs, -1)
            prod = []
            for q in range(j_cls):
                rows = pl.ds(pl.multiple_of(q * k0n + c * PACKED_SUBLANE, PACKED_SUBLANE), PACKED_SUBLANE)
                part = slice(half * SUBLANE, (half + 1) * SUBLANE)
                kq = (kre_ref[rows, :].astype(F32)[part], kim_ref[rows, :].astype(F32)[part])
                prod.append(_cmul(spec[q], kq))
            back = _fft_list(prod, 1)
            for j in range(j_cls):
                w_scr[j, r8, :] = back[j][0]
                w_scr[j, i8, :] = back[j][1]
        return carry

    lax.fori_loop(0, k0n // PACKED_SUBLANE, body, 0, unroll=2)

    bias = bias_ref[...]
    for j in range(j_cls):
        rows = slice(j * mcls, (j + 1) * mcls)
        v2 = jnp.concatenate([w_scr[j, 0:k0n, :], w_scr[j, k0n:2 * k0n, :]], axis=1).astype(BF16)
        prod = jnp.dot(cst_ref[j], v2, preferred_element_type=F32)
        ys = (prod[0:mcls, 0:tn] - prod[mcls:2 * mcls, tn:2 * tn],
              prod[mcls:2 * mcls, 0:tn] + prod[0:mcls, tn:2 * tn])
        for b2 in range(2):
            conv = ys[b2] + z_ref[0, b2, rows, :].astype(F32) * bias
            o_ref[0, b2, rows, :] = (g_ref[0, b2, rows, :].astype(F32) * conv).astype(o_ref.dtype)


def _fftconv_call(tables, z, z_col0, gate, gate_col0, kre, kim, k_col0, bias):
    cs, cst = tables
    bsz, n, _ = z.shape
    hw = bias.shape[0]
    tn = LANE
    k0n, mcls = cs.shape[1] // 2, cs.shape[2]
    zo, go, ko = z_col0 // tn, gate_col0 // tn, k_col0 // tn
    z4 = z.reshape(bsz // 2, 2, n, z.shape[2])
    g4 = gate.reshape(bsz // 2, 2, n, gate.shape[2])
    const = lambda shape: pl.BlockSpec(shape, lambda j, p: (0,) * len(shape),
                                       pipeline_mode=pl.Buffered(1))
    out = pl.pallas_call(
        functools.partial(_fftconv_kernel, k0n=k0n, mcls=mcls),
        grid=(hw // tn, bsz // 2),
        in_specs=[
            const(cs.shape), const(cst.shape),
            pl.BlockSpec((1, 2, n, tn), lambda j, p: (p, 0, 0, zo + j)),
            pl.BlockSpec((1, 2, n, tn), lambda j, p: (p, 0, 0, go + j)),
            pl.BlockSpec((2 * n, tn), lambda j, p: (0, ko + j)),
            pl.BlockSpec((2 * n, tn), lambda j, p: (0, ko + j)),
            pl.BlockSpec((1, tn), lambda j, p: (0, j)),
        ],
        out_specs=pl.BlockSpec((1, 2, n, tn), lambda j, p: (p, 0, 0, j)),
        out_shape=jax.ShapeDtypeStruct((bsz // 2, 2, n, hw), BF16),
        scratch_shapes=[pltpu.VMEM((HY_CLASSES, 2 * k0n, tn), F32)],
        compiler_params=_params(("arbitrary", "arbitrary")),
        name="fft_conv",
    )(cs, cst, z4, g4, kre, kim, bias.reshape(1, hw))
    return out.reshape(bsz, n, hw)


def _wout_kernel(r_ref, y_ref, wa_ref, wb_ref, x_ref, gate_ref, o_ref):
    y = (jnp.dot(r_ref[0], wa_ref[...], preferred_element_type=F32)
         + jnp.dot(y_ref[0], wb_ref[...], preferred_element_type=F32))
    o_ref[0] = x_ref[0] + gate_ref[0] * y


def _wout_call(r, yh, w_out, x, gate):
    bsz, n, d = x.shape
    rw = r.shape[2]
    assert d == 2 * rw and yh.shape[2] == rw
    tm = _tile(n, ROW_TILE)
    return pl.pallas_call(
        _wout_kernel,
        grid=(bsz, n // tm),
        in_specs=[
            pl.BlockSpec((1, tm, rw), lambda b, i: (b, i, 0)),
            pl.BlockSpec((1, tm, d - rw), lambda b, i: (b, i, 0)),
            pl.BlockSpec((rw, d), lambda b, i: (0, 0)),
            pl.BlockSpec((rw, d), lambda b, i: (1, 0)),
            pl.BlockSpec((1, tm, d), lambda b, i: (b, i, 0)),
            pl.BlockSpec((1, 1, d), lambda b, i: (b, 0, 0)),
        ],
        out_specs=pl.BlockSpec((1, tm, d), lambda b, i: (b, i, 0)),
        out_shape=jax.ShapeDtypeStruct((bsz, n, d), F32),
        compiler_params=_params(("arbitrary", "arbitrary")),
        name="mixer_out",
    )(r, yh, w_out, w_out, x, gate)


def _pool_kernel(h_ref, mc_ref, icol_ref, w_ref, sc_ref, x_ref, gate_ref, o_ref, cp_scr, pm_scr,
                 *, n, tm, to):
    g = pl.program_id(1)
    i = pl.program_id(2)
    n_rows = n // GRID_W

    @pl.when(i == 0)
    def _():
        for t in range(n // tm):
            rows = slice(t * tm, (t + 1) * tm)
            cp_scr[rows, :] = jnp.dot(mc_ref[0], h_ref[0, rows, :], preferred_element_type=F32)

    def grid_row(r):
        return slice(r * GRID_W, (r + 1) * GRID_W)

    for gi, win in enumerate(POOL_WINDOWS):
        @pl.when(jnp.logical_and(i == 0, g == gi))
        def _(win=win):
            lo, hi = win // 2, win - 1 - win // 2
            icol = icol_ref[0]
            acc = cp_scr[grid_row(0), :]
            for r in range(1, hi + 1):
                acc = acc + cp_scr[grid_row(r), :]
            for r in range(n_rows):
                if r > 0 and r + hi < n_rows:
                    acc = acc + cp_scr[grid_row(r + hi), :]
                if r - lo - 1 >= 0:
                    acc = acc - cp_scr[grid_row(r - lo - 1), :]
                cnt_r = min(r + hi, n_rows - 1) + 1 - max(r - lo, 0)
                mean = acc * (icol * (1.0 / cnt_r))
                pm_scr[grid_row(r), :] = (mean - h_ref[0, grid_row(r), :].astype(F32)).astype(BF16)

    rows = pl.ds(pl.multiple_of(i * to, to), to)
    y = jnp.dot(pm_scr[rows, :], w_ref[0], preferred_element_type=F32) * sc_ref[...]
    o_ref[0] = x_ref[0] + gate_ref[0] * y


def _pool_tables(tm, pg):
    tok = np.arange(tm)
    r_t, c_t = tok // GRID_W, tok % GRID_W
    mats, inv = [], []
    for win in POOL_WINDOWS:
        lo, hi = -(win // 2), win - 1 - win // 2
        dc = c_t[None, :] - c_t[:, None]
        mats.append(((r_t[None, :] == r_t[:, None]) & (dc >= lo) & (dc <= hi)).astype(np.float32))
        idx_c = np.arange(GRID_W)
        cnt_c = np.minimum(idx_c + hi, GRID_W - 1) + 1 - np.maximum(idx_c + lo, 0)
        inv.append(np.broadcast_to((1.0 / cnt_c)[:, None], (GRID_W, pg)))
    return (jnp.asarray(np.stack(mats), dtype=BF16),
            jnp.asarray(np.stack(inv).astype(np.float32)))


def _pool_call(h, pool_w, pool_scale, x, gate):
    bsz, n, d = x.shape
    ng = len(POOL_WINDOWS)
    pg = d // ng
    tm = _tile(n, ROW_TILE)
    to = _tile(n, 2 * WIDE_TILE)
    assert tm % GRID_W == 0 and n % GRID_W == 0
    mats, icol = _pool_tables(tm, pg)
    return pl.pallas_call(
        functools.partial(_pool_kernel, n=n, tm=tm, to=to),
        grid=(bsz, ng, n // to),
        in_specs=[
            pl.BlockSpec((1, n, pg), lambda b, g, i: (b, 0, g)),
            pl.BlockSpec((1, tm, tm), lambda b, g, i: (g, 0, 0)),
            pl.BlockSpec((1, GRID_W, pg), lambda b, g, i: (g, 0, 0)),
            pl.BlockSpec((1, pg, pg), lambda b, g, i: (g, 0, 0)),
            pl.BlockSpec((1, pg), lambda b, g, i: (0, g)),
            pl.BlockSpec((1, to, pg), lambda b, g, i: (b, i, g)),
            pl.BlockSpec((1, 1, pg), lambda b, g, i: (b, 0, g)),
        ],
        out_specs=pl.BlockSpec((1, to, pg), lambda b, g, i: (b, i, g)),
        out_shape=jax.ShapeDtypeStruct((bsz, n, d), F32),
        scratch_shapes=[pltpu.VMEM((n, pg), F32), pltpu.VMEM((n, pg), BF16)],
        compiler_params=_params(("arbitrary", "arbitrary", "arbitrary")),
        name="pool_mixer",
    )(h, mats, icol, pool_w, pool_scale.reshape(1, d), x, gate)


def _rotary_tables(n):
    pos = np.arange(n)
    row = (pos // GRID_W).astype(np.float32)
    col = (pos % GRID_W).astype(np.float32)
    n_freq = HEAD_DIM // 4
    inv = (np.float32(ROPE_BASE) ** (-np.arange(n_freq, dtype=np.float32) / np.float32(n_freq))).astype(np.float32)
    ang = np.concatenate([row[:, None] * inv, col[:, None] * inv], axis=-1).astype(np.float32)
    cos, sin = np.cos(ang.astype(np.float64)), np.sin(ang.astype(np.float64))
    cos2 = np.concatenate([cos, cos], axis=-1).astype(np.float32)
    sin2 = np.concatenate([-sin, sin], axis=-1).astype(np.float32)
    return jnp.asarray(cos2), jnp.asarray(sin2)


def kernel(x, c, ctx, c_ctx, w_mod, b_mod, ffn1_w1, ffn1_w3, ffn1_w2, ffn2_w1, ffn2_w3, ffn2_w2, ab_w_in, ab_w_out, ret_log_decay, hy_conv_w, hy_conv_b, hy_f_w1, hy_f_b1, hy_f_w2, hy_f_b2, hy_f_w3, hy_f_b3, hy_f_freq, hy_f_w4, hy_bias, pool_w, pool_scale, final_gain):
    bsz, n, d = x.shape
    n_ctx = ctx.shape[1]
    depth = w_mod.shape[0]
    assert depth == 2, "layer 0 = retention/Hyena mixer, layer 1 = pooling mixer"
    ret_w = RET_HEADS * HEAD_DIM
    hw = d - ret_w
    assert hy_f_w4.shape[2] == HY_ORDER * 2 * hw and bsz % 2 == 0

    c_all = jnp.concatenate([c, c_ctx[None, :], jnp.zeros((SUBLANE - bsz - 1, d), F32)], axis=0)
    mod = _mod_call(c_all, w_mod, b_mod).reshape(depth, SUBLANE, N_MOD, d)

    def mod_rows(layer, base, count=3):
        return mod[layer, :bsz, base:base + count]

    def mod_ctx(base, count=3):
        return mod[0, bsz:bsz + 1, base:base + count]

    n_f = ffn1_w1.shape[2] // _tile(ffn1_w1.shape[2], FFN_TF)
    w13 = _cast_pairs_call([(ffn1_w1, ffn1_w3), (ffn2_w1, ffn2_w3)], n_f)
    w_in_shape, w_out_shape = ab_w_in.shape[1:], ab_w_out.shape[1:]
    w2a, w2b, w_in, w_out = _cast_call(
        [ffn1_w2, ffn2_w2, ab_w_in[0].reshape(depth, -1, w_in_shape[1]),
         ab_w_out[0].reshape(depth, -1, w_out_shape[1])], CAST_STEPS)
    w_in, w_out = w_in.reshape(w_in_shape), w_out.reshape(w_out_shape)
    ffn1_w, ffn2_w = (w13[0], w2a), (w13[1], w2b)

    x, h = _ffn_call(x, mod_rows(0, 0), *ffn1_w, 0, modn=mod_rows(0, 3))
    _, h_c = _ffn_call(ctx.reshape(1, bsz * n_ctx, d), mod_ctx(0), *ffn1_w, 0, modn=mod_ctx(3))

    p = _matmul_call(h.reshape(bsz * n, d), w_in, BF16, tm_pref=2 * WIDE_TILE,
                     b_cols=(0, 4 * ret_w)).reshape(bsz, n, -1)
    kvc = _matmul_call(h_c.reshape(bsz * n_ctx, d), w_in, BF16, b_cols=(ret_w, 3 * ret_w))
    kvc = kvc.reshape(bsz, n_ctx, 2 * ret_w)

    cos2, sin2 = _rotary_tables(n)
    r = _ret_call(p, kvc, ret_log_decay[0], cos2, sin2, ret_w)

    mcls = n // HY_CLASSES
    h_cm = h.reshape(bsz, mcls, HY_CLASSES, d).transpose(0, 2, 1, 3).reshape(bsz * n, d)
    p_hy = _matmul_call(h_cm, w_in, BF16, tm_pref=2 * WIDE_TILE,
                        b_cols=(4 * ret_w, w_in.shape[1])).reshape(bsz, n, -1)
    hf, hb = _filt_call(n, hy_f_w1[0], hy_f_b1[0], hy_f_w2[0], hy_f_b2[0], hy_f_w3[0],
                        hy_f_b3[0], hy_f_freq[0], hy_f_w4[0], hw)
    tables = _fft_tables(n)
    kre, kim = _spec_call(tables, hf, hb)
    u = _sconv_call(p_hy, hy_conv_w[0], hy_conv_b[0])
    z = _fftconv_call(tables, u, 0, u, hw, kre, kim, 0, hy_bias[0, 0])
    yh = _fftconv_call(tables, z, 0, u, 2 * hw, kre, kim, hw, hy_bias[0, 1])
    yh = yh.reshape(bsz, HY_CLASSES, mcls, hw).transpose(0, 2, 1, 3).reshape(bsz, n, hw)

    x = _wout_call(r, yh, w_out, x, mod_rows(0, 5, 1))
    x = _ffn_call(x, mod_rows(0, 6), *ffn2_w, 0)

    x, h = _ffn_call(x, mod_rows(1, 0), *ffn1_w, 1, modn=mod_rows(1, 3))
    x = _pool_call(h, pool_w[0].astype(BF16), pool_scale[0], x, mod_rows(1, 5, 1))
    return _ffn_call(x, mod_rows(1, 6), *ffn2_w, 1, gain=final_gain)
```
